```python
import jax
import jax.numpy as jnp
from jax import lax
import numpy as np

D_MODEL = 1024
BATCH = 2
SEQ = 8192
DEPTH = 2

GRID_W = 64
CTX_LEN = 256
EPS = 1e-6

N_BRANCH = 4
BRANCH_W = D_MODEL // 2
CHUNK = GRID_W
CONV_W = 4
CONV_LEFT = 2

RG_W = BRANCH_W
RG_BLOCKS = 8
RG_BW = RG_W // RG_BLOCKS
RG_C = 8.0
GLA_H = 4
GLA_DK = BRANCH_W // (2 * GLA_H)
GLA_DV = BRANCH_W // GLA_H
GLA_RANK = 16
GLA_GATE_NORM = 16.0
HG_H = 4
HG_DK = 128
HG_DV = BRANCH_W // HG_H
ML_H = 4
ML_DK = BRANCH_W // ML_H
ML_DV = BRANCH_W // ML_H
N_GROUPS = 4
EXPERTS_PER_GROUP = 4
N_EXPERTS = N_GROUPS * EXPERTS_PER_GROUP
TOP_K = 2
D_EXPERT = D_MODEL // 2
MOE_BLOCK = 128

IN_PARTS = (
    ('rg_x', RG_W), ('rg_y', RG_W),
    ('gla_q', GLA_H * GLA_DK), ('gla_k', GLA_H * GLA_DK), ('gla_v', GLA_H * GLA_DV),
    ('gla_g', GLA_H * GLA_DV), ('gla_lr', 2 * GLA_RANK),
    ('hg_q', HG_H * HG_DK), ('hg_i', HG_H * HG_DV), ('hg_f', 2 * HG_H * HG_DK), ('hg_g', HG_H * HG_DV),
    ('ml_q', ML_H * ML_DK), ('ml_k', ML_H * ML_DK), ('ml_v', ML_H * ML_DV), ('ml_o', ML_H * ML_DV),
    ('ml_if', 2 * 2 * ML_H),
)
IN_WIDTH = sum(w for _, w in IN_PARTS)

kernel_name = 'hybrid_bidir_recurrent_hmoe_dit'


def _rmsnorm(x, g):
    xf = x.astype(jnp.float32)
    y = xf * lax.rsqrt(jnp.mean(jnp.square(xf), axis=-1, keepdims=True) + EPS)
    return (y * g.astype(jnp.float32)).astype(x.dtype)


def _modulate(h, g, shift, scale):
    return _rmsnorm(h, g) * (1.0 + scale) + shift


def _split_in(z):
    parts, off = {}, 0
    for name, width in IN_PARTS:
        parts[name] = z[..., off:off + width]
        off += width
    return parts


def _seg_flip(a, n_ctx, axis):
    ctx_part, lat_part = jnp.split(a, [n_ctx], axis=axis)
    return jnp.concatenate([jnp.flip(ctx_part, axis), jnp.flip(lat_part, axis)], axis=axis)


def _bidirectional(scan_fn, n_ctx, axis, fwd_args, bwd_args):
    y_fwd = scan_fn(*fwd_args)
    y_bwd = scan_fn(*[_seg_flip(a, n_ctx, axis) for a in bwd_args])
    return y_fwd + _seg_flip(y_bwd, n_ctx, axis)


def _dwconv_centred(u, w, b):
    t = u.shape[1]
    up = jnp.pad(u, ((0, 0), (CONV_LEFT, CONV_W - 1 - CONV_LEFT), (0, 0)))
    return b + sum(w[j] * up[:, j:j + t] for j in range(CONV_W))


def _conv_two_seqs(u, n_ctx, w, b):
    return jnp.concatenate([_dwconv_centred(u[:, :n_ctx], w, b), _dwconv_centred(u[:, n_ctx:], w, b)], axis=1)


def _heads(a, h):
    bsz, n, _ = a.shape
    return a.reshape(bsz, n, h, -1).transpose(0, 2, 1, 3)


def _unheads(a):
    bsz, h, n, d = a.shape
    return a.transpose(0, 2, 1, 3).reshape(bsz, n, h * d)


def _to_chunks(a):
    bsz, h, n = a.shape[:3]
    return jnp.moveaxis(a.reshape((bsz, h, n // CHUNK, CHUNK) + a.shape[3:]), 2, 0)


def _from_chunks(a):
    nc, bsz, h, l = a.shape[:4]
    return jnp.moveaxis(a, 0, 2).reshape((bsz, h, nc * l) + a.shape[4:])


def _lin_combine(left, right):
    a_l, b_l = left
    a_r, b_r = right
    return a_l * a_r, a_r * b_l + b_r


def _linear_scan(a, bterm):
    _, h = lax.associative_scan(_lin_combine, (a, bterm), axis=1)
    return h


def _gated_linear_scan(q, k, v, log_g):
    bsz, h, _, dk = q.shape
    dv = v.shape[-1]
    tril = jnp.tril(jnp.ones((CHUNK, CHUNK), dtype=bool))

    def chunk_step(state, blk):
        qc, kc, vc, gc = blk
        b = jnp.cumsum(gc, axis=2)
        rel = jnp.where(tril[:, :, None], b[:, :, :, None, :] - b[:, :, None, :, :], -jnp.inf)
        scores = jnp.einsum('bhid,bhjd,bhijd->bhij', qc, kc, jnp.exp(rel))
        o = (jnp.einsum('bhij,bhjv->bhiv', scores, vc)
             + jnp.einsum('bhid,bhdv->bhiv', qc * jnp.exp(b), state))
        b_end = b[:, :, -1:, :]
        state = (jnp.exp(b_end[:, :, 0, :, None]) * state
                 + jnp.einsum('bhjd,bhjv->bhdv', kc * jnp.exp(b_end - b), vc))
        return state, o

    s0 = jnp.zeros((bsz, h, dk, dv), jnp.float32)
    _, o = lax.scan(chunk_step, s0, tuple(_to_chunks(a.astype(jnp.float32)) for a in (q, k, v, log_g)))
    return _from_chunks(o)


def _mlstm_scan(q, k, v, i_pre, log_f):
    bsz, h, _, dk = q.shape
    dv = v.shape[-1]
    tril = jnp.tril(jnp.ones((CHUNK, CHUNK), dtype=bool))

    def chunk_step(carry, blk):
        c_mat, n_vec, m_prev = carry
        qc, kc, vc, ic, fc = blk
        b = jnp.cumsum(fc, axis=-1)
        log_w = jnp.where(tril, b[..., :, None] - b[..., None, :] + ic[..., None, :], -jnp.inf)
        log_inter = b + m_prev[..., None]
        m = jnp.maximum(log_inter, jnp.max(log_w, axis=-1))
        w_inter = jnp.exp(log_inter - m)
        s = jnp.einsum('bhid,bhjd->bhij', qc, kc) * jnp.exp(log_w - m[..., None])
        num = (jnp.einsum('bhij,bhjv->bhiv', s, vc)
               + w_inter[..., None] * jnp.einsum('bhid,bhdv->bhiv', qc, c_mat))
        den = jnp.sum(s, axis=-1) + w_inter * jnp.einsum('bhid,bhd->bhi', qc, n_vec)
        h_out = num / jnp.maximum(jnp.abs(den), jnp.exp(-m))[..., None]
        m_new = m[..., -1]
        w_end = jnp.exp(b[..., -1:] - b + ic - m_new[..., None])
        decay = jnp.exp(b[..., -1] + m_prev - m_new)
        c_mat = decay[..., None, None] * c_mat + jnp.einsum('bhj,bhjd,bhjv->bhdv', w_end, kc, vc)
        n_vec = decay[..., None] * n_vec + jnp.einsum('bhj,bhjd->bhd', w_end, kc)
        return (c_mat, n_vec, m_new), h_out

    init = (jnp.zeros((bsz, h, dk, dv), jnp.float32), jnp.zeros((bsz, h, dk), jnp.float32),
            jnp.zeros((bsz, h), jnp.float32))
    _, o = lax.scan(chunk_step, init, tuple(_to_chunks(a.astype(jnp.float32)) for a in (q, k, v, i_pre, log_f)))
    return _from_chunks(o)


def _rglru_coeffs(xb, gate_w, gate_b, lam):
    bsz, n, _ = xb.shape
    blk = xb.reshape(bsz, n, RG_BLOCKS, RG_BW)
    pre = (jnp.einsum('bnki,gkij->gbnkj', blk, gate_w.astype(jnp.float32)).reshape(2, bsz, n, RG_W)
           + gate_b.astype(jnp.float32)[:, None, None, :])
    r, i = jax.nn.sigmoid(pre[0]), jax.nn.sigmoid(pre[1])
    log_a = -RG_C * r * jax.nn.softplus(-lam.astype(jnp.float32))
    return jnp.exp(log_a), jnp.sqrt(-jnp.expm1(2.0 * log_a)) * (i * xb)


def _rglru_branch(p, lp, n_ctx):
    xb = _conv_two_seqs(p['rg_x'], n_ctx, lp['rg_conv_w'], lp['rg_conv_b']).astype(jnp.float32)
    a_f, b_f = _rglru_coeffs(xb, lp['rg_gate_w'][0], lp['rg_gate_b'][0], lp['rg_lambda'][0])
    a_b, b_b = _rglru_coeffs(xb, lp['rg_gate_w'][1], lp['rg_gate_b'][1], lp['rg_lambda'][1])
    h = _bidirectional(_linear_scan, n_ctx, 1, (a_f, b_f), (a_b, b_b))
    return (jax.nn.gelu(p['rg_y'].astype(jnp.float32)) * h).astype(p['rg_x'].dtype)


def _gla_branch(p, lp, n_ctx):
    f32 = jnp.float32
    bsz, n, _ = p['gla_q'].shape
    q = _heads(p['gla_q'].astype(f32), GLA_H) * GLA_DK ** -0.5
    k = _heads(p['gla_k'].astype(f32), GLA_H)
    v = _heads(p['gla_v'].astype(f32), GLA_H)
    lr = p['gla_lr'].astype(f32).reshape(bsz, n, 2, GLA_RANK)
    pre = (jnp.einsum('bndr,drk->dbnk', lr, lp['gla_w_lr'].astype(f32))
           + lp['gla_b_lr'].astype(f32)[:, None, None, :])
    log_a = jax.nn.log_sigmoid(pre) / GLA_GATE_NORM
    o = _bidirectional(_gated_linear_scan, n_ctx, 2,
                       (q, k, v, _heads(log_a[0], GLA_H)), (q, k, v, _heads(log_a[1], GLA_H)))
    o = _rmsnorm(o, lp['gla_norm_g'])
    return (_unheads(o) * jax.nn.silu(p['gla_g'].astype(f32))).astype(p['gla_q'].dtype)


def _hgrn2_branch(p, lp, lb, n_ctx):
    f32 = jnp.float32
    bsz, n, _ = p['hg_q'].shape
    q = _heads(jax.nn.silu(p['hg_q'].astype(f32)), HG_H) * HG_DK ** -0.5
    v = _heads(p['hg_i'].astype(f32), HG_H)
    f = p['hg_f'].astype(f32).reshape(bsz, n, 2, HG_H * HG_DK)
    log_forget = jnp.logaddexp(jnp.log(lb), jnp.log1p(-lb) + jax.nn.log_sigmoid(f))
    key = (1.0 - lb) * jax.nn.sigmoid(-f)
    o = _bidirectional(_gated_linear_scan, n_ctx, 2,
                       (q, _heads(key[:, :, 0], HG_H), v, _heads(log_forget[:, :, 0], HG_H)),
                       (q, _heads(key[:, :, 1], HG_H), v, _heads(log_forget[:, :, 1], HG_H)))
    o = _rmsnorm(o, lp['hgrn_norm_g'])
    return (_unheads(o) * jax.nn.silu(p['hg_g'].astype(f32))).astype(p['hg_q'].dtype)


def _mlstm_branch(p, lp, n_ctx):
    f32 = jnp.float32
    bsz, n, _ = p['ml_q'].shape
    qk = _conv_two_seqs(jnp.concatenate([p['ml_q'], p['ml_k']], axis=-1), n_ctx, lp['ml_conv_w'], lp['ml_conv_b'])
    qk = jax.nn.silu(qk.astype(f32))
    q = _heads(qk[..., :ML_H * ML_DK], ML_H) * ML_DK ** -0.5
    k = _heads(qk[..., ML_H * ML_DK:], ML_H)
    v = _heads(p['ml_v'].astype(f32), ML_H)
    g = p['ml_if'].astype(f32).reshape(bsz, n, 2, 2, ML_H) + lp['ml_gate_b'].astype(f32)
    g = g.transpose(2, 3, 0, 4, 1)
    i_pre, log_f = g[:, 0], jax.nn.log_sigmoid(g[:, 1])
    h = _bidirectional(_mlstm_scan, n_ctx, 2, (q, k, v, i_pre[0], log_f[0]), (q, k, v, i_pre[1], log_f[1]))
    h = _rmsnorm(h, lp['ml_norm_g'])
    return (jax.nn.sigmoid(p['ml_o'].astype(f32)) * _unheads(h)).astype(p['ml_q'].dtype)


def _token_mixer(u, n_ctx, lp, lb, latent_only):
    p = _split_in(u @ lp['w_in'])
    ys = jnp.stack([_rglru_branch(p, lp, n_ctx), _gla_branch(p, lp, n_ctx),
                    _hgrn2_branch(p, lp, lb, n_ctx), _mlstm_branch(p, lp, n_ctx)], axis=2)
    if latent_only:
        ys, u = ys[:, n_ctx:], u[:, n_ctx:]
    bsz, n, _ = u.shape
    gates = jax.nn.sigmoid(u @ lp['w_merge'] + lp['b_merge']).reshape(bsz, n, N_BRANCH, D_MODEL)
    branch_out = jnp.einsum('bnkc,kcd->bnkd', ys, lp['w_branch'])
    return jnp.sum(gates * branch_out, axis=2) @ lp['w_out']


def _routed_experts(xf, experts, weights, w1, w3, w2):
    n_tok, d = xf.shape
    n_asg = n_tok * TOP_K
    n_blocks = (n_asg + N_EXPERTS * (MOE_BLOCK - 1) + MOE_BLOCK - 1) // MOE_BLOCK
    flat_e = experts.reshape(-1)
    order = jnp.argsort(flat_e)
    e_sorted = flat_e[order]
    counts = jnp.bincount(flat_e, length=N_EXPERTS)
    padded = (counts + MOE_BLOCK - 1) // MOE_BLOCK * MOE_BLOCK
    pad_end = jnp.cumsum(padded)
    seg_start = jnp.cumsum(counts) - counts
    slot = (pad_end - padded)[e_sorted] + jnp.arange(n_asg) - seg_start[e_sorted]
    tok_sorted = (order // TOP_K).astype(jnp.int32)
    slot_tok = jnp.full((n_blocks * MOE_BLOCK,), n_tok, jnp.int32).at[slot].set(tok_sorted)
    x_pad = jnp.concatenate([xf, jnp.zeros((1, d), xf.dtype)], axis=0)
    x_blocks = x_pad[slot_tok].reshape(n_blocks, MOE_BLOCK, d)
    block_expert = jnp.minimum(jnp.searchsorted(pad_end, jnp.arange(n_blocks) * MOE_BLOCK, side='right'),
                               N_EXPERTS - 1)

    def expert_mlp(args):
        xb, e = args
        return (jax.nn.silu(xb @ w1[e]) * (xb @ w3[e])) @ w2[e]

    y_slots = lax.map(expert_mlp, (x_blocks, block_expert)).reshape(-1, d)
    y_asg = (y_slots[slot] * weights.reshape(-1)[order][:, None]).astype(xf.dtype)
    return jnp.zeros_like(xf).at[tok_sorted].add(y_asg)


def _hier_moe(u, lp):
    bsz, n, d = u.shape
    xf = u.reshape(-1, d)
    n_tok = xf.shape[0]
    g_logits = (xf @ lp['moe_w_group'] + lp['moe_b_group']).astype(jnp.float32)
    g_top, grp = lax.top_k(g_logits, 1)
    p_grp = jnp.exp(g_top - jax.nn.logsumexp(g_logits, axis=-1, keepdims=True))
    e_logits = (xf @ lp['moe_w_expert'] + lp['moe_b_expert']).astype(jnp.float32)
    e_logits = e_logits.reshape(n_tok, N_GROUPS, EXPERTS_PER_GROUP)
    idx = jnp.broadcast_to(grp[:, :, None], (n_tok, 1, EXPERTS_PER_GROUP))
    e_in = jnp.take_along_axis(e_logits, idx, axis=1)[:, 0]
    e_top, e_idx = lax.top_k(e_in, TOP_K)
    weights = p_grp * jax.nn.softmax(e_top, axis=-1)
    experts = grp * EXPERTS_PER_GROUP + e_idx
    y = _routed_experts(xf, experts, weights, lp['moe_w1'], lp['moe_w3'], lp['moe_w2'])
    return y.reshape(bsz, n, d)


def _layer(h_ctx, h_lat, c, c_ctx, lp, lb, last):
    n_ctx = h_ctx.shape[1]
    mod_lat = jax.nn.silu(c) @ lp['ada_w'] + lp['ada_b']
    mod_ctx = jax.nn.silu(c_ctx) @ lp['ada_w'] + lp['ada_b']
    sh1_l, sc1_l, g1_l, sh2_l, sc2_l, g2_l = jnp.split(mod_lat[:, None, :], 6, axis=-1)
    sh1_c, sc1_c, g1_c, sh2_c, sc2_c, g2_c = jnp.split(mod_ctx, 6, axis=-1)
    u = jnp.concatenate([_modulate(h_ctx, lp['norm_mix_g'], sh1_c, sc1_c),
                         _modulate(h_lat, lp['norm_mix_g'], sh1_l, sc1_l)], axis=1)
    mix = _token_mixer(u, n_ctx, lp, lb, latent_only=last)
    if last:
        h_lat = h_lat + g1_l * mix
        u2 = _modulate(h_lat, lp['norm_ffn_g'], sh2_l, sc2_l)
        return None, h_lat + g2_l * _hier_moe(u2, lp)
    h_ctx = h_ctx + g1_c * mix[:, :n_ctx]
    h_lat = h_lat + g1_l * mix[:, n_ctx:]
    u2 = jnp.concatenate([_modulate(h_ctx, lp['norm_ffn_g'], sh2_c, sc2_c),
                          _modulate(h_lat, lp['norm_ffn_g'], sh2_l, sc2_l)], axis=1)
    ffn = _hier_moe(u2, lp)
    return h_ctx + g2_c * ffn[:, :n_ctx], h_lat + g2_l * ffn[:, n_ctx:]


def setup_inputs(seed: int = 0) -> dict:
    key = jax.random.key(seed)
    ks = iter(jax.random.split(key, 48))

    def nrm(shape, scale):
        return scale * jax.random.normal(next(ks), shape, jnp.float32)

    def gain(shape):
        return 1.0 + nrm(shape, 0.05)

    d = D_MODEL
    return {
        'x': nrm((BATCH, SEQ, d), 1.0),
        'c': nrm((BATCH, d), 1.0),
        'ctx': nrm((BATCH, CTX_LEN, d), 1.0),
        'c_ctx': nrm((d,), 1.0),
        'ada_w': nrm((DEPTH, d, 6 * d), 0.5 * d ** -0.5),
        'ada_b': nrm((DEPTH, 6 * d), 0.01),
        'norm_mix_g': gain((DEPTH, d)),
        'norm_ffn_g': gain((DEPTH, d)),
        'w_in': nrm((DEPTH, d, IN_WIDTH), d ** -0.5),
        'rg_conv_w': nrm((DEPTH, CONV_W, RG_W), 0.5),
        'rg_conv_b': nrm((DEPTH, RG_W), 0.01),
        'rg_gate_w': nrm((DEPTH, 2, 2, RG_BLOCKS, RG_BW, RG_BW), RG_BW ** -0.5),
        'rg_gate_b': nrm((DEPTH, 2, 2, RG_W), 0.1),
        'rg_lambda': 5.0 + nrm((DEPTH, 2, RG_W), 0.5),
        'gla_w_lr': nrm((DEPTH, 2, GLA_RANK, GLA_H * GLA_DK), GLA_RANK ** -0.5),
        'gla_b_lr': nrm((DEPTH, 2, GLA_H * GLA_DK), 0.1),
        'gla_norm_g': gain((DEPTH, GLA_DV)),
        'hgrn_lb_logits': nrm((DEPTH, 2, HG_H * HG_DK), 1.0),
        'hgrn_norm_g': gain((DEPTH, HG_DV)),
        'ml_conv_w': nrm((DEPTH, CONV_W, 2 * ML_H * ML_DK), 0.5),
        'ml_conv_b': nrm((DEPTH, 2 * ML_H * ML_DK), 0.01),
        'ml_gate_b': nrm((DEPTH, 2, 2, ML_H), 0.5) + jnp.array([0.0, 3.0], jnp.float32)[:, None],
        'ml_norm_g': gain((DEPTH, ML_DV)),
        'w_branch': nrm((DEPTH, N_BRANCH, BRANCH_W, d), BRANCH_W ** -0.5),
        'w_merge': nrm((DEPTH, d, N_BRANCH * d), d ** -0.5),
        'b_merge': nrm((DEPTH, N_BRANCH * d), 0.1),
        'w_out': nrm((DEPTH, d, d), d ** -0.5),
        'moe_w_group': nrm((DEPTH, d, N_GROUPS), d ** -0.5),
        'moe_b_group': nrm((DEPTH, N_GROUPS), 0.01),
        'moe_w_expert': nrm((DEPTH, d, N_EXPERTS), d ** -0.5),
        'moe_b_expert': nrm((DEPTH, N_EXPERTS), 0.01),
        'moe_w1': nrm((DEPTH, N_EXPERTS, d, D_EXPERT), d ** -0.5),
        'moe_w3': nrm((DEPTH, N_EXPERTS, d, D_EXPERT), d ** -0.5),
        'moe_w2': nrm((DEPTH, N_EXPERTS, D_EXPERT, d), D_EXPERT ** -0.5),
        'final_norm_g': gain((d,)),
    }


def reference(x, c, ctx, c_ctx, ada_w, ada_b, norm_mix_g, norm_ffn_g, w_in, rg_conv_w, rg_conv_b,
              rg_gate_w, rg_gate_b, rg_lambda, gla_w_lr, gla_b_lr, gla_norm_g, hgrn_lb_logits,
              hgrn_norm_g, ml_conv_w, ml_conv_b, ml_gate_b, ml_norm_g, w_branch, w_merge, b_merge,
              w_out, moe_w_group, moe_b_group, moe_w_expert, moe_b_expert, moe_w1, moe_w3, moe_w2,
              final_norm_g):
    lb_cum = jnp.cumsum(jax.nn.softmax(hgrn_lb_logits.astype(jnp.float32), axis=0), axis=0)
    hgrn_lb = lb_cum - lb_cum[:1]
    h_ctx, h_lat = ctx, x
    for l in range(DEPTH):
        lp = dict(ada_w=ada_w[l], ada_b=ada_b[l], norm_mix_g=norm_mix_g[l], norm_ffn_g=norm_ffn_g[l],
                  w_in=w_in[l], rg_conv_w=rg_conv_w[l], rg_conv_b=rg_conv_b[l], rg_gate_w=rg_gate_w[l],
                  rg_gate_b=rg_gate_b[l], rg_lambda=rg_lambda[l], gla_w_lr=gla_w_lr[l], gla_b_lr=gla_b_lr[l],
                  gla_norm_g=gla_norm_g[l], hgrn_norm_g=hgrn_norm_g[l], ml_conv_w=ml_conv_w[l],
                  ml_conv_b=ml_conv_b[l], ml_gate_b=ml_gate_b[l], ml_norm_g=ml_norm_g[l],
                  w_branch=w_branch[l], w_merge=w_merge[l], b_merge=b_merge[l], w_out=w_out[l],
                  moe_w_group=moe_w_group[l], moe_b_group=moe_b_group[l], moe_w_expert=moe_w_expert[l],
                  moe_b_expert=moe_b_expert[l], moe_w1=moe_w1[l], moe_w3=moe_w3[l], moe_w2=moe_w2[l])
        h_ctx, h_lat = _layer(h_ctx, h_lat, c, c_ctx, lp, hgrn_lb[l], last=(l == DEPTH - 1))
    return _rmsnorm(h_lat, final_norm_g)
```

```python
import functools

import numpy as np
import jax
import jax.numpy as jnp
from jax import lax
from jax.experimental import pallas as pl
from jax.experimental.pallas import tpu as pltpu

F32 = jnp.float32
BF16 = jnp.bfloat16

EPS = 1e-6
TM = 256
CHUNK = 64
SUBLANES = 8
N_HEADS = 4
HEAD_W = 128
BRANCH_W = 512
CONV_W = 4
CONV_LEFT = 2
RG_C = 8.0
GLA_DK = 64
GLA_RANK = 16
GLA_GATE_NORM = 16.0
N_GROUPS = 4
EXPERTS_PER_GROUP = 4
N_EXPERTS = 16
N_LEVELS = 6
VMEM_LIMIT = 56 * 1024 * 1024

W_RG = 2 * BRANCH_W
W_GLA = 4 * BRANCH_W + HEAD_W
W_HG = 5 * BRANCH_W
W_ML = 4 * BRANCH_W + HEAD_W


def _mm(a, b):
    return jnp.dot(a.astype(BF16), b.astype(BF16), preferred_element_type=F32)


def _mm_nt(a, b):
    return lax.dot_general(a.astype(BF16), b.astype(BF16), (((1,), (1,)), ((), ())),
                           preferred_element_type=F32)


def _mm_tn(a, b):
    return lax.dot_general(a.astype(BF16), b.astype(BF16), (((0,), (0,)), ((), ())),
                           preferred_element_type=F32)


def _mm_sel(sel, x):
    x1 = x.astype(BF16)
    r1 = x - x1.astype(F32)
    x2 = r1.astype(BF16)
    x3 = (r1 - x2.astype(F32)).astype(BF16)
    dot = functools.partial(jnp.dot, preferred_element_type=F32)
    return dot(sel, x1) + dot(sel, x2) + dot(sel, x3)


def _log_sigmoid(x):
    return jnp.minimum(x, 0.0) - jnp.log1p(jnp.exp(-jnp.abs(x)))


def _silu(x):
    return x * jax.nn.sigmoid(x)


def _rmsnorm_rows(x, g):
    return x * lax.rsqrt(jnp.mean(x * x, axis=-1, keepdims=True) + EPS) * g


def _head_rmsnorm(o, g):
    parts = [_rmsnorm_rows(o[:, h * HEAD_W:(h + 1) * HEAD_W], g) for h in range(N_HEADS)]
    return jnp.concatenate(parts, axis=-1)


def _bwd_tile(s, nc, nt):
    return jnp.where(s < nc, nc - 1 - s, nt - 1 - (s - nc))


def _const_spec(shape):
    nd = len(shape)
    return pl.BlockSpec(shape, lambda *_: (0,) * nd)


def _params(sem):
    return pltpu.CompilerParams(dimension_semantics=sem, vmem_limit_bytes=VMEM_LIMIT)


def _mod_kernel(c_ref, w_ref, b_ref, o_ref):
    cv = _silu(c_ref[...])
    o_ref[...] = jnp.dot(cv, w_ref[...], precision=lax.Precision.HIGHEST,
                         preferred_element_type=F32) + b_ref[...]


def _mod_call(cvec, ada_w, ada_b):
    depth, d, six_d = ada_w.shape
    tn = 1024
    return pl.pallas_call(
        _mod_kernel,
        grid=(depth, six_d // tn),
        in_specs=[pl.BlockSpec((SUBLANES, d), lambda l, j: (0, 0)),
                  pl.BlockSpec((None, d, tn), lambda l, j: (l, 0, j)),
                  pl.BlockSpec((None, 1, tn), lambda l, j: (l, 0, j))],
        out_specs=pl.BlockSpec((None, SUBLANES, tn), lambda l, j: (l, 0, j)),
        out_shape=jax.ShapeDtypeStruct((depth, SUBLANES, six_d), F32),
        compiler_params=_params(("parallel", "parallel")),
        name="adaln_mod",
    )(cvec, ada_w, ada_b.reshape(depth, 1, six_d))


def _stage_a_kernel(h_ref, mod_ref, g_ref, w_rg, w_gla, w_hg, w_ml,
                    z_rg, z_gla, z_hg, z_ml, u_ref):
    x = h_ref[...]
    u = _rmsnorm_rows(x, g_ref[...]) * (1.0 + mod_ref[1:2, :]) + mod_ref[0:1, :]
    ub = u.astype(BF16)
    u_ref[...] = ub
    z_rg[...] = jnp.dot(ub, w_rg[...], preferred_element_type=F32)
    z_gla[...] = jnp.dot(ub, w_gla[...], preferred_element_type=F32)
    z_hg[...] = jnp.dot(ub, w_hg[...], preferred_element_type=F32)
    z_ml[...] = jnp.dot(ub, w_ml[...], preferred_element_type=F32)


def _stage_a_call(h, mod, norm_g, w_rg, w_gla, w_hg, w_ml, nc):
    bsz, n, d = h.shape
    nt = n // TM
    tm = TM // 2

    def tok(w):
        return pl.BlockSpec((None, tm, w), lambda b, t: (b, t, 0))

    return pl.pallas_call(
        _stage_a_kernel,
        grid=(bsz, n // tm),
        in_specs=[tok(d),
                  pl.BlockSpec((None, 6, d), lambda b, t: (jnp.where(t * tm < nc * TM, bsz, b), 0, 0)),
                  _const_spec((1, d)),
                  _const_spec((d, W_RG)), _const_spec((d, W_GLA)),
                  _const_spec((d, W_HG)), _const_spec((d, W_ML))],
        out_specs=[tok(W_RG), tok(W_GLA), tok(W_HG), tok(W_ML), tok(d)],
        out_shape=[jax.ShapeDtypeStruct((bsz, n, W_RG), F32),
                   jax.ShapeDtypeStruct((bsz, n, W_GLA), F32),
                   jax.ShapeDtypeStruct((bsz, n, W_HG), F32),
                   jax.ShapeDtypeStruct((bsz, n, W_ML), F32),
                   jax.ShapeDtypeStruct((bsz, n, d), BF16)],
        compiler_params=_params(("parallel", "parallel")),
        name="stage_a",
    )(h, mod, norm_g.reshape(1, d), w_rg, w_gla, w_hg, w_ml)


def _conv_tile(x_ref, prev_ref, next_ref, pad_ref, cw_ref, cb_ref, tile, nc, nt):
    prev_ok = jnp.logical_and(tile != 0, tile != nc)
    next_ok = jnp.logical_and(tile != nc - 1, tile != nt - 1)
    pad_ref[0:SUBLANES, :] = jnp.where(prev_ok, prev_ref[...], 0.0)
    pad_ref[SUBLANES:SUBLANES + TM, :] = x_ref[...]
    pad_ref[SUBLANES + TM:2 * SUBLANES + TM, :] = jnp.where(next_ok, next_ref[...], 0.0)
    acc = cb_ref[...]
    for j in range(CONV_W):
        off = SUBLANES - CONV_LEFT + j
        acc = acc + cw_ref[j:j + 1, :] * pad_ref[off:off + TM, :]
    return acc


def _halo_specs(width, col_block, tile_fn, n):
    rows = TM // SUBLANES
    last = n // SUBLANES - 1
    cur = pl.BlockSpec((None, TM, width), lambda b, s: (b, tile_fn(s), col_block))
    prev = pl.BlockSpec((None, SUBLANES, width),
                        lambda b, s: (b, jnp.maximum(tile_fn(s) * rows - 1, 0), col_block))
    nxt = pl.BlockSpec((None, SUBLANES, width),
                       lambda b, s: (b, jnp.minimum((tile_fn(s) + 1) * rows, last), col_block))
    return [cur, prev, nxt]


def _scan_tile(a, b, reverse):
    n = a.shape[0]
    row = lax.broadcasted_iota(jnp.int32, a.shape, 0)
    k = 1
    while k < n:
        if reverse:
            a_s, b_s, ok = pltpu.roll(a, n - k, 0), pltpu.roll(b, n - k, 0), row < n - k
        else:
            a_s, b_s, ok = pltpu.roll(a, k, 0), pltpu.roll(b, k, 0), row >= k
        b = b + a * jnp.where(ok, b_s, 0.0)
        a = a * jnp.where(ok, a_s, 1.0)
        k *= 2
    return a, b


def _rg_kernel(xf, xf_p, xf_n, xb, xb_p, xb_n, cw, cb, gw, gb, lam,
               hf_ref, hb_ref, pad_ref, carry_ref, *, nc, nt):
    s = pl.program_id(1)

    @pl.when(s == 0)
    def _():
        carry_ref[...] = jnp.zeros_like(carry_ref)

    dirs = ((xf, xf_p, xf_n, hf_ref, s, False),
            (xb, xb_p, xb_n, hb_ref, _bwd_tile(s, nc, nt), True))
    for d, (x_ref, p_ref, n_ref, o_ref, tile, reverse) in enumerate(dirs):
        x = _conv_tile(x_ref, p_ref, n_ref, pad_ref, cw, cb, tile, nc, nt)
        r = jax.nn.sigmoid(_mm(x, gw[d, 0]) + gb[2 * d:2 * d + 1, :])
        i = jax.nn.sigmoid(_mm(x, gw[d, 1]) + gb[2 * d + 1:2 * d + 2, :])
        lam_d = lam[d:d + 1, :]
        softplus = jnp.maximum(-lam_d, 0.0) + jnp.log1p(jnp.exp(-jnp.abs(lam_d)))
        log_a = -RG_C * r * softplus
        a = jnp.exp(log_a)
        t = jnp.tanh(log_a)
        bt = jnp.sqrt(-2.0 * t / (1.0 - t)) * (i * x)
        a_cum, h_loc = _scan_tile(a, bt, reverse)
        h = h_loc + a_cum * carry_ref[d:d + 1, :]
        o_ref[...] = h
        carry_ref[d:d + 1, :] = h[0:1, :] if reverse else h[TM - 1:TM, :]


def _rg_call(z_rg, cw, cb, gw, gb, lam, nc):
    bsz, n, _ = z_rg.shape
    nt = n // TM
    w = BRANCH_W
    fwd = lambda s: s
    bwd = lambda s: _bwd_tile(s, nc, nt)
    out = lambda fn: pl.BlockSpec((None, TM, w), lambda b, s: (b, fn(s), 0))
    return pl.pallas_call(
        functools.partial(_rg_kernel, nc=nc, nt=nt),
        grid=(bsz, nt),
        in_specs=_halo_specs(w, 0, fwd, n) + _halo_specs(w, 0, bwd, n) + [
            _const_spec((CONV_W, w)), _const_spec((1, w)),
            _const_spec((2, 2, w, w)), _const_spec((4, w)), _const_spec((2, w))],
        out_specs=[out(fwd), out(bwd)],
        out_shape=[jax.ShapeDtypeStruct((bsz, n, w), F32)] * 2,
        scratch_shapes=[pltpu.VMEM((TM + 2 * SUBLANES, w), F32), pltpu.VMEM((2, w), F32)],
        compiler_params=_params(("parallel", "arbitrary")),
        name="rglru",
    )(z_rg, z_rg, z_rg, z_rg, z_rg, z_rg, cw, cb, gw, gb, lam)


def _decay_tables():
    n = CHUNK
    sel = np.zeros((N_LEVELS + 2, n, n), np.float32)
    masks = np.zeros((N_LEVELS + 1, n, n), np.float32)
    for lvl in range(N_LEVELS):
        half = n >> (lvl + 1)
        for r in range(n):
            start = (r // (2 * half)) * 2 * half
            ref = start + half - 1
            if r - start >= half:
                sel[lvl, r, ref + 1:r + 1] = 1.0
                masks[lvl, r, start:start + half] = 1.0
            else:
                sel[lvl, r, r + 1:ref + 1] = 1.0
    masks[N_LEVELS] = np.eye(n, dtype=np.float32)
    sel[N_LEVELS] = np.tril(np.ones((n, n), np.float32))
    sel[N_LEVELS + 1] = 1.0 - sel[N_LEVELS]
    sel_f, masks_f = sel.reshape(-1, n), masks
    sel_b = sel[:, ::-1, ::-1].reshape(-1, n)
    masks_b = masks[:, ::-1, ::-1]
    return (jnp.asarray(np.stack([sel_f, sel_b]), BF16),
            jnp.asarray(np.stack([masks_f, masks_b]), F32))


def _gla_chunk(q, k, v, g, sel, masks, st_ref, d, reverse):
    x = jnp.exp(_mm_sel(sel, g))
    outs = []
    for h in range(N_HEADS):
        cols = slice(h * HEAD_W, (h + 1) * HEAD_W)
        qh, kh, vh = q[:, cols], k[:, cols], v[:, cols]
        sc = masks[N_LEVELS] * _mm_nt(qh, kh)
        for lvl in range(N_LEVELS):
            xl = x[lvl * CHUNK:(lvl + 1) * CHUNK, cols]
            sc = sc + masks[lvl] * _mm_nt(qh * xl, kh * xl)
        x_cum = x[N_LEVELS * CHUNK:(N_LEVELS + 1) * CHUNK, cols]
        x_rest = x[(N_LEVELS + 1) * CHUNK:(N_LEVELS + 2) * CHUNK, cols]
        st = st_ref[d, h]
        outs.append(_mm(sc, vh) + _mm_nt(qh * x_cum, st))
        x_end = x_cum[0:1, :] if reverse else x_cum[CHUNK - 1:CHUNK, :]
        st_ref[d, h] = st * x_end + _mm_tn(vh, kh * x_rest)
    return jnp.concatenate(outs, axis=-1)


def _gla_tile(q, k, v, g, sel_ref, mask_ref, st_ref, o_ref, d, reverse):
    sel = sel_ref[d]
    masks = mask_ref[d]
    order = range(TM // CHUNK - 1, -1, -1) if reverse else range(TM // CHUNK)
    for c in order:
        rows = slice(c * CHUNK, (c + 1) * CHUNK)
        o_ref[rows, :] = _gla_chunk(q[rows], k[rows], v[rows], g[rows], sel, masks, st_ref, d, reverse)


def _gla_kernel(qf, kf, vf, lf, qb, kb, vb, lb, wlr, blr, sel_ref, mask_ref,
                of_ref, ob_ref, st_ref):
    @pl.when(pl.program_id(1) == 0)
    def _():
        st_ref[...] = jnp.zeros_like(st_ref)

    dirs = ((qf, kf, vf, lf, of_ref, False), (qb, kb, vb, lb, ob_ref, True))
    for d, (q_ref, k_ref, v_ref, l_ref, o_ref, reverse) in enumerate(dirs):
        pre = _mm(l_ref[...], wlr[d]) + blr[d:d + 1, :]
        g = _log_sigmoid(pre) * (1.0 / GLA_GATE_NORM)
        q = q_ref[...] * (GLA_DK ** -0.5)
        _gla_tile(q, k_ref[...], v_ref[...], g, sel_ref, mask_ref, st_ref, o_ref, d, reverse)


def _hg_kernel(qf, ff, vf, qb, fb, vb, lbp, sel_ref, mask_ref, of_ref, ob_ref, st_ref):
    @pl.when(pl.program_id(1) == 0)
    def _():
        st_ref[...] = jnp.zeros_like(st_ref)

    dirs = ((qf, ff, vf, of_ref, False), (qb, fb, vb, ob_ref, True))
    for d, (q_ref, f_ref, v_ref, o_ref, reverse) in enumerate(dirs):
        f = f_ref[...]
        lb = lbp[d:d + 1, :]
        log_lb = lbp[2 + d:3 + d, :]
        log_1m = lbp[4 + d:5 + d, :]
        c = log_1m + _log_sigmoid(f)
        g = jnp.maximum(log_lb, c) + jnp.log1p(jnp.exp(-jnp.abs(log_lb - c)))
        k = (1.0 - lb) * jax.nn.sigmoid(-f)
        q = _silu(q_ref[...]) * (HEAD_W ** -0.5)
        _gla_tile(q, k, v_ref[...], g, sel_ref, mask_ref, st_ref, o_ref, d, reverse)


def _mixer_specs(widths_cols, tile_fn):
    return [pl.BlockSpec((None, TM, w), lambda b, s, c=c: (b, tile_fn(s), c)) for w, c in widths_cols]


def _gla_call(z_gla, wlr, blr, sel, masks, nc):
    bsz, n, _ = z_gla.shape
    nt = n // TM
    w = BRANCH_W
    fwd = lambda s: s
    bwd = lambda s: _bwd_tile(s, nc, nt)
    cols = [(w, 0), (w, 1), (w, 2), (HEAD_W, 4 * w // HEAD_W)]
    out = lambda fn: pl.BlockSpec((None, TM, w), lambda b, s: (b, fn(s), 0))
    return pl.pallas_call(
        _gla_kernel,
        grid=(bsz, nt),
        in_specs=_mixer_specs(cols, fwd) + _mixer_specs(cols, bwd) + [
            _const_spec(wlr.shape), _const_spec(blr.shape),
            _const_spec(sel.shape), _const_spec(masks.shape)],
        out_specs=[out(fwd), out(bwd)],
        out_shape=[jax.ShapeDtypeStruct((bsz, n, w), F32)] * 2,
        scratch_shapes=[pltpu.VMEM((2, N_HEADS, HEAD_W, HEAD_W), F32)],
        compiler_params=_params(("parallel", "arbitrary")),
        name="gla",
    )(*([z_gla] * 8), wlr, blr, sel, masks)


def _hg_call(z_hg, lbp, sel, masks, nc):
    bsz, n, _ = z_hg.shape
    nt = n // TM
    w = BRANCH_W
    fwd = lambda s: s
    bwd = lambda s: _bwd_tile(s, nc, nt)
    out = lambda fn: pl.BlockSpec((None, TM, w), lambda b, s: (b, fn(s), 0))
    return pl.pallas_call(
        _hg_kernel,
        grid=(bsz, nt),
        in_specs=_mixer_specs([(w, 0), (w, 2), (w, 1)], fwd) + _mixer_specs([(w, 0), (w, 3), (w, 1)], bwd) + [
            _const_spec(lbp.shape), _const_spec(sel.shape), _const_spec(masks.shape)],
        out_specs=[out(fwd), out(bwd)],
        out_shape=[jax.ShapeDtypeStruct((bsz, n, w), F32)] * 2,
        scratch_shapes=[pltpu.VMEM((2, N_HEADS, HEAD_W, HEAD_W), F32)],
        compiler_params=_params(("parallel", "arbitrary")),
        name="hgrn2",
    )(*([z_hg] * 6), lbp, sel, masks)


def _ml_chunk(q, k, v, gates, log_f, tri_sel, tri_mask, ct_ref, n_ref, m_ref, d, reverse):
    b_all = _mm_sel(tri_sel, log_f)
    b_t = b_all.T
    g_t = gates.T
    last = 0 if reverse else CHUNK - 1
    outs = []
    for h in range(N_HEADS):
        cols = slice(h * HEAD_W, (h + 1) * HEAD_W)
        qh, kh, vh = q[:, cols], k[:, cols], v[:, cols]
        ci, cf = d * 8 + h, d * 8 + 4 + h
        st = d * N_HEADS + h
        b_col = b_all[:, cf:cf + 1]
        i_col = gates[:, ci:ci + 1]
        b_row = b_t[cf:cf + 1, :]
        i_row = g_t[ci:ci + 1, :]
        m_prev = m_ref[st:st + 1, 0:1]
        log_w = jnp.where(tri_mask > 0.0, b_col - b_row + i_row, -jnp.inf)
        log_inter = b_col + m_prev
        m = jnp.maximum(log_inter, jnp.max(log_w, axis=-1, keepdims=True))
        w_inter = jnp.exp(log_inter - m)
        s = _mm_nt(qh, kh) * jnp.exp(log_w - m)
        ct = ct_ref[st]
        n_vec = n_ref[st:st + 1, :]
        num = _mm(s, vh) + w_inter * _mm_nt(qh, ct)
        den = jnp.sum(s, axis=-1, keepdims=True) + w_inter * jnp.sum(qh * n_vec, axis=-1, keepdims=True)
        outs.append(num / jnp.maximum(jnp.abs(den), jnp.exp(-m)))
        m_new = m[last:last + 1, :]
        b_end = b_col[last:last + 1, :]
        w_end = jnp.exp(b_end - b_col + i_col - m_new)
        decay = jnp.exp(b_end + m_prev - m_new)
        kw = kh * w_end
        ct_ref[st] = decay * ct + _mm_tn(vh, kw)
        n_ref[st:st + 1, :] = decay * n_vec + jnp.sum(kw, axis=0, keepdims=True)
        m_ref[st:st + 1, :] = jnp.broadcast_to(m_new, (1, HEAD_W))
    return jnp.concatenate(outs, axis=-1)


def _ml_kernel(xf, xf_p, xf_n, vf, gf, xb, xb_p, xb_n, vb, gb_, cw, cb, gbias, tri_sel, tri_mask,
               of_ref, ob_ref, pad_ref, ct_ref, n_ref, m_ref, *, nc, nt):
    s = pl.program_id(1)

    @pl.when(s == 0)
    def _():
        ct_ref[...] = jnp.zeros_like(ct_ref)
        n_ref[...] = jnp.zeros_like(n_ref)
        m_ref[...] = jnp.zeros_like(m_ref)

    dirs = ((xf, xf_p, xf_n, vf, gf, of_ref, s, False),
            (xb, xb_p, xb_n, vb, gb_, ob_ref, _bwd_tile(s, nc, nt), True))
    for d, (x_ref, p_ref, nx_ref, v_ref, g_ref, o_ref, tile, reverse) in enumerate(dirs):
        qk = _silu(_conv_tile(x_ref, p_ref, nx_ref, pad_ref, cw, cb, tile, nc, nt))
        q = qk[:, :BRANCH_W] * (HEAD_W ** -0.5)
        k = qk[:, BRANCH_W:]
        v = v_ref[...]
        gates = g_ref[...] + gbias[...]
        log_f = _log_sigmoid(gates)
        order = range(TM // CHUNK - 1, -1, -1) if reverse else range(TM // CHUNK)
        for c in order:
            rows = slice(c * CHUNK, (c + 1) * CHUNK)
            o_ref[rows, :] = _ml_chunk(q[rows], k[rows], v[rows], gates[rows], log_f[rows],
                                       tri_sel[d], tri_mask[d], ct_ref, n_ref, m_ref, d, reverse)


def _ml_call(z_ml, cw, cb, gbias, tri_sel, tri_mask, nc):
    bsz, n, _ = z_ml.shape
    nt = n // TM
    w = BRANCH_W
    fwd = lambda s: s
    bwd = lambda s: _bwd_tile(s, nc, nt)
    out = lambda fn: pl.BlockSpec((None, TM, w), lambda b, s: (b, fn(s), 0))

    def side(fn):
        return _halo_specs(2 * w, 0, fn, n) + _mixer_specs([(w, 2), (HEAD_W, 4 * w // HEAD_W)], fn)

    return pl.pallas_call(
        functools.partial(_ml_kernel, nc=nc, nt=nt),
        grid=(bsz, nt),
        in_specs=side(fwd) + side(bwd) + [
            _const_spec((CONV_W, 2 * w)), _const_spec((1, 2 * w)), _const_spec((1, HEAD_W)),
            _const_spec(tri_sel.shape), _const_spec(tri_mask.shape)],
        out_specs=[out(fwd), out(bwd)],
        out_shape=[jax.ShapeDtypeStruct((bsz, n, w), F32)] * 2,
        scratch_shapes=[pltpu.VMEM((TM + 2 * SUBLANES, 2 * w), F32),
                        pltpu.VMEM((2 * N_HEADS, HEAD_W, HEAD_W), F32),
                        pltpu.VMEM((2 * N_HEADS, HEAD_W), F32),
                        pltpu.VMEM((2 * N_HEADS, HEAD_W), F32)],
        compiler_params=_params(("parallel", "arbitrary")),
        name="mlstm",
    )(*([z_ml] * 10), cw, cb, gbias, tri_sel, tri_mask)


def _gelu_tanh(x):
    return 0.5 * x * (1.0 + jnp.tanh(0.7978845608028654 * (x + 0.044715 * (x * x * x))))


def _route(logits):
    col = lax.broadcasted_iota(jnp.int32, logits.shape, 1)
    colf = col.astype(F32)
    is_g = jnp.logical_and(col >= N_EXPERTS, col < N_EXPERTS + N_GROUPS)
    is_e = col < N_EXPERTS
    neg = -jnp.inf
    big = 1e9
    gl = jnp.where(is_g, logits, neg)
    g_max = jnp.max(gl, axis=-1, keepdims=True)
    grp = jnp.min(jnp.where(gl == g_max, colf, big), axis=-1, keepdims=True) - N_EXPERTS
    p_grp = 1.0 / jnp.sum(jnp.exp(gl - g_max), axis=-1, keepdims=True)
    col_grp = lax.shift_right_logical(col, EXPERTS_PER_GROUP.bit_length() - 1)
    in_grp = jnp.logical_and(is_e, col_grp.astype(F32) == grp)
    e1 = jnp.where(in_grp, logits, neg)
    top1 = jnp.max(e1, axis=-1, keepdims=True)
    idx1 = jnp.min(jnp.where(e1 == top1, colf, big), axis=-1, keepdims=True)
    e2 = jnp.where(colf == idx1, neg, e1)
    top2 = jnp.max(e2, axis=-1, keepdims=True)
    idx2 = jnp.min(jnp.where(e2 == top2, colf, big), axis=-1, keepdims=True)
    t = jnp.exp(top2 - top1)
    w1 = 1.0 / (1.0 + t)
    w2 = t / (1.0 + t)
    return p_grp * jnp.where(colf == idx1, w1, jnp.where(colf == idx2, w2, 0.0))


def _stage_c_kernel(u_ref, h_ref, mod_ref, rg_y, rg_f, rg_b, gla_g, gla_f, gla_b, hg_g, hg_f, hg_b,
                    ml_o, ml_f, ml_b, gains, w_merge, b_merge, w_branch, w_out, ffn_g, w_route, b_route,
                    h_out, u2_out, wt_out):
    ys = (
        _gelu_tanh(rg_y[...]) * (rg_f[...] + rg_b[...]),
        _head_rmsnorm(gla_f[...] + gla_b[...], gains[0:1, :]) * _silu(gla_g[...]),
        _head_rmsnorm(hg_f[...] + hg_b[...], gains[1:2, :]) * _silu(hg_g[...]),
        jax.nn.sigmoid(ml_o[...]) * _head_rmsnorm(ml_f[...] + ml_b[...], gains[2:3, :]),
    )
    u = u_ref[...]
    d = u.shape[-1]
    merged = None
    for kk, y in enumerate(ys):
        gate = jax.nn.sigmoid(jnp.dot(u, w_merge[:, kk * d:(kk + 1) * d], preferred_element_type=F32)
                              + b_merge[:, kk * d:(kk + 1) * d])
        term = gate * _mm(y, w_branch[kk])
        merged = term if merged is None else merged + term
    mix = _mm(merged, w_out[...])
    h_new = h_ref[...] + mod_ref[2:3, :] * mix
    h_out[...] = h_new
    u2 = _rmsnorm_rows(h_new, ffn_g[...]) * (1.0 + mod_ref[4:5, :]) + mod_ref[3:4, :]
    u2_out[...] = u2.astype(BF16)
    logits = jnp.dot(u2, w_route[...], precision=lax.Precision.HIGHEST,
                     preferred_element_type=F32) + b_route[...]
    wt_out[...] = _route(logits)


def _stage_c_call(u, h, mod, z_rg, rg_f, rg_b, z_gla, gla_f, gla_b, z_hg, hg_f, hg_b, z_ml, ml_f, ml_b,
                  gains, w_merge, b_merge, w_branch, w_out, ffn_g, w_route, b_route, nc, t0):
    bsz, n, d = h.shape
    nt = n // TM
    w = BRANCH_W

    def tok(width, col=0):
        return pl.BlockSpec((None, TM, width), lambda b, t: (b, t + t0, col))

    mod_spec = pl.BlockSpec((None, 6, d), lambda b, t: (jnp.where(t + t0 < nc, bsz, b), 0, 0))
    return pl.pallas_call(
        _stage_c_kernel,
        grid=(bsz, nt - t0),
        in_specs=[tok(d), tok(d), mod_spec,
                  tok(w, 1), tok(w), tok(w),
                  tok(w, 3), tok(w), tok(w),
                  tok(w, 4), tok(w), tok(w),
                  tok(w, 3), tok(w), tok(w),
                  _const_spec(gains.shape), _const_spec(w_merge.shape), _const_spec(b_merge.shape),
                  _const_spec(w_branch.shape), _const_spec(w_out.shape), _const_spec(ffn_g.shape),
                  _const_spec(w_route.shape), _const_spec(b_route.shape)],
        out_specs=[tok(d), tok(d), tok(HEAD_W)],
        out_shape=[jax.ShapeDtypeStruct((bsz, n, d), F32),
                   jax.ShapeDtypeStruct((bsz, n, d), BF16),
                   jax.ShapeDtypeStruct((bsz, n, HEAD_W), F32)],
        compiler_params=_params(("parallel", "parallel")),
        name="stage_c",
    )(u, h, mod, z_rg, rg_f, rg_b, z_gla, gla_f, gla_b, z_hg, hg_f, hg_b, z_ml, ml_f, ml_b,
      gains, w_merge, b_merge, w_branch, w_out, ffn_g, w_route, b_route)


def _moe_kernel(u2_ref, wt_ref, h_ref, mod_ref, fin_g, w1, w3, w2, o_ref, acc_ref, *, final):
    e = pl.program_id(2)

    @pl.when(e == 0)
    def _():
        acc_ref[...] = jnp.zeros_like(acc_ref)

    x = u2_ref[...]
    h1 = jnp.dot(x, w1[...], preferred_element_type=F32)
    h3 = jnp.dot(x, w3[...], preferred_element_type=F32)
    y = _mm(_silu(h1) * h3, w2[...])
    wt = wt_ref[...]
    col = lax.broadcasted_iota(jnp.int32, wt.shape, 1)
    w_e = jnp.sum(jnp.where(col == e, wt, 0.0), axis=-1, keepdims=True)
    acc_ref[...] += w_e * y

    @pl.when(e == N_EXPERTS - 1)
    def _():
        out = h_ref[...] + mod_ref[5:6, :] * acc_ref[...]
        if final:
            out = _rmsnorm_rows(out, fin_g[...])
        o_ref[...] = out


def _moe_call(u2, wt, h, mod, fin_g, w1, w3, w2, nc, t0, final):
    bsz, n, d = h.shape
    nt = n // TM
    de = w1.shape[-1]

    def tok(width):
        return pl.BlockSpec((None, TM, width), lambda b, t, e: (b, t + t0, 0))

    mod_spec = pl.BlockSpec((None, 6, d), lambda b, t, e: (jnp.where(t + t0 < nc, bsz, b), 0, 0))
    n_out = n - t0 * TM if final else n
    o_off = 0 if final else t0
    return pl.pallas_call(
        functools.partial(_moe_kernel, final=final),
        grid=(bsz, nt - t0, N_EXPERTS),
        in_specs=[tok(d), tok(HEAD_W), tok(d), mod_spec, _const_spec((1, d)),
                  pl.BlockSpec((None, d, de), lambda b, t, e: (e, 0, 0)),
                  pl.BlockSpec((None, d, de), lambda b, t, e: (e, 0, 0)),
                  pl.BlockSpec((None, de, d), lambda b, t, e: (e, 0, 0))],
        out_specs=pl.BlockSpec((None, TM, d), lambda b, t, e: (b, t + o_off, 0)),
        out_shape=jax.ShapeDtypeStruct((bsz, n_out, d), F32),
        scratch_shapes=[pltpu.VMEM((TM, d), F32)],
        compiler_params=_params(("parallel", "parallel", "arbitrary")),
        name="moe",
    )(u2, wt, h, mod, fin_g, w1, w3, w2)


def _pad_heads(w, dk):
    lead = w.shape[:-1]
    w = w.reshape(lead + (N_HEADS, dk))
    w = jnp.pad(w, [(0, 0)] * len(lead) + [(0, 0), (0, HEAD_W - dk)])
    return w.reshape(lead + (N_HEADS * HEAD_W,))


def _pad_cols(w, width):
    return jnp.pad(w, [(0, 0)] * (w.ndim - 1) + [(0, width - w.shape[-1])])


def _split_w_in(w_in):
    bw = BRANCH_W
    sizes = [bw, bw, N_HEADS * GLA_DK, N_HEADS * GLA_DK, bw, bw, 2 * GLA_RANK,
             bw, bw, 2 * bw, bw, bw, bw, bw, bw, 16]
    offs = np.cumsum([0] + sizes)
    p = [w_in[:, offs[i]:offs[i + 1]] for i in range(len(sizes))]
    w_rg = jnp.concatenate([p[0], p[1]], axis=1)
    w_gla = jnp.concatenate([_pad_heads(p[2], GLA_DK), _pad_heads(p[3], GLA_DK), p[4], p[5],
                             _pad_cols(p[6], HEAD_W)], axis=1)
    w_hg = jnp.concatenate([p[7], p[8], p[9], p[10]], axis=1)
    w_ml = jnp.concatenate([p[11], p[12], p[13], p[14], _pad_cols(p[15], HEAD_W)], axis=1)
    return [w.astype(BF16) for w in (w_rg, w_gla, w_hg, w_ml)]


def _block_diag(w):
    k, n = w.shape[-3], w.shape[-1]
    eye = jnp.eye(k, dtype=w.dtype)
    full = jnp.einsum('...kij,kl->...kilj', w, eye)
    return full.reshape(w.shape[:-3] + (k * n, k * n))


def kernel(x, c, ctx, c_ctx, ada_w, ada_b, norm_mix_g, norm_ffn_g, w_in, rg_conv_w, rg_conv_b, rg_gate_w, rg_gate_b, rg_lambda, gla_w_lr, gla_b_lr, gla_norm_g, hgrn_lb_logits, hgrn_norm_g, ml_conv_w, ml_conv_b, ml_gate_b, ml_norm_g, w_branch, w_merge, b_merge, w_out, moe_w_group, moe_b_group, moe_w_expert, moe_b_expert, moe_w1, moe_w3, moe_w2, final_norm_g):
    bsz, seq, d = x.shape
    n_ctx = ctx.shape[1]
    depth = ada_w.shape[0]
    assert n_ctx % TM == 0 and seq % TM == 0 and d == 2 * BRANCH_W
    nc = n_ctx // TM

    h = jnp.concatenate([ctx, x], axis=1)
    cvec = jnp.zeros((SUBLANES, d), F32).at[:bsz].set(c).at[bsz].set(c_ctx)
    mod_all = _mod_call(cvec, ada_w, ada_b).reshape(depth, SUBLANES, 6, d)[:, :bsz + 1]

    lb_cum = jnp.cumsum(jax.nn.softmax(hgrn_lb_logits.astype(F32), axis=0), axis=0)
    hgrn_lb = lb_cum - lb_cum[:1]
    sel, masks = _decay_tables()
    tri_sel = sel[:, N_LEVELS * CHUNK:(N_LEVELS + 1) * CHUNK, :]
    tri_mask = masks[:, :N_LEVELS].sum(axis=1) + masks[:, N_LEVELS]

    out = None
    for l in range(depth):
        last = l == depth - 1
        mod = mod_all[l]
        w_rg, w_gla, w_hg, w_ml = _split_w_in(w_in[l])
        z_rg, z_gla, z_hg, z_ml, u = _stage_a_call(h, mod, norm_mix_g[l], w_rg, w_gla, w_hg, w_ml, nc)

        rg_f, rg_b = _rg_call(z_rg, rg_conv_w[l], rg_conv_b[l].reshape(1, -1),
                              _block_diag(rg_gate_w[l]).astype(BF16),
                              rg_gate_b[l].reshape(4, -1), rg_lambda[l], nc)

        wlr = jnp.zeros((2, HEAD_W, N_HEADS * HEAD_W), F32)
        wlr_p = _pad_heads(gla_w_lr[l], GLA_DK)
        wlr = wlr.at[0, :GLA_RANK].set(wlr_p[0]).at[1, GLA_RANK:2 * GLA_RANK].set(wlr_p[1])
        gla_f, gla_b = _gla_call(z_gla, wlr.astype(BF16), _pad_heads(gla_b_lr[l], GLA_DK), sel, masks, nc)

        lb = hgrn_lb[l]
        lbp = jnp.concatenate([lb, jnp.log(lb), jnp.log1p(-lb), jnp.zeros((2, lb.shape[-1]), F32)], axis=0)
        hg_f, hg_b = _hg_call(z_hg, lbp, sel, masks, nc)

        gbias = _pad_cols(ml_gate_b[l].reshape(1, -1), HEAD_W)
        ml_f, ml_b = _ml_call(z_ml, ml_conv_w[l], ml_conv_b[l].reshape(1, -1), gbias, tri_sel, tri_mask, nc)

        gains = jnp.zeros((SUBLANES, HEAD_W), F32).at[0].set(gla_norm_g[l]).at[1].set(hgrn_norm_g[l]).at[2].set(ml_norm_g[l])
        w_route = _pad_cols(jnp.concatenate([moe_w_expert[l], moe_w_group[l]], axis=1), HEAD_W)
        b_route = _pad_cols(jnp.concatenate([moe_b_expert[l], moe_b_group[l]]).reshape(1, -1), HEAD_W)
        t0 = nc if last else 0
        h_mid, u2, wt = _stage_c_call(
            u, h, mod, z_rg, rg_f, rg_b, z_gla, gla_f, gla_b, z_hg, hg_f, hg_b, z_ml, ml_f, ml_b,
            gains, w_merge[l].astype(BF16), b_merge[l].reshape(1, -1), w_branch[l].astype(BF16),
            w_out[l].astype(BF16), norm_ffn_g[l].reshape(1, -1), w_route, b_route, nc, t0)

        res = _moe_call(u2, wt, h_mid, mod, final_norm_g.reshape(1, -1), moe_w1[l].astype(BF16),
                        moe_w3[l].astype(BF16), moe_w2[l].astype(BF16), nc, t0, last)
        if last:
            out = res
        else:
            h = res
    return out
```

```python
import functools

import numpy as np
import jax
import jax.numpy as jnp
from jax import lax
from jax.experimental import pallas as pl
from jax.experimental.pallas import tpu as pltpu

F32 = jnp.float32
BF16 = jnp.bfloat16

EPS = 1e-6
TM = 256
CHUNK = 64
SUBLANES = 8
N_HEADS = 4
HEAD_W = 128
BRANCH_W = 512
CONV_W = 4
CONV_LEFT = 2
RG_C = 8.0
GLA_DK = 64
GLA_RANK = 16
GLA_GATE_NORM = 16.0
N_GROUPS = 4
EXPERTS_PER_GROUP = 4
N_EXPERTS = 16
N_LEVELS = 6
MILD_DECAY = -40.0
VMEM_LIMIT = 56 * 1024 * 1024

W_RG = 2 * BRANCH_W
W_GLA = 4 * BRANCH_W + HEAD_W
W_HG = 5 * BRANCH_W
W_ML = 4 * BRANCH_W + HEAD_W


def _mm(a, b):
    return jnp.dot(a.astype(BF16), b.astype(BF16), preferred_element_type=F32)


def _mm_nt(a, b):
    return lax.dot_general(a.astype(BF16), b.astype(BF16), (((1,), (1,)), ((), ())),
                           preferred_element_type=F32)


def _mm_tn(a, b):
    return lax.dot_general(a.astype(BF16), b.astype(BF16), (((0,), (0,)), ((), ())),
                           preferred_element_type=F32)


def _mm_sel(sel, x):
    x1 = x.astype(BF16)
    r1 = x - x1.astype(F32)
    x2 = r1.astype(BF16)
    x3 = (r1 - x2.astype(F32)).astype(BF16)
    dot = functools.partial(jnp.dot, preferred_element_type=F32)
    return dot(sel, x1) + dot(sel, x2) + dot(sel, x3)


def _log_sigmoid(x):
    return jnp.minimum(x, 0.0) - jnp.log1p(jnp.exp(-jnp.abs(x)))


def _silu(x):
    return x * jax.nn.sigmoid(x)


def _rmsnorm_rows(x, g):
    return x * lax.rsqrt(jnp.mean(x * x, axis=-1, keepdims=True) + EPS) * g


def _head_rmsnorm(o, g):
    parts = [_rmsnorm_rows(o[:, h * HEAD_W:(h + 1) * HEAD_W], g) for h in range(N_HEADS)]
    return jnp.concatenate(parts, axis=-1)


def _bwd_tile(s, nc, nt):
    return jnp.where(s < nc, nc - 1 - s, nt - 1 - (s - nc))


def _const_spec(shape):
    nd = len(shape)
    return pl.BlockSpec(shape, lambda *_: (0,) * nd)


def _params(sem):
    return pltpu.CompilerParams(dimension_semantics=sem, vmem_limit_bytes=VMEM_LIMIT)


def _mod_kernel(c_ref, w_ref, b_ref, o_ref):
    cv = _silu(c_ref[...])
    o_ref[...] = jnp.dot(cv, w_ref[...], precision=lax.Precision.HIGHEST,
                         preferred_element_type=F32) + b_ref[...]


def _mod_call(cvec, ada_w, ada_b):
    depth, d, six_d = ada_w.shape
    tn = 1024
    return pl.pallas_call(
        _mod_kernel,
        grid=(depth, six_d // tn),
        in_specs=[pl.BlockSpec((SUBLANES, d), lambda l, j: (0, 0)),
                  pl.BlockSpec((None, d, tn), lambda l, j: (l, 0, j)),
                  pl.BlockSpec((None, 1, tn), lambda l, j: (l, 0, j))],
        out_specs=pl.BlockSpec((None, SUBLANES, tn), lambda l, j: (l, 0, j)),
        out_shape=jax.ShapeDtypeStruct((depth, SUBLANES, six_d), F32),
        compiler_params=_params(("parallel", "parallel")),
        name="adaln_mod",
    )(cvec, ada_w, ada_b.reshape(depth, 1, six_d))


def _stage_a_kernel(h_ref, mod_ref, g_ref, w_rg, w_gla, w_hg, w_ml,
                    z_rg, z_gla, z_hg, z_ml, u_ref):
    x = h_ref[...]
    u = _rmsnorm_rows(x, g_ref[...]) * (1.0 + mod_ref[1:2, :]) + mod_ref[0:1, :]
    ub = u.astype(BF16)
    u_ref[...] = ub
    z_rg[...] = jnp.dot(ub, w_rg[...], preferred_element_type=F32)
    z_gla[...] = jnp.dot(ub, w_gla[...], preferred_element_type=F32)
    z_hg[...] = jnp.dot(ub, w_hg[...], preferred_element_type=F32)
    z_ml[...] = jnp.dot(ub, w_ml[...], preferred_element_type=F32)


def _stage_a_call(h, mod, norm_g, w_rg, w_gla, w_hg, w_ml, nc):
    bsz, n, d = h.shape
    nt = n // TM
    tm = TM // 2

    def tok(w):
        return pl.BlockSpec((None, tm, w), lambda b, t: (b, t, 0))

    return pl.pallas_call(
        _stage_a_kernel,
        grid=(bsz, n // tm),
        in_specs=[tok(d),
                  pl.BlockSpec((None, 6, d), lambda b, t: (jnp.where(t * tm < nc * TM, bsz, b), 0, 0)),
                  _const_spec((1, d)),
                  _const_spec((d, W_RG)), _const_spec((d, W_GLA)),
                  _const_spec((d, W_HG)), _const_spec((d, W_ML))],
        out_specs=[tok(W_RG), tok(W_GLA), tok(W_HG), tok(W_ML), tok(d)],
        out_shape=[jax.ShapeDtypeStruct((bsz, n, W_RG), F32),
                   jax.ShapeDtypeStruct((bsz, n, W_GLA), F32),
                   jax.ShapeDtypeStruct((bsz, n, W_HG), F32),
                   jax.ShapeDtypeStruct((bsz, n, W_ML), F32),
                   jax.ShapeDtypeStruct((bsz, n, d), BF16)],
        compiler_params=_params(("parallel", "parallel")),
        name="stage_a",
    )(h, mod, norm_g.reshape(1, d), w_rg, w_gla, w_hg, w_ml)


def _conv_tile(x_ref, prev_ref, next_ref, pad_ref, cw_ref, cb_ref, tile, nc, nt):
    prev_ok = jnp.logical_and(tile != 0, tile != nc)
    next_ok = jnp.logical_and(tile != nc - 1, tile != nt - 1)
    pad_ref[0:SUBLANES, :] = jnp.where(prev_ok, prev_ref[...], 0.0)
    pad_ref[SUBLANES:SUBLANES + TM, :] = x_ref[...]
    pad_ref[SUBLANES + TM:2 * SUBLANES + TM, :] = jnp.where(next_ok, next_ref[...], 0.0)
    acc = cb_ref[...]
    for j in range(CONV_W):
        off = SUBLANES - CONV_LEFT + j
        acc = acc + cw_ref[j:j + 1, :] * pad_ref[off:off + TM, :]
    return acc


def _halo_specs(width, col_block, tile_fn, n):
    rows = TM // SUBLANES
    last = n // SUBLANES - 1
    cur = pl.BlockSpec((None, TM, width), lambda b, s: (b, tile_fn(s), col_block))
    prev = pl.BlockSpec((None, SUBLANES, width),
                        lambda b, s: (b, jnp.maximum(tile_fn(s) * rows - 1, 0), col_block))
    nxt = pl.BlockSpec((None, SUBLANES, width),
                       lambda b, s: (b, jnp.minimum((tile_fn(s) + 1) * rows, last), col_block))
    return [cur, prev, nxt]


def _scan_rows8(a, b, reverse):
    n = a.shape[0]
    pos = jnp.bitwise_and(lax.broadcasted_iota(jnp.int32, a.shape, 0), SUBLANES - 1)
    k = 1
    while k < SUBLANES:
        if reverse:
            a_s, b_s, ok = pltpu.roll(a, n - k, 0), pltpu.roll(b, n - k, 0), pos < SUBLANES - k
        else:
            a_s, b_s, ok = pltpu.roll(a, k, 0), pltpu.roll(b, k, 0), pos >= k
        b = b + a * jnp.where(ok, b_s, 0.0)
        a = a * jnp.where(ok, a_s, 1.0)
        k *= 2
    return a, b


def _scan_tile(a, b, carry, o_ref, reverse):
    a8, b8 = _scan_rows8(a, b, reverse)
    groups = a.shape[0] // SUBLANES
    order = range(groups - 1, -1, -1) if reverse else range(groups)
    for r in order:
        rows = slice(r * SUBLANES, (r + 1) * SUBLANES)
        h = b8[rows] + a8[rows] * carry
        o_ref[rows, :] = h
        carry = h[0:1, :] if reverse else h[SUBLANES - 1:SUBLANES, :]
    return carry


def _rg_kernel(xf, xf_p, xf_n, xb, xb_p, xb_n, cw, cb, gw, gb, lam,
               hf_ref, hb_ref, pad_ref, carry_ref, *, nc, nt):
    s = pl.program_id(1)

    @pl.when(s == 0)
    def _():
        carry_ref[...] = jnp.zeros_like(carry_ref)

    dirs = ((xf, xf_p, xf_n, hf_ref, s, False),
            (xb, xb_p, xb_n, hb_ref, _bwd_tile(s, nc, nt), True))
    for d, (x_ref, p_ref, n_ref, o_ref, tile, reverse) in enumerate(dirs):
        x = _conv_tile(x_ref, p_ref, n_ref, pad_ref, cw, cb, tile, nc, nt)
        r = jax.nn.sigmoid(_mm(x, gw[d, 0]) + gb[2 * d:2 * d + 1, :])
        i = jax.nn.sigmoid(_mm(x, gw[d, 1]) + gb[2 * d + 1:2 * d + 2, :])
        lam_d = lam[d:d + 1, :]
        softplus = jnp.maximum(-lam_d, 0.0) + jnp.log1p(jnp.exp(-jnp.abs(lam_d)))
        log_a = -RG_C * r * softplus
        a = jnp.exp(log_a)
        t = jnp.tanh(log_a)
        bt = jnp.sqrt(-2.0 * t / (1.0 - t)) * (i * x)
        carry_ref[d:d + 1, :] = _scan_tile(a, bt, carry_ref[d:d + 1, :], o_ref, reverse)


def _rg_call(z_rg, cw, cb, gw, gb, lam, nc):
    bsz, n, _ = z_rg.shape
    nt = n // TM
    w = BRANCH_W
    fwd = lambda s: s
    bwd = lambda s: _bwd_tile(s, nc, nt)
    out = lambda fn: pl.BlockSpec((None, TM, w), lambda b, s: (b, fn(s), 0))
    return pl.pallas_call(
        functools.partial(_rg_kernel, nc=nc, nt=nt),
        grid=(bsz, nt),
        in_specs=_halo_specs(w, 0, fwd, n) + _halo_specs(w, 0, bwd, n) + [
            _const_spec((CONV_W, w)), _const_spec((1, w)),
            _const_spec((2, 2, w, w)), _const_spec((4, w)), _const_spec((2, w))],
        out_specs=[out(fwd), out(bwd)],
        out_shape=[jax.ShapeDtypeStruct((bsz, n, w), F32)] * 2,
        scratch_shapes=[pltpu.VMEM((TM + 2 * SUBLANES, w), F32), pltpu.VMEM((2, w), F32)],
        compiler_params=_params(("parallel", "arbitrary")),
        name="rglru",
    )(z_rg, z_rg, z_rg, z_rg, z_rg, z_rg, cw, cb, gw, gb, lam)


def _decay_tables():
    n = CHUNK
    sel = np.zeros((N_LEVELS + 2, n, n), np.float32)
    masks = np.zeros((N_LEVELS + 2, n, n), np.float32)
    for lvl in range(N_LEVELS):
        half = n >> (lvl + 1)
        for r in range(n):
            start = (r // (2 * half)) * 2 * half
            ref = start + half - 1
            if r - start >= half:
                sel[lvl, r, ref + 1:r + 1] = 1.0
                masks[lvl, r, start:start + half] = 1.0
            else:
                sel[lvl, r, r + 1:ref + 1] = 1.0
    masks[N_LEVELS] = np.eye(n, dtype=np.float32)
    masks[N_LEVELS + 1] = np.tril(np.ones((n, n), np.float32))
    sel[N_LEVELS] = np.tril(np.ones((n, n), np.float32))
    sel[N_LEVELS + 1] = 1.0 - sel[N_LEVELS]
    sel_f, masks_f = sel.reshape(-1, n), masks
    sel_b = sel[:, ::-1, ::-1].reshape(-1, n)
    masks_b = masks[:, ::-1, ::-1]
    return (jnp.asarray(np.stack([sel_f, sel_b]), BF16),
            jnp.asarray(np.stack([masks_f, masks_b]), F32))


def _gla_chunk(q, k, v, g, sel, masks, st_ref, d, reverse):
    x = jnp.exp(_mm_sel(sel, g))
    outs = []
    for h in range(N_HEADS):
        cols = slice(h * HEAD_W, (h + 1) * HEAD_W)
        qh, kh, vh = q[:, cols], k[:, cols], v[:, cols]
        sc = masks[N_LEVELS] * _mm_nt(qh, kh)
        for lvl in range(N_LEVELS):
            xl = x[lvl * CHUNK:(lvl + 1) * CHUNK, cols]
            sc = sc + masks[lvl] * _mm_nt(qh * xl, kh * xl)
        x_cum = x[N_LEVELS * CHUNK:(N_LEVELS + 1) * CHUNK, cols]
        x_rest = x[(N_LEVELS + 1) * CHUNK:(N_LEVELS + 2) * CHUNK, cols]
        st = st_ref[d, h]
        outs.append(_mm(sc, vh) + _mm_nt(qh * x_cum, st))
        x_end = x_cum[0:1, :] if reverse else x_cum[CHUNK - 1:CHUNK, :]
        st_ref[d, h] = st * x_end + _mm_tn(vh, kh * x_rest)
    return jnp.concatenate(outs, axis=-1)


def _gla_chunk_mild(q, k, v, g, sel2, tri, st_ref, d, reverse):
    p = _mm_sel(sel2, g)
    p_cum, p_rest = p[:CHUNK], p[CHUNK:]
    x_cum, x_rest, x_inv = jnp.exp(p_cum), jnp.exp(p_rest), jnp.exp(-p_rest)
    outs = []
    for h in range(N_HEADS):
        cols = slice(h * HEAD_W, (h + 1) * HEAD_W)
        qh, kh, vh = q[:, cols], k[:, cols], v[:, cols]
        kx = kh * x_rest[:, cols]
        sc = tri * _mm_nt(qh * x_inv[:, cols], kx)
        st = st_ref[d, h]
        outs.append(_mm(sc, vh) + _mm_nt(qh * x_cum[:, cols], st))
        x_end = x_cum[0:1, cols] if reverse else x_cum[CHUNK - 1:CHUNK, cols]
        st_ref[d, h] = st * x_end + _mm_tn(vh, kx)
    return jnp.concatenate(outs, axis=-1)


def _gla_tile(q, k, v, g, sel_ref, mask_ref, st_ref, o_ref, d, reverse):
    order = range(TM // CHUNK - 1, -1, -1) if reverse else range(TM // CHUNK)
    chunks = [slice(c * CHUNK, (c + 1) * CHUNK) for c in order]
    total = None
    for rows in chunks:
        t = jnp.sum(g[rows], axis=0, keepdims=True)
        total = t if total is None else jnp.minimum(total, t)
    mild = jnp.min(total) > MILD_DECAY

    @pl.when(mild)
    def _():
        sel2 = sel_ref[d, N_LEVELS * CHUNK:(N_LEVELS + 2) * CHUNK, :]
        tri = mask_ref[d, N_LEVELS + 1]
        for rows in chunks:
            o_ref[rows, :] = _gla_chunk_mild(q[rows], k[rows], v[rows], g[rows], sel2, tri, st_ref, d, reverse)

    @pl.when(jnp.logical_not(mild))
    def _():
        sel = sel_ref[d]
        masks = mask_ref[d]
        for rows in chunks:
            o_ref[rows, :] = _gla_chunk(q[rows], k[rows], v[rows], g[rows], sel, masks, st_ref, d, reverse)


def _gla_kernel(qf, kf, vf, lf, qb, kb, vb, lb, wlr, blr, sel_ref, mask_ref,
                of_ref, ob_ref, st_ref):
    @pl.when(pl.program_id(1) == 0)
    def _():
        st_ref[...] = jnp.zeros_like(st_ref)

    dirs = ((qf, kf, vf, lf, of_ref, False), (qb, kb, vb, lb, ob_ref, True))
    for d, (q_ref, k_ref, v_ref, l_ref, o_ref, reverse) in enumerate(dirs):
        pre = _mm(l_ref[...], wlr[d]) + blr[d:d + 1, :]
        g = _log_sigmoid(pre) * (1.0 / GLA_GATE_NORM)
        q = q_ref[...] * (GLA_DK ** -0.5)
        _gla_tile(q, k_ref[...], v_ref[...], g, sel_ref, mask_ref, st_ref, o_ref, d, reverse)


def _hg_kernel(qf, ff, vf, qb, fb, vb, lbp, sel_ref, mask_ref, of_ref, ob_ref, st_ref):
    @pl.when(pl.program_id(1) == 0)
    def _():
        st_ref[...] = jnp.zeros_like(st_ref)

    dirs = ((qf, ff, vf, of_ref, False), (qb, fb, vb, ob_ref, True))
    for d, (q_ref, f_ref, v_ref, o_ref, reverse) in enumerate(dirs):
        f = f_ref[...]
        lb = lbp[d:d + 1, :]
        log_lb = lbp[2 + d:3 + d, :]
        log_1m = lbp[4 + d:5 + d, :]
        c = log_1m + _log_sigmoid(f)
        g = jnp.maximum(log_lb, c) + jnp.log1p(jnp.exp(-jnp.abs(log_lb - c)))
        k = (1.0 - lb) * jax.nn.sigmoid(-f)
        q = _silu(q_ref[...]) * (HEAD_W ** -0.5)
        _gla_tile(q, k, v_ref[...], g, sel_ref, mask_ref, st_ref, o_ref, d, reverse)


def _mixer_specs(widths_cols, tile_fn):
    return [pl.BlockSpec((None, TM, w), lambda b, s, c=c: (b, tile_fn(s), c)) for w, c in widths_cols]


def _gla_call(z_gla, wlr, blr, sel, masks, nc):
    bsz, n, _ = z_gla.shape
    nt = n // TM
    w = BRANCH_W
    fwd = lambda s: s
    bwd = lambda s: _bwd_tile(s, nc, nt)
    cols = [(w, 0), (w, 1), (w, 2), (HEAD_W, 4 * w // HEAD_W)]
    out = lambda fn: pl.BlockSpec((None, TM, w), lambda b, s: (b, fn(s), 0))
    return pl.pallas_call(
        _gla_kernel,
        grid=(bsz, nt),
        in_specs=_mixer_specs(cols, fwd) + _mixer_specs(cols, bwd) + [
            _const_spec(wlr.shape), _const_spec(blr.shape),
            _const_spec(sel.shape), _const_spec(masks.shape)],
        out_specs=[out(fwd), out(bwd)],
        out_shape=[jax.ShapeDtypeStruct((bsz, n, w), F32)] * 2,
        scratch_shapes=[pltpu.VMEM((2, N_HEADS, HEAD_W, HEAD_W), F32)],
        compiler_params=_params(("parallel", "arbitrary")),
        name="gla",
    )(*([z_gla] * 8), wlr, blr, sel, masks)


def _hg_call(z_hg, lbp, sel, masks, nc):
    bsz, n, _ = z_hg.shape
    nt = n // TM
    w = BRANCH_W
    fwd = lambda s: s
    bwd = lambda s: _bwd_tile(s, nc, nt)
    out = lambda fn: pl.BlockSpec((None, TM, w), lambda b, s: (b, fn(s), 0))
    return pl.pallas_call(
        _hg_kernel,
        grid=(bsz, nt),
        in_specs=_mixer_specs([(w, 0), (w, 2), (w, 1)], fwd) + _mixer_specs([(w, 0), (w, 3), (w, 1)], bwd) + [
            _const_spec(lbp.shape), _const_spec(sel.shape), _const_spec(masks.shape)],
        out_specs=[out(fwd), out(bwd)],
        out_shape=[jax.ShapeDtypeStruct((bsz, n, w), F32)] * 2,
        scratch_shapes=[pltpu.VMEM((2, N_HEADS, HEAD_W, HEAD_W), F32)],
        compiler_params=_params(("parallel", "arbitrary")),
        name="hgrn2",
    )(*([z_hg] * 6), lbp, sel, masks)


def _ml_chunk(q, k, v, gates, log_f, tri_sel, tri_mask, ct_ref, n_ref, m_ref, d, reverse):
    b_all = _mm_sel(tri_sel, log_f)
    b_t = b_all.T
    g_t = gates.T
    last = 0 if reverse else CHUNK - 1
    outs = []
    for h in range(N_HEADS):
        cols = slice(h * HEAD_W, (h + 1) * HEAD_W)
        qh, kh, vh = q[:, cols], k[:, cols], v[:, cols]
        ci, cf = d * 8 + h, d * 8 + 4 + h
        st = d * N_HEADS + h
        b_col = b_all[:, cf:cf + 1]
        i_col = gates[:, ci:ci + 1]
        b_row = b_t[cf:cf + 1, :]
        i_row = g_t[ci:ci + 1, :]
        m_prev = m_ref[st:st + 1, 0:1]
        log_w = jnp.where(tri_mask > 0.0, b_col - b_row + i_row, -jnp.inf)
        log_inter = b_col + m_prev
        m = jnp.maximum(log_inter, jnp.max(log_w, axis=-1, keepdims=True))
        w_inter = jnp.exp(log_inter - m)
        s = _mm_nt(qh, kh) * jnp.exp(log_w - m)
        ct = ct_ref[st]
        n_vec = n_ref[st:st + 1, :]
        num = _mm(s, vh) + w_inter * _mm_nt(qh, ct)
        den = jnp.sum(s, axis=-1, keepdims=True) + w_inter * jnp.sum(qh * n_vec, axis=-1, keepdims=True)
        outs.append(num / jnp.maximum(jnp.abs(den), jnp.exp(-m)))
        m_new = m[last:last + 1, :]
        b_end = b_col[last:last + 1, :]
        w_end = jnp.exp(b_end - b_col + i_col - m_new)
        decay = jnp.exp(b_end + m_prev - m_new)
        kw = kh * w_end
        ct_ref[st] = decay * ct + _mm_tn(vh, kw)
        n_ref[st:st + 1, :] = decay * n_vec + jnp.sum(kw, axis=0, keepdims=True)
        m_ref[st:st + 1, :] = jnp.broadcast_to(m_new, (1, HEAD_W))
    return jnp.concatenate(outs, axis=-1)


def _ml_kernel(xf, xf_p, xf_n, vf, gf, xb, xb_p, xb_n, vb, gb_, cw, cb, gbias, tri_sel, tri_mask,
               of_ref, ob_ref, pad_ref, ct_ref, n_ref, m_ref, *, nc, nt):
    s = pl.program_id(1)

    @pl.when(s == 0)
    def _():
        ct_ref[...] = jnp.zeros_like(ct_ref)
        n_ref[...] = jnp.zeros_like(n_ref)
        m_ref[...] = jnp.zeros_like(m_ref)

    dirs = ((xf, xf_p, xf_n, vf, gf, of_ref, s, False),
            (xb, xb_p, xb_n, vb, gb_, ob_ref, _bwd_tile(s, nc, nt), True))
    for d, (x_ref, p_ref, nx_ref, v_ref, g_ref, o_ref, tile, reverse) in enumerate(dirs):
        qk = _silu(_conv_tile(x_ref, p_ref, nx_ref, pad_ref, cw, cb, tile, nc, nt))
        q = qk[:, :BRANCH_W] * (HEAD_W ** -0.5)
        k = qk[:, BRANCH_W:]
        v = v_ref[...]
        gates = g_ref[...] + gbias[...]
        log_f = _log_sigmoid(gates)
        order = range(TM // CHUNK - 1, -1, -1) if reverse else range(TM // CHUNK)
        for c in order:
            rows = slice(c * CHUNK, (c + 1) * CHUNK)
            o_ref[rows, :] = _ml_chunk(q[rows], k[rows], v[rows], gates[rows], log_f[rows],
                                       tri_sel[d], tri_mask[d], ct_ref, n_ref, m_ref, d, reverse)


def _ml_call(z_ml, cw, cb, gbias, tri_sel, tri_mask, nc):
    bsz, n, _ = z_ml.shape
    nt = n // TM
    w = BRANCH_W
    fwd = lambda s: s
    bwd = lambda s: _bwd_tile(s, nc, nt)
    out = lambda fn: pl.BlockSpec((None, TM, w), lambda b, s: (b, fn(s), 0))

    def side(fn):
        return _halo_specs(2 * w, 0, fn, n) + _mixer_specs([(w, 2), (HEAD_W, 4 * w // HEAD_W)], fn)

    return pl.pallas_call(
        functools.partial(_ml_kernel, nc=nc, nt=nt),
        grid=(bsz, nt),
        in_specs=side(fwd) + side(bwd) + [
            _const_spec((CONV_W, 2 * w)), _const_spec((1, 2 * w)), _const_spec((1, HEAD_W)),
            _const_spec(tri_sel.shape), _const_spec(tri_mask.shape)],
        out_specs=[out(fwd), out(bwd)],
        out_shape=[jax.ShapeDtypeStruct((bsz, n, w), F32)] * 2,
        scratch_shapes=[pltpu.VMEM((TM + 2 * SUBLANES, 2 * w), F32),
                        pltpu.VMEM((2 * N_HEADS, HEAD_W, HEAD_W), F32),
                        pltpu.VMEM((2 * N_HEADS, HEAD_W), F32),
                        pltpu.VMEM((2 * N_HEADS, HEAD_W), F32)],
        compiler_params=_params(("parallel", "arbitrary")),
        name="mlstm",
    )(*([z_ml] * 10), cw, cb, gbias, tri_sel, tri_mask)


def _gelu_tanh(x):
    return 0.5 * x * (1.0 + jnp.tanh(0.7978845608028654 * (x + 0.044715 * (x * x * x))))


def _route(logits):
    col = lax.broadcasted_iota(jnp.int32, logits.shape, 1)
    colf = col.astype(F32)
    is_g = jnp.logical_and(col >= N_EXPERTS, col < N_EXPERTS + N_GROUPS)
    is_e = col < N_EXPERTS
    neg = -jnp.inf
    big = 1e9
    gl = jnp.where(is_g, logits, neg)
    g_max = jnp.max(gl, axis=-1, keepdims=True)
    grp = jnp.min(jnp.where(gl == g_max, colf, big), axis=-1, keepdims=True) - N_EXPERTS
    p_grp = 1.0 / jnp.sum(jnp.exp(gl - g_max), axis=-1, keepdims=True)
    col_grp = lax.shift_right_logical(col, EXPERTS_PER_GROUP.bit_length() - 1)
    in_grp = jnp.logical_and(is_e, col_grp.astype(F32) == grp)
    e1 = jnp.where(in_grp, logits, neg)
    top1 = jnp.max(e1, axis=-1, keepdims=True)
    idx1 = jnp.min(jnp.where(e1 == top1, colf, big), axis=-1, keepdims=True)
    e2 = jnp.where(colf == idx1, neg, e1)
    top2 = jnp.max(e2, axis=-1, keepdims=True)
    idx2 = jnp.min(jnp.where(e2 == top2, colf, big), axis=-1, keepdims=True)
    t = jnp.exp(top2 - top1)
    w1 = 1.0 / (1.0 + t)
    w2 = t / (1.0 + t)
    return p_grp * jnp.where(colf == idx1, w1, jnp.where(colf == idx2, w2, 0.0))


def _stage_c_kernel(u_ref, h_ref, mod_ref, rg_y, rg_f, rg_b, gla_g, gla_f, gla_b, hg_g, hg_f, hg_b,
                    ml_o, ml_f, ml_b, gains, w_merge, b_merge, w_branch, w_out, ffn_g, w_route, b_route,
                    h_out, u2_out, wt_out):
    ys = (
        _gelu_tanh(rg_y[...]) * (rg_f[...] + rg_b[...]),
        _head_rmsnorm(gla_f[...] + gla_b[...], gains[0:1, :]) * _silu(gla_g[...]),
        _head_rmsnorm(hg_f[...] + hg_b[...], gains[1:2, :]) * _silu(hg_g[...]),
        jax.nn.sigmoid(ml_o[...]) * _head_rmsnorm(ml_f[...] + ml_b[...], gains[2:3, :]),
    )
    u = u_ref[...]
    d = u.shape[-1]
    merged = None
    for kk, y in enumerate(ys):
        gate = jax.nn.sigmoid(jnp.dot(u, w_merge[:, kk * d:(kk + 1) * d], preferred_element_type=F32)
                              + b_merge[:, kk * d:(kk + 1) * d])
        term = gate * _mm(y, w_branch[kk])
        merged = term if merged is None else merged + term
    mix = _mm(merged, w_out[...])
    h_new = h_ref[...] + mod_ref[2:3, :] * mix
    h_out[...] = h_new
    u2 = _rmsnorm_rows(h_new, ffn_g[...]) * (1.0 + mod_ref[4:5, :]) + mod_ref[3:4, :]
    u2_out[...] = u2.astype(BF16)
    logits = jnp.dot(u2, w_route[...], precision=lax.Precision.HIGHEST,
                     preferred_element_type=F32) + b_route[...]
    wt_out[...] = _route(logits)


def _stage_c_call(u, h, mod, z_rg, rg_f, rg_b, z_gla, gla_f, gla_b, z_hg, hg_f, hg_b, z_ml, ml_f, ml_b,
                  gains, w_merge, b_merge, w_branch, w_out, ffn_g, w_route, b_route, nc, t0):
    bsz, n, d = h.shape
    nt = n // TM
    w = BRANCH_W

    n_out = n - t0 * TM

    def tok(width, col=0):
        return pl.BlockSpec((None, TM, width), lambda b, t: (b, t + t0, col))

    def out_tok(width):
        return pl.BlockSpec((None, TM, width), lambda b, t: (b, t, 0))

    mod_spec = pl.BlockSpec((None, 6, d), lambda b, t: (jnp.where(t + t0 < nc, bsz, b), 0, 0))
    return pl.pallas_call(
        _stage_c_kernel,
        grid=(bsz, nt - t0),
        in_specs=[tok(d), tok(d), mod_spec,
                  tok(w, 1), tok(w), tok(w),
                  tok(w, 3), tok(w), tok(w),
                  tok(w, 4), tok(w), tok(w),
                  tok(w, 3), tok(w), tok(w),
                  _const_spec(gains.shape), _const_spec(w_merge.shape), _const_spec(b_merge.shape),
                  _const_spec(w_branch.shape), _const_spec(w_out.shape), _const_spec(ffn_g.shape),
                  _const_spec(w_route.shape), _const_spec(b_route.shape)],
        out_specs=[out_tok(d), out_tok(d), out_tok(HEAD_W)],
        out_shape=[jax.ShapeDtypeStruct((bsz, n_out, d), F32),
                   jax.ShapeDtypeStruct((bsz, n_out, d), BF16),
                   jax.ShapeDtypeStruct((bsz, n_out, HEAD_W), F32)],
        compiler_params=_params(("parallel", "parallel")),
        name="stage_c",
    )(u, h, mod, z_rg, rg_f, rg_b, z_gla, gla_f, gla_b, z_hg, hg_f, hg_b, z_ml, ml_f, ml_b,
      gains, w_merge, b_merge, w_branch, w_out, ffn_g, w_route, b_route)


def _moe_kernel(u2_ref, wt_ref, h_ref, mod_lo, mod_hi, fin_g, w1, w3, w2, o_ref, acc_ref, *, final):
    e = pl.program_id(1)

    @pl.when(e == 0)
    def _():
        acc_ref[...] = jnp.zeros_like(acc_ref)

    x = u2_ref[...]
    h1 = jnp.dot(x, w1[...], preferred_element_type=F32)
    h3 = jnp.dot(x, w3[...], preferred_element_type=F32)
    y = _mm(_silu(h1) * h3, w2[...])
    wt = wt_ref[...]
    col = lax.broadcasted_iota(jnp.int32, wt.shape, 1)
    w_e = jnp.sum(jnp.where(col == e, wt, 0.0), axis=-1, keepdims=True)
    acc_ref[...] += w_e * y

    @pl.when(e == N_EXPERTS - 1)
    def _():
        for half, mod_ref in enumerate((mod_lo, mod_hi)):
            rows = slice(half * TM, (half + 1) * TM)
            out = h_ref[rows, :] + mod_ref[5:6, :] * acc_ref[rows, :]
            if final:
                out = _rmsnorm_rows(out, fin_g[...])
            o_ref[rows, :] = out


def _moe_call(u2, wt, h, mod, fin_g, w1, w3, w2, nc, t0, final):
    bsz, n_out, d = h.shape
    de = w1.shape[-1]
    te = 2 * TM
    rows = bsz * n_out
    halves = n_out // TM

    def tok(width):
        return pl.BlockSpec((te, width), lambda t, e: (t, 0))

    def mod_spec(half):
        def index(t, e):
            piece = 2 * t + half
            b, pos = piece // halves, piece % halves
            return (jnp.where(pos + t0 < nc, bsz, b), 0, 0)
        return pl.BlockSpec((None, 6, d), index)

    out = pl.pallas_call(
        functools.partial(_moe_kernel, final=final),
        grid=(rows // te, N_EXPERTS),
        in_specs=[tok(d), tok(HEAD_W), tok(d), mod_spec(0), mod_spec(1), _const_spec((1, d)),
                  pl.BlockSpec((None, d, de), lambda t, e: (e, 0, 0)),
                  pl.BlockSpec((None, d, de), lambda t, e: (e, 0, 0)),
                  pl.BlockSpec((None, de, d), lambda t, e: (e, 0, 0))],
        out_specs=tok(d),
        out_shape=jax.ShapeDtypeStruct((rows, d), F32),
        scratch_shapes=[pltpu.VMEM((te, d), F32)],
        compiler_params=_params(("parallel", "arbitrary")),
        name="moe",
    )(u2.reshape(rows, d), wt.reshape(rows, HEAD_W), h.reshape(rows, d), mod, mod, fin_g, w1, w3, w2)
    return out.reshape(bsz, n_out, d)


def _pad_heads(w, dk):
    lead = w.shape[:-1]
    w = w.reshape(lead + (N_HEADS, dk))
    w = jnp.pad(w, [(0, 0)] * len(lead) + [(0, 0), (0, HEAD_W - dk)])
    return w.reshape(lead + (N_HEADS * HEAD_W,))


def _pad_cols(w, width):
    return jnp.pad(w, [(0, 0)] * (w.ndim - 1) + [(0, width - w.shape[-1])])


def _split_w_in(w_in):
    bw = BRANCH_W
    sizes = [bw, bw, N_HEADS * GLA_DK, N_HEADS * GLA_DK, bw, bw, 2 * GLA_RANK,
             bw, bw, 2 * bw, bw, bw, bw, bw, bw, 16]
    offs = np.cumsum([0] + sizes)
    p = [w_in[:, offs[i]:offs[i + 1]] for i in range(len(sizes))]
    w_rg = jnp.concatenate([p[0], p[1]], axis=1)
    w_gla = jnp.concatenate([_pad_heads(p[2], GLA_DK), _pad_heads(p[3], GLA_DK), p[4], p[5],
                             _pad_cols(p[6], HEAD_W)], axis=1)
    w_hg = jnp.concatenate([p[7], p[8], p[9], p[10]], axis=1)
    w_ml = jnp.concatenate([p[11], p[12], p[13], p[14], _pad_cols(p[15], HEAD_W)], axis=1)
    return [w.astype(BF16) for w in (w_rg, w_gla, w_hg, w_ml)]


def _block_diag(w):
    k, n = w.shape[-3], w.shape[-1]
    eye = jnp.eye(k, dtype=w.dtype)
    full = jnp.einsum('...kij,kl->...kilj', w, eye)
    return full.reshape(w.shape[:-3] + (k * n, k * n))


def kernel(x, c, ctx, c_ctx, ada_w, ada_b, norm_mix_g, norm_ffn_g, w_in, rg_conv_w, rg_conv_b, rg_gate_w, rg_gate_b, rg_lambda, gla_w_lr, gla_b_lr, gla_norm_g, hgrn_lb_logits, hgrn_norm_g, ml_conv_w, ml_conv_b, ml_gate_b, ml_norm_g, w_branch, w_merge, b_merge, w_out, moe_w_group, moe_b_group, moe_w_expert, moe_b_expert, moe_w1, moe_w3, moe_w2, final_norm_g):
    bsz, seq, d = x.shape
    n_ctx = ctx.shape[1]
    depth = ada_w.shape[0]
    assert n_ctx % TM == 0 and seq % TM == 0 and d == 2 * BRANCH_W
    nc = n_ctx // TM

    h = jnp.concatenate([ctx, x], axis=1)
    cvec = jnp.zeros((SUBLANES, d), F32).at[:bsz].set(c).at[bsz].set(c_ctx)
    mod_all = _mod_call(cvec, ada_w, ada_b).reshape(depth, SUBLANES, 6, d)[:, :bsz + 1]

    lb_cum = jnp.cumsum(jax.nn.softmax(hgrn_lb_logits.astype(F32), axis=0), axis=0)
    hgrn_lb = lb_cum - lb_cum[:1]
    sel, masks = _decay_tables()
    tri_sel = sel[:, N_LEVELS * CHUNK:(N_LEVELS + 1) * CHUNK, :]
    tri_mask = masks[:, N_LEVELS + 1]

    out = None
    for l in range(depth):
        last = l == depth - 1
        mod = mod_all[l]
        w_rg, w_gla, w_hg, w_ml = _split_w_in(w_in[l])
        z_rg, z_gla, z_hg, z_ml, u = _stage_a_call(h, mod, norm_mix_g[l], w_rg, w_gla, w_hg, w_ml, nc)

        rg_f, rg_b = _rg_call(z_rg, rg_conv_w[l], rg_conv_b[l].reshape(1, -1),
                              _block_diag(rg_gate_w[l]).astype(BF16),
                              rg_gate_b[l].reshape(4, -1), rg_lambda[l], nc)

        wlr = jnp.zeros((2, HEAD_W, N_HEADS * HEAD_W), F32)
        wlr_p = _pad_heads(gla_w_lr[l], GLA_DK)
        wlr = wlr.at[0, :GLA_RANK].set(wlr_p[0]).at[1, GLA_RANK:2 * GLA_RANK].set(wlr_p[1])
        gla_f, gla_b = _gla_call(z_gla, wlr.astype(BF16), _pad_heads(gla_b_lr[l], GLA_DK), sel, masks, nc)

        lb = hgrn_lb[l]
        lbp = jnp.concatenate([lb, jnp.log(lb), jnp.log1p(-lb), jnp.zeros((2, lb.shape[-1]), F32)], axis=0)
        hg_f, hg_b = _hg_call(z_hg, lbp, sel, masks, nc)

        gbias = _pad_cols(ml_gate_b[l].reshape(1, -1), HEAD_W)
        ml_f, ml_b = _ml_call(z_ml, ml_conv_w[l], ml_conv_b[l].reshape(1, -1), gbias, tri_sel, tri_mask, nc)

        gains = jnp.zeros((SUBLANES, HEAD_W), F32).at[0].set(gla_norm_g[l]).at[1].set(hgrn_norm_g[l]).at[2].set(ml_norm_g[l])
        w_route = _pad_cols(jnp.concatenate([moe_w_expert[l], moe_w_group[l]], axis=1), HEAD_W)
        b_route = _pad_cols(jnp.concatenate([moe_b_expert[l], moe_b_group[l]]).reshape(1, -1), HEAD_W)
        t0 = nc if last else 0
        h_mid, u2, wt = _stage_c_call(
            u, h, mod, z_rg, rg_f, rg_b, z_gla, gla_f, gla_b, z_hg, hg_f, hg_b, z_ml, ml_f, ml_b,
            gains, w_merge[l].astype(BF16), b_merge[l].reshape(1, -1), w_branch[l].astype(BF16),
            w_out[l].astype(BF16), norm_ffn_g[l].reshape(1, -1), w_route, b_route, nc, t0)

        res = _moe_call(u2, wt, h_mid, mod, final_norm_g.reshape(1, -1), moe_w1[l].astype(BF16),
                        moe_w3[l].astype(BF16), moe_w2[l].astype(BF16), nc, t0, last)
        if last:
            out = res
        else:
            h = res
    return out
```

```python
import functools

import numpy as np
import jax
import jax.numpy as jnp
from jax import lax
from jax.experimental import pallas as pl
from jax.experimental.pallas import tpu as pltpu

F32 = jnp.float32
BF16 = jnp.bfloat16

EPS = 1e-6
TM = 256
CHUNK = 64
SUBLANES = 8
N_HEADS = 4
HEAD_W = 128
BRANCH_W = 512
CONV_W = 4
CONV_LEFT = 2
RG_C = 8.0
GLA_DK = 64
GLA_RANK = 16
GLA_GATE_NORM = 16.0
N_GROUPS = 4
EXPERTS_PER_GROUP = 4
N_EXPERTS = 16
N_LEVELS = 6
MILD_DECAY = -40.0
VMEM_LIMIT = 56 * 1024 * 1024

W_RG = 2 * BRANCH_W
W_GLA = 4 * BRANCH_W + HEAD_W
W_HG = 5 * BRANCH_W
W_ML = 4 * BRANCH_W + 2 * HEAD_W
SPLIT_LEVELS = (0, 2, N_LEVELS)


def _mm(a, b):
    return jnp.dot(a.astype(BF16), b.astype(BF16), preferred_element_type=F32)


def _mm_nt(a, b):
    return lax.dot_general(a.astype(BF16), b.astype(BF16), (((1,), (1,)), ((), ())),
                           preferred_element_type=F32)


def _mm_tn(a, b):
    return lax.dot_general(a.astype(BF16), b.astype(BF16), (((0,), (0,)), ((), ())),
                           preferred_element_type=F32)


def _mm_sel(sel, x):
    x1 = x.astype(BF16)
    x2 = (x - x1.astype(F32)).astype(BF16)
    dot = functools.partial(jnp.dot, preferred_element_type=F32)
    return dot(sel, x1) + dot(sel, x2)


def _log_sigmoid(x):
    return jnp.minimum(x, 0.0) - jnp.log1p(jnp.exp(-jnp.abs(x)))


def _silu(x):
    return x * jax.nn.sigmoid(x)


def _rmsnorm_rows(x, g):
    return x * lax.rsqrt(jnp.mean(x * x, axis=-1, keepdims=True) + EPS) * g


def _head_rmsnorm(o, g):
    parts = [_rmsnorm_rows(o[:, h * HEAD_W:(h + 1) * HEAD_W], g) for h in range(N_HEADS)]
    return jnp.concatenate(parts, axis=-1)


def _bwd_tile(s, nc, nt):
    return jnp.where(s < nc, nc - 1 - s, nt - 1 - (s - nc))


def _const_spec(shape):
    nd = len(shape)
    return pl.BlockSpec(shape, lambda *_: (0,) * nd)


def _params(sem):
    return pltpu.CompilerParams(dimension_semantics=sem, vmem_limit_bytes=VMEM_LIMIT)


def _mod_kernel(c_ref, w_ref, b_ref, o_ref):
    cv = _silu(c_ref[...])
    o_ref[...] = jnp.dot(cv, w_ref[...], precision=lax.Precision.HIGHEST,
                         preferred_element_type=F32) + b_ref[...]


def _mod_call(cvec, ada_w, ada_b):
    depth, d, six_d = ada_w.shape
    tn = 1024
    return pl.pallas_call(
        _mod_kernel,
        grid=(depth, six_d // tn),
        in_specs=[pl.BlockSpec((SUBLANES, d), lambda l, j: (0, 0)),
                  pl.BlockSpec((None, d, tn), lambda l, j: (l, 0, j)),
                  pl.BlockSpec((None, 1, tn), lambda l, j: (l, 0, j))],
        out_specs=pl.BlockSpec((None, SUBLANES, tn), lambda l, j: (l, 0, j)),
        out_shape=jax.ShapeDtypeStruct((depth, SUBLANES, six_d), F32),
        compiler_params=_params(("parallel", "parallel")),
        name="adaln_mod",
    )(cvec, ada_w, ada_b.reshape(depth, 1, six_d))


def _stage_a_kernel(h_ref, mod_ref, g_ref, w_rg, w_gla, w_hg, w_ml,
                    z_rg, z_gla, z_hg, z_ml, u_ref):
    x = h_ref[...]
    u = _rmsnorm_rows(x, g_ref[...]) * (1.0 + mod_ref[1:2, :]) + mod_ref[0:1, :]
    ub = u.astype(BF16)
    u_ref[...] = ub
    z_rg[...] = jnp.dot(ub, w_rg[...], preferred_element_type=F32)
    z_gla[...] = jnp.dot(ub, w_gla[...], preferred_element_type=F32)
    z_hg[...] = jnp.dot(ub, w_hg[...], preferred_element_type=F32)
    z_ml[...] = jnp.dot(ub, w_ml[...], preferred_element_type=F32)


def _stage_a_call(h, mod, norm_g, w_rg, w_gla, w_hg, w_ml, nc):
    bsz, n, d = h.shape
    nt = n // TM
    tm = TM // 2

    def tok(w):
        return pl.BlockSpec((None, tm, w), lambda b, t: (b, t, 0))

    return pl.pallas_call(
        _stage_a_kernel,
        grid=(bsz, n // tm),
        in_specs=[tok(d),
                  pl.BlockSpec((None, 6, d), lambda b, t: (jnp.where(t * tm < nc * TM, bsz, b), 0, 0)),
                  _const_spec((1, d)),
                  _const_spec((d, W_RG)), _const_spec((d, W_GLA)),
                  _const_spec((d, W_HG)), _const_spec((d, W_ML))],
        out_specs=[tok(W_RG), tok(W_GLA), tok(W_HG), tok(W_ML), tok(d)],
        out_shape=[jax.ShapeDtypeStruct((bsz, n, W_RG), F32),
                   jax.ShapeDtypeStruct((bsz, n, W_GLA), F32),
                   jax.ShapeDtypeStruct((bsz, n, W_HG), F32),
                   jax.ShapeDtypeStruct((bsz, n, W_ML), F32),
                   jax.ShapeDtypeStruct((bsz, n, d), BF16)],
        compiler_params=_params(("parallel", "parallel")),
        name="stage_a",
    )(h, mod, norm_g.reshape(1, d), w_rg, w_gla, w_hg, w_ml)


def _conv_tile(x_ref, prev_ref, next_ref, pad_ref, cw_ref, cb_ref, tile, nc, nt):
    prev_ok = jnp.logical_and(tile != 0, tile != nc)
    next_ok = jnp.logical_and(tile != nc - 1, tile != nt - 1)
    pad_ref[0:SUBLANES, :] = jnp.where(prev_ok, prev_ref[...], 0.0)
    pad_ref[SUBLANES:SUBLANES + TM, :] = x_ref[...]
    pad_ref[SUBLANES + TM:2 * SUBLANES + TM, :] = jnp.where(next_ok, next_ref[...], 0.0)
    acc = cb_ref[...]
    for j in range(CONV_W):
        off = SUBLANES - CONV_LEFT + j
        acc = acc + cw_ref[j:j + 1, :] * pad_ref[off:off + TM, :]
    return acc


def _halo_specs(width, col_block, tile_fn, n):
    rows = TM // SUBLANES
    last = n // SUBLANES - 1
    cur = pl.BlockSpec((None, TM, width), lambda b, s: (b, tile_fn(s), col_block))
    prev = pl.BlockSpec((None, SUBLANES, width),
                        lambda b, s: (b, jnp.maximum(tile_fn(s) * rows - 1, 0), col_block))
    nxt = pl.BlockSpec((None, SUBLANES, width),
                       lambda b, s: (b, jnp.minimum((tile_fn(s) + 1) * rows, last), col_block))
    return [cur, prev, nxt]


def _scan_rows8(a, b, reverse):
    n = a.shape[0]
    pos = jnp.bitwise_and(lax.broadcasted_iota(jnp.int32, a.shape, 0), SUBLANES - 1)
    k = 1
    while k < SUBLANES:
        if reverse:
            a_s, b_s, ok = pltpu.roll(a, n - k, 0), pltpu.roll(b, n - k, 0), pos < SUBLANES - k
        else:
            a_s, b_s, ok = pltpu.roll(a, k, 0), pltpu.roll(b, k, 0), pos >= k
        b = b + a * jnp.where(ok, b_s, 0.0)
        a = a * jnp.where(ok, a_s, 1.0)
        k *= 2
    return a, b


def _scan_tile(a, b, carry, o_ref, reverse):
    a8, b8 = _scan_rows8(a, b, reverse)
    groups = a.shape[0] // SUBLANES
    order = range(groups - 1, -1, -1) if reverse else range(groups)
    for r in order:
        rows = slice(r * SUBLANES, (r + 1) * SUBLANES)
        h = b8[rows] + a8[rows] * carry
        o_ref[rows, :] = h
        carry = h[0:1, :] if reverse else h[SUBLANES - 1:SUBLANES, :]
    return carry


def _rg_kernel(xf, xf_p, xf_n, xb, xb_p, xb_n, cw, cb, gw, gb, lam,
               hf_ref, hb_ref, pad_ref, carry_ref, *, nc, nt):
    s = pl.program_id(1)

    @pl.when(s == 0)
    def _():
        carry_ref[...] = jnp.zeros_like(carry_ref)

    dirs = ((xf, xf_p, xf_n, hf_ref, s, False),
            (xb, xb_p, xb_n, hb_ref, _bwd_tile(s, nc, nt), True))
    for d, (x_ref, p_ref, n_ref, o_ref, tile, reverse) in enumerate(dirs):
        x = _conv_tile(x_ref, p_ref, n_ref, pad_ref, cw, cb, tile, nc, nt)
        r = jax.nn.sigmoid(_mm(x, gw[d, 0]) + gb[2 * d:2 * d + 1, :])
        i = jax.nn.sigmoid(_mm(x, gw[d, 1]) + gb[2 * d + 1:2 * d + 2, :])
        lam_d = lam[d:d + 1, :]
        softplus = jnp.maximum(-lam_d, 0.0) + jnp.log1p(jnp.exp(-jnp.abs(lam_d)))
        log_a = -RG_C * r * softplus
        a = jnp.exp(log_a)
        t = jnp.tanh(log_a)
        bt = jnp.sqrt(-2.0 * t / (1.0 - t)) * (i * x)
        carry_ref[d:d + 1, :] = _scan_tile(a, bt, carry_ref[d:d + 1, :], o_ref, reverse)


def _rg_call(z_rg, cw, cb, gw, gb, lam, nc):
    bsz, n, _ = z_rg.shape
    nt = n // TM
    w = BRANCH_W
    fwd = lambda s: s
    bwd = lambda s: _bwd_tile(s, nc, nt)
    out = lambda fn: pl.BlockSpec((None, TM, w), lambda b, s: (b, fn(s), 0))
    return pl.pallas_call(
        functools.partial(_rg_kernel, nc=nc, nt=nt),
        grid=(bsz, nt),
        in_specs=_halo_specs(w, 0, fwd, n) + _halo_specs(w, 0, bwd, n) + [
            _const_spec((CONV_W, w)), _const_spec((1, w)),
            _const_spec((2, 2, w, w)), _const_spec((4, w)), _const_spec((2, w))],
        out_specs=[out(fwd), out(bwd)],
        out_shape=[jax.ShapeDtypeStruct((bsz, n, w), F32)] * 2,
        scratch_shapes=[pltpu.VMEM((TM + 2 * SUBLANES, w), F32), pltpu.VMEM((2, w), F32)],
        compiler_params=_params(("parallel", "arbitrary")),
        name="rglru",
    )(z_rg, z_rg, z_rg, z_rg, z_rg, z_rg, cw, cb, gw, gb, lam)


def _decay_tables(levels):
    n = CHUNK
    blk = n >> levels
    sel = np.zeros((levels + 3, n, n), np.float32)
    masks = np.zeros((levels + 1, n, n), np.float32)
    for lvl in range(levels):
        half = n >> (lvl + 1)
        for r in range(n):
            start = (r // (2 * half)) * 2 * half
            ref = start + half - 1
            if r - start >= half:
                sel[lvl, r, ref + 1:r + 1] = 1.0
                masks[lvl, r, start:start + half] = 1.0
            else:
                sel[lvl, r, r + 1:ref + 1] = 1.0
    sel[levels] = np.tril(np.ones((n, n), np.float32))
    sel[levels + 1] = 1.0 - sel[levels]
    for r in range(n):
        start = (r // blk) * blk
        sel[levels + 2, r, r + 1:start + blk] = 1.0
        masks[levels, r, start:r + 1] = 1.0
    if levels == 0 or levels == N_LEVELS:
        sel = sel[:levels + 2]
    sel_b = sel[:, ::-1, ::-1].reshape(-1, n)
    masks_b = masks[:, ::-1, ::-1]
    return (jnp.asarray(np.stack([sel.reshape(-1, n), sel_b]), BF16),
            jnp.asarray(np.stack([masks, masks_b]), F32))


def _gla_chunk(q, k, v, g, sel, masks, st_ref, d, reverse, levels):
    p = _mm_sel(sel, g)
    x = jnp.exp(p[:(levels + 2) * CHUNK])
    x_cum = x[levels * CHUNK:(levels + 1) * CHUNK]
    x_rest = x[(levels + 1) * CHUNK:(levels + 2) * CHUNK]
    if levels == 0:
        xk_blk, xq_blk = x_rest, jnp.exp(-p[CHUNK:2 * CHUNK])
    elif levels == N_LEVELS:
        xk_blk = xq_blk = None
    else:
        p_blk = p[(levels + 2) * CHUNK:]
        xk_blk, xq_blk = jnp.exp(p_blk), jnp.exp(-p_blk)
    outs = []
    for h in range(N_HEADS):
        cols = slice(h * HEAD_W, (h + 1) * HEAD_W)
        qh, kh, vh = q[:, cols], k[:, cols], v[:, cols]
        kx = kh * x_rest[:, cols]
        if xk_blk is None:
            sc = masks[levels] * _mm_nt(qh, kh)
        elif levels == 0:
            sc = masks[levels] * _mm_nt(qh * xq_blk[:, cols], kx)
        else:
            sc = masks[levels] * _mm_nt(qh * xq_blk[:, cols], kh * xk_blk[:, cols])
        for lvl in range(levels):
            xl = x[lvl * CHUNK:(lvl + 1) * CHUNK, cols]
            sc = sc + masks[lvl] * _mm_nt(qh * xl, kh * xl)
        st = st_ref[d, h]
        outs.append(_mm(sc, vh) + _mm_nt(qh * x_cum[:, cols], st))
        x_end = x_cum[0:1, cols] if reverse else x_cum[CHUNK - 1:CHUNK, cols]
        st_ref[d, h] = st * x_end + _mm_tn(vh, kx)
    return jnp.concatenate(outs, axis=-1)


def _gla_tile(q, k, v, g, tabs, st_ref, o_ref, d, reverse):
    order = range(TM // CHUNK - 1, -1, -1) if reverse else range(TM // CHUNK)
    chunks = [slice(c * CHUNK, (c + 1) * CHUNK) for c in order]

    def min_block_sum(block):
        sums = jnp.sum(g.reshape(TM // block, block, g.shape[-1]), axis=1)
        return jnp.min(sums)

    conds = []
    taken = None
    for levels in SPLIT_LEVELS[:-1]:
        ok = min_block_sum(CHUNK >> levels) > MILD_DECAY
        conds.append(ok if taken is None else jnp.logical_and(ok, jnp.logical_not(taken)))
        taken = ok if taken is None else jnp.logical_or(taken, ok)
    conds.append(jnp.logical_not(taken))

    for levels, cond, (sel_ref, mask_ref) in zip(SPLIT_LEVELS, conds, tabs):
        @pl.when(cond)
        def _(levels=levels, sel_ref=sel_ref, mask_ref=mask_ref):
            sel = sel_ref[d]
            masks = mask_ref[d]
            for rows in chunks:
                o_ref[rows, :] = _gla_chunk(q[rows], k[rows], v[rows], g[rows], sel, masks,
                                            st_ref, d, reverse, levels)


def _pairs(refs):
    return tuple(zip(refs[0::2], refs[1::2]))


def _gla_kernel(qf, kf, vf, lf, qb, kb, vb, lb, wlr, blr, *rest):
    tabs, (of_ref, ob_ref, st_ref) = _pairs(rest[:-3]), rest[-3:]

    @pl.when(pl.program_id(1) == 0)
    def _():
        st_ref[...] = jnp.zeros_like(st_ref)

    dirs = ((qf, kf, vf, lf, of_ref, False), (qb, kb, vb, lb, ob_ref, True))
    for d, (q_ref, k_ref, v_ref, l_ref, o_ref, reverse) in enumerate(dirs):
        pre = _mm(l_ref[...], wlr[d]) + blr[d:d + 1, :]
        g = _log_sigmoid(pre) * (1.0 / GLA_GATE_NORM)
        q = q_ref[...] * (GLA_DK ** -0.5)
        _gla_tile(q, k_ref[...], v_ref[...], g, tabs, st_ref, o_ref, d, reverse)


def _hg_kernel(qf, ff, vf, qb, fb, vb, lbp, *rest):
    tabs, (of_ref, ob_ref, st_ref) = _pairs(rest[:-3]), rest[-3:]

    @pl.when(pl.program_id(1) == 0)
    def _():
        st_ref[...] = jnp.zeros_like(st_ref)

    dirs = ((qf, ff, vf, of_ref, False), (qb, fb, vb, ob_ref, True))
    for d, (q_ref, f_ref, v_ref, o_ref, reverse) in enumerate(dirs):
        f = f_ref[...]
        lb = lbp[d:d + 1, :]
        log_lb = lbp[2 + d:3 + d, :]
        log_1m = lbp[4 + d:5 + d, :]
        c = log_1m + _log_sigmoid(f)
        g = jnp.maximum(log_lb, c) + jnp.log1p(jnp.exp(-jnp.abs(log_lb - c)))
        k = (1.0 - lb) * jax.nn.sigmoid(-f)
        q = _silu(q_ref[...]) * (HEAD_W ** -0.5)
        _gla_tile(q, k, v_ref[...], g, tabs, st_ref, o_ref, d, reverse)


def _mixer_specs(widths_cols, tile_fn):
    return [pl.BlockSpec((None, TM, w), lambda b, s, c=c: (b, tile_fn(s), c)) for w, c in widths_cols]


def _gla_call(z_gla, wlr, blr, tables, nc):
    bsz, n, _ = z_gla.shape
    nt = n // TM
    w = BRANCH_W
    fwd = lambda s: s
    bwd = lambda s: _bwd_tile(s, nc, nt)
    cols = [(w, 0), (w, 1), (w, 2), (HEAD_W, 4 * w // HEAD_W)]
    out = lambda fn: pl.BlockSpec((None, TM, w), lambda b, s: (b, fn(s), 0))
    return pl.pallas_call(
        _gla_kernel,
        grid=(bsz, nt),
        in_specs=_mixer_specs(cols, fwd) + _mixer_specs(cols, bwd) + [
            _const_spec(wlr.shape), _const_spec(blr.shape)] + [_const_spec(t.shape) for t in tables],
        out_specs=[out(fwd), out(bwd)],
        out_shape=[jax.ShapeDtypeStruct((bsz, n, w), F32)] * 2,
        scratch_shapes=[pltpu.VMEM((2, N_HEADS, HEAD_W, HEAD_W), F32)],
        compiler_params=_params(("parallel", "arbitrary")),
        name="gla",
    )(*([z_gla] * 8), wlr, blr, *tables)


def _hg_call(z_hg, lbp, tables, nc):
    bsz, n, _ = z_hg.shape
    nt = n // TM
    w = BRANCH_W
    fwd = lambda s: s
    bwd = lambda s: _bwd_tile(s, nc, nt)
    out = lambda fn: pl.BlockSpec((None, TM, w), lambda b, s: (b, fn(s), 0))
    return pl.pallas_call(
        _hg_kernel,
        grid=(bsz, nt),
        in_specs=_mixer_specs([(w, 0), (w, 2), (w, 1)], fwd) + _mixer_specs([(w, 0), (w, 3), (w, 1)], bwd) + [
            _const_spec(lbp.shape)] + [_const_spec(t.shape) for t in tables],
        out_specs=[out(fwd), out(bwd)],
        out_shape=[jax.ShapeDtypeStruct((bsz, n, w), F32)] * 2,
        scratch_shapes=[pltpu.VMEM((2, N_HEADS, HEAD_W, HEAD_W), F32)],
        compiler_params=_params(("parallel", "arbitrary")),
        name="hgrn2",
    )(*([z_hg] * 6), lbp, *tables)


def _cummax_rows(a, reverse):
    n = a.shape[0]
    row = lax.broadcasted_iota(jnp.int32, a.shape, 0)
    k = 1
    while k < n:
        if reverse:
            a_s, ok = pltpu.roll(a, n - k, 0), row < n - k
        else:
            a_s, ok = pltpu.roll(a, k, 0), row >= k
        a = jnp.maximum(a, jnp.where(ok, a_s, -jnp.inf))
        k *= 2
    return a


def _ml_chunk(q, k, v, gi, log_f, tri_sel, tri_mask, ct_ref, m_ref, d, reverse):
    b = _mm_sel(tri_sel, log_f)
    a = gi - b
    m_prev = m_ref[d:d + 1, :]
    m_rel = jnp.maximum(_cummax_rows(a, reverse), m_prev)
    w_inter = jnp.exp(m_prev - m_rel)
    exp_neg_m = jnp.exp(-(b + m_rel))
    last = 0 if reverse else CHUNK - 1
    b_end = b[last:last + 1, :]
    m_new = b_end + m_rel[last:last + 1, :]
    w_end = jnp.exp(b_end - b + gi - m_new)
    decay = jnp.exp(b_end + m_prev - m_new)
    m_ref[d:d + 1, :] = m_new
    a_t = a.T
    ones = jnp.ones((CHUNK, HEAD_W), BF16)
    outs = []
    for h in range(N_HEADS):
        cols = slice(h * HEAD_W, (h + 1) * HEAD_W)
        qh, kh, vh = q[:, cols], k[:, cols], v[:, cols]
        c = d * N_HEADS + h
        log_w = jnp.where(tri_mask > 0.0, a_t[c:c + 1, :] - m_rel[:, c:c + 1], -jnp.inf)
        s = _mm_nt(qh, kh) * jnp.exp(log_w)
        v_ext = jnp.concatenate([vh.astype(BF16), ones], axis=-1)
        ct = ct_ref[c]
        tot = _mm(s, v_ext) + w_inter[:, c:c + 1] * _mm_nt(qh, ct)
        num, den = tot[:, :HEAD_W], tot[:, HEAD_W:]
        outs.append(num / jnp.maximum(jnp.abs(den), exp_neg_m[:, c:c + 1]))
        ct_ref[c] = decay[:, c:c + 1] * ct + _mm_tn(v_ext, kh * w_end[:, c:c + 1])
    return jnp.concatenate(outs, axis=-1)


def _ml_kernel(xf, xf_p, xf_n, vf, gif, gff, xb, xb_p, xb_n, vb, gib, gfb, cw, cb, gbias,
               tri_sel, tri_mask, of_ref, ob_ref, pad_ref, ct_ref, m_ref, *, nc, nt):
    s = pl.program_id(1)

    @pl.when(s == 0)
    def _():
        ct_ref[...] = jnp.zeros_like(ct_ref)
        m_ref[...] = jnp.zeros_like(m_ref)

    dirs = ((xf, xf_p, xf_n, vf, gif, gff, of_ref, s, False),
            (xb, xb_p, xb_n, vb, gib, gfb, ob_ref, _bwd_tile(s, nc, nt), True))
    for d, (x_ref, p_ref, nx_ref, v_ref, gi_ref, gf_ref, o_ref, tile, reverse) in enumerate(dirs):
        qk = _silu(_conv_tile(x_ref, p_ref, nx_ref, pad_ref, cw, cb, tile, nc, nt))
        q = qk[:, :BRANCH_W] * (HEAD_W ** -0.5)
        k = qk[:, BRANCH_W:]
        v = v_ref[...]
        gi = gi_ref[...] + gbias[0:1, :]
        log_f = _log_sigmoid(gf_ref[...] + gbias[1:2, :])
        order = range(TM // CHUNK - 1, -1, -1) if reverse else range(TM // CHUNK)
        for c in order:
            rows = slice(c * CHUNK, (c + 1) * CHUNK)
            o_ref[rows, :] = _ml_chunk(q[rows], k[rows], v[rows], gi[rows], log_f[rows],
                                       tri_sel[d], tri_mask[d], ct_ref, m_ref, d, reverse)


def _ml_call(z_ml, cw, cb, gbias, tri_sel, tri_mask, nc):
    bsz, n, _ = z_ml.shape
    nt = n // TM
    w = BRANCH_W
    fwd = lambda s: s
    bwd = lambda s: _bwd_tile(s, nc, nt)
    out = lambda fn: pl.BlockSpec((None, TM, w), lambda b, s: (b, fn(s), 0))
    gate_col = 4 * w // HEAD_W

    def side(fn):
        return _halo_specs(2 * w, 0, fn, n) + _mixer_specs(
            [(w, 2), (HEAD_W, gate_col), (HEAD_W, gate_col + 1)], fn)

    return pl.pallas_call(
        functools.partial(_ml_kernel, nc=nc, nt=nt),
        grid=(bsz, nt),
        in_specs=side(fwd) + side(bwd) + [
            _const_spec((CONV_W, 2 * w)), _const_spec((1, 2 * w)), _const_spec((2, HEAD_W)),
            _const_spec(tri_sel.shape), _const_spec(tri_mask.shape)],
        out_specs=[out(fwd), out(bwd)],
        out_shape=[jax.ShapeDtypeStruct((bsz, n, w), F32)] * 2,
        scratch_shapes=[pltpu.VMEM((TM + 2 * SUBLANES, 2 * w), F32),
                        pltpu.VMEM((2 * N_HEADS, 2 * HEAD_W, HEAD_W), F32),
                        pltpu.VMEM((2, HEAD_W), F32)],
        compiler_params=_params(("parallel", "arbitrary")),
        name="mlstm",
    )(*([z_ml] * 12), cw, cb, gbias, tri_sel, tri_mask)


def _gelu_tanh(x):
    return 0.5 * x * (1.0 + jnp.tanh(0.7978845608028654 * (x + 0.044715 * (x * x * x))))


def _route(logits):
    col = lax.broadcasted_iota(jnp.int32, logits.shape, 1)
    colf = col.astype(F32)
    is_g = jnp.logical_and(col >= N_EXPERTS, col < N_EXPERTS + N_GROUPS)
    is_e = col < N_EXPERTS
    neg = -jnp.inf
    big = 1e9
    gl = jnp.where(is_g, logits, neg)
    g_max = jnp.max(gl, axis=-1, keepdims=True)
    grp = jnp.min(jnp.where(gl == g_max, colf, big), axis=-1, keepdims=True) - N_EXPERTS
    p_grp = 1.0 / jnp.sum(jnp.exp(gl - g_max), axis=-1, keepdims=True)
    col_grp = lax.shift_right_logical(col, EXPERTS_PER_GROUP.bit_length() - 1)
    in_grp = jnp.logical_and(is_e, col_grp.astype(F32) == grp)
    e1 = jnp.where(in_grp, logits, neg)
    top1 = jnp.max(e1, axis=-1, keepdims=True)
    idx1 = jnp.min(jnp.where(e1 == top1, colf, big), axis=-1, keepdims=True)
    e2 = jnp.where(colf == idx1, neg, e1)
    top2 = jnp.max(e2, axis=-1, keepdims=True)
    idx2 = jnp.min(jnp.where(e2 == top2, colf, big), axis=-1, keepdims=True)
    t = jnp.exp(top2 - top1)
    w1 = 1.0 / (1.0 + t)
    w2 = t / (1.0 + t)
    return p_grp * jnp.where(colf == idx1, w1, jnp.where(colf == idx2, w2, 0.0))


def _stage_c_kernel(u_ref, h_ref, mod_ref, rg_y, rg_f, rg_b, gla_g, gla_f, gla_b, hg_g, hg_f, hg_b,
                    ml_o, ml_f, ml_b, gains, w_merge, b_merge, w_branch, w_out, ffn_g, w_route, b_route,
                    h_out, u2_out, wt_out):
    ys = (
        _gelu_tanh(rg_y[...]) * (rg_f[...] + rg_b[...]),
        _head_rmsnorm(gla_f[...] + gla_b[...], gains[0:1, :]) * _silu(gla_g[...]),
        _head_rmsnorm(hg_f[...] + hg_b[...], gains[1:2, :]) * _silu(hg_g[...]),
        jax.nn.sigmoid(ml_o[...]) * _head_rmsnorm(ml_f[...] + ml_b[...], gains[2:3, :]),
    )
    u = u_ref[...]
    d = u.shape[-1]
    merged = None
    for kk, y in enumerate(ys):
        gate = jax.nn.sigmoid(jnp.dot(u, w_merge[:, kk * d:(kk + 1) * d], preferred_element_type=F32)
                              + b_merge[:, kk * d:(kk + 1) * d])
        term = gate * _mm(y, w_branch[kk])
        merged = term if merged is None else merged + term
    mix = _mm(merged, w_out[...])
    h_new = h_ref[...] + mod_ref[2:3, :] * mix
    h_out[...] = h_new
    u2 = _rmsnorm_rows(h_new, ffn_g[...]) * (1.0 + mod_ref[4:5, :]) + mod_ref[3:4, :]
    u2_out[...] = u2.astype(BF16)
    logits = jnp.dot(u2, w_route[...], precision=lax.Precision.HIGHEST,
                     preferred_element_type=F32) + b_route[...]
    wt_out[...] = _route(logits)


def _stage_c_call(u, h, mod, z_rg, rg_f, rg_b, z_gla, gla_f, gla_b, z_hg, hg_f, hg_b, z_ml, ml_f, ml_b,
                  gains, w_merge, b_merge, w_branch, w_out, ffn_g, w_route, b_route, nc, t0):
    bsz, n, d = h.shape
    nt = n // TM
    w = BRANCH_W

    n_out = n - t0 * TM

    def tok(width, col=0):
        return pl.BlockSpec((None, TM, width), lambda b, t: (b, t + t0, col))

    def out_tok(width):
        return pl.BlockSpec((None, TM, width), lambda b, t: (b, t, 0))

    mod_spec = pl.BlockSpec((None, 6, d), lambda b, t: (jnp.where(t + t0 < nc, bsz, b), 0, 0))
    return pl.pallas_call(
        _stage_c_kernel,
        grid=(bsz, nt - t0),
        in_specs=[tok(d), tok(d), mod_spec,
                  tok(w, 1), tok(w), tok(w),
                  tok(w, 3), tok(w), tok(w),
                  tok(w, 4), tok(w), tok(w),
                  tok(w, 3), tok(w), tok(w),
                  _const_spec(gains.shape), _const_spec(w_merge.shape), _const_spec(b_merge.shape),
                  _const_spec(w_branch.shape), _const_spec(w_out.shape), _const_spec(ffn_g.shape),
                  _const_spec(w_route.shape), _const_spec(b_route.shape)],
        out_specs=[out_tok(d), out_tok(d), out_tok(HEAD_W)],
        out_shape=[jax.ShapeDtypeStruct((bsz, n_out, d), F32),
                   jax.ShapeDtypeStruct((bsz, n_out, d), BF16),
                   jax.ShapeDtypeStruct((bsz, n_out, HEAD_W), F32)],
        compiler_params=_params(("parallel", "parallel")),
        name="stage_c",
    )(u, h, mod, z_rg, rg_f, rg_b, z_gla, gla_f, gla_b, z_hg, hg_f, hg_b, z_ml, ml_f, ml_b,
      gains, w_merge, b_merge, w_branch, w_out, ffn_g, w_route, b_route)


def _moe_kernel(u2_ref, wt_ref, h_ref, mod_lo, mod_hi, fin_g, w1, w3, w2, o_ref, acc_ref, *, final):
    e = pl.program_id(1)

    @pl.when(e == 0)
    def _():
        acc_ref[...] = jnp.zeros_like(acc_ref)

    x = u2_ref[...]
    h1 = jnp.dot(x, w1[...], preferred_element_type=F32)
    h3 = jnp.dot(x, w3[...], preferred_element_type=F32)
    y = _mm(_silu(h1) * h3, w2[...])
    wt = wt_ref[...]
    col = lax.broadcasted_iota(jnp.int32, wt.shape, 1)
    w_e = jnp.sum(jnp.where(col == e, wt, 0.0), axis=-1, keepdims=True)
    acc_ref[...] += w_e * y

    @pl.when(e == N_EXPERTS - 1)
    def _():
        for half, mod_ref in enumerate((mod_lo, mod_hi)):
            rows = slice(half * TM, (half + 1) * TM)
            out = h_ref[rows, :] + mod_ref[5:6, :] * acc_ref[rows, :]
            if final:
                out = _rmsnorm_rows(out, fin_g[...])
            o_ref[rows, :] = out


def _moe_call(u2, wt, h, mod, fin_g, w1, w3, w2, nc, t0, final):
    bsz, n_out, d = h.shape
    de = w1.shape[-1]
    te = 2 * TM
    rows = bsz * n_out
    halves = n_out // TM

    def tok(width):
        return pl.BlockSpec((te, width), lambda t, e: (t, 0))

    def mod_spec(half):
        def index(t, e):
            piece = 2 * t + half
            b, pos = piece // halves, piece % halves
            return (jnp.where(pos + t0 < nc, bsz, b), 0, 0)
        return pl.BlockSpec((None, 6, d), index)

    out = pl.pallas_call(
        functools.partial(_moe_kernel, final=final),
        grid=(rows // te, N_EXPERTS),
        in_specs=[tok(d), tok(HEAD_W), tok(d), mod_spec(0), mod_spec(1), _const_spec((1, d)),
                  pl.BlockSpec((None, d, de), lambda t, e: (e, 0, 0)),
                  pl.BlockSpec((None, d, de), lambda t, e: (e, 0, 0)),
                  pl.BlockSpec((None, de, d), lambda t, e: (e, 0, 0))],
        out_specs=tok(d),
        out_shape=jax.ShapeDtypeStruct((rows, d), F32),
        scratch_shapes=[pltpu.VMEM((te, d), F32)],
        compiler_params=_params(("parallel", "arbitrary")),
        name="moe",
    )(u2.reshape(rows, d), wt.reshape(rows, HEAD_W), h.reshape(rows, d), mod, mod, fin_g, w1, w3, w2)
    return out.reshape(bsz, n_out, d)


def _pad_heads(w, dk):
    lead = w.shape[:-1]
    w = w.reshape(lead + (N_HEADS, dk))
    w = jnp.pad(w, [(0, 0)] * len(lead) + [(0, 0), (0, HEAD_W - dk)])
    return w.reshape(lead + (N_HEADS * HEAD_W,))


def _pad_cols(w, width):
    return jnp.pad(w, [(0, 0)] * (w.ndim - 1) + [(0, width - w.shape[-1])])


def _split_w_in(w_in):
    bw = BRANCH_W
    sizes = [bw, bw, N_HEADS * GLA_DK, N_HEADS * GLA_DK, bw, bw, 2 * GLA_RANK,
             bw, bw, 2 * bw, bw, bw, bw, bw, bw, 16]
    offs = np.cumsum([0] + sizes)
    p = [w_in[:, offs[i]:offs[i + 1]] for i in range(len(sizes))]
    w_rg = jnp.concatenate([p[0], p[1]], axis=1)
    w_gla = jnp.concatenate([_pad_heads(p[2], GLA_DK), _pad_heads(p[3], GLA_DK), p[4], p[5],
                             _pad_cols(p[6], HEAD_W)], axis=1)
    w_hg = jnp.concatenate([p[7], p[8], p[9], p[10]], axis=1)
    gates = p[15].reshape(-1, 2, 2, N_HEADS)
    w_gi = _pad_cols(gates[:, :, 0].reshape(-1, 2 * N_HEADS), HEAD_W)
    w_gf = _pad_cols(gates[:, :, 1].reshape(-1, 2 * N_HEADS), HEAD_W)
    w_ml = jnp.concatenate([p[11], p[12], p[13], p[14], w_gi, w_gf], axis=1)
    return [w.astype(BF16) for w in (w_rg, w_gla, w_hg, w_ml)]


def _block_diag(w):
    k, n = w.shape[-3], w.shape[-1]
    eye = jnp.eye(k, dtype=w.dtype)
    full = jnp.einsum('...kij,kl->...kilj', w, eye)
    return full.reshape(w.shape[:-3] + (k * n, k * n))


def kernel(x, c, ctx, c_ctx, ada_w, ada_b, norm_mix_g, norm_ffn_g, w_in, rg_conv_w, rg_conv_b, rg_gate_w, rg_gate_b, rg_lambda, gla_w_lr, gla_b_lr, gla_norm_g, hgrn_lb_logits, hgrn_norm_g, ml_conv_w, ml_conv_b, ml_gate_b, ml_norm_g, w_branch, w_merge, b_merge, w_out, moe_w_group, moe_b_group, moe_w_expert, moe_b_expert, moe_w1, moe_w3, moe_w2, final_norm_g):
    bsz, seq, d = x.shape
    n_ctx = ctx.shape[1]
    depth = ada_w.shape[0]
    assert n_ctx % TM == 0 and seq % TM == 0 and d == 2 * BRANCH_W
    nc = n_ctx // TM

    h = jnp.concatenate([ctx, x], axis=1)
    cvec = jnp.zeros((SUBLANES, d), F32).at[:bsz].set(c).at[bsz].set(c_ctx)
    mod_all = _mod_call(cvec, ada_w, ada_b).reshape(depth, SUBLANES, 6, d)[:, :bsz + 1]

    lb_cum = jnp.cumsum(jax.nn.softmax(hgrn_lb_logits.astype(F32), axis=0), axis=0)
    hgrn_lb = lb_cum - lb_cum[:1]
    tables = [t for levels in SPLIT_LEVELS for t in _decay_tables(levels)]
    tri_sel = tables[0][:, :CHUNK, :]
    tri_mask = tables[1][:, 0]

    out = None
    for l in range(depth):
        last = l == depth - 1
        mod = mod_all[l]
        w_rg, w_gla, w_hg, w_ml = _split_w_in(w_in[l])
        z_rg, z_gla, z_hg, z_ml, u = _stage_a_call(h, mod, norm_mix_g[l], w_rg, w_gla, w_hg, w_ml, nc)

        rg_f, rg_b = _rg_call(z_rg, rg_conv_w[l], rg_conv_b[l].reshape(1, -1),
                              _block_diag(rg_gate_w[l]).astype(BF16),
                              rg_gate_b[l].reshape(4, -1), rg_lambda[l], nc)

        wlr = jnp.zeros((2, HEAD_W, N_HEADS * HEAD_W), F32)
        wlr_p = _pad_heads(gla_w_lr[l], GLA_DK)
        wlr = wlr.at[0, :GLA_RANK].set(wlr_p[0]).at[1, GLA_RANK:2 * GLA_RANK].set(wlr_p[1])
        gla_f, gla_b = _gla_call(z_gla, wlr.astype(BF16), _pad_heads(gla_b_lr[l], GLA_DK), tables, nc)

        lb = hgrn_lb[l]
        lbp = jnp.concatenate([lb, jnp.log(lb), jnp.log1p(-lb), jnp.zeros((2, lb.shape[-1]), F32)], axis=0)
        hg_f, hg_b = _hg_call(z_hg, lbp, tables, nc)

        gbias = _pad_cols(ml_gate_b[l].transpose(1, 0, 2).reshape(2, -1), HEAD_W)
        ml_f, ml_b = _ml_call(z_ml, ml_conv_w[l], ml_conv_b[l].reshape(1, -1), gbias, tri_sel, tri_mask, nc)

        gains = jnp.zeros((SUBLANES, HEAD_W), F32).at[0].set(gla_norm_g[l]).at[1].set(hgrn_norm_g[l]).at[2].set(ml_norm_g[l])
        w_route = _pad_cols(jnp.concatenate([moe_w_expert[l], moe_w_group[l]], axis=1), HEAD_W)
        b_route = _pad_cols(jnp.concatenate([moe_b_expert[l], moe_b_group[l]]).reshape(1, -1), HEAD_W)
        t0 = nc if last else 0
        h_mid, u2, wt = _stage_c_call(
            u, h, mod, z_rg, rg_f, rg_b, z_gla, gla_f, gla_b, z_hg, hg_f, hg_b, z_ml, ml_f, ml_b,
            gains, w_merge[l].astype(BF16), b_merge[l].reshape(1, -1), w_branch[l].astype(BF16),
            w_out[l].astype(BF16), norm_ffn_g[l].reshape(1, -1), w_route, b_route, nc, t0)

        res = _moe_call(u2, wt, h_mid, mod, final_norm_g.reshape(1, -1), moe_w1[l].astype(BF16),
                        moe_w3[l].astype(BF16), moe_w2[l].astype(BF16), nc, t0, last)
        if last:
            out = res
        else:
            h = res
    return out
```

```python
import functools

import numpy as np
import jax
import jax.numpy as jnp
from jax import lax
from jax.experimental import pallas as pl
from jax.experimental.pallas import tpu as pltpu

F32 = jnp.float32
BF16 = jnp.bfloat16

EPS = 1e-6
TM = 256
CHUNK = 64
SUBLANES = 8
N_HEADS = 4
HEAD_W = 128
BRANCH_W = 512
CONV_W = 4
CONV_LEFT = 2
RG_C = 8.0
GLA_DK = 64
GLA_RANK = 16
GLA_GATE_NORM = 16.0
N_GROUPS = 4
EXPERTS_PER_GROUP = 4
N_EXPERTS = 16
N_LEVELS = 6
MILD_DECAY = -40.0
VMEM_LIMIT = 56 * 1024 * 1024

W_RG = 2 * BRANCH_W
W_GLA = 4 * BRANCH_W + HEAD_W
W_HG = 5 * BRANCH_W
W_ML = 4 * BRANCH_W + 2 * HEAD_W
SPLIT_LEVELS = (0, 2, N_LEVELS)


def _mm(a, b):
    return jnp.dot(a.astype(BF16), b.astype(BF16), preferred_element_type=F32)


def _mm_nt(a, b):
    return lax.dot_general(a.astype(BF16), b.astype(BF16), (((1,), (1,)), ((), ())),
                           preferred_element_type=F32)


def _mm_tn(a, b):
    return lax.dot_general(a.astype(BF16), b.astype(BF16), (((0,), (0,)), ((), ())),
                           preferred_element_type=F32)


def _mm_sel(sel, x):
    x1 = x.astype(BF16)
    x2 = (x - x1.astype(F32)).astype(BF16)
    dot = functools.partial(jnp.dot, preferred_element_type=F32)
    return dot(sel, x1) + dot(sel, x2)


def _log_sigmoid(x):
    return jnp.minimum(x, 0.0) - jnp.log1p(jnp.exp(-jnp.abs(x)))


def _silu(x):
    return x * jax.nn.sigmoid(x)


def _rmsnorm_rows(x, g):
    return x * lax.rsqrt(jnp.mean(x * x, axis=-1, keepdims=True) + EPS) * g


def _head_rmsnorm(o, g):
    parts = [_rmsnorm_rows(o[:, h * HEAD_W:(h + 1) * HEAD_W], g) for h in range(N_HEADS)]
    return jnp.concatenate(parts, axis=-1)


def _bwd_tile(s, nc, nt):
    return jnp.where(s < nc, nc - 1 - s, nt - 1 - (s - nc))


def _const_spec(shape):
    nd = len(shape)
    return pl.BlockSpec(shape, lambda *_: (0,) * nd)


def _params(sem):
    return pltpu.CompilerParams(dimension_semantics=sem, vmem_limit_bytes=VMEM_LIMIT)


def _mod_kernel(c_ref, w_ref, b_ref, o_ref):
    cv = _silu(c_ref[...])
    o_ref[...] = jnp.dot(cv, w_ref[...], precision=lax.Precision.HIGHEST,
                         preferred_element_type=F32) + b_ref[...]


def _mod_call(cvec, ada_w, ada_b):
    depth, d, six_d = ada_w.shape
    tn = 1024
    return pl.pallas_call(
        _mod_kernel,
        grid=(depth, six_d // tn),
        in_specs=[pl.BlockSpec((SUBLANES, d), lambda l, j: (0, 0)),
                  pl.BlockSpec((None, d, tn), lambda l, j: (l, 0, j)),
                  pl.BlockSpec((None, 1, tn), lambda l, j: (l, 0, j))],
        out_specs=pl.BlockSpec((None, SUBLANES, tn), lambda l, j: (l, 0, j)),
        out_shape=jax.ShapeDtypeStruct((depth, SUBLANES, six_d), F32),
        compiler_params=_params(("parallel", "parallel")),
        name="adaln_mod",
    )(cvec, ada_w, ada_b.reshape(depth, 1, six_d))


def _stage_a_kernel(h_ref, mod_ref, g_ref, w_rg, w_gla, w_hg, w_ml,
                    z_rg, z_gla, z_hg, z_ml, u_ref):
    x = h_ref[...]
    u = _rmsnorm_rows(x, g_ref[...]) * (1.0 + mod_ref[1:2, :]) + mod_ref[0:1, :]
    ub = u.astype(BF16)
    u_ref[...] = ub
    z_rg[...] = jnp.dot(ub, w_rg[...], preferred_element_type=F32)
    z_gla[...] = jnp.dot(ub, w_gla[...], preferred_element_type=F32)
    z_hg[...] = jnp.dot(ub, w_hg[...], preferred_element_type=F32)
    z_ml[...] = jnp.dot(ub, w_ml[...], preferred_element_type=F32)


def _stage_a_call(h, mod, norm_g, w_rg, w_gla, w_hg, w_ml, nc):
    bsz, n, d = h.shape
    nt = n // TM
    tm = TM // 2

    def tok(w):
        return pl.BlockSpec((None, tm, w), lambda b, t: (b, t, 0))

    return pl.pallas_call(
        _stage_a_kernel,
        grid=(bsz, n // tm),
        in_specs=[tok(d),
                  pl.BlockSpec((None, 6, d), lambda b, t: (jnp.where(t * tm < nc * TM, bsz, b), 0, 0)),
                  _const_spec((1, d)),
                  _const_spec((d, W_RG)), _const_spec((d, W_GLA)),
                  _const_spec((d, W_HG)), _const_spec((d, W_ML))],
        out_specs=[tok(W_RG), tok(W_GLA), tok(W_HG), tok(W_ML), tok(d)],
        out_shape=[jax.ShapeDtypeStruct((bsz, n, W_RG), F32),
                   jax.ShapeDtypeStruct((bsz, n, W_GLA), F32),
                   jax.ShapeDtypeStruct((bsz, n, W_HG), F32),
                   jax.ShapeDtypeStruct((bsz, n, W_ML), F32),
                   jax.ShapeDtypeStruct((bsz, n, d), BF16)],
        compiler_params=_params(("parallel", "parallel")),
        name="stage_a",
    )(h, mod, norm_g.reshape(1, d), w_rg, w_gla, w_hg, w_ml)


def _conv_tile(x_ref, prev_ref, next_ref, pad_ref, cw_ref, cb_ref, tile, nc, nt):
    prev_ok = jnp.logical_and(tile != 0, tile != nc)
    next_ok = jnp.logical_and(tile != nc - 1, tile != nt - 1)
    pad_ref[0:SUBLANES, :] = jnp.where(prev_ok, prev_ref[...], 0.0)
    pad_ref[SUBLANES:SUBLANES + TM, :] = x_ref[...]
    pad_ref[SUBLANES + TM:2 * SUBLANES + TM, :] = jnp.where(next_ok, next_ref[...], 0.0)
    acc = cb_ref[...]
    for j in range(CONV_W):
        off = SUBLANES - CONV_LEFT + j
        acc = acc + cw_ref[j:j + 1, :] * pad_ref[off:off + TM, :]
    return acc


def _halo_specs(width, col_block, tile_fn, n):
    rows = TM // SUBLANES
    last = n // SUBLANES - 1
    cur = pl.BlockSpec((None, TM, width), lambda b, s: (b, tile_fn(s), col_block))
    prev = pl.BlockSpec((None, SUBLANES, width),
                        lambda b, s: (b, jnp.maximum(tile_fn(s) * rows - 1, 0), col_block))
    nxt = pl.BlockSpec((None, SUBLANES, width),
                       lambda b, s: (b, jnp.minimum((tile_fn(s) + 1) * rows, last), col_block))
    return [cur, prev, nxt]


def _scan_rows8(a, b, reverse):
    n = a.shape[0]
    pos = jnp.bitwise_and(lax.broadcasted_iota(jnp.int32, a.shape, 0), SUBLANES - 1)
    k = 1
    while k < SUBLANES:
        if reverse:
            a_s, b_s, ok = pltpu.roll(a, n - k, 0), pltpu.roll(b, n - k, 0), pos < SUBLANES - k
        else:
            a_s, b_s, ok = pltpu.roll(a, k, 0), pltpu.roll(b, k, 0), pos >= k
        b = b + a * jnp.where(ok, b_s, 0.0)
        a = a * jnp.where(ok, a_s, 1.0)
        k *= 2
    return a, b


def _scan_tile(a, b, carry, o_ref, reverse):
    a8, b8 = _scan_rows8(a, b, reverse)
    groups = a.shape[0] // SUBLANES
    order = range(groups - 1, -1, -1) if reverse else range(groups)
    for r in order:
        rows = slice(r * SUBLANES, (r + 1) * SUBLANES)
        h = b8[rows] + a8[rows] * carry
        o_ref[rows, :] = h
        carry = h[0:1, :] if reverse else h[SUBLANES - 1:SUBLANES, :]
    return carry


def _rg_kernel(xf, xf_p, xf_n, xb, xb_p, xb_n, cw, cb, gw, gb, lam,
               hf_ref, hb_ref, pad_ref, carry_ref, *, nc, nt):
    s = pl.program_id(1)

    @pl.when(s == 0)
    def _():
        carry_ref[...] = jnp.zeros_like(carry_ref)

    dirs = ((xf, xf_p, xf_n, hf_ref, s, False),
            (xb, xb_p, xb_n, hb_ref, _bwd_tile(s, nc, nt), True))
    for d, (x_ref, p_ref, n_ref, o_ref, tile, reverse) in enumerate(dirs):
        x = _conv_tile(x_ref, p_ref, n_ref, pad_ref, cw, cb, tile, nc, nt)
        r = jax.nn.sigmoid(_mm(x, gw[d, 0]) + gb[2 * d:2 * d + 1, :])
        i = jax.nn.sigmoid(_mm(x, gw[d, 1]) + gb[2 * d + 1:2 * d + 2, :])
        lam_d = lam[d:d + 1, :]
        softplus = jnp.maximum(-lam_d, 0.0) + jnp.log1p(jnp.exp(-jnp.abs(lam_d)))
        log_a = -RG_C * r * softplus
        a = jnp.exp(log_a)
        t = jnp.tanh(log_a)
        bt = jnp.sqrt(-2.0 * t / (1.0 - t)) * (i * x)
        carry_ref[d:d + 1, :] = _scan_tile(a, bt, carry_ref[d:d + 1, :], o_ref, reverse)


def _rg_call(z_rg, cw, cb, gw, gb, lam, nc):
    bsz, n, _ = z_rg.shape
    nt = n // TM
    w = BRANCH_W
    fwd = lambda s: s
    bwd = lambda s: _bwd_tile(s, nc, nt)
    out = lambda fn: pl.BlockSpec((None, TM, w), lambda b, s: (b, fn(s), 0))
    return pl.pallas_call(
        functools.partial(_rg_kernel, nc=nc, nt=nt),
        grid=(bsz, nt),
        in_specs=_halo_specs(w, 0, fwd, n) + _halo_specs(w, 0, bwd, n) + [
            _const_spec((CONV_W, w)), _const_spec((1, w)),
            _const_spec((2, 2, w, w)), _const_spec((4, w)), _const_spec((2, w))],
        out_specs=[out(fwd), out(bwd)],
        out_shape=[jax.ShapeDtypeStruct((bsz, n, w), F32)] * 2,
        scratch_shapes=[pltpu.VMEM((TM + 2 * SUBLANES, w), F32), pltpu.VMEM((2, w), F32)],
        compiler_params=_params(("parallel", "arbitrary")),
        name="rglru",
    )(z_rg, z_rg, z_rg, z_rg, z_rg, z_rg, cw, cb, gw, gb, lam)


def _decay_tables(levels):
    n = CHUNK
    blk = n >> levels
    sel = np.zeros((levels + 3, n, n), np.float32)
    masks = np.zeros((levels + 1, n, n), np.float32)
    for lvl in range(levels):
        half = n >> (lvl + 1)
        for r in range(n):
            start = (r // (2 * half)) * 2 * half
            ref = start + half - 1
            if r - start >= half:
                sel[lvl, r, ref + 1:r + 1] = 1.0
                masks[lvl, r, start:start + half] = 1.0
            else:
                sel[lvl, r, r + 1:ref + 1] = 1.0
    sel[levels] = np.tril(np.ones((n, n), np.float32))
    sel[levels + 1] = 1.0 - sel[levels]
    for r in range(n):
        start = (r // blk) * blk
        sel[levels + 2, r, r + 1:start + blk] = 1.0
        masks[levels, r, start:r + 1] = 1.0
    if levels == 0 or levels == N_LEVELS:
        sel = sel[:levels + 2]
    sel_b = sel[:, ::-1, ::-1].reshape(-1, n)
    masks_b = masks[:, ::-1, ::-1]
    return (jnp.asarray(np.stack([sel.reshape(-1, n), sel_b]), BF16),
            jnp.asarray(np.stack([masks, masks_b]), F32))


def _ref_rows(levels, reverse):
    def runs(block, offset):
        return [(s + offset, block) for s in range(0, CHUNK, block)]

    out = [runs(CHUNK >> lvl, (CHUNK >> (lvl + 1)) - 1) for lvl in range(levels)]
    out.append(runs(CHUNK, CHUNK - 1))
    out.append(runs(CHUNK >> levels, (CHUNK >> levels) - 1))
    if reverse:
        out = [[(CHUNK - 1 - r, n) for r, n in reversed(segs)] for segs in out]
    return out


def _rows_bcast(b, segs):
    return jnp.concatenate([jnp.broadcast_to(b[r:r + 1, :], (n, b.shape[-1])) for r, n in segs], axis=0)


def _gla_chunk(q, k, v, g, sel, masks, st_ref, d, reverse, levels):
    if levels == N_LEVELS:
        x = jnp.exp(_mm_sel(sel, g))
        x_lvl = [x[lvl * CHUNK:(lvl + 1) * CHUNK] for lvl in range(levels)]
        x_cum = x[levels * CHUNK:(levels + 1) * CHUNK]
        x_rest = x[(levels + 1) * CHUNK:(levels + 2) * CHUNK]
        xk_blk = xq_blk = None
    else:
        b = _mm_sel(sel[levels * CHUNK:(levels + 1) * CHUNK], g)
        refs = _ref_rows(levels, reverse)
        x_cum = jnp.exp(b)
        x_lvl = [jnp.exp(-jnp.abs(b - _rows_bcast(b, segs))) for segs in refs[:levels]]
        x_rest = jnp.exp(-jnp.abs(b - _rows_bcast(b, refs[levels])))
        if levels == 0:
            xk_blk, xq_blk = x_rest, jnp.exp(jnp.abs(b - _rows_bcast(b, refs[levels])))
        else:
            d_blk = jnp.abs(b - _rows_bcast(b, refs[levels + 1]))
            xk_blk, xq_blk = jnp.exp(-d_blk), jnp.exp(d_blk)
    outs = []
    for h in range(N_HEADS):
        cols = slice(h * HEAD_W, (h + 1) * HEAD_W)
        qh, kh, vh = q[:, cols], k[:, cols], v[:, cols]
        kx = kh * x_rest[:, cols]
        if xk_blk is None:
            sc = _mm_nt(qh, kh)
        elif levels == 0:
            sc = _mm_nt(qh * xq_blk[:, cols], kx)
        else:
            sc = _mm_nt(qh * xq_blk[:, cols], kh * xk_blk[:, cols])
        sc = jnp.where(masks[levels] > 0.0, sc, 0.0)
        for lvl in range(levels):
            xl = x_lvl[lvl][:, cols]
            sc = sc + jnp.where(masks[lvl] > 0.0, _mm_nt(qh * xl, kh * xl), 0.0)
        st = st_ref[d, h]
        outs.append(_mm(sc, vh) + _mm_nt(qh * x_cum[:, cols], st))
        x_end = x_cum[0:1, cols] if reverse else x_cum[CHUNK - 1:CHUNK, cols]
        st_ref[d, h] = st * x_end + _mm_tn(vh, kx)
    return jnp.concatenate(outs, axis=-1)


def _gla_step(sides, tabs, st_ref):
    n_chunks = TM // CHUNK

    def min_block_sum(block):
        mins = [jnp.min(jnp.sum(g.reshape(TM // block, block, g.shape[-1]), axis=1))
                for _, _, _, g, _ in sides]
        return jnp.minimum(mins[0], mins[1])

    conds = []
    taken = None
    for levels in SPLIT_LEVELS[:-1]:
        ok = min_block_sum(CHUNK >> levels) > MILD_DECAY
        conds.append(ok if taken is None else jnp.logical_and(ok, jnp.logical_not(taken)))
        taken = ok if taken is None else jnp.logical_or(taken, ok)
    conds.append(jnp.logical_not(taken))

    for levels, cond, (sel_ref, mask_ref) in zip(SPLIT_LEVELS, conds, tabs):
        @pl.when(cond)
        def _(levels=levels, sel_ref=sel_ref, mask_ref=mask_ref):
            for c in range(n_chunks):
                for d, (q, k, v, g, o_ref) in enumerate(sides):
                    reverse = d == 1
                    cc = n_chunks - 1 - c if reverse else c
                    rows = slice(cc * CHUNK, (cc + 1) * CHUNK)
                    o_ref[rows, :] = _gla_chunk(q[rows], k[rows], v[rows], g[rows], sel_ref[d],
                                                mask_ref[d], st_ref, d, reverse, levels)


def _pairs(refs):
    return tuple(zip(refs[0::2], refs[1::2]))


def _gla_kernel(zf, zb, wlr, blr, *rest):
    tabs, (of_ref, ob_ref, st_ref) = _pairs(rest[:-3]), rest[-3:]
    w = BRANCH_W

    @pl.when(pl.program_id(1) == 0)
    def _():
        st_ref[...] = jnp.zeros_like(st_ref)

    sides = []
    for d, (z_ref, o_ref) in enumerate(((zf, of_ref), (zb, ob_ref))):
        pre = _mm(z_ref[:, 4 * w:4 * w + HEAD_W], wlr[d]) + blr[d:d + 1, :]
        g = _log_sigmoid(pre) * (1.0 / GLA_GATE_NORM)
        q = z_ref[:, 0:w] * (GLA_DK ** -0.5)
        sides.append((q, z_ref[:, w:2 * w], z_ref[:, 2 * w:3 * w], g, o_ref))
    _gla_step(sides, tabs, st_ref)


def _hg_kernel(zf, zb, lbp, *rest):
    tabs, (of_ref, ob_ref, st_ref) = _pairs(rest[:-3]), rest[-3:]
    w = BRANCH_W

    @pl.when(pl.program_id(1) == 0)
    def _():
        st_ref[...] = jnp.zeros_like(st_ref)

    sides = []
    for d, (z_ref, o_ref) in enumerate(((zf, of_ref), (zb, ob_ref))):
        f = z_ref[:, (2 + d) * w:(3 + d) * w]
        lb = lbp[d:d + 1, :]
        log_lb = lbp[2 + d:3 + d, :]
        log_1m = lbp[4 + d:5 + d, :]
        c = log_1m + _log_sigmoid(f)
        g = jnp.maximum(log_lb, c) + jnp.log1p(jnp.exp(-jnp.abs(log_lb - c)))
        k = (1.0 - lb) * jax.nn.sigmoid(-f)
        q = _silu(z_ref[:, 0:w]) * (HEAD_W ** -0.5)
        sides.append((q, k, z_ref[:, w:2 * w], g, o_ref))
    _gla_step(sides, tabs, st_ref)


def _mixer_specs(widths_cols, tile_fn):
    return [pl.BlockSpec((None, TM, w), lambda b, s, c=c: (b, tile_fn(s), c)) for w, c in widths_cols]


def _gla_call(z_gla, wlr, blr, tables, nc):
    bsz, n, _ = z_gla.shape
    nt = n // TM
    w = BRANCH_W
    fwd = lambda s: s
    bwd = lambda s: _bwd_tile(s, nc, nt)
    cols = [(W_GLA, 0)]
    out = lambda fn: pl.BlockSpec((None, TM, w), lambda b, s: (b, fn(s), 0))
    return pl.pallas_call(
        _gla_kernel,
        grid=(bsz, nt),
        in_specs=_mixer_specs(cols, fwd) + _mixer_specs(cols, bwd) + [
            _const_spec(wlr.shape), _const_spec(blr.shape)] + [_const_spec(t.shape) for t in tables],
        out_specs=[out(fwd), out(bwd)],
        out_shape=[jax.ShapeDtypeStruct((bsz, n, w), F32)] * 2,
        scratch_shapes=[pltpu.VMEM((2, N_HEADS, HEAD_W, HEAD_W), F32)],
        compiler_params=_params(("parallel", "arbitrary")),
        name="gla",
    )(z_gla, z_gla, wlr, blr, *tables)


def _hg_call(z_hg, lbp, tables, nc):
    bsz, n, _ = z_hg.shape
    nt = n // TM
    w = BRANCH_W
    fwd = lambda s: s
    bwd = lambda s: _bwd_tile(s, nc, nt)
    out = lambda fn: pl.BlockSpec((None, TM, w), lambda b, s: (b, fn(s), 0))
    return pl.pallas_call(
        _hg_kernel,
        grid=(bsz, nt),
        in_specs=_mixer_specs([(4 * w, 0)], fwd) + _mixer_specs([(4 * w, 0)], bwd) + [
            _const_spec(lbp.shape)] + [_const_spec(t.shape) for t in tables],
        out_specs=[out(fwd), out(bwd)],
        out_shape=[jax.ShapeDtypeStruct((bsz, n, w), F32)] * 2,
        scratch_shapes=[pltpu.VMEM((2, N_HEADS, HEAD_W, HEAD_W), F32)],
        compiler_params=_params(("parallel", "arbitrary")),
        name="hgrn2",
    )(z_hg, z_hg, lbp, *tables)


def _cummax_rows(a, reverse):
    n = a.shape[0]
    row = lax.broadcasted_iota(jnp.int32, a.shape, 0)
    k = 1
    while k < n:
        if reverse:
            a_s, ok = pltpu.roll(a, n - k, 0), row < n - k
        else:
            a_s, ok = pltpu.roll(a, k, 0), row >= k
        a = jnp.maximum(a, jnp.where(ok, a_s, -jnp.inf))
        k *= 2
    return a


def _ml_chunk(q, k, v, gi, log_f, tri_sel, tri_mask, ct_ref, m_ref, d, reverse):
    b = _mm_sel(tri_sel, log_f)
    a = gi - b
    m_prev = m_ref[d:d + 1, :]
    m_rel = jnp.maximum(_cummax_rows(a, reverse), m_prev)
    w_inter = jnp.exp(m_prev - m_rel)
    exp_neg_m = jnp.exp(-(b + m_rel))
    last = 0 if reverse else CHUNK - 1
    b_end = b[last:last + 1, :]
    m_new = b_end + m_rel[last:last + 1, :]
    w_end = jnp.exp(b_end - b + gi - m_new)
    decay = jnp.exp(b_end + m_prev - m_new)
    m_ref[d:d + 1, :] = m_new
    a_t = a.T
    ones = jnp.ones((CHUNK, HEAD_W), BF16)
    outs = []
    for h in range(N_HEADS):
        cols = slice(h * HEAD_W, (h + 1) * HEAD_W)
        qh, kh, vh = q[:, cols], k[:, cols], v[:, cols]
        c = d * N_HEADS + h
        log_w = jnp.where(tri_mask > 0.0, a_t[c:c + 1, :] - m_rel[:, c:c + 1], -jnp.inf)
        s = _mm_nt(qh, kh) * jnp.exp(log_w)
        v_ext = jnp.concatenate([vh.astype(BF16), ones], axis=-1)
        ct = ct_ref[c]
        tot = _mm(s, v_ext) + w_inter[:, c:c + 1] * _mm_nt(qh, ct)
        num, den = tot[:, :HEAD_W], tot[:, HEAD_W:]
        outs.append(num / jnp.maximum(jnp.abs(den), exp_neg_m[:, c:c + 1]))
        ct_ref[c] = decay[:, c:c + 1] * ct + _mm_tn(v_ext, kh * w_end[:, c:c + 1])
    return jnp.concatenate(outs, axis=-1)


def _ml_kernel(zf, zf_p, zf_n, zb, zb_p, zb_n, cw, cb, gbias,
               tri_sel, tri_mask, of_ref, ob_ref, pad_ref, ct_ref, m_ref, *, nc, nt):
    s = pl.program_id(1)
    w = BRANCH_W

    @pl.when(s == 0)
    def _():
        ct_ref[...] = jnp.zeros_like(ct_ref)
        m_ref[...] = jnp.zeros_like(m_ref)

    dirs = ((zf, zf_p, zf_n, of_ref, s, False),
            (zb, zb_p, zb_n, ob_ref, _bwd_tile(s, nc, nt), True))
    for d, (z_ref, p_ref, nx_ref, o_ref, tile, reverse) in enumerate(dirs):
        qk = _silu(_conv_tile(z_ref.at[:, 0:2 * w], p_ref, nx_ref, pad_ref, cw, cb, tile, nc, nt))
        q = qk[:, :w] * (HEAD_W ** -0.5)
        k = qk[:, w:]
        v = z_ref[:, 2 * w:3 * w]
        gi = z_ref[:, 4 * w:4 * w + HEAD_W] + gbias[0:1, :]
        log_f = _log_sigmoid(z_ref[:, 4 * w + HEAD_W:4 * w + 2 * HEAD_W] + gbias[1:2, :])
        order = range(TM // CHUNK - 1, -1, -1) if reverse else range(TM // CHUNK)
        for c in order:
            rows = slice(c * CHUNK, (c + 1) * CHUNK)
            o_ref[rows, :] = _ml_chunk(q[rows], k[rows], v[rows], gi[rows], log_f[rows],
                                       tri_sel[d], tri_mask[d], ct_ref, m_ref, d, reverse)


def _ml_call(z_ml, cw, cb, gbias, tri_sel, tri_mask, nc):
    bsz, n, _ = z_ml.shape
    nt = n // TM
    w = BRANCH_W
    fwd = lambda s: s
    bwd = lambda s: _bwd_tile(s, nc, nt)
    out = lambda fn: pl.BlockSpec((None, TM, w), lambda b, s: (b, fn(s), 0))

    def side(fn):
        return _mixer_specs([(W_ML, 0)], fn) + _halo_specs(2 * w, 0, fn, n)[1:]

    return pl.pallas_call(
        functools.partial(_ml_kernel, nc=nc, nt=nt),
        grid=(bsz, nt),
        in_specs=side(fwd) + side(bwd) + [
            _const_spec((CONV_W, 2 * w)), _const_spec((1, 2 * w)), _const_spec((2, HEAD_W)),
            _const_spec(tri_sel.shape), _const_spec(tri_mask.shape)],
        out_specs=[out(fwd), out(bwd)],
        out_shape=[jax.ShapeDtypeStruct((bsz, n, w), F32)] * 2,
        scratch_shapes=[pltpu.VMEM((TM + 2 * SUBLANES, 2 * w), F32),
                        pltpu.VMEM((2 * N_HEADS, 2 * HEAD_W, HEAD_W), F32),
                        pltpu.VMEM((2, HEAD_W), F32)],
        compiler_params=_params(("parallel", "arbitrary")),
        name="mlstm",
    )(*([z_ml] * 6), cw, cb, gbias, tri_sel, tri_mask)


def _gelu_tanh(x):
    return 0.5 * x * (1.0 + jnp.tanh(0.7978845608028654 * (x + 0.044715 * (x * x * x))))


def _route(logits):
    col = lax.broadcasted_iota(jnp.int32, logits.shape, 1)
    colf = col.astype(F32)
    is_g = jnp.logical_and(col >= N_EXPERTS, col < N_EXPERTS + N_GROUPS)
    is_e = col < N_EXPERTS
    neg = -jnp.inf
    big = 1e9
    gl = jnp.where(is_g, logits, neg)
    g_max = jnp.max(gl, axis=-1, keepdims=True)
    grp = jnp.min(jnp.where(gl == g_max, colf, big), axis=-1, keepdims=True) - N_EXPERTS
    p_grp = 1.0 / jnp.sum(jnp.exp(gl - g_max), axis=-1, keepdims=True)
    col_grp = lax.shift_right_logical(col, EXPERTS_PER_GROUP.bit_length() - 1)
    in_grp = jnp.logical_and(is_e, col_grp.astype(F32) == grp)
    e1 = jnp.where(in_grp, logits, neg)
    top1 = jnp.max(e1, axis=-1, keepdims=True)
    idx1 = jnp.min(jnp.where(e1 == top1, colf, big), axis=-1, keepdims=True)
    e2 = jnp.where(colf == idx1, neg, e1)
    top2 = jnp.max(e2, axis=-1, keepdims=True)
    idx2 = jnp.min(jnp.where(e2 == top2, colf, big), axis=-1, keepdims=True)
    t = jnp.exp(top2 - top1)
    w1 = 1.0 / (1.0 + t)
    w2 = t / (1.0 + t)
    return p_grp * jnp.where(colf == idx1, w1, jnp.where(colf == idx2, w2, 0.0))


def _stage_c_kernel(u_ref, h_ref, mod_ref, rg_y, rg_f, rg_b, gla_g, gla_f, gla_b, hg_g, hg_f, hg_b,
                    ml_o, ml_f, ml_b, gains, w_merge, b_merge, w_branch, w_out, ffn_g, w_route, b_route,
                    h_out, u2_out, wt_out):
    ys = (
        _gelu_tanh(rg_y[...]) * (rg_f[...] + rg_b[...]),
        _head_rmsnorm(gla_f[...] + gla_b[...], gains[0:1, :]) * _silu(gla_g[...]),
        _head_rmsnorm(hg_f[...] + hg_b[...], gains[1:2, :]) * _silu(hg_g[...]),
        jax.nn.sigmoid(ml_o[...]) * _head_rmsnorm(ml_f[...] + ml_b[...], gains[2:3, :]),
    )
    u = u_ref[...]
    d = u.shape[-1]
    merged = None
    for kk, y in enumerate(ys):
        gate = jax.nn.sigmoid(jnp.dot(u, w_merge[:, kk * d:(kk + 1) * d], preferred_element_type=F32)
                              + b_merge[:, kk * d:(kk + 1) * d])
        term = gate * _mm(y, w_branch[kk])
        merged = term if merged is None else merged + term
    mix = _mm(merged, w_out[...])
    h_new = h_ref[...] + mod_ref[2:3, :] * mix
    h_out[...] = h_new
    u2 = _rmsnorm_rows(h_new, ffn_g[...]) * (1.0 + mod_ref[4:5, :]) + mod_ref[3:4, :]
    u2_out[...] = u2.astype(BF16)
    logits = jnp.dot(u2, w_route[...], precision=lax.Precision.HIGHEST,
                     preferred_element_type=F32) + b_route[...]
    wt_out[...] = _route(logits)


def _stage_c_call(u, h, mod, z_rg, rg_f, rg_b, z_gla, gla_f, gla_b, z_hg, hg_f, hg_b, z_ml, ml_f, ml_b,
                  gains, w_merge, b_merge, w_branch, w_out, ffn_g, w_route, b_route, nc, t0):
    bsz, n, d = h.shape
    nt = n // TM
    w = BRANCH_W

    n_out = n - t0 * TM

    def tok(width, col=0):
        return pl.BlockSpec((None, TM, width), lambda b, t: (b, t + t0, col))

    def out_tok(width):
        return pl.BlockSpec((None, TM, width), lambda b, t: (b, t, 0))

    mod_spec = pl.BlockSpec((None, 6, d), lambda b, t: (jnp.where(t + t0 < nc, bsz, b), 0, 0))
    return pl.pallas_call(
        _stage_c_kernel,
        grid=(bsz, nt - t0),
        in_specs=[tok(d), tok(d), mod_spec,
                  tok(w, 1), tok(w), tok(w),
                  tok(w, 3), tok(w), tok(w),
                  tok(w, 4), tok(w), tok(w),
                  tok(w, 3), tok(w), tok(w),
                  _const_spec(gains.shape), _const_spec(w_merge.shape), _const_spec(b_merge.shape),
                  _const_spec(w_branch.shape), _const_spec(w_out.shape), _const_spec(ffn_g.shape),
                  _const_spec(w_route.shape), _const_spec(b_route.shape)],
        out_specs=[out_tok(d), out_tok(d), out_tok(HEAD_W)],
        out_shape=[jax.ShapeDtypeStruct((bsz, n_out, d), F32),
                   jax.ShapeDtypeStruct((bsz, n_out, d), BF16),
                   jax.ShapeDtypeStruct((bsz, n_out, HEAD_W), F32)],
        compiler_params=_params(("parallel", "parallel")),
        name="stage_c",
    )(u, h, mod, z_rg, rg_f, rg_b, z_gla, gla_f, gla_b, z_hg, hg_f, hg_b, z_ml, ml_f, ml_b,
      gains, w_merge, b_merge, w_branch, w_out, ffn_g, w_route, b_route)


def _moe_kernel(u2_ref, wt_ref, h_ref, mod_lo, mod_hi, fin_g, w1, w3, w2, o_ref, acc_ref, *, final):
    e = pl.program_id(1)

    @pl.when(e == 0)
    def _():
        acc_ref[...] = jnp.zeros_like(acc_ref)

    x = u2_ref[...]
    h1 = jnp.dot(x, w1[...], preferred_element_type=F32)
    h3 = jnp.dot(x, w3[...], preferred_element_type=F32)
    y = _mm(_silu(h1) * h3, w2[...])
    wt = wt_ref[...]
    col = lax.broadcasted_iota(jnp.int32, wt.shape, 1)
    w_e = jnp.sum(jnp.where(col == e, wt, 0.0), axis=-1, keepdims=True)
    acc_ref[...] += w_e * y

    @pl.when(e == N_EXPERTS - 1)
    def _():
        for half, mod_ref in enumerate((mod_lo, mod_hi)):
            rows = slice(half * TM, (half + 1) * TM)
            out = h_ref[rows, :] + mod_ref[5:6, :] * acc_ref[rows, :]
            if final:
                out = _rmsnorm_rows(out, fin_g[...])
            o_ref[rows, :] = out


def _moe_call(u2, wt, h, mod, fin_g, w1, w3, w2, nc, t0, final):
    bsz, n_out, d = h.shape
    de = w1.shape[-1]
    te = 2 * TM
    rows = bsz * n_out
    halves = n_out // TM

    def tok(width):
        return pl.BlockSpec((te, width), lambda t, e: (t, 0))

    def mod_spec(half):
        def index(t, e):
            piece = 2 * t + half
            b, pos = piece // halves, piece % halves
            return (jnp.where(pos + t0 < nc, bsz, b), 0, 0)
        return pl.BlockSpec((None, 6, d), index)

    out = pl.pallas_call(
        functools.partial(_moe_kernel, final=final),
        grid=(rows // te, N_EXPERTS),
        in_specs=[tok(d), tok(HEAD_W), tok(d), mod_spec(0), mod_spec(1), _const_spec((1, d)),
                  pl.BlockSpec((None, d, de), lambda t, e: (e, 0, 0)),
                  pl.BlockSpec((None, d, de), lambda t, e: (e, 0, 0)),
                  pl.BlockSpec((None, de, d), lambda t, e: (e, 0, 0))],
        out_specs=tok(d),
        out_shape=jax.ShapeDtypeStruct((rows, d), F32),
        scratch_shapes=[pltpu.VMEM((te, d), F32)],
        compiler_params=_params(("parallel", "arbitrary")),
        name="moe",
    )(u2.reshape(rows, d), wt.reshape(rows, HEAD_W), h.reshape(rows, d), mod, mod, fin_g, w1, w3, w2)
    return out.reshape(bsz, n_out, d)


def _pad_heads(w, dk):
    lead = w.shape[:-1]
    w = w.reshape(lead + (N_HEADS, dk))
    w = jnp.pad(w, [(0, 0)] * len(lead) + [(0, 0), (0, HEAD_W - dk)])
    return w.reshape(lead + (N_HEADS * HEAD_W,))


def _pad_cols(w, width):
    return jnp.pad(w, [(0, 0)] * (w.ndim - 1) + [(0, width - w.shape[-1])])


def _split_w_in(w_in):
    bw = BRANCH_W
    sizes = [bw, bw, N_HEADS * GLA_DK, N_HEADS * GLA_DK, bw, bw, 2 * GLA_RANK,
             bw, bw, 2 * bw, bw, bw, bw, bw, bw, 16]
    offs = np.cumsum([0] + sizes)
    p = [w_in[:, offs[i]:offs[i + 1]] for i in range(len(sizes))]
    w_rg = jnp.concatenate([p[0], p[1]], axis=1)
    w_gla = jnp.concatenate([_pad_heads(p[2], GLA_DK), _pad_heads(p[3], GLA_DK), p[4], p[5],
                             _pad_cols(p[6], HEAD_W)], axis=1)
    w_hg = jnp.concatenate([p[7], p[8], p[9], p[10]], axis=1)
    gates = p[15].reshape(-1, 2, 2, N_HEADS)
    w_gi = _pad_cols(gates[:, :, 0].reshape(-1, 2 * N_HEADS), HEAD_W)
    w_gf = _pad_cols(gates[:, :, 1].reshape(-1, 2 * N_HEADS), HEAD_W)
    w_ml = jnp.concatenate([p[11], p[12], p[13], p[14], w_gi, w_gf], axis=1)
    return [w.astype(BF16) for w in (w_rg, w_gla, w_hg, w_ml)]


def _block_diag(w):
    k, n = w.shape[-3], w.shape[-1]
    eye = jnp.eye(k, dtype=w.dtype)
    full = jnp.einsum('...kij,kl->...kilj', w, eye)
    return full.reshape(w.shape[:-3] + (k * n, k * n))


def kernel(x, c, ctx, c_ctx, ada_w, ada_b, norm_mix_g, norm_ffn_g, w_in, rg_conv_w, rg_conv_b, rg_gate_w, rg_gate_b, rg_lambda, gla_w_lr, gla_b_lr, gla_norm_g, hgrn_lb_logits, hgrn_norm_g, ml_conv_w, ml_conv_b, ml_gate_b, ml_norm_g, w_branch, w_merge, b_merge, w_out, moe_w_group, moe_b_group, moe_w_expert, moe_b_expert, moe_w1, moe_w3, moe_w2, final_norm_g):
    bsz, seq, d = x.shape
    n_ctx = ctx.shape[1]
    depth = ada_w.shape[0]
    assert n_ctx % TM == 0 and seq % TM == 0 and d == 2 * BRANCH_W
    nc = n_ctx // TM

    h = jnp.concatenate([ctx, x], axis=1)
    cvec = jnp.zeros((SUBLANES, d), F32).at[:bsz].set(c).at[bsz].set(c_ctx)
    mod_all = _mod_call(cvec, ada_w, ada_b).reshape(depth, SUBLANES, 6, d)[:, :bsz + 1]

    lb_cum = jnp.cumsum(jax.nn.softmax(hgrn_lb_logits.astype(F32), axis=0), axis=0)
    hgrn_lb = lb_cum - lb_cum[:1]
    tables = [t for levels in SPLIT_LEVELS for t in _decay_tables(levels)]
    tri_sel = tables[0][:, :CHUNK, :]
    tri_mask = tables[1][:, 0]

    out = None
    for l in range(depth):
        last = l == depth - 1
        mod = mod_all[l]
        w_rg, w_gla, w_hg, w_ml = _split_w_in(w_in[l])
        z_rg, z_gla, z_hg, z_ml, u = _stage_a_call(h, mod, norm_mix_g[l], w_rg, w_gla, w_hg, w_ml, nc)

        rg_f, rg_b = _rg_call(z_rg, rg_conv_w[l], rg_conv_b[l].reshape(1, -1),
                              _block_diag(rg_gate_w[l]).astype(BF16),
                              rg_gate_b[l].reshape(4, -1), rg_lambda[l], nc)

        wlr = jnp.zeros((2, HEAD_W, N_HEADS * HEAD_W), F32)
        wlr_p = _pad_heads(gla_w_lr[l], GLA_DK)
        wlr = wlr.at[0, :GLA_RANK].set(wlr_p[0]).at[1, GLA_RANK:2 * GLA_RANK].set(wlr_p[1])
        gla_f, gla_b = _gla_call(z_gla, wlr.astype(BF16), _pad_heads(gla_b_lr[l], GLA_DK), tables, nc)

        lb = hgrn_lb[l]
        lbp = jnp.concatenate([lb, jnp.log(lb), jnp.log1p(-lb), jnp.zeros((2, lb.shape[-1]), F32)], axis=0)
        hg_f, hg_b = _hg_call(z_hg, lbp, tables, nc)

        gbias = _pad_cols(ml_gate_b[l].transpose(1, 0, 2).reshape(2, -1), HEAD_W)
        ml_f, ml_b = _ml_call(z_ml, ml_conv_w[l], ml_conv_b[l].reshape(1, -1), gbias, tri_sel, tri_mask, nc)

        gains = jnp.zeros((SUBLANES, HEAD_W), F32).at[0].set(gla_norm_g[l]).at[1].set(hgrn_norm_g[l]).at[2].set(ml_norm_g[l])
        w_route = _pad_cols(jnp.concatenate([moe_w_expert[l], moe_w_group[l]], axis=1), HEAD_W)
        b_route = _pad_cols(jnp.concatenate([moe_b_expert[l], moe_b_group[l]]).reshape(1, -1), HEAD_W)
        t0 = nc if last else 0
        h_mid, u2, wt = _stage_c_call(
            u, h, mod, z_rg, rg_f, rg_b, z_gla, gla_f, gla_b, z_hg, hg_f, hg_b, z_ml, ml_f, ml_b,
            gains, w_merge[l].astype(BF16), b_merge[l].reshape(1, -1), w_branch[l].astype(BF16),
            w_out[l].astype(BF16), norm_ffn_g[l].reshape(1, -1), w_route, b_route, nc, t0)

        res = _moe_call(u2, wt, h_mid, mod, final_norm_g.reshape(1, -1), moe_w1[l].astype(BF16),
                        moe_w3[l].astype(BF16), moe_w2[l].astype(BF16), nc, t0, last)
        if last:
            out = res
        else:
            h = res
    return out
```

```python
import functools

import numpy as np
import jax
import jax.numpy as jnp
from jax import lax
from jax.experimental import pallas as pl
from jax.experimental.pallas import tpu as pltpu

F32 = jnp.float32
BF16 = jnp.bfloat16

EPS = 1e-6
TM = 256
CHUNK = 64
SUBLANES = 8
N_HEADS = 4
HEAD_W = 128
BRANCH_W = 512
CONV_W = 4
CONV_LEFT = 2
RG_C = 8.0
GLA_DK = 64
GLA_RANK = 16
GLA_GATE_NORM = 16.0
N_GROUPS = 4
EXPERTS_PER_GROUP = 4
N_EXPERTS = 16
MOE_BLK = 256
ROUTE_IDX = 16
ROUTE_W = 18
N_LEVELS = 6
MILD_DECAY = -40.0
VMEM_LIMIT = 56 * 1024 * 1024

W_RG = 2 * BRANCH_W
W_GLA = 4 * BRANCH_W + HEAD_W
W_HG = 5 * BRANCH_W
W_ML = 4 * BRANCH_W + 2 * HEAD_W
SPLIT_LEVELS = (0, 2, N_LEVELS)


def _mm(a, b):
    return jnp.dot(a.astype(BF16), b.astype(BF16), preferred_element_type=F32)


def _mm_nt(a, b):
    return lax.dot_general(a.astype(BF16), b.astype(BF16), (((1,), (1,)), ((), ())),
                           preferred_element_type=F32)


def _mm_tn(a, b):
    return lax.dot_general(a.astype(BF16), b.astype(BF16), (((0,), (0,)), ((), ())),
                           preferred_element_type=F32)


def _mm_sel(sel, x):
    x1 = x.astype(BF16)
    x2 = (x - x1.astype(F32)).astype(BF16)
    dot = functools.partial(jnp.dot, preferred_element_type=F32)
    return dot(sel, x1) + dot(sel, x2)


def _log_sigmoid(x):
    return jnp.minimum(x, 0.0) - jnp.log1p(jnp.exp(-jnp.abs(x)))


def _silu(x):
    return x * jax.nn.sigmoid(x)


def _rmsnorm_rows(x, g):
    return x * lax.rsqrt(jnp.mean(x * x, axis=-1, keepdims=True) + EPS) * g


def _head_rmsnorm(o, g):
    parts = [_rmsnorm_rows(o[:, h * HEAD_W:(h + 1) * HEAD_W], g) for h in range(N_HEADS)]
    return jnp.concatenate(parts, axis=-1)


def _bwd_tile(s, nc, nt):
    return jnp.where(s < nc, nc - 1 - s, nt - 1 - (s - nc))


def _const_spec(shape):
    nd = len(shape)
    return pl.BlockSpec(shape, lambda *_: (0,) * nd)


def _params(sem):
    return pltpu.CompilerParams(dimension_semantics=sem, vmem_limit_bytes=VMEM_LIMIT)


def _mod_kernel(c_ref, w_ref, b_ref, o_ref):
    cv = _silu(c_ref[...])
    o_ref[...] = jnp.dot(cv, w_ref[...], precision=lax.Precision.HIGHEST,
                         preferred_element_type=F32) + b_ref[...]


def _mod_call(cvec, ada_w, ada_b):
    depth, d, six_d = ada_w.shape
    tn = 1024
    return pl.pallas_call(
        _mod_kernel,
        grid=(depth, six_d // tn),
        in_specs=[pl.BlockSpec((SUBLANES, d), lambda l, j: (0, 0)),
                  pl.BlockSpec((None, d, tn), lambda l, j: (l, 0, j)),
                  pl.BlockSpec((None, 1, tn), lambda l, j: (l, 0, j))],
        out_specs=pl.BlockSpec((None, SUBLANES, tn), lambda l, j: (l, 0, j)),
        out_shape=jax.ShapeDtypeStruct((depth, SUBLANES, six_d), F32),
        compiler_params=_params(("parallel", "parallel")),
        name="adaln_mod",
    )(cvec, ada_w, ada_b.reshape(depth, 1, six_d))


def _stage_a_kernel(h_ref, mod_ref, g_ref, w_rg, w_gla, w_hg, w_ml,
                    z_rg, z_gla, z_hg, z_ml, u_ref):
    x = h_ref[...]
    u = _rmsnorm_rows(x, g_ref[...]) * (1.0 + mod_ref[1:2, :]) + mod_ref[0:1, :]
    ub = u.astype(BF16)
    u_ref[...] = ub
    z_rg[...] = jnp.dot(ub, w_rg[...], preferred_element_type=F32)
    z_gla[...] = jnp.dot(ub, w_gla[...], preferred_element_type=F32)
    z_hg[...] = jnp.dot(ub, w_hg[...], preferred_element_type=F32)
    z_ml[...] = jnp.dot(ub, w_ml[...], preferred_element_type=F32)


def _stage_a_call(h, mod, norm_g, w_rg, w_gla, w_hg, w_ml, nc):
    bsz, n, d = h.shape
    nt = n // TM
    tm = TM // 2

    def tok(w):
        return pl.BlockSpec((None, tm, w), lambda b, t: (b, t, 0))

    return pl.pallas_call(
        _stage_a_kernel,
        grid=(bsz, n // tm),
        in_specs=[tok(d),
                  pl.BlockSpec((None, 6, d), lambda b, t: (jnp.where(t * tm < nc * TM, bsz, b), 0, 0)),
                  _const_spec((1, d)),
                  _const_spec((d, W_RG)), _const_spec((d, W_GLA)),
                  _const_spec((d, W_HG)), _const_spec((d, W_ML))],
        out_specs=[tok(W_RG), tok(W_GLA), tok(W_HG), tok(W_ML), tok(d)],
        out_shape=[jax.ShapeDtypeStruct((bsz, n, W_RG), F32),
                   jax.ShapeDtypeStruct((bsz, n, W_GLA), F32),
                   jax.ShapeDtypeStruct((bsz, n, W_HG), F32),
                   jax.ShapeDtypeStruct((bsz, n, W_ML), F32),
                   jax.ShapeDtypeStruct((bsz, n, d), BF16)],
        compiler_params=_params(("parallel", "parallel")),
        name="stage_a",
    )(h, mod, norm_g.reshape(1, d), w_rg, w_gla, w_hg, w_ml)


def _conv_tile(x_ref, prev_ref, next_ref, pad_ref, cw_ref, cb_ref, tile, nc, nt):
    prev_ok = jnp.logical_and(tile != 0, tile != nc)
    next_ok = jnp.logical_and(tile != nc - 1, tile != nt - 1)
    pad_ref[0:SUBLANES, :] = jnp.where(prev_ok, prev_ref[...], 0.0)
    pad_ref[SUBLANES:SUBLANES + TM, :] = x_ref[...]
    pad_ref[SUBLANES + TM:2 * SUBLANES + TM, :] = jnp.where(next_ok, next_ref[...], 0.0)
    acc = cb_ref[...]
    for j in range(CONV_W):
        off = SUBLANES - CONV_LEFT + j
        acc = acc + cw_ref[j:j + 1, :] * pad_ref[off:off + TM, :]
    return acc


def _halo_specs(width, col_block, tile_fn, n):
    rows = TM // SUBLANES
    last = n // SUBLANES - 1
    cur = pl.BlockSpec((None, TM, width), lambda b, s: (b, tile_fn(s), col_block))
    prev = pl.BlockSpec((None, SUBLANES, width),
                        lambda b, s: (b, jnp.maximum(tile_fn(s) * rows - 1, 0), col_block))
    nxt = pl.BlockSpec((None, SUBLANES, width),
                       lambda b, s: (b, jnp.minimum((tile_fn(s) + 1) * rows, last), col_block))
    return [cur, prev, nxt]


def _scan_rows8(a, b, reverse):
    n = a.shape[0]
    pos = jnp.bitwise_and(lax.broadcasted_iota(jnp.int32, a.shape, 0), SUBLANES - 1)
    k = 1
    while k < SUBLANES:
        if reverse:
            a_s, b_s, ok = pltpu.roll(a, n - k, 0), pltpu.roll(b, n - k, 0), pos < SUBLANES - k
        else:
            a_s, b_s, ok = pltpu.roll(a, k, 0), pltpu.roll(b, k, 0), pos >= k
        b = b + a * jnp.where(ok, b_s, 0.0)
        a = a * jnp.where(ok, a_s, 1.0)
        k *= 2
    return a, b


def _scan_tile(a, b, carry, o_ref, reverse):
    a8, b8 = _scan_rows8(a, b, reverse)
    groups = a.shape[0] // SUBLANES
    order = range(groups - 1, -1, -1) if reverse else range(groups)
    for r in order:
        rows = slice(r * SUBLANES, (r + 1) * SUBLANES)
        h = b8[rows] + a8[rows] * carry
        o_ref[rows, :] = h
        carry = h[0:1, :] if reverse else h[SUBLANES - 1:SUBLANES, :]
    return carry


def _rg_kernel(xf, xf_p, xf_n, xb, xb_p, xb_n, cw, cb, gw, gb, lam,
               hf_ref, hb_ref, pad_ref, carry_ref, *, nc, nt):
    s = pl.program_id(1)

    @pl.when(s == 0)
    def _():
        carry_ref[...] = jnp.zeros_like(carry_ref)

    dirs = ((xf, xf_p, xf_n, hf_ref, s, False),
            (xb, xb_p, xb_n, hb_ref, _bwd_tile(s, nc, nt), True))
    for d, (x_ref, p_ref, n_ref, o_ref, tile, reverse) in enumerate(dirs):
        x = _conv_tile(x_ref, p_ref, n_ref, pad_ref, cw, cb, tile, nc, nt)
        r = jax.nn.sigmoid(_mm(x, gw[d, 0]) + gb[2 * d:2 * d + 1, :])
        i = jax.nn.sigmoid(_mm(x, gw[d, 1]) + gb[2 * d + 1:2 * d + 2, :])
        lam_d = lam[d:d + 1, :]
        softplus = jnp.maximum(-lam_d, 0.0) + jnp.log1p(jnp.exp(-jnp.abs(lam_d)))
        log_a = -RG_C * r * softplus
        a = jnp.exp(log_a)
        t = jnp.tanh(log_a)
        bt = jnp.sqrt(-2.0 * t / (1.0 - t)) * (i * x)
        carry_ref[d:d + 1, :] = _scan_tile(a, bt, carry_ref[d:d + 1, :], o_ref, reverse)


def _rg_call(z_rg, cw, cb, gw, gb, lam, nc):
    bsz, n, _ = z_rg.shape
    nt = n // TM
    w = BRANCH_W
    fwd = lambda s: s
    bwd = lambda s: _bwd_tile(s, nc, nt)
    out = lambda fn: pl.BlockSpec((None, TM, w), lambda b, s: (b, fn(s), 0))
    return pl.pallas_call(
        functools.partial(_rg_kernel, nc=nc, nt=nt),
        grid=(bsz, nt),
        in_specs=_halo_specs(w, 0, fwd, n) + _halo_specs(w, 0, bwd, n) + [
            _const_spec((CONV_W, w)), _const_spec((1, w)),
            _const_spec((2, 2, w, w)), _const_spec((4, w)), _const_spec((2, w))],
        out_specs=[out(fwd), out(bwd)],
        out_shape=[jax.ShapeDtypeStruct((bsz, n, w), F32)] * 2,
        scratch_shapes=[pltpu.VMEM((TM + 2 * SUBLANES, w), F32), pltpu.VMEM((2, w), F32)],
        compiler_params=_params(("parallel", "arbitrary")),
        name="rglru",
    )(z_rg, z_rg, z_rg, z_rg, z_rg, z_rg, cw, cb, gw, gb, lam)


def _decay_tables(levels):
    n = CHUNK
    blk = n >> levels
    sel = np.zeros((levels + 3, n, n), np.float32)
    masks = np.zeros((levels + 1, n, n), np.float32)
    for lvl in range(levels):
        half = n >> (lvl + 1)
        for r in range(n):
            start = (r // (2 * half)) * 2 * half
            ref = start + half - 1
            if r - start >= half:
                sel[lvl, r, ref + 1:r + 1] = 1.0
                masks[lvl, r, start:start + half] = 1.0
            else:
                sel[lvl, r, r + 1:ref + 1] = 1.0
    sel[levels] = np.tril(np.ones((n, n), np.float32))
    sel[levels + 1] = 1.0 - sel[levels]
    for r in range(n):
        start = (r // blk) * blk
        sel[levels + 2, r, r + 1:start + blk] = 1.0
        masks[levels, r, start:r + 1] = 1.0
    if levels == 0 or levels == N_LEVELS:
        sel = sel[:levels + 2]
    sel_b = sel[:, ::-1, ::-1].reshape(-1, n)
    masks_b = masks[:, ::-1, ::-1]
    return (jnp.asarray(np.stack([sel.reshape(-1, n), sel_b]), BF16),
            jnp.asarray(np.stack([masks, masks_b]), F32))


def _ref_rows(levels, reverse):
    def runs(block, offset):
        return [(s + offset, block) for s in range(0, CHUNK, block)]

    out = [runs(CHUNK >> lvl, (CHUNK >> (lvl + 1)) - 1) for lvl in range(levels)]
    out.append(runs(CHUNK, CHUNK - 1))
    out.append(runs(CHUNK >> levels, (CHUNK >> levels) - 1))
    if reverse:
        out = [[(CHUNK - 1 - r, n) for r, n in reversed(segs)] for segs in out]
    return out


def _rows_bcast(b, segs):
    return jnp.concatenate([jnp.broadcast_to(b[r:r + 1, :], (n, b.shape[-1])) for r, n in segs], axis=0)


def _gla_chunk(q, k, v, g, sel, masks, st_ref, d, reverse, levels):
    if levels == N_LEVELS:
        x = jnp.exp(_mm_sel(sel, g))
        x_lvl = [x[lvl * CHUNK:(lvl + 1) * CHUNK] for lvl in range(levels)]
        x_cum = x[levels * CHUNK:(levels + 1) * CHUNK]
        x_rest = x[(levels + 1) * CHUNK:(levels + 2) * CHUNK]
        xk_blk = xq_blk = None
    else:
        b = _mm_sel(sel[levels * CHUNK:(levels + 1) * CHUNK], g)
        refs = _ref_rows(levels, reverse)
        x_cum = jnp.exp(b)
        x_lvl = [jnp.exp(-jnp.abs(b - _rows_bcast(b, segs))) for segs in refs[:levels]]
        x_rest = jnp.exp(-jnp.abs(b - _rows_bcast(b, refs[levels])))
        if levels == 0:
            xk_blk, xq_blk = x_rest, jnp.exp(jnp.abs(b - _rows_bcast(b, refs[levels])))
        else:
            d_blk = jnp.abs(b - _rows_bcast(b, refs[levels + 1]))
            xk_blk, xq_blk = jnp.exp(-d_blk), jnp.exp(d_blk)
    outs = []
    for h in range(N_HEADS):
        cols = slice(h * HEAD_W, (h + 1) * HEAD_W)
        qh, kh, vh = q[:, cols], k[:, cols], v[:, cols]
        kx = kh * x_rest[:, cols]
        if xk_blk is None:
            sc = _mm_nt(qh, kh)
        elif levels == 0:
            sc = _mm_nt(qh * xq_blk[:, cols], kx)
        else:
            sc = _mm_nt(qh * xq_blk[:, cols], kh * xk_blk[:, cols])
        sc = jnp.where(masks[levels] > 0.0, sc, 0.0)
        for lvl in range(levels):
            xl = x_lvl[lvl][:, cols]
            sc = sc + jnp.where(masks[lvl] > 0.0, _mm_nt(qh * xl, kh * xl), 0.0)
        st = st_ref[d, h]
        outs.append(_mm(sc, vh) + _mm_nt(qh * x_cum[:, cols], st))
        x_end = x_cum[0:1, cols] if reverse else x_cum[CHUNK - 1:CHUNK, cols]
        st_ref[d, h] = st * x_end + _mm_tn(vh, kx)
    return jnp.concatenate(outs, axis=-1)


def _gla_step(sides, tabs, st_ref):
    n_chunks = TM // CHUNK

    def min_block_sum(block):
        mins = [jnp.min(jnp.sum(g.reshape(TM // block, block, g.shape[-1]), axis=1))
                for _, _, _, g, _ in sides]
        return jnp.minimum(mins[0], mins[1])

    conds = []
    taken = None
    for levels in SPLIT_LEVELS[:-1]:
        ok = min_block_sum(CHUNK >> levels) > MILD_DECAY
        conds.append(ok if taken is None else jnp.logical_and(ok, jnp.logical_not(taken)))
        taken = ok if taken is None else jnp.logical_or(taken, ok)
    conds.append(jnp.logical_not(taken))

    for levels, cond, (sel_ref, mask_ref) in zip(SPLIT_LEVELS, conds, tabs):
        @pl.when(cond)
        def _(levels=levels, sel_ref=sel_ref, mask_ref=mask_ref):
            for c in range(n_chunks):
                for d, (q, k, v, g, o_ref) in enumerate(sides):
                    reverse = d == 1
                    cc = n_chunks - 1 - c if reverse else c
                    rows = slice(cc * CHUNK, (cc + 1) * CHUNK)
                    o_ref[rows, :] = _gla_chunk(q[rows], k[rows], v[rows], g[rows], sel_ref[d],
                                                mask_ref[d], st_ref, d, reverse, levels)


def _pairs(refs):
    return tuple(zip(refs[0::2], refs[1::2]))


def _gla_kernel(zf, zb, wlr, blr, *rest):
    tabs, (of_ref, ob_ref, st_ref) = _pairs(rest[:-3]), rest[-3:]
    w = BRANCH_W

    @pl.when(pl.program_id(1) == 0)
    def _():
        st_ref[...] = jnp.zeros_like(st_ref)

    sides = []
    for d, (z_ref, o_ref) in enumerate(((zf, of_ref), (zb, ob_ref))):
        pre = _mm(z_ref[:, 4 * w:4 * w + HEAD_W], wlr[d]) + blr[d:d + 1, :]
        g = _log_sigmoid(pre) * (1.0 / GLA_GATE_NORM)
        q = z_ref[:, 0:w] * (GLA_DK ** -0.5)
        sides.append((q, z_ref[:, w:2 * w], z_ref[:, 2 * w:3 * w], g, o_ref))
    _gla_step(sides, tabs, st_ref)


def _hg_kernel(zf, zb, lbp, *rest):
    tabs, (of_ref, ob_ref, st_ref) = _pairs(rest[:-3]), rest[-3:]
    w = BRANCH_W

    @pl.when(pl.program_id(1) == 0)
    def _():
        st_ref[...] = jnp.zeros_like(st_ref)

    sides = []
    for d, (z_ref, o_ref) in enumerate(((zf, of_ref), (zb, ob_ref))):
        f = z_ref[:, (2 + d) * w:(3 + d) * w]
        lb = lbp[d:d + 1, :]
        log_lb = lbp[2 + d:3 + d, :]
        log_1m = lbp[4 + d:5 + d, :]
        c = log_1m + _log_sigmoid(f)
        g = jnp.maximum(log_lb, c) + jnp.log1p(jnp.exp(-jnp.abs(log_lb - c)))
        k = (1.0 - lb) * jax.nn.sigmoid(-f)
        q = _silu(z_ref[:, 0:w]) * (HEAD_W ** -0.5)
        sides.append((q, k, z_ref[:, w:2 * w], g, o_ref))
    _gla_step(sides, tabs, st_ref)


def _mixer_specs(widths_cols, tile_fn):
    return [pl.BlockSpec((None, TM, w), lambda b, s, c=c: (b, tile_fn(s), c)) for w, c in widths_cols]


def _gla_call(z_gla, wlr, blr, tables, nc):
    bsz, n, _ = z_gla.shape
    nt = n // TM
    w = BRANCH_W
    fwd = lambda s: s
    bwd = lambda s: _bwd_tile(s, nc, nt)
    cols = [(W_GLA, 0)]
    out = lambda fn: pl.BlockSpec((None, TM, w), lambda b, s: (b, fn(s), 0))
    return pl.pallas_call(
        _gla_kernel,
        grid=(bsz, nt),
        in_specs=_mixer_specs(cols, fwd) + _mixer_specs(cols, bwd) + [
            _const_spec(wlr.shape), _const_spec(blr.shape)] + [_const_spec(t.shape) for t in tables],
        out_specs=[out(fwd), out(bwd)],
        out_shape=[jax.ShapeDtypeStruct((bsz, n, w), F32)] * 2,
        scratch_shapes=[pltpu.VMEM((2, N_HEADS, HEAD_W, HEAD_W), F32)],
        compiler_params=_params(("parallel", "arbitrary")),
        name="gla",
    )(z_gla, z_gla, wlr, blr, *tables)


def _hg_call(z_hg, lbp, tables, nc):
    bsz, n, _ = z_hg.shape
    nt = n // TM
    w = BRANCH_W
    fwd = lambda s: s
    bwd = lambda s: _bwd_tile(s, nc, nt)
    out = lambda fn: pl.BlockSpec((None, TM, w), lambda b, s: (b, fn(s), 0))
    return pl.pallas_call(
        _hg_kernel,
        grid=(bsz, nt),
        in_specs=_mixer_specs([(4 * w, 0)], fwd) + _mixer_specs([(4 * w, 0)], bwd) + [
            _const_spec(lbp.shape)] + [_const_spec(t.shape) for t in tables],
        out_specs=[out(fwd), out(bwd)],
        out_shape=[jax.ShapeDtypeStruct((bsz, n, w), F32)] * 2,
        scratch_shapes=[pltpu.VMEM((2, N_HEADS, HEAD_W, HEAD_W), F32)],
        compiler_params=_params(("parallel", "arbitrary")),
        name="hgrn2",
    )(z_hg, z_hg, lbp, *tables)


def _cummax_rows(a, reverse):
    n = a.shape[0]
    row = lax.broadcasted_iota(jnp.int32, a.shape, 0)
    k = 1
    while k < n:
        if reverse:
            a_s, ok = pltpu.roll(a, n - k, 0), row < n - k
        else:
            a_s, ok = pltpu.roll(a, k, 0), row >= k
        a = jnp.maximum(a, jnp.where(ok, a_s, -jnp.inf))
        k *= 2
    return a


def _ml_chunk(q, k, v, gi, log_f, tri_sel, tri_mask, ct_ref, m_ref, d, reverse):
    b = _mm_sel(tri_sel, log_f)
    a = gi - b
    m_prev = m_ref[d:d + 1, :]
    m_rel = jnp.maximum(_cummax_rows(a, reverse), m_prev)
    w_inter = jnp.exp(m_prev - m_rel)
    exp_neg_m = jnp.exp(-(b + m_rel))
    last = 0 if reverse else CHUNK - 1
    b_end = b[last:last + 1, :]
    m_new = b_end + m_rel[last:last + 1, :]
    w_end = jnp.exp(b_end - b + gi - m_new)
    decay = jnp.exp(b_end + m_prev - m_new)
    m_ref[d:d + 1, :] = m_new
    a_t = a.T
    ones = jnp.ones((CHUNK, HEAD_W), BF16)
    outs = []
    for h in range(N_HEADS):
        cols = slice(h * HEAD_W, (h + 1) * HEAD_W)
        qh, kh, vh = q[:, cols], k[:, cols], v[:, cols]
        c = d * N_HEADS + h
        log_w = jnp.where(tri_mask > 0.0, a_t[c:c + 1, :] - m_rel[:, c:c + 1], -jnp.inf)
        s = _mm_nt(qh, kh) * jnp.exp(log_w)
        v_ext = jnp.concatenate([vh.astype(BF16), ones], axis=-1)
        ct = ct_ref[c]
        tot = _mm(s, v_ext) + w_inter[:, c:c + 1] * _mm_nt(qh, ct)
        num, den = tot[:, :HEAD_W], tot[:, HEAD_W:]
        outs.append(num / jnp.maximum(jnp.abs(den), exp_neg_m[:, c:c + 1]))
        ct_ref[c] = decay[:, c:c + 1] * ct + _mm_tn(v_ext, kh * w_end[:, c:c + 1])
    return jnp.concatenate(outs, axis=-1)


def _ml_kernel(zf, zf_p, zf_n, zb, zb_p, zb_n, cw, cb, gbias,
               tri_sel, tri_mask, of_ref, ob_ref, pad_ref, ct_ref, m_ref, *, nc, nt):
    s = pl.program_id(1)
    w = BRANCH_W

    @pl.when(s == 0)
    def _():
        ct_ref[...] = jnp.zeros_like(ct_ref)
        m_ref[...] = jnp.zeros_like(m_ref)

    dirs = ((zf, zf_p, zf_n, of_ref, s, False),
            (zb, zb_p, zb_n, ob_ref, _bwd_tile(s, nc, nt), True))
    for d, (z_ref, p_ref, nx_ref, o_ref, tile, reverse) in enumerate(dirs):
        qk = _silu(_conv_tile(z_ref.at[:, 0:2 * w], p_ref, nx_ref, pad_ref, cw, cb, tile, nc, nt))
        q = qk[:, :w] * (HEAD_W ** -0.5)
        k = qk[:, w:]
        v = z_ref[:, 2 * w:3 * w]
        gi = z_ref[:, 4 * w:4 * w + HEAD_W] + gbias[0:1, :]
        log_f = _log_sigmoid(z_ref[:, 4 * w + HEAD_W:4 * w + 2 * HEAD_W] + gbias[1:2, :])
        order = range(TM // CHUNK - 1, -1, -1) if reverse else range(TM // CHUNK)
        for c in order:
            rows = slice(c * CHUNK, (c + 1) * CHUNK)
            o_ref[rows, :] = _ml_chunk(q[rows], k[rows], v[rows], gi[rows], log_f[rows],
                                       tri_sel[d], tri_mask[d], ct_ref, m_ref, d, reverse)


def _ml_call(z_ml, cw, cb, gbias, tri_sel, tri_mask, nc):
    bsz, n, _ = z_ml.shape
    nt = n // TM
    w = BRANCH_W
    fwd = lambda s: s
    bwd = lambda s: _bwd_tile(s, nc, nt)
    out = lambda fn: pl.BlockSpec((None, TM, w), lambda b, s: (b, fn(s), 0))

    def side(fn):
        return _mixer_specs([(W_ML, 0)], fn) + _halo_specs(2 * w, 0, fn, n)[1:]

    return pl.pallas_call(
        functools.partial(_ml_kernel, nc=nc, nt=nt),
        grid=(bsz, nt),
        in_specs=side(fwd) + side(bwd) + [
            _const_spec((CONV_W, 2 * w)), _const_spec((1, 2 * w)), _const_spec((2, HEAD_W)),
            _const_spec(tri_sel.shape), _const_spec(tri_mask.shape)],
        out_specs=[out(fwd), out(bwd)],
        out_shape=[jax.ShapeDtypeStruct((bsz, n, w), F32)] * 2,
        scratch_shapes=[pltpu.VMEM((TM + 2 * SUBLANES, 2 * w), F32),
                        pltpu.VMEM((2 * N_HEADS, 2 * HEAD_W, HEAD_W), F32),
                        pltpu.VMEM((2, HEAD_W), F32)],
        compiler_params=_params(("parallel", "arbitrary")),
        name="mlstm",
    )(*([z_ml] * 6), cw, cb, gbias, tri_sel, tri_mask)


def _gelu_tanh(x):
    return 0.5 * x * (1.0 + jnp.tanh(0.7978845608028654 * (x + 0.044715 * (x * x * x))))


def _route(logits):
    col = lax.broadcasted_iota(jnp.int32, logits.shape, 1)
    colf = col.astype(F32)
    is_g = jnp.logical_and(col >= N_EXPERTS, col < N_EXPERTS + N_GROUPS)
    is_e = col < N_EXPERTS
    neg = -jnp.inf
    big = 1e9
    gl = jnp.where(is_g, logits, neg)
    g_max = jnp.max(gl, axis=-1, keepdims=True)
    grp = jnp.min(jnp.where(gl == g_max, colf, big), axis=-1, keepdims=True) - N_EXPERTS
    p_grp = 1.0 / jnp.sum(jnp.exp(gl - g_max), axis=-1, keepdims=True)
    col_grp = lax.shift_right_logical(col, EXPERTS_PER_GROUP.bit_length() - 1)
    in_grp = jnp.logical_and(is_e, col_grp.astype(F32) == grp)
    e1 = jnp.where(in_grp, logits, neg)
    top1 = jnp.max(e1, axis=-1, keepdims=True)
    idx1 = jnp.min(jnp.where(e1 == top1, colf, big), axis=-1, keepdims=True)
    e2 = jnp.where(colf == idx1, neg, e1)
    top2 = jnp.max(e2, axis=-1, keepdims=True)
    idx2 = jnp.min(jnp.where(e2 == top2, colf, big), axis=-1, keepdims=True)
    t = jnp.exp(top2 - top1)
    w1 = p_grp / (1.0 + t)
    w2 = p_grp * (t / (1.0 + t))
    out = jnp.where(col == ROUTE_IDX, idx1, jnp.where(col == ROUTE_IDX + 1, idx2, 0.0))
    return jnp.where(col == ROUTE_W, w1, jnp.where(col == ROUTE_W + 1, w2, out))


def _route_onehot(route):
    colf = lax.broadcasted_iota(jnp.int32, route.shape, 1).astype(F32)
    oh1 = colf == route[:, ROUTE_IDX:ROUTE_IDX + 1]
    oh2 = colf == route[:, ROUTE_IDX + 1:ROUTE_IDX + 2]
    return jnp.where(jnp.logical_or(oh1, oh2), 1.0, 0.0), oh1, oh2


def _stage_c_kernel(u_ref, h_ref, mod_ref, rg_y, rg_f, rg_b, gla_g, gla_f, gla_b, hg_g, hg_f, hg_b,
                    ml_o, ml_f, ml_b, gains, w_merge, b_merge, w_branch, w_out, ffn_g, w_route, b_route,
                    h_out, u2_out, wt_out, cnt_out):
    ys = (
        _gelu_tanh(rg_y[...]) * (rg_f[...] + rg_b[...]),
        _head_rmsnorm(gla_f[...] + gla_b[...], gains[0:1, :]) * _silu(gla_g[...]),
        _head_rmsnorm(hg_f[...] + hg_b[...], gains[1:2, :]) * _silu(hg_g[...]),
        jax.nn.sigmoid(ml_o[...]) * _head_rmsnorm(ml_f[...] + ml_b[...], gains[2:3, :]),
    )
    u = u_ref[...]
    d = u.shape[-1]
    merged = None
    for kk, y in enumerate(ys):
        gate = jax.nn.sigmoid(jnp.dot(u, w_merge[:, kk * d:(kk + 1) * d], preferred_element_type=F32)
                              + b_merge[:, kk * d:(kk + 1) * d])
        term = gate * _mm(y, w_branch[kk])
        merged = term if merged is None else merged + term
    mix = _mm(merged, w_out[...])
    h_new = h_ref[...] + mod_ref[2:3, :] * mix
    h_out[...] = h_new
    u2 = _rmsnorm_rows(h_new, ffn_g[...]) * (1.0 + mod_ref[4:5, :]) + mod_ref[3:4, :]
    u2_out[...] = u2
    logits = jnp.dot(u2, w_route[...], precision=lax.Precision.HIGHEST,
                     preferred_element_type=F32) + b_route[...]
    route = _route(logits)
    wt_out[...] = route
    cnt = jnp.sum(_route_onehot(route)[0], axis=0, keepdims=True)
    cnt_out[...] = jnp.broadcast_to(cnt, cnt_out.shape)


def _stage_c_call(u, h, mod, z_rg, rg_f, rg_b, z_gla, gla_f, gla_b, z_hg, hg_f, hg_b, z_ml, ml_f, ml_b,
                  gains, w_merge, b_merge, w_branch, w_out, ffn_g, w_route, b_route, nc, t0):
    bsz, n, d = h.shape
    nt = n // TM
    w = BRANCH_W

    n_out = n - t0 * TM

    def tok(width, col=0):
        return pl.BlockSpec((None, TM, width), lambda b, t: (b, t + t0, col))

    def out_tok(width):
        return pl.BlockSpec((None, TM, width), lambda b, t: (b, t, 0))

    mod_spec = pl.BlockSpec((None, 6, d), lambda b, t: (jnp.where(t + t0 < nc, bsz, b), 0, 0))
    return pl.pallas_call(
        _stage_c_kernel,
        grid=(bsz, nt - t0),
        in_specs=[tok(d), tok(d), mod_spec,
                  tok(w, 1), tok(w), tok(w),
                  tok(w, 3), tok(w), tok(w),
                  tok(w, 4), tok(w), tok(w),
                  tok(w, 3), tok(w), tok(w),
                  _const_spec(gains.shape), _const_spec(w_merge.shape), _const_spec(b_merge.shape),
                  _const_spec(w_branch.shape), _const_spec(w_out.shape), _const_spec(ffn_g.shape),
                  _const_spec(w_route.shape), _const_spec(b_route.shape)],
        out_specs=[out_tok(d), out_tok(d), out_tok(HEAD_W),
                   pl.BlockSpec((None, None, SUBLANES, HEAD_W), lambda b, t: (b, t, 0, 0))],
        out_shape=[jax.ShapeDtypeStruct((bsz, n_out, d), F32),
                   jax.ShapeDtypeStruct((bsz, n_out, d), F32),
                   jax.ShapeDtypeStruct((bsz, n_out, HEAD_W), F32),
                   jax.ShapeDtypeStruct((bsz, nt - t0, SUBLANES, HEAD_W), F32)],
        compiler_params=_params(("parallel", "parallel")),
        name="stage_c",
    )(u, h, mod, z_rg, rg_f, rg_b, z_gla, gla_f, gla_b, z_hg, hg_f, hg_b, z_ml, ml_f, ml_b,
      gains, w_merge, b_merge, w_branch, w_out, ffn_g, w_route, b_route)


def _slots_to_smem(slots_vmem, slots_smem, sem):
    cp = pltpu.make_async_copy(slots_vmem, slots_smem, sem)
    cp.start()
    cp.wait()


def _row_copy(src, src_row, dst, dst_row, sem):
    return pltpu.make_async_copy(src.at[pl.ds(src_row, 1), :], dst.at[pl.ds(dst_row, 1), :], sem)


def _dispatch_kernel(route_ref, base_ref, tri_ref, x_hbm, xs_in, xs_hbm, slots_ref,
                     slots_smem, sem_s, sem_r):
    del xs_in
    route = route_ref[...]
    onehot, oh1, oh2 = _route_onehot(route)
    rank = jnp.dot(tri_ref[...], onehot.astype(BF16), preferred_element_type=F32)
    pos = base_ref[...] + rank
    slot_a = jnp.sum(jnp.where(oh1, pos, 0.0), axis=-1, keepdims=True)
    slot_b = jnp.sum(jnp.where(oh2, pos, 0.0), axis=-1, keepdims=True)
    col = lax.broadcasted_iota(jnp.int32, route.shape, 1)
    both = jnp.where(col == 0, slot_a, jnp.where(col == 1, slot_b, 0.0))
    slots_ref[...] = both.T[0:SUBLANES, :].astype(jnp.int32)
    _slots_to_smem(slots_ref, slots_smem, sem_s)

    row0 = pl.program_id(0) * TM

    def start(t, carry):
        for j in range(2):
            _row_copy(x_hbm, row0 + t, xs_hbm, slots_smem[j, t], sem_r).start()
        return carry

    def wait(t, carry):
        for j in range(2):
            _row_copy(x_hbm, 0, xs_hbm, 0, sem_r).wait()
        return carry

    lax.fori_loop(0, TM, start, 0)
    lax.fori_loop(0, TM, wait, 0)


def _experts_kernel(blk_expert, n_used, x_ref, w1, w3, w2, o_ref):
    del blk_expert

    @pl.when(pl.program_id(0) < n_used[0])
    def _():
        x = x_ref[...].astype(BF16)
        h1 = jnp.dot(x, w1[...], preferred_element_type=F32)
        h3 = jnp.dot(x, w3[...], preferred_element_type=F32)
        o_ref[...] = _mm(_silu(h1) * h3, w2[...])

    @pl.when(pl.program_id(0) >= n_used[0])
    def _():
        o_ref[...] = jnp.zeros_like(o_ref)


def _combine_kernel(slots_ref, route_ref, h_ref, mod_ref, fin_g, ys_hbm, o_ref,
                    slots_smem, buf, sem_s, sem_r, *, final):
    _slots_to_smem(slots_ref, slots_smem, sem_s)

    def start(t, carry):
        for j in range(2):
            _row_copy(ys_hbm, slots_smem[j, t], buf.at[j], t, sem_r).start()
        return carry

    def wait(t, carry):
        for j in range(2):
            _row_copy(ys_hbm, 0, buf.at[j], 0, sem_r).wait()
        return carry

    lax.fori_loop(0, TM, start, 0)
    lax.fori_loop(0, TM, wait, 0)
    route = route_ref[...]
    y = route[:, ROUTE_W:ROUTE_W + 1] * buf[0] + route[:, ROUTE_W + 1:ROUTE_W + 2] * buf[1]
    out = h_ref[...] + mod_ref[5:6, :] * y
    if final:
        out = _rmsnorm_rows(out, fin_g[...])
    o_ref[...] = out


def _moe_call(u2, route, cnt, h, mod, fin_g, w1, w3, w2, nc, t0, final):
    bsz, n_out, d = h.shape
    de = w1.shape[-1]
    rows = bsz * n_out
    tiles = rows // TM
    tiles_per_row = n_out // TM
    n_blocks = (2 * rows + N_EXPERTS * (MOE_BLK - 1)) // MOE_BLK + 1

    counts = cnt[:, :, 0, :N_EXPERTS].reshape(tiles, N_EXPERTS)
    total = jnp.sum(counts, axis=0)
    padded = jnp.ceil(total / MOE_BLK) * MOE_BLK
    seg_end = jnp.cumsum(padded)
    base = (seg_end - padded)[None, :] + jnp.cumsum(counts, axis=0) - counts
    base = _pad_cols(base, HEAD_W).reshape(tiles, 1, HEAD_W)
    blk_start = jnp.arange(n_blocks, dtype=F32) * MOE_BLK
    blk_expert = jnp.minimum(jnp.sum(blk_start[:, None] >= seg_end[None, :], axis=1), N_EXPERTS - 1)
    n_used = (seg_end[-1] / MOE_BLK).astype(jnp.int32).reshape(1)
    tri = jnp.asarray(np.tril(np.ones((TM, TM), np.float32), -1), BF16)

    x_flat = u2.reshape(rows, d)
    xs, slots = pl.pallas_call(
        _dispatch_kernel,
        grid=(tiles,),
        in_specs=[pl.BlockSpec((TM, HEAD_W), lambda t: (t, 0)),
                  pl.BlockSpec((None, 1, HEAD_W), lambda t: (t, 0, 0)),
                  _const_spec((TM, TM)),
                  pl.BlockSpec(memory_space=pl.ANY), pl.BlockSpec(memory_space=pl.ANY)],
        out_specs=[pl.BlockSpec(memory_space=pl.ANY),
                   pl.BlockSpec((SUBLANES, TM), lambda t: (t, 0))],
        out_shape=[jax.ShapeDtypeStruct((n_blocks * MOE_BLK, d), F32),
                   jax.ShapeDtypeStruct((tiles * SUBLANES, TM), jnp.int32)],
        scratch_shapes=[pltpu.SMEM((SUBLANES, TM), jnp.int32),
                        pltpu.SemaphoreType.DMA, pltpu.SemaphoreType.DMA],
        input_output_aliases={4: 0},
        compiler_params=_params(("arbitrary",)),
        name="moe_dispatch",
    )(route.reshape(rows, HEAD_W), base, tri, x_flat, jnp.zeros((n_blocks * MOE_BLK, d), F32))

    ys = pl.pallas_call(
        _experts_kernel,
        grid_spec=pltpu.PrefetchScalarGridSpec(
            num_scalar_prefetch=2,
            grid=(n_blocks,),
            in_specs=[pl.BlockSpec((MOE_BLK, d), lambda b, be, nu: (b, 0)),
                      pl.BlockSpec((None, d, de), lambda b, be, nu: (be[b], 0, 0)),
                      pl.BlockSpec((None, d, de), lambda b, be, nu: (be[b], 0, 0)),
                      pl.BlockSpec((None, de, d), lambda b, be, nu: (be[b], 0, 0))],
            out_specs=pl.BlockSpec((MOE_BLK, d), lambda b, be, nu: (b, 0))),
        out_shape=jax.ShapeDtypeStruct((n_blocks * MOE_BLK, d), F32),
        compiler_params=_params(("arbitrary",)),
        name="moe_experts",
    )(blk_expert.astype(jnp.int32), n_used, xs, w1, w3, w2)

    def mod_index(t):
        b, pos = t // tiles_per_row, t % tiles_per_row
        return (jnp.where(pos + t0 < nc, bsz, b), 0, 0)

    out = pl.pallas_call(
        functools.partial(_combine_kernel, final=final),
        grid=(tiles,),
        in_specs=[pl.BlockSpec((SUBLANES, TM), lambda t: (t, 0)),
                  pl.BlockSpec((TM, HEAD_W), lambda t: (t, 0)),
                  pl.BlockSpec((TM, d), lambda t: (t, 0)),
                  pl.BlockSpec((None, 6, d), mod_index),
                  _const_spec((1, d)),
                  pl.BlockSpec(memory_space=pl.ANY)],
        out_specs=pl.BlockSpec((TM, d), lambda t: (t, 0)),
        out_shape=jax.ShapeDtypeStruct((rows, d), F32),
        scratch_shapes=[pltpu.SMEM((SUBLANES, TM), jnp.int32),
                        pltpu.VMEM((2, TM, d), F32),
                        pltpu.SemaphoreType.DMA, pltpu.SemaphoreType.DMA],
        compiler_params=_params(("arbitrary",)),
        name="moe_combine",
    )(slots, route.reshape(rows, HEAD_W), h.reshape(rows, d), mod, fin_g, ys)
    return out.reshape(bsz, n_out, d)


def _pad_heads(w, dk):
    lead = w.shape[:-1]
    w = w.reshape(lead + (N_HEADS, dk))
    w = jnp.pad(w, [(0, 0)] * len(lead) + [(0, 0), (0, HEAD_W - dk)])
    return w.reshape(lead + (N_HEADS * HEAD_W,))


def _pad_cols(w, width):
    return jnp.pad(w, [(0, 0)] * (w.ndim - 1) + [(0, width - w.shape[-1])])


def _split_w_in(w_in):
    bw = BRANCH_W
    sizes = [bw, bw, N_HEADS * GLA_DK, N_HEADS * GLA_DK, bw, bw, 2 * GLA_RANK,
             bw, bw, 2 * bw, bw, bw, bw, bw, bw, 16]
    offs = np.cumsum([0] + sizes)
    p = [w_in[:, offs[i]:offs[i + 1]] for i in range(len(sizes))]
    w_rg = jnp.concatenate([p[0], p[1]], axis=1)
    w_gla = jnp.concatenate([_pad_heads(p[2], GLA_DK), _pad_heads(p[3], GLA_DK), p[4], p[5],
                             _pad_cols(p[6], HEAD_W)], axis=1)
    w_hg = jnp.concatenate([p[7], p[8], p[9], p[10]], axis=1)
    gates = p[15].reshape(-1, 2, 2, N_HEADS)
    w_gi = _pad_cols(gates[:, :, 0].reshape(-1, 2 * N_HEADS), HEAD_W)
    w_gf = _pad_cols(gates[:, :, 1].reshape(-1, 2 * N_HEADS), HEAD_W)
    w_ml = jnp.concatenate([p[11], p[12], p[13], p[14], w_gi, w_gf], axis=1)
    return [w.astype(BF16) for w in (w_rg, w_gla, w_hg, w_ml)]


def _block_diag(w):
    k, n = w.shape[-3], w.shape[-1]
    eye = jnp.eye(k, dtype=w.dtype)
    full = jnp.einsum('...kij,kl->...kilj', w, eye)
    return full.reshape(w.shape[:-3] + (k * n, k * n))


def kernel(x, c, ctx, c_ctx, ada_w, ada_b, norm_mix_g, norm_ffn_g, w_in, rg_conv_w, rg_conv_b, rg_gate_w, rg_gate_b, rg_lambda, gla_w_lr, gla_b_lr, gla_norm_g, hgrn_lb_logits, hgrn_norm_g, ml_conv_w, ml_conv_b, ml_gate_b, ml_norm_g, w_branch, w_merge, b_merge, w_out, moe_w_group, moe_b_group, moe_w_expert, moe_b_expert, moe_w1, moe_w3, moe_w2, final_norm_g):
    bsz, seq, d = x.shape
    n_ctx = ctx.shape[1]
    depth = ada_w.shape[0]
    assert n_ctx % TM == 0 and seq % TM == 0 and d == 2 * BRANCH_W
    nc = n_ctx // TM

    h = jnp.concatenate([ctx, x], axis=1)
    cvec = jnp.zeros((SUBLANES, d), F32).at[:bsz].set(c).at[bsz].set(c_ctx)
    mod_all = _mod_call(cvec, ada_w, ada_b).reshape(depth, SUBLANES, 6, d)[:, :bsz + 1]

    lb_cum = jnp.cumsum(jax.nn.softmax(hgrn_lb_logits.astype(F32), axis=0), axis=0)
    hgrn_lb = lb_cum - lb_cum[:1]
    tables = [t for levels in SPLIT_LEVELS for t in _decay_tables(levels)]
    tri_sel = tables[0][:, :CHUNK, :]
    tri_mask = tables[1][:, 0]

    out = None
    for l in range(depth):
        last = l == depth - 1
        mod = mod_all[l]
        w_rg, w_gla, w_hg, w_ml = _split_w_in(w_in[l])
        z_rg, z_gla, z_hg, z_ml, u = _stage_a_call(h, mod, norm_mix_g[l], w_rg, w_gla, w_hg, w_ml, nc)

        rg_f, rg_b = _rg_call(z_rg, rg_conv_w[l], rg_conv_b[l].reshape(1, -1),
                              _block_diag(rg_gate_w[l]).astype(BF16),
                              rg_gate_b[l].reshape(4, -1), rg_lambda[l], nc)

        wlr = jnp.zeros((2, HEAD_W, N_HEADS * HEAD_W), F32)
        wlr_p = _pad_heads(gla_w_lr[l], GLA_DK)
        wlr = wlr.at[0, :GLA_RANK].set(wlr_p[0]).at[1, GLA_RANK:2 * GLA_RANK].set(wlr_p[1])
        gla_f, gla_b = _gla_call(z_gla, wlr.astype(BF16), _pad_heads(gla_b_lr[l], GLA_DK), tables, nc)

        lb = hgrn_lb[l]
        lbp = jnp.concatenate([lb, jnp.log(lb), jnp.log1p(-lb), jnp.zeros((2, lb.shape[-1]), F32)], axis=0)
        hg_f, hg_b = _hg_call(z_hg, lbp, tables, nc)

        gbias = _pad_cols(ml_gate_b[l].transpose(1, 0, 2).reshape(2, -1), HEAD_W)
        ml_f, ml_b = _ml_call(z_ml, ml_conv_w[l], ml_conv_b[l].reshape(1, -1), gbias, tri_sel, tri_mask, nc)

        gains = jnp.zeros((SUBLANES, HEAD_W), F32).at[0].set(gla_norm_g[l]).at[1].set(hgrn_norm_g[l]).at[2].set(ml_norm_g[l])
        w_route = _pad_cols(jnp.concatenate([moe_w_expert[l], moe_w_group[l]], axis=1), HEAD_W)
        b_route = _pad_cols(jnp.concatenate([moe_b_expert[l], moe_b_group[l]]).reshape(1, -1), HEAD_W)
        t0 = nc if last else 0
        h_mid, u2, route, cnt = _stage_c_call(
            u, h, mod, z_rg, rg_f, rg_b, z_gla, gla_f, gla_b, z_hg, hg_f, hg_b, z_ml, ml_f, ml_b,
            gains, w_merge[l].astype(BF16), b_merge[l].reshape(1, -1), w_branch[l].astype(BF16),
            w_out[l].astype(BF16), norm_ffn_g[l].reshape(1, -1), w_route, b_route, nc, t0)

        res = _moe_call(u2, route, cnt, h_mid, mod, final_norm_g.reshape(1, -1), moe_w1[l].astype(BF16),
                        moe_w3[l].astype(BF16), moe_w2[l].astype(BF16), nc, t0, last)
        if last:
            out = res
        else:
            h = res
    return out
```

```python
import functools

import numpy as np
import jax
import jax.numpy as jnp
from jax import lax
from jax.experimental import pallas as pl
from jax.experimental.pallas import tpu as pltpu

F32 = jnp.float32
BF16 = jnp.bfloat16

EPS = 1e-6
TM = 256
CHUNK = 64
SUBLANES = 8
N_HEADS = 4
HEAD_W = 128
BRANCH_W = 512
CONV_W = 4
CONV_LEFT = 2
RG_C = 8.0
GLA_DK = 64
GLA_RANK = 16
GLA_GATE_NORM = 16.0
N_GROUPS = 4
EXPERTS_PER_GROUP = 4
N_EXPERTS = 16
MOE_BLK = 256
ROUTE_IDX = 16
ROUTE_W = 18
N_LEVELS = 6
MILD_DECAY = -40.0
VMEM_LIMIT = 56 * 1024 * 1024

W_RG = 2 * BRANCH_W
W_GLA = 4 * BRANCH_W + HEAD_W
W_HG = 5 * BRANCH_W
W_ML = 4 * BRANCH_W + 2 * HEAD_W
SPLIT_LEVELS = (0, 2, N_LEVELS)


def _mm(a, b):
    return jnp.dot(a.astype(BF16), b.astype(BF16), preferred_element_type=F32)


def _mm_nt(a, b):
    return lax.dot_general(a.astype(BF16), b.astype(BF16), (((1,), (1,)), ((), ())),
                           preferred_element_type=F32)


def _mm_tn(a, b):
    return lax.dot_general(a.astype(BF16), b.astype(BF16), (((0,), (0,)), ((), ())),
                           preferred_element_type=F32)


def _mm_sel(sel, x):
    x1 = x.astype(BF16)
    x2 = (x - x1.astype(F32)).astype(BF16)
    dot = functools.partial(jnp.dot, preferred_element_type=F32)
    return dot(sel, x1) + dot(sel, x2)


def _log_sigmoid(x):
    return jnp.minimum(x, 0.0) - jnp.log1p(jnp.exp(-jnp.abs(x)))


def _silu(x):
    return x * jax.nn.sigmoid(x)


def _rmsnorm_rows(x, g):
    return x * lax.rsqrt(jnp.mean(x * x, axis=-1, keepdims=True) + EPS) * g


def _head_rmsnorm(o, g):
    parts = [_rmsnorm_rows(o[:, h * HEAD_W:(h + 1) * HEAD_W], g) for h in range(N_HEADS)]
    return jnp.concatenate(parts, axis=-1)


def _bwd_tile(s, nc, nt):
    return jnp.where(s < nc, nc - 1 - s, nt - 1 - (s - nc))


def _const_spec(shape):
    nd = len(shape)
    return pl.BlockSpec(shape, lambda *_: (0,) * nd)


def _params(sem):
    return pltpu.CompilerParams(dimension_semantics=sem, vmem_limit_bytes=VMEM_LIMIT)


def _mod_kernel(c_ref, w_ref, b_ref, o_ref):
    cv = _silu(c_ref[...])
    o_ref[...] = jnp.dot(cv, w_ref[...], precision=lax.Precision.HIGHEST,
                         preferred_element_type=F32) + b_ref[...]


def _mod_call(cvec, ada_w, ada_b):
    depth, d, six_d = ada_w.shape
    tn = 1024
    return pl.pallas_call(
        _mod_kernel,
        grid=(depth, six_d // tn),
        in_specs=[pl.BlockSpec((SUBLANES, d), lambda l, j: (0, 0)),
                  pl.BlockSpec((None, d, tn), lambda l, j: (l, 0, j)),
                  pl.BlockSpec((None, 1, tn), lambda l, j: (l, 0, j))],
        out_specs=pl.BlockSpec((None, SUBLANES, tn), lambda l, j: (l, 0, j)),
        out_shape=jax.ShapeDtypeStruct((depth, SUBLANES, six_d), F32),
        compiler_params=_params(("parallel", "parallel")),
        name="adaln_mod",
    )(cvec, ada_w, ada_b.reshape(depth, 1, six_d))


def _stage_a_kernel(h_ref, mod_ref, g_ref, w_rg, w_gla, w_hg, w_ml,
                    z_rg, z_gla, z_hg, z_ml, u_ref):
    x = h_ref[...]
    u = _rmsnorm_rows(x, g_ref[...]) * (1.0 + mod_ref[1:2, :]) + mod_ref[0:1, :]
    ub = u.astype(BF16)
    u_ref[...] = ub
    z_rg[...] = jnp.dot(ub, w_rg[...], preferred_element_type=F32)
    z_gla[...] = jnp.dot(ub, w_gla[...], preferred_element_type=F32)
    z_hg[...] = jnp.dot(ub, w_hg[...], preferred_element_type=F32)
    z_ml[...] = jnp.dot(ub, w_ml[...], preferred_element_type=F32)


def _stage_a_call(h, mod, norm_g, w_rg, w_gla, w_hg, w_ml, nc):
    bsz, n, d = h.shape
    nt = n // TM
    tm = TM // 2

    def tok(w):
        return pl.BlockSpec((None, tm, w), lambda b, t: (b, t, 0))

    return pl.pallas_call(
        _stage_a_kernel,
        grid=(bsz, n // tm),
        in_specs=[tok(d),
                  pl.BlockSpec((None, 6, d), lambda b, t: (jnp.where(t * tm < nc * TM, bsz, b), 0, 0)),
                  _const_spec((1, d)),
                  _const_spec((d, W_RG)), _const_spec((d, W_GLA)),
                  _const_spec((d, W_HG)), _const_spec((d, W_ML))],
        out_specs=[tok(W_RG), tok(W_GLA), tok(W_HG), tok(W_ML), tok(d)],
        out_shape=[jax.ShapeDtypeStruct((bsz, n, W_RG), F32),
                   jax.ShapeDtypeStruct((bsz, n, W_GLA), F32),
                   jax.ShapeDtypeStruct((bsz, n, W_HG), F32),
                   jax.ShapeDtypeStruct((bsz, n, W_ML), F32),
                   jax.ShapeDtypeStruct((bsz, n, d), BF16)],
        compiler_params=_params(("parallel", "parallel")),
        name="stage_a",
    )(h, mod, norm_g.reshape(1, d), w_rg, w_gla, w_hg, w_ml)


def _conv_tile(x_ref, prev_ref, next_ref, pad_ref, cw_ref, cb_ref, tile, nc, nt):
    prev_ok = jnp.logical_and(tile != 0, tile != nc)
    next_ok = jnp.logical_and(tile != nc - 1, tile != nt - 1)
    pad_ref[0:SUBLANES, :] = jnp.where(prev_ok, prev_ref[...], 0.0)
    pad_ref[SUBLANES:SUBLANES + TM, :] = x_ref[...]
    pad_ref[SUBLANES + TM:2 * SUBLANES + TM, :] = jnp.where(next_ok, next_ref[...], 0.0)
    acc = cb_ref[...]
    for j in range(CONV_W):
        off = SUBLANES - CONV_LEFT + j
        acc = acc + cw_ref[j:j + 1, :] * pad_ref[off:off + TM, :]
    return acc


def _halo_specs(width, col_block, tile_fn, n):
    rows = TM // SUBLANES
    last = n // SUBLANES - 1
    cur = pl.BlockSpec((None, TM, width), lambda b, s: (b, tile_fn(s), col_block))
    prev = pl.BlockSpec((None, SUBLANES, width),
                        lambda b, s: (b, jnp.maximum(tile_fn(s) * rows - 1, 0), col_block))
    nxt = pl.BlockSpec((None, SUBLANES, width),
                       lambda b, s: (b, jnp.minimum((tile_fn(s) + 1) * rows, last), col_block))
    return [cur, prev, nxt]


def _scan_rows8(a, b, reverse):
    n = a.shape[0]
    pos = jnp.bitwise_and(lax.broadcasted_iota(jnp.int32, a.shape, 0), SUBLANES - 1)
    k = 1
    while k < SUBLANES:
        if reverse:
            a_s, b_s, ok = pltpu.roll(a, n - k, 0), pltpu.roll(b, n - k, 0), pos < SUBLANES - k
        else:
            a_s, b_s, ok = pltpu.roll(a, k, 0), pltpu.roll(b, k, 0), pos >= k
        b = b + a * jnp.where(ok, b_s, 0.0)
        a = a * jnp.where(ok, a_s, 1.0)
        k *= 2
    return a, b


def _scan_tile(a, b, carry, o_ref, reverse):
    a8, b8 = _scan_rows8(a, b, reverse)
    groups = a.shape[0] // SUBLANES
    order = range(groups - 1, -1, -1) if reverse else range(groups)
    for r in order:
        rows = slice(r * SUBLANES, (r + 1) * SUBLANES)
        h = b8[rows] + a8[rows] * carry
        o_ref[rows, :] = h
        carry = h[0:1, :] if reverse else h[SUBLANES - 1:SUBLANES, :]
    return carry


def _rg_kernel(xf, xf_p, xf_n, xb, xb_p, xb_n, cw, cb, gw, gb, lam,
               hf_ref, hb_ref, pad_ref, carry_ref, *, nc, nt):
    s = pl.program_id(1)

    @pl.when(s == 0)
    def _():
        carry_ref[...] = jnp.zeros_like(carry_ref)

    dirs = ((xf, xf_p, xf_n, hf_ref, s, False),
            (xb, xb_p, xb_n, hb_ref, _bwd_tile(s, nc, nt), True))
    for d, (x_ref, p_ref, n_ref, o_ref, tile, reverse) in enumerate(dirs):
        x = _conv_tile(x_ref, p_ref, n_ref, pad_ref, cw, cb, tile, nc, nt)
        r = jax.nn.sigmoid(_mm(x, gw[d, 0]) + gb[2 * d:2 * d + 1, :])
        i = jax.nn.sigmoid(_mm(x, gw[d, 1]) + gb[2 * d + 1:2 * d + 2, :])
        lam_d = lam[d:d + 1, :]
        softplus = jnp.maximum(-lam_d, 0.0) + jnp.log1p(jnp.exp(-jnp.abs(lam_d)))
        log_a = -RG_C * r * softplus
        a = jnp.exp(log_a)
        t = jnp.tanh(log_a)
        bt = jnp.sqrt(-2.0 * t / (1.0 - t)) * (i * x)
        carry_ref[d:d + 1, :] = _scan_tile(a, bt, carry_ref[d:d + 1, :], o_ref, reverse)


def _rg_call(z_rg, cw, cb, gw, gb, lam, nc):
    bsz, n, _ = z_rg.shape
    nt = n // TM
    w = BRANCH_W
    fwd = lambda s: s
    bwd = lambda s: _bwd_tile(s, nc, nt)
    out = lambda fn: pl.BlockSpec((None, TM, w), lambda b, s: (b, fn(s), 0))
    return pl.pallas_call(
        functools.partial(_rg_kernel, nc=nc, nt=nt),
        grid=(bsz, nt),
        in_specs=_halo_specs(w, 0, fwd, n) + _halo_specs(w, 0, bwd, n) + [
            _const_spec((CONV_W, w)), _const_spec((1, w)),
            _const_spec((2, 2, w, w)), _const_spec((4, w)), _const_spec((2, w))],
        out_specs=[out(fwd), out(bwd)],
        out_shape=[jax.ShapeDtypeStruct((bsz, n, w), F32)] * 2,
        scratch_shapes=[pltpu.VMEM((TM + 2 * SUBLANES, w), F32), pltpu.VMEM((2, w), F32)],
        compiler_params=_params(("parallel", "arbitrary")),
        name="rglru",
    )(z_rg, z_rg, z_rg, z_rg, z_rg, z_rg, cw, cb, gw, gb, lam)


def _decay_tables(levels):
    n = CHUNK
    blk = n >> levels
    sel = np.zeros((levels + 3, n, n), np.float32)
    masks = np.zeros((levels + 1, n, n), np.float32)
    for lvl in range(levels):
        half = n >> (lvl + 1)
        for r in range(n):
            start = (r // (2 * half)) * 2 * half
            ref = start + half - 1
            if r - start >= half:
                sel[lvl, r, ref + 1:r + 1] = 1.0
                masks[lvl, r, start:start + half] = 1.0
            else:
                sel[lvl, r, r + 1:ref + 1] = 1.0
    sel[levels] = np.tril(np.ones((n, n), np.float32))
    sel[levels + 1] = 1.0 - sel[levels]
    for r in range(n):
        start = (r // blk) * blk
        sel[levels + 2, r, r + 1:start + blk] = 1.0
        masks[levels, r, start:r + 1] = 1.0
    if levels == 0 or levels == N_LEVELS:
        sel = sel[:levels + 2]
    sel_b = sel[:, ::-1, ::-1].reshape(-1, n)
    masks_b = masks[:, ::-1, ::-1]
    return (jnp.asarray(np.stack([sel.reshape(-1, n), sel_b]), BF16),
            jnp.asarray(np.stack([masks, masks_b]), F32))


def _ref_rows(levels, reverse):
    def runs(block, offset):
        return [(s + offset, block) for s in range(0, CHUNK, block)]

    out = [runs(CHUNK >> lvl, (CHUNK >> (lvl + 1)) - 1) for lvl in range(levels)]
    out.append(runs(CHUNK, CHUNK - 1))
    out.append(runs(CHUNK >> levels, (CHUNK >> levels) - 1))
    if reverse:
        out = [[(CHUNK - 1 - r, n) for r, n in reversed(segs)] for segs in out]
    return out


def _rows_bcast(b, segs):
    return jnp.concatenate([jnp.broadcast_to(b[r:r + 1, :], (n, b.shape[-1])) for r, n in segs], axis=0)


def _gla_chunk(q, k, v, g, sel, masks, st_ref, d, reverse, levels):
    if levels == N_LEVELS:
        x = jnp.exp(_mm_sel(sel, g))
        x_lvl = [x[lvl * CHUNK:(lvl + 1) * CHUNK] for lvl in range(levels)]
        x_cum = x[levels * CHUNK:(levels + 1) * CHUNK]
        x_rest = x[(levels + 1) * CHUNK:(levels + 2) * CHUNK]
        xk_blk = xq_blk = None
    else:
        b = _mm_sel(sel[levels * CHUNK:(levels + 1) * CHUNK], g)
        refs = _ref_rows(levels, reverse)
        x_cum = jnp.exp(b)
        x_lvl = [jnp.exp(-jnp.abs(b - _rows_bcast(b, segs))) for segs in refs[:levels]]
        x_rest = jnp.exp(-jnp.abs(b - _rows_bcast(b, refs[levels])))
        if levels == 0:
            xk_blk, xq_blk = x_rest, jnp.exp(jnp.abs(b - _rows_bcast(b, refs[levels])))
        else:
            d_blk = jnp.abs(b - _rows_bcast(b, refs[levels + 1]))
            xk_blk, xq_blk = jnp.exp(-d_blk), jnp.exp(d_blk)
    outs = []
    for h in range(N_HEADS):
        cols = slice(h * HEAD_W, (h + 1) * HEAD_W)
        qh, kh, vh = q[:, cols], k[:, cols], v[:, cols]
        kx = kh * x_rest[:, cols]
        if xk_blk is None:
            sc = _mm_nt(qh, kh)
        elif levels == 0:
            sc = _mm_nt(qh * xq_blk[:, cols], kx)
        else:
            sc = _mm_nt(qh * xq_blk[:, cols], kh * xk_blk[:, cols])
        sc = jnp.where(masks[levels] > 0.0, sc, 0.0)
        for lvl in range(levels):
            xl = x_lvl[lvl][:, cols]
            sc = sc + jnp.where(masks[lvl] > 0.0, _mm_nt(qh * xl, kh * xl), 0.0)
        st = st_ref[d, h]
        outs.append(_mm(sc, vh) + _mm_nt(qh * x_cum[:, cols], st))
        x_end = x_cum[0:1, cols] if reverse else x_cum[CHUNK - 1:CHUNK, cols]
        st_ref[d, h] = st * x_end + _mm_tn(vh, kx)
    return jnp.concatenate(outs, axis=-1)


def _gla_step(sides, tabs, st_ref):
    n_chunks = TM // CHUNK

    def min_block_sum(block):
        mins = [jnp.min(jnp.sum(g.reshape(TM // block, block, g.shape[-1]), axis=1))
                for _, _, _, g, _ in sides]
        return jnp.minimum(mins[0], mins[1])

    conds = []
    taken = None
    for levels in SPLIT_LEVELS[:-1]:
        ok = min_block_sum(CHUNK >> levels) > MILD_DECAY
        conds.append(ok if taken is None else jnp.logical_and(ok, jnp.logical_not(taken)))
        taken = ok if taken is None else jnp.logical_or(taken, ok)
    conds.append(jnp.logical_not(taken))

    for levels, cond, (sel_ref, mask_ref) in zip(SPLIT_LEVELS, conds, tabs):
        @pl.when(cond)
        def _(levels=levels, sel_ref=sel_ref, mask_ref=mask_ref):
            for c in range(n_chunks):
                for d, (q, k, v, g, o_ref) in enumerate(sides):
                    reverse = d == 1
                    cc = n_chunks - 1 - c if reverse else c
                    rows = slice(cc * CHUNK, (cc + 1) * CHUNK)
                    o_ref[rows, :] = _gla_chunk(q[rows], k[rows], v[rows], g[rows], sel_ref[d],
                                                mask_ref[d], st_ref, d, reverse, levels)


def _pairs(refs):
    return tuple(zip(refs[0::2], refs[1::2]))


def _gla_kernel(zf, zb, wlr, blr, *rest):
    tabs, (of_ref, ob_ref, st_ref) = _pairs(rest[:-3]), rest[-3:]
    w = BRANCH_W

    @pl.when(pl.program_id(1) == 0)
    def _():
        st_ref[...] = jnp.zeros_like(st_ref)

    sides = []
    for d, (z_ref, o_ref) in enumerate(((zf, of_ref), (zb, ob_ref))):
        pre = _mm(z_ref[:, 4 * w:4 * w + HEAD_W], wlr[d]) + blr[d:d + 1, :]
        g = _log_sigmoid(pre) * (1.0 / GLA_GATE_NORM)
        q = z_ref[:, 0:w] * (GLA_DK ** -0.5)
        sides.append((q, z_ref[:, w:2 * w], z_ref[:, 2 * w:3 * w], g, o_ref))
    _gla_step(sides, tabs, st_ref)


def _hg_kernel(zf, zb, lbp, *rest):
    tabs, (of_ref, ob_ref, st_ref) = _pairs(rest[:-3]), rest[-3:]
    w = BRANCH_W

    @pl.when(pl.program_id(1) == 0)
    def _():
        st_ref[...] = jnp.zeros_like(st_ref)

    sides = []
    for d, (z_ref, o_ref) in enumerate(((zf, of_ref), (zb, ob_ref))):
        f = z_ref[:, (2 + d) * w:(3 + d) * w]
        lb = lbp[d:d + 1, :]
        log_lb = lbp[2 + d:3 + d, :]
        log_1m = lbp[4 + d:5 + d, :]
        c = log_1m + _log_sigmoid(f)
        g = jnp.maximum(log_lb, c) + jnp.log1p(jnp.exp(-jnp.abs(log_lb - c)))
        k = (1.0 - lb) * jax.nn.sigmoid(-f)
        q = _silu(z_ref[:, 0:w]) * (HEAD_W ** -0.5)
        sides.append((q, k, z_ref[:, w:2 * w], g, o_ref))
    _gla_step(sides, tabs, st_ref)


def _mixer_specs(widths_cols, tile_fn):
    return [pl.BlockSpec((None, TM, w), lambda b, s, c=c: (b, tile_fn(s), c)) for w, c in widths_cols]


def _gla_call(z_gla, wlr, blr, tables, nc):
    bsz, n, _ = z_gla.shape
    nt = n // TM
    w = BRANCH_W
    fwd = lambda s: s
    bwd = lambda s: _bwd_tile(s, nc, nt)
    cols = [(W_GLA, 0)]
    out = lambda fn: pl.BlockSpec((None, TM, w), lambda b, s: (b, fn(s), 0))
    return pl.pallas_call(
        _gla_kernel,
        grid=(bsz, nt),
        in_specs=_mixer_specs(cols, fwd) + _mixer_specs(cols, bwd) + [
            _const_spec(wlr.shape), _const_spec(blr.shape)] + [_const_spec(t.shape) for t in tables],
        out_specs=[out(fwd), out(bwd)],
        out_shape=[jax.ShapeDtypeStruct((bsz, n, w), F32)] * 2,
        scratch_shapes=[pltpu.VMEM((2, N_HEADS, HEAD_W, HEAD_W), F32)],
        compiler_params=_params(("parallel", "arbitrary")),
        name="gla",
    )(z_gla, z_gla, wlr, blr, *tables)


def _hg_call(z_hg, lbp, tables, nc):
    bsz, n, _ = z_hg.shape
    nt = n // TM
    w = BRANCH_W
    fwd = lambda s: s
    bwd = lambda s: _bwd_tile(s, nc, nt)
    out = lambda fn: pl.BlockSpec((None, TM, w), lambda b, s: (b, fn(s), 0))
    return pl.pallas_call(
        _hg_kernel,
        grid=(bsz, nt),
        in_specs=_mixer_specs([(4 * w, 0)], fwd) + _mixer_specs([(4 * w, 0)], bwd) + [
            _const_spec(lbp.shape)] + [_const_spec(t.shape) for t in tables],
        out_specs=[out(fwd), out(bwd)],
        out_shape=[jax.ShapeDtypeStruct((bsz, n, w), F32)] * 2,
        scratch_shapes=[pltpu.VMEM((2, N_HEADS, HEAD_W, HEAD_W), F32)],
        compiler_params=_params(("parallel", "arbitrary")),
        name="hgrn2",
    )(z_hg, z_hg, lbp, *tables)


def _cummax_rows(a, reverse):
    n = a.shape[0]
    row = lax.broadcasted_iota(jnp.int32, a.shape, 0)
    k = 1
    while k < n:
        if reverse:
            a_s, ok = pltpu.roll(a, n - k, 0), row < n - k
        else:
            a_s, ok = pltpu.roll(a, k, 0), row >= k
        a = jnp.maximum(a, jnp.where(ok, a_s, -jnp.inf))
        k *= 2
    return a


def _ml_chunk(q, k, v, gi, log_f, tri_sel, tri_mask, ct_ref, m_ref, d, reverse):
    b = _mm_sel(tri_sel, log_f)
    a = gi - b
    m_prev = m_ref[d:d + 1, :]
    m_rel = jnp.maximum(_cummax_rows(a, reverse), m_prev)
    w_inter = jnp.exp(m_prev - m_rel)
    exp_neg_m = jnp.exp(-(b + m_rel))
    last = 0 if reverse else CHUNK - 1
    b_end = b[last:last + 1, :]
    m_new = b_end + m_rel[last:last + 1, :]
    w_end = jnp.exp(b_end - b + gi - m_new)
    decay = jnp.exp(b_end + m_prev - m_new)
    m_ref[d:d + 1, :] = m_new
    a_t = a.T
    ones = jnp.ones((CHUNK, HEAD_W), BF16)
    outs = []
    for h in range(N_HEADS):
        cols = slice(h * HEAD_W, (h + 1) * HEAD_W)
        qh, kh, vh = q[:, cols], k[:, cols], v[:, cols]
        c = d * N_HEADS + h
        log_w = jnp.where(tri_mask > 0.0, a_t[c:c + 1, :] - m_rel[:, c:c + 1], -jnp.inf)
        s = _mm_nt(qh, kh) * jnp.exp(log_w)
        v_ext = jnp.concatenate([vh.astype(BF16), ones], axis=-1)
        ct = ct_ref[c]
        tot = _mm(s, v_ext) + w_inter[:, c:c + 1] * _mm_nt(qh, ct)
        num, den = tot[:, :HEAD_W], tot[:, HEAD_W:]
        outs.append(num / jnp.maximum(jnp.abs(den), exp_neg_m[:, c:c + 1]))
        ct_ref[c] = decay[:, c:c + 1] * ct + _mm_tn(v_ext, kh * w_end[:, c:c + 1])
    return jnp.concatenate(outs, axis=-1)


def _ml_kernel(zf, zf_p, zf_n, zb, zb_p, zb_n, cw, cb, gbias,
               tri_sel, tri_mask, of_ref, ob_ref, pad_ref, ct_ref, m_ref, *, nc, nt):
    s = pl.program_id(1)
    w = BRANCH_W

    @pl.when(s == 0)
    def _():
        ct_ref[...] = jnp.zeros_like(ct_ref)
        m_ref[...] = jnp.zeros_like(m_ref)

    dirs = ((zf, zf_p, zf_n, of_ref, s, False),
            (zb, zb_p, zb_n, ob_ref, _bwd_tile(s, nc, nt), True))
    for d, (z_ref, p_ref, nx_ref, o_ref, tile, reverse) in enumerate(dirs):
        qk = _silu(_conv_tile(z_ref.at[:, 0:2 * w], p_ref, nx_ref, pad_ref, cw, cb, tile, nc, nt))
        q = qk[:, :w] * (HEAD_W ** -0.5)
        k = qk[:, w:]
        v = z_ref[:, 2 * w:3 * w]
        gi = z_ref[:, 4 * w:4 * w + HEAD_W] + gbias[0:1, :]
        log_f = _log_sigmoid(z_ref[:, 4 * w + HEAD_W:4 * w + 2 * HEAD_W] + gbias[1:2, :])
        order = range(TM // CHUNK - 1, -1, -1) if reverse else range(TM // CHUNK)
        for c in order:
            rows = slice(c * CHUNK, (c + 1) * CHUNK)
            o_ref[rows, :] = _ml_chunk(q[rows], k[rows], v[rows], gi[rows], log_f[rows],
                                       tri_sel[d], tri_mask[d], ct_ref, m_ref, d, reverse)


def _ml_call(z_ml, cw, cb, gbias, tri_sel, tri_mask, nc):
    bsz, n, _ = z_ml.shape
    nt = n // TM
    w = BRANCH_W
    fwd = lambda s: s
    bwd = lambda s: _bwd_tile(s, nc, nt)
    out = lambda fn: pl.BlockSpec((None, TM, w), lambda b, s: (b, fn(s), 0))

    def side(fn):
        return _mixer_specs([(W_ML, 0)], fn) + _halo_specs(2 * w, 0, fn, n)[1:]

    return pl.pallas_call(
        functools.partial(_ml_kernel, nc=nc, nt=nt),
        grid=(bsz, nt),
        in_specs=side(fwd) + side(bwd) + [
            _const_spec((CONV_W, 2 * w)), _const_spec((1, 2 * w)), _const_spec((2, HEAD_W)),
            _const_spec(tri_sel.shape), _const_spec(tri_mask.shape)],
        out_specs=[out(fwd), out(bwd)],
        out_shape=[jax.ShapeDtypeStruct((bsz, n, w), F32)] * 2,
        scratch_shapes=[pltpu.VMEM((TM + 2 * SUBLANES, 2 * w), F32),
                        pltpu.VMEM((2 * N_HEADS, 2 * HEAD_W, HEAD_W), F32),
                        pltpu.VMEM((2, HEAD_W), F32)],
        compiler_params=_params(("parallel", "arbitrary")),
        name="mlstm",
    )(*([z_ml] * 6), cw, cb, gbias, tri_sel, tri_mask)


def _gelu_tanh(x):
    return 0.5 * x * (1.0 + jnp.tanh(0.7978845608028654 * (x + 0.044715 * (x * x * x))))


def _route(logits):
    col = lax.broadcasted_iota(jnp.int32, logits.shape, 1)
    colf = col.astype(F32)
    is_g = jnp.logical_and(col >= N_EXPERTS, col < N_EXPERTS + N_GROUPS)
    is_e = col < N_EXPERTS
    neg = -jnp.inf
    big = 1e9
    gl = jnp.where(is_g, logits, neg)
    g_max = jnp.max(gl, axis=-1, keepdims=True)
    grp = jnp.min(jnp.where(gl == g_max, colf, big), axis=-1, keepdims=True) - N_EXPERTS
    p_grp = 1.0 / jnp.sum(jnp.exp(gl - g_max), axis=-1, keepdims=True)
    col_grp = lax.shift_right_logical(col, EXPERTS_PER_GROUP.bit_length() - 1)
    in_grp = jnp.logical_and(is_e, col_grp.astype(F32) == grp)
    e1 = jnp.where(in_grp, logits, neg)
    top1 = jnp.max(e1, axis=-1, keepdims=True)
    idx1 = jnp.min(jnp.where(e1 == top1, colf, big), axis=-1, keepdims=True)
    e2 = jnp.where(colf == idx1, neg, e1)
    top2 = jnp.max(e2, axis=-1, keepdims=True)
    idx2 = jnp.min(jnp.where(e2 == top2, colf, big), axis=-1, keepdims=True)
    t = jnp.exp(top2 - top1)
    w1 = p_grp / (1.0 + t)
    w2 = p_grp * (t / (1.0 + t))
    out = jnp.where(col == ROUTE_IDX, idx1, jnp.where(col == ROUTE_IDX + 1, idx2, 0.0))
    return jnp.where(col == ROUTE_W, w1, jnp.where(col == ROUTE_W + 1, w2, out))


def _route_onehot(route):
    colf = lax.broadcasted_iota(jnp.int32, route.shape, 1).astype(F32)
    oh1 = colf == route[:, ROUTE_IDX:ROUTE_IDX + 1]
    oh2 = colf == route[:, ROUTE_IDX + 1:ROUTE_IDX + 2]
    return jnp.where(jnp.logical_or(oh1, oh2), 1.0, 0.0), oh1, oh2


def _stage_c_kernel(u_ref, h_ref, mod_ref, rg_y, rg_f, rg_b, gla_g, gla_f, gla_b, hg_g, hg_f, hg_b,
                    ml_o, ml_f, ml_b, gains, w_merge, b_merge, w_branch, w_out, ffn_g, w_route, b_route,
                    h_out, u2_out, wt_out, cnt_out):
    ys = (
        _gelu_tanh(rg_y[...]) * (rg_f[...] + rg_b[...]),
        _head_rmsnorm(gla_f[...] + gla_b[...], gains[0:1, :]) * _silu(gla_g[...]),
        _head_rmsnorm(hg_f[...] + hg_b[...], gains[1:2, :]) * _silu(hg_g[...]),
        jax.nn.sigmoid(ml_o[...]) * _head_rmsnorm(ml_f[...] + ml_b[...], gains[2:3, :]),
    )
    u = u_ref[...]
    d = u.shape[-1]
    merged = None
    for kk, y in enumerate(ys):
        gate = jax.nn.sigmoid(jnp.dot(u, w_merge[:, kk * d:(kk + 1) * d], preferred_element_type=F32)
                              + b_merge[:, kk * d:(kk + 1) * d])
        term = gate * _mm(y, w_branch[kk])
        merged = term if merged is None else merged + term
    mix = _mm(merged, w_out[...])
    h_new = h_ref[...] + mod_ref[2:3, :] * mix
    h_out[...] = h_new
    u2 = _rmsnorm_rows(h_new, ffn_g[...]) * (1.0 + mod_ref[4:5, :]) + mod_ref[3:4, :]
    _store_tiled(u2_out, u2)
    logits = jnp.dot(u2, w_route[...], precision=lax.Precision.HIGHEST,
                     preferred_element_type=F32) + b_route[...]
    route = _route(logits)
    wt_out[...] = route
    cnt = jnp.sum(_route_onehot(route)[0], axis=0, keepdims=True)
    cnt_out[...] = jnp.broadcast_to(cnt, cnt_out.shape)


def _stage_c_call(u, h, mod, z_rg, rg_f, rg_b, z_gla, gla_f, gla_b, z_hg, hg_f, hg_b, z_ml, ml_f, ml_b,
                  gains, w_merge, b_merge, w_branch, w_out, ffn_g, w_route, b_route, nc, t0):
    bsz, n, d = h.shape
    nt = n // TM
    w = BRANCH_W

    n_out = n - t0 * TM

    def tok(width, col=0):
        return pl.BlockSpec((None, TM, width), lambda b, t: (b, t + t0, col))

    def out_tok(width):
        return pl.BlockSpec((None, TM, width), lambda b, t: (b, t, 0))

    mod_spec = pl.BlockSpec((None, 6, d), lambda b, t: (jnp.where(t + t0 < nc, bsz, b), 0, 0))
    return pl.pallas_call(
        _stage_c_kernel,
        grid=(bsz, nt - t0),
        in_specs=[tok(d), tok(d), mod_spec,
                  tok(w, 1), tok(w), tok(w),
                  tok(w, 3), tok(w), tok(w),
                  tok(w, 4), tok(w), tok(w),
                  tok(w, 3), tok(w), tok(w),
                  _const_spec(gains.shape), _const_spec(w_merge.shape), _const_spec(b_merge.shape),
                  _const_spec(w_branch.shape), _const_spec(w_out.shape), _const_spec(ffn_g.shape),
                  _const_spec(w_route.shape), _const_spec(b_route.shape)],
        out_specs=[out_tok(d),
                   pl.BlockSpec((None, TM * SUBLANES, HEAD_W), lambda b, t: (b, t, 0)),
                   out_tok(HEAD_W),
                   pl.BlockSpec((None, None, SUBLANES, HEAD_W), lambda b, t: (b, t, 0, 0))],
        out_shape=[jax.ShapeDtypeStruct((bsz, n_out, d), F32),
                   jax.ShapeDtypeStruct((bsz, n_out * SUBLANES, HEAD_W), F32),
                   jax.ShapeDtypeStruct((bsz, n_out, HEAD_W), F32),
                   jax.ShapeDtypeStruct((bsz, nt - t0, SUBLANES, HEAD_W), F32)],
        compiler_params=_params(("parallel", "parallel")),
        name="stage_c",
    )(u, h, mod, z_rg, rg_f, rg_b, z_gla, gla_f, gla_b, z_hg, hg_f, hg_b, z_ml, ml_f, ml_b,
      gains, w_merge, b_merge, w_branch, w_out, ffn_g, w_route, b_route)


def _slots_to_smem(slots_vmem, slots_smem, sem):
    cp = pltpu.make_async_copy(slots_vmem, slots_smem, sem)
    cp.start()
    cp.wait()


def _row_copy(src, src_row, dst, dst_row, sem):
    def tile(ref, row):
        return ref.at[pl.ds(pl.multiple_of(row * SUBLANES, SUBLANES), SUBLANES), :]
    return pltpu.make_async_copy(tile(src, src_row), tile(dst, dst_row), sem)


def _to_tiles(x):
    return [x[:, k * HEAD_W:(k + 1) * HEAD_W] for k in range(SUBLANES)]


def _store_tiled(ref, x):
    rows = x.shape[0]
    for k, piece in enumerate(_to_tiles(x)):
        ref[pl.ds(k, rows, stride=SUBLANES), :] = piece


def _load_tiled(ref, rows):
    return jnp.concatenate([ref[pl.ds(k, rows, stride=SUBLANES), :] for k in range(SUBLANES)], axis=-1)


def _dispatch_kernel(route_ref, base_ref, tri_ref, x_ref, xs_in, xs_hbm, slots_ref,
                     slots_smem, sem_s, sem_r):
    del xs_in
    route = route_ref[...]
    onehot, oh1, oh2 = _route_onehot(route)
    rank = jnp.dot(tri_ref[...], onehot.astype(BF16), preferred_element_type=F32)
    pos = base_ref[...] + rank
    slot_a = jnp.sum(jnp.where(oh1, pos, 0.0), axis=-1, keepdims=True)
    slot_b = jnp.sum(jnp.where(oh2, pos, 0.0), axis=-1, keepdims=True)
    col = lax.broadcasted_iota(jnp.int32, route.shape, 1)
    both = jnp.where(col == 0, slot_a, jnp.where(col == 1, slot_b, 0.0))
    slots_ref[...] = both.T[0:SUBLANES, :].astype(jnp.int32)
    _slots_to_smem(slots_ref, slots_smem, sem_s)

    def start(t, carry):
        for j in range(2):
            _row_copy(x_ref, t, xs_hbm, slots_smem[j, t], sem_r).start(priority=j)
        return carry

    def wait(t, carry):
        for j in range(2):
            _row_copy(x_ref, 0, xs_hbm, 0, sem_r).wait()
        return carry

    lax.fori_loop(0, TM, start, 0, unroll=8)
    lax.fori_loop(0, TM, wait, 0, unroll=8)


def _experts_kernel(blk_expert, n_used, x_ref, w1, w3, w2, o_ref):
    del blk_expert

    @pl.when(pl.program_id(0) < n_used[0])
    def _():
        x = _load_tiled(x_ref, MOE_BLK).astype(BF16)
        h1 = jnp.dot(x, w1[...], preferred_element_type=F32)
        h3 = jnp.dot(x, w3[...], preferred_element_type=F32)
        _store_tiled(o_ref, _mm(_silu(h1) * h3, w2[...]))

    @pl.when(pl.program_id(0) >= n_used[0])
    def _():
        o_ref[...] = jnp.zeros_like(o_ref)


def _combine_kernel(slots_ref, route_ref, h_ref, mod_ref, fin_g, ys_hbm, o_ref,
                    slots_smem, buf, sem_s, sem_r, *, final):
    _slots_to_smem(slots_ref, slots_smem, sem_s)

    def start(t, carry):
        for j in range(2):
            _row_copy(ys_hbm, slots_smem[j, t], buf.at[j], t, sem_r).start(priority=j)
        return carry

    def wait(t, carry):
        for j in range(2):
            _row_copy(ys_hbm, 0, buf.at[j], 0, sem_r).wait()
        return carry

    lax.fori_loop(0, TM, start, 0, unroll=8)
    lax.fori_loop(0, TM, wait, 0, unroll=8)
    route = route_ref[...]
    y = (route[:, ROUTE_W:ROUTE_W + 1] * _load_tiled(buf.at[0], TM)
         + route[:, ROUTE_W + 1:ROUTE_W + 2] * _load_tiled(buf.at[1], TM))
    out = h_ref[...] + mod_ref[5:6, :] * y
    if final:
        out = _rmsnorm_rows(out, fin_g[...])
    o_ref[...] = out


def _moe_call(u2, route, cnt, h, mod, fin_g, w1, w3, w2, nc, t0, final):
    bsz, n_out, d = h.shape
    de = w1.shape[-1]
    rows = bsz * n_out
    tiles = rows // TM
    tiles_per_row = n_out // TM
    n_blocks = (2 * rows + N_EXPERTS * (MOE_BLK - 1)) // MOE_BLK + 1

    counts = cnt[:, :, 0, :N_EXPERTS].reshape(tiles, N_EXPERTS)
    total = jnp.sum(counts, axis=0)
    padded = jnp.ceil(total / MOE_BLK) * MOE_BLK
    seg_end = jnp.cumsum(padded)
    base = (seg_end - padded)[None, :] + jnp.cumsum(counts, axis=0) - counts
    base = _pad_cols(base, HEAD_W).reshape(tiles, 1, HEAD_W)
    blk_start = jnp.arange(n_blocks, dtype=F32) * MOE_BLK
    blk_expert = jnp.minimum(jnp.sum(blk_start[:, None] >= seg_end[None, :], axis=1), N_EXPERTS - 1)
    n_used = (seg_end[-1] / MOE_BLK).astype(jnp.int32).reshape(1)
    tri = jnp.asarray(np.tril(np.ones((TM, TM), np.float32), -1), BF16)

    assert d == SUBLANES * HEAD_W
    slot_rows = n_blocks * MOE_BLK * SUBLANES
    xs, slots = pl.pallas_call(
        _dispatch_kernel,
        grid=(tiles,),
        in_specs=[pl.BlockSpec((TM, HEAD_W), lambda t: (t, 0)),
                  pl.BlockSpec((None, 1, HEAD_W), lambda t: (t, 0, 0)),
                  _const_spec((TM, TM)),
                  pl.BlockSpec((TM * SUBLANES, HEAD_W), lambda t: (t, 0)),
                  pl.BlockSpec(memory_space=pl.ANY)],
        out_specs=[pl.BlockSpec(memory_space=pl.ANY),
                   pl.BlockSpec((SUBLANES, TM), lambda t: (t, 0))],
        out_shape=[jax.ShapeDtypeStruct((slot_rows, HEAD_W), F32),
                   jax.ShapeDtypeStruct((tiles * SUBLANES, TM), jnp.int32)],
        scratch_shapes=[pltpu.SMEM((SUBLANES, TM), jnp.int32),
                        pltpu.SemaphoreType.DMA, pltpu.SemaphoreType.DMA],
        input_output_aliases={4: 0},
        compiler_params=_params(("arbitrary",)),
        name="moe_dispatch",
    )(route.reshape(rows, HEAD_W), base, tri, u2.reshape(rows * SUBLANES, HEAD_W),
      jnp.zeros((slot_rows, HEAD_W), F32))

    blk_spec = pl.BlockSpec((MOE_BLK * SUBLANES, HEAD_W), lambda b, be, nu: (b, 0))
    ys = pl.pallas_call(
        _experts_kernel,
        grid_spec=pltpu.PrefetchScalarGridSpec(
            num_scalar_prefetch=2,
            grid=(n_blocks,),
            in_specs=[blk_spec,
                      pl.BlockSpec((None, d, de), lambda b, be, nu: (be[b], 0, 0)),
                      pl.BlockSpec((None, d, de), lambda b, be, nu: (be[b], 0, 0)),
                      pl.BlockSpec((None, de, d), lambda b, be, nu: (be[b], 0, 0))],
            out_specs=blk_spec),
        out_shape=jax.ShapeDtypeStruct((slot_rows, HEAD_W), F32),
        compiler_params=_params(("arbitrary",)),
        name="moe_experts",
    )(blk_expert.astype(jnp.int32), n_used, xs, w1, w3, w2)

    def mod_index(t):
        b, pos = t // tiles_per_row, t % tiles_per_row
        return (jnp.where(pos + t0 < nc, bsz, b), 0, 0)

    out = pl.pallas_call(
        functools.partial(_combine_kernel, final=final),
        grid=(tiles,),
        in_specs=[pl.BlockSpec((SUBLANES, TM), lambda t: (t, 0)),
                  pl.BlockSpec((TM, HEAD_W), lambda t: (t, 0)),
                  pl.BlockSpec((TM, d), lambda t: (t, 0)),
                  pl.BlockSpec((None, 6, d), mod_index),
                  _const_spec((1, d)),
                  pl.BlockSpec(memory_space=pl.ANY)],
        out_specs=pl.BlockSpec((TM, d), lambda t: (t, 0)),
        out_shape=jax.ShapeDtypeStruct((rows, d), F32),
        scratch_shapes=[pltpu.SMEM((SUBLANES, TM), jnp.int32),
                        pltpu.VMEM((2, TM * SUBLANES, HEAD_W), F32),
                        pltpu.SemaphoreType.DMA, pltpu.SemaphoreType.DMA],
        compiler_params=_params(("arbitrary",)),
        name="moe_combine",
    )(slots, route.reshape(rows, HEAD_W), h.reshape(rows, d), mod, fin_g, ys)
    return out.reshape(bsz, n_out, d)


def _pad_heads(w, dk):
    lead = w.shape[:-1]
    w = w.reshape(lead + (N_HEADS, dk))
    w = jnp.pad(w, [(0, 0)] * len(lead) + [(0, 0), (0, HEAD_W - dk)])
    return w.reshape(lead + (N_HEADS * HEAD_W,))


def _pad_cols(w, width):
    return jnp.pad(w, [(0, 0)] * (w.ndim - 1) + [(0, width - w.shape[-1])])


def _split_w_in(w_in):
    bw = BRANCH_W
    sizes = [bw, bw, N_HEADS * GLA_DK, N_HEADS * GLA_DK, bw, bw, 2 * GLA_RANK,
             bw, bw, 2 * bw, bw, bw, bw, bw, bw, 16]
    offs = np.cumsum([0] + sizes)
    p = [w_in[:, offs[i]:offs[i + 1]] for i in range(len(sizes))]
    w_rg = jnp.concatenate([p[0], p[1]], axis=1)
    w_gla = jnp.concatenate([_pad_heads(p[2], GLA_DK), _pad_heads(p[3], GLA_DK), p[4], p[5],
                             _pad_cols(p[6], HEAD_W)], axis=1)
    w_hg = jnp.concatenate([p[7], p[8], p[9], p[10]], axis=1)
    gates = p[15].reshape(-1, 2, 2, N_HEADS)
    w_gi = _pad_cols(gates[:, :, 0].reshape(-1, 2 * N_HEADS), HEAD_W)
    w_gf = _pad_cols(gates[:, :, 1].reshape(-1, 2 * N_HEADS), HEAD_W)
    w_ml = jnp.concatenate([p[11], p[12], p[13], p[14], w_gi, w_gf], axis=1)
    return [w.astype(BF16) for w in (w_rg, w_gla, w_hg, w_ml)]


def _block_diag(w):
    k, n = w.shape[-3], w.shape[-1]
    eye = jnp.eye(k, dtype=w.dtype)
    full = jnp.einsum('...kij,kl->...kilj', w, eye)
    return full.reshape(w.shape[:-3] + (k * n, k * n))


def kernel(x, c, ctx, c_ctx, ada_w, ada_b, norm_mix_g, norm_ffn_g, w_in, rg_conv_w, rg_conv_b, rg_gate_w, rg_gate_b, rg_lambda, gla_w_lr, gla_b_lr, gla_norm_g, hgrn_lb_logits, hgrn_norm_g, ml_conv_w, ml_conv_b, ml_gate_b, ml_norm_g, w_branch, w_merge, b_merge, w_out, moe_w_group, moe_b_group, moe_w_expert, moe_b_expert, moe_w1, moe_w3, moe_w2, final_norm_g):
    bsz, seq, d = x.shape
    n_ctx = ctx.shape[1]
    depth = ada_w.shape[0]
    assert n_ctx % TM == 0 and seq % TM == 0 and d == 2 * BRANCH_W
    nc = n_ctx // TM

    h = jnp.concatenate([ctx, x], axis=1)
    cvec = jnp.zeros((SUBLANES, d), F32).at[:bsz].set(c).at[bsz].set(c_ctx)
    mod_all = _mod_call(cvec, ada_w, ada_b).reshape(depth, SUBLANES, 6, d)[:, :bsz + 1]

    lb_cum = jnp.cumsum(jax.nn.softmax(hgrn_lb_logits.astype(F32), axis=0), axis=0)
    hgrn_lb = lb_cum - lb_cum[:1]
    tables = [t for levels in SPLIT_LEVELS for t in _decay_tables(levels)]
    tri_sel = tables[0][:, :CHUNK, :]
    tri_mask = tables[1][:, 0]

    out = None
    for l in range(depth):
        last = l == depth - 1
        mod = mod_all[l]
        w_rg, w_gla, w_hg, w_ml = _split_w_in(w_in[l])
        z_rg, z_gla, z_hg, z_ml, u = _stage_a_call(h, mod, norm_mix_g[l], w_rg, w_gla, w_hg, w_ml, nc)

        rg_f, rg_b = _rg_call(z_rg, rg_conv_w[l], rg_conv_b[l].reshape(1, -1),
                              _block_diag(rg_gate_w[l]).astype(BF16),
                              rg_gate_b[l].reshape(4, -1), rg_lambda[l], nc)

        wlr = jnp.zeros((2, HEAD_W, N_HEADS * HEAD_W), F32)
        wlr_p = _pad_heads(gla_w_lr[l], GLA_DK)
        wlr = wlr.at[0, :GLA_RANK].set(wlr_p[0]).at[1, GLA_RANK:2 * GLA_RANK].set(wlr_p[1])
        gla_f, gla_b = _gla_call(z_gla, wlr.astype(BF16), _pad_heads(gla_b_lr[l], GLA_DK), tables, nc)

        lb = hgrn_lb[l]
        lbp = jnp.concatenate([lb, jnp.log(lb), jnp.log1p(-lb), jnp.zeros((2, lb.shape[-1]), F32)], axis=0)
        hg_f, hg_b = _hg_call(z_hg, lbp, tables, nc)

        gbias = _pad_cols(ml_gate_b[l].transpose(1, 0, 2).reshape(2, -1), HEAD_W)
        ml_f, ml_b = _ml_call(z_ml, ml_conv_w[l], ml_conv_b[l].reshape(1, -1), gbias, tri_sel, tri_mask, nc)

        gains = jnp.zeros((SUBLANES, HEAD_W), F32).at[0].set(gla_norm_g[l]).at[1].set(hgrn_norm_g[l]).at[2].set(ml_norm_g[l])
        w_route = _pad_cols(jnp.concatenate([moe_w_expert[l], moe_w_group[l]], axis=1), HEAD_W)
        b_route = _pad_cols(jnp.concatenate([moe_b_expert[l], moe_b_group[l]]).reshape(1, -1), HEAD_W)
        t0 = nc if last else 0
        h_mid, u2, route, cnt = _stage_c_call(
            u, h, mod, z_rg, rg_f, rg_b, z_gla, gla_f, gla_b, z_hg, hg_f, hg_b, z_ml, ml_f, ml_b,
            gains, w_merge[l].astype(BF16), b_merge[l].reshape(1, -1), w_branch[l].astype(BF16),
            w_out[l].astype(BF16), norm_ffn_g[l].reshape(1, -1), w_route, b_route, nc, t0)

        res = _moe_call(u2, route, cnt, h_mid, mod, final_norm_g.reshape(1, -1), moe_w1[l].astype(BF16),
                        moe_w3[l].astype(BF16), moe_w2[l].astype(BF16), nc, t0, last)
        if last:
            out = res
        else:
            h = res
    return out
```

```python
import functools

import numpy as np
import jax
import jax.numpy as jnp
from jax import lax
from jax.experimental import pallas as pl
from jax.experimental.pallas import tpu as pltpu

F32 = jnp.float32
BF16 = jnp.bfloat16

EPS = 1e-6
TM = 256
CHUNK = 64
SUBLANES = 8
N_HEADS = 4
HEAD_W = 128
BRANCH_W = 512
CONV_W = 4
CONV_LEFT = 2
RG_C = 8.0
GLA_DK = 64
GLA_RANK = 16
GLA_GATE_NORM = 16.0
N_GROUPS = 4
EXPERTS_PER_GROUP = 4
N_EXPERTS = 16
MOE_BLK = 256
ROUTE_IDX = 16
ROUTE_W = 18
N_LEVELS = 6
MILD_DECAY = -40.0
VMEM_LIMIT = 56 * 1024 * 1024

W_RG = 2 * BRANCH_W
W_GLA = 4 * BRANCH_W + HEAD_W
W_HG = 5 * BRANCH_W
W_ML = 4 * BRANCH_W + 2 * HEAD_W
SPLIT_LEVELS = (0, 2, N_LEVELS)


def _mm(a, b):
    return jnp.dot(a.astype(BF16), b.astype(BF16), preferred_element_type=F32)


def _mm_nt(a, b):
    return lax.dot_general(a.astype(BF16), b.astype(BF16), (((1,), (1,)), ((), ())),
                           preferred_element_type=F32)


def _mm_tn(a, b):
    return lax.dot_general(a.astype(BF16), b.astype(BF16), (((0,), (0,)), ((), ())),
                           preferred_element_type=F32)


def _mm_sel(sel, x):
    x1 = x.astype(BF16)
    x2 = (x - x1.astype(F32)).astype(BF16)
    dot = functools.partial(jnp.dot, preferred_element_type=F32)
    return dot(sel, x1) + dot(sel, x2)


def _log_sigmoid(x):
    return jnp.minimum(x, 0.0) - jnp.log1p(jnp.exp(-jnp.abs(x)))


def _sigmoid(x):
    return 0.5 * jnp.tanh(0.5 * x) + 0.5


def _silu(x):
    return x * _sigmoid(x)


def _rmsnorm_rows(x, g):
    return x * lax.rsqrt(jnp.mean(x * x, axis=-1, keepdims=True) + EPS) * g


def _head_rmsnorm(o, g):
    parts = [_rmsnorm_rows(o[:, h * HEAD_W:(h + 1) * HEAD_W], g) for h in range(N_HEADS)]
    return jnp.concatenate(parts, axis=-1)


def _bwd_tile(s, nc, nt):
    return jnp.where(s < nc, nc - 1 - s, nt - 1 - (s - nc))


def _const_spec(shape):
    nd = len(shape)
    return pl.BlockSpec(shape, lambda *_: (0,) * nd)


def _params(sem):
    return pltpu.CompilerParams(dimension_semantics=sem, vmem_limit_bytes=VMEM_LIMIT)


def _mod_kernel(c_ref, w_ref, b_ref, o_ref):
    cv = _silu(c_ref[...])
    o_ref[...] = jnp.dot(cv, w_ref[...], precision=lax.Precision.HIGHEST,
                         preferred_element_type=F32) + b_ref[...]


def _mod_call(cvec, ada_w, ada_b):
    depth, d, six_d = ada_w.shape
    tn = 1024
    return pl.pallas_call(
        _mod_kernel,
        grid=(depth, six_d // tn),
        in_specs=[pl.BlockSpec((SUBLANES, d), lambda l, j: (0, 0)),
                  pl.BlockSpec((None, d, tn), lambda l, j: (l, 0, j)),
                  pl.BlockSpec((None, 1, tn), lambda l, j: (l, 0, j))],
        out_specs=pl.BlockSpec((None, SUBLANES, tn), lambda l, j: (l, 0, j)),
        out_shape=jax.ShapeDtypeStruct((depth, SUBLANES, six_d), F32),
        compiler_params=_params(("parallel", "parallel")),
        name="adaln_mod",
    )(cvec, ada_w, ada_b.reshape(depth, 1, six_d))


def _stage_a_kernel(h_ref, mod_ref, g_ref, w_rg, w_gla, w_hg, w_ml,
                    z_rg, z_gla, z_hg, z_ml, u_ref):
    x = h_ref[...]
    u = _rmsnorm_rows(x, g_ref[...]) * (1.0 + mod_ref[1:2, :]) + mod_ref[0:1, :]
    ub = u.astype(BF16)
    u_ref[...] = ub
    z_rg[...] = jnp.dot(ub, w_rg[...], preferred_element_type=F32)
    z_gla[...] = jnp.dot(ub, w_gla[...], preferred_element_type=F32)
    z_hg[...] = jnp.dot(ub, w_hg[...], preferred_element_type=F32)
    z_ml[...] = jnp.dot(ub, w_ml[...], preferred_element_type=F32)


def _stage_a_call(h, mod, norm_g, w_rg, w_gla, w_hg, w_ml, nc):
    bsz, n, d = h.shape
    nt = n // TM
    tm = TM // 2

    def tok(w):
        return pl.BlockSpec((None, tm, w), lambda b, t: (b, t, 0))

    return pl.pallas_call(
        _stage_a_kernel,
        grid=(bsz, n // tm),
        in_specs=[tok(d),
                  pl.BlockSpec((None, 6, d), lambda b, t: (jnp.where(t * tm < nc * TM, bsz, b), 0, 0)),
                  _const_spec((1, d)),
                  _const_spec((d, W_RG)), _const_spec((d, W_GLA)),
                  _const_spec((d, W_HG)), _const_spec((d, W_ML))],
        out_specs=[tok(W_RG), tok(W_GLA), tok(W_HG), tok(W_ML), tok(d)],
        out_shape=[jax.ShapeDtypeStruct((bsz, n, W_RG), F32),
                   jax.ShapeDtypeStruct((bsz, n, W_GLA), F32),
                   jax.ShapeDtypeStruct((bsz, n, W_HG), F32),
                   jax.ShapeDtypeStruct((bsz, n, W_ML), F32),
                   jax.ShapeDtypeStruct((bsz, n, d), BF16)],
        compiler_params=_params(("parallel", "parallel")),
        name="stage_a",
    )(h, mod, norm_g.reshape(1, d), w_rg, w_gla, w_hg, w_ml)


def _conv_tile(x_ref, prev_ref, next_ref, pad_ref, cw_ref, cb_ref, tile, nc, nt):
    prev_ok = jnp.logical_and(tile != 0, tile != nc)
    next_ok = jnp.logical_and(tile != nc - 1, tile != nt - 1)
    pad_ref[0:SUBLANES, :] = jnp.where(prev_ok, prev_ref[...], 0.0)
    pad_ref[SUBLANES:SUBLANES + TM, :] = x_ref[...]
    pad_ref[SUBLANES + TM:2 * SUBLANES + TM, :] = jnp.where(next_ok, next_ref[...], 0.0)
    acc = cb_ref[...]
    for j in range(CONV_W):
        off = SUBLANES - CONV_LEFT + j
        acc = acc + cw_ref[j:j + 1, :] * pad_ref[off:off + TM, :]
    return acc


def _halo_specs(width, col_block, tile_fn, n):
    rows = TM // SUBLANES
    last = n // SUBLANES - 1
    cur = pl.BlockSpec((None, TM, width), lambda b, s: (b, tile_fn(s), col_block))
    prev = pl.BlockSpec((None, SUBLANES, width),
                        lambda b, s: (b, jnp.maximum(tile_fn(s) * rows - 1, 0), col_block))
    nxt = pl.BlockSpec((None, SUBLANES, width),
                       lambda b, s: (b, jnp.minimum((tile_fn(s) + 1) * rows, last), col_block))
    return [cur, prev, nxt]


def _scan_rows8(a, b, reverse):
    n = a.shape[0]
    pos = jnp.bitwise_and(lax.broadcasted_iota(jnp.int32, a.shape, 0), SUBLANES - 1)
    k = 1
    while k < SUBLANES:
        if reverse:
            a_s, b_s, ok = pltpu.roll(a, n - k, 0), pltpu.roll(b, n - k, 0), pos < SUBLANES - k
        else:
            a_s, b_s, ok = pltpu.roll(a, k, 0), pltpu.roll(b, k, 0), pos >= k
        b = b + a * jnp.where(ok, b_s, 0.0)
        a = a * jnp.where(ok, a_s, 1.0)
        k *= 2
    return a, b


def _scan_tile(a, b, carry, o_ref, reverse):
    a8, b8 = _scan_rows8(a, b, reverse)
    groups = a.shape[0] // SUBLANES
    order = range(groups - 1, -1, -1) if reverse else range(groups)
    for r in order:
        rows = slice(r * SUBLANES, (r + 1) * SUBLANES)
        h = b8[rows] + a8[rows] * carry
        o_ref[rows, :] = h
        carry = h[0:1, :] if reverse else h[SUBLANES - 1:SUBLANES, :]
    return carry


def _rg_kernel(xf, xf_p, xf_n, xb, xb_p, xb_n, cw, cb, gw, gb, lam,
               hf_ref, hb_ref, pad_ref, carry_ref, *, nc, nt):
    s = pl.program_id(1)

    @pl.when(s == 0)
    def _():
        carry_ref[...] = jnp.zeros_like(carry_ref)

    dirs = ((xf, xf_p, xf_n, hf_ref, s, False),
            (xb, xb_p, xb_n, hb_ref, _bwd_tile(s, nc, nt), True))
    for d, (x_ref, p_ref, n_ref, o_ref, tile, reverse) in enumerate(dirs):
        x = _conv_tile(x_ref, p_ref, n_ref, pad_ref, cw, cb, tile, nc, nt)
        r = _sigmoid(_mm(x, gw[d, 0]) + gb[2 * d:2 * d + 1, :])
        i = _sigmoid(_mm(x, gw[d, 1]) + gb[2 * d + 1:2 * d + 2, :])
        lam_d = lam[d:d + 1, :]
        softplus = jnp.maximum(-lam_d, 0.0) + jnp.log1p(jnp.exp(-jnp.abs(lam_d)))
        log_a = -RG_C * r * softplus
        a = jnp.exp(log_a)
        t = jnp.tanh(log_a)
        bt = jnp.sqrt(-2.0 * t / (1.0 - t)) * (i * x)
        carry_ref[d:d + 1, :] = _scan_tile(a, bt, carry_ref[d:d + 1, :], o_ref, reverse)


def _rg_call(z_rg, cw, cb, gw, gb, lam, nc):
    bsz, n, _ = z_rg.shape
    nt = n // TM
    w = BRANCH_W
    fwd = lambda s: s
    bwd = lambda s: _bwd_tile(s, nc, nt)
    out = lambda fn: pl.BlockSpec((None, TM, w), lambda b, s: (b, fn(s), 0))
    return pl.pallas_call(
        functools.partial(_rg_kernel, nc=nc, nt=nt),
        grid=(bsz, nt),
        in_specs=_halo_specs(w, 0, fwd, n) + _halo_specs(w, 0, bwd, n) + [
            _const_spec((CONV_W, w)), _const_spec((1, w)),
            _const_spec((2, 2, w, w)), _const_spec((4, w)), _const_spec((2, w))],
        out_specs=[out(fwd), out(bwd)],
        out_shape=[jax.ShapeDtypeStruct((bsz, n, w), F32)] * 2,
        scratch_shapes=[pltpu.VMEM((TM + 2 * SUBLANES, w), F32), pltpu.VMEM((2, w), F32)],
        compiler_params=_params(("parallel", "arbitrary")),
        name="rglru",
    )(z_rg, z_rg, z_rg, z_rg, z_rg, z_rg, cw, cb, gw, gb, lam)


def _decay_tables(levels):
    n = CHUNK
    blk = n >> levels
    sel = np.zeros((levels + 3, n, n), np.float32)
    masks = np.zeros((levels + 1, n, n), np.float32)
    for lvl in range(levels):
        half = n >> (lvl + 1)
        for r in range(n):
            start = (r // (2 * half)) * 2 * half
            ref = start + half - 1
            if r - start >= half:
                sel[lvl, r, ref + 1:r + 1] = 1.0
                masks[lvl, r, start:start + half] = 1.0
            else:
                sel[lvl, r, r + 1:ref + 1] = 1.0
    sel[levels] = np.tril(np.ones((n, n), np.float32))
    sel[levels + 1] = 1.0 - sel[levels]
    for r in range(n):
        start = (r // blk) * blk
        sel[levels + 2, r, r + 1:start + blk] = 1.0
        masks[levels, r, start:r + 1] = 1.0
    if levels == 0 or levels == N_LEVELS:
        sel = sel[:levels + 2]
    sel_b = sel[:, ::-1, ::-1].reshape(-1, n)
    masks_b = masks[:, ::-1, ::-1]
    return (jnp.asarray(np.stack([sel.reshape(-1, n), sel_b]), BF16),
            jnp.asarray(np.stack([masks, masks_b]), F32))


def _ref_rows(levels, reverse):
    def runs(block, offset):
        return [(s + offset, block) for s in range(0, CHUNK, block)]

    out = [runs(CHUNK >> lvl, (CHUNK >> (lvl + 1)) - 1) for lvl in range(levels)]
    out.append(runs(CHUNK, CHUNK - 1))
    out.append(runs(CHUNK >> levels, (CHUNK >> levels) - 1))
    if reverse:
        out = [[(CHUNK - 1 - r, n) for r, n in reversed(segs)] for segs in out]
    return out


def _rows_bcast(b, segs):
    return jnp.concatenate([jnp.broadcast_to(b[r:r + 1, :], (n, b.shape[-1])) for r, n in segs], axis=0)


def _gla_chunk(q, k, v, g, sel, masks, st_ref, d, reverse, levels):
    if levels == N_LEVELS:
        x = jnp.exp(_mm_sel(sel, g))
        x_lvl = [x[lvl * CHUNK:(lvl + 1) * CHUNK] for lvl in range(levels)]
        x_cum = x[levels * CHUNK:(levels + 1) * CHUNK]
        x_rest = x[(levels + 1) * CHUNK:(levels + 2) * CHUNK]
        xk_blk = xq_blk = None
    else:
        b = _mm_sel(sel[levels * CHUNK:(levels + 1) * CHUNK], g)
        refs = _ref_rows(levels, reverse)
        x_cum = jnp.exp(b)
        x_lvl = [jnp.exp(-jnp.abs(b - _rows_bcast(b, segs))) for segs in refs[:levels]]
        x_rest = jnp.exp(-jnp.abs(b - _rows_bcast(b, refs[levels])))
        if levels == 0:
            xk_blk, xq_blk = x_rest, jnp.exp(jnp.abs(b - _rows_bcast(b, refs[levels])))
        else:
            d_blk = jnp.abs(b - _rows_bcast(b, refs[levels + 1]))
            xk_blk, xq_blk = jnp.exp(-d_blk), jnp.exp(d_blk)
    outs = []
    for h in range(N_HEADS):
        cols = slice(h * HEAD_W, (h + 1) * HEAD_W)
        qh, kh, vh = q[:, cols], k[:, cols], v[:, cols]
        kx = kh * x_rest[:, cols]
        if xk_blk is None:
            sc = _mm_nt(qh, kh)
        elif levels == 0:
            sc = _mm_nt(qh * xq_blk[:, cols], kx)
        else:
            sc = _mm_nt(qh * xq_blk[:, cols], kh * xk_blk[:, cols])
        sc = jnp.where(masks[levels] > 0.0, sc, 0.0)
        for lvl in range(levels):
            xl = x_lvl[lvl][:, cols]
            sc = sc + jnp.where(masks[lvl] > 0.0, _mm_nt(qh * xl, kh * xl), 0.0)
        st = st_ref[d, h]
        outs.append(_mm(sc, vh) + _mm_nt(qh * x_cum[:, cols], st))
        x_end = x_cum[0:1, cols] if reverse else x_cum[CHUNK - 1:CHUNK, cols]
        st_ref[d, h] = st * x_end + _mm_tn(vh, kx)
    return jnp.concatenate(outs, axis=-1)


def _gla_step(sides, tabs, st_ref):
    n_chunks = TM // CHUNK

    def min_block_sum(block):
        mins = [jnp.min(jnp.sum(side[3].reshape(TM // block, block, side[3].shape[-1]), axis=1))
                for side in sides]
        return functools.reduce(jnp.minimum, mins)

    conds = []
    taken = None
    for levels in SPLIT_LEVELS[:-1]:
        ok = min_block_sum(CHUNK >> levels) > MILD_DECAY
        conds.append(ok if taken is None else jnp.logical_and(ok, jnp.logical_not(taken)))
        taken = ok if taken is None else jnp.logical_or(taken, ok)
    conds.append(jnp.logical_not(taken))

    for levels, cond, (sel_ref, mask_ref) in zip(SPLIT_LEVELS, conds, tabs):
        @pl.when(cond)
        def _(levels=levels, sel_ref=sel_ref, mask_ref=mask_ref):
            for c in range(n_chunks):
                for i, (q, k, v, g, o_ref, reverse) in enumerate(sides):
                    cc = n_chunks - 1 - c if reverse else c
                    rows = slice(cc * CHUNK, (cc + 1) * CHUNK)
                    o_ref[rows, :] = _gla_chunk(q[rows], k[rows], v[rows], g[rows],
                                                sel_ref[int(reverse)], mask_ref[int(reverse)],
                                                st_ref, i, reverse, levels)


def _pairs(refs):
    return tuple(zip(refs[0::2], refs[1::2]))


def _gla_kernel(zf, zb, wlr, blr, *rest):
    tabs, (of_ref, ob_ref, st_ref) = _pairs(rest[:-3]), rest[-3:]
    w = BRANCH_W

    @pl.when(pl.program_id(0) == 0)
    def _():
        st_ref[...] = jnp.zeros_like(st_ref)

    sides = []
    for b in range(zf.shape[0]):
        for d, (z_ref, o_ref) in enumerate(((zf, of_ref), (zb, ob_ref))):
            pre = _mm(z_ref[b, :, 4 * w:4 * w + HEAD_W], wlr[d]) + blr[d:d + 1, :]
            g = _log_sigmoid(pre) * (1.0 / GLA_GATE_NORM)
            q = z_ref[b, :, 0:w] * (GLA_DK ** -0.5)
            sides.append((q, z_ref[b, :, w:2 * w], z_ref[b, :, 2 * w:3 * w], g, o_ref.at[b], d == 1))
    _gla_step(sides, tabs, st_ref)


def _hg_kernel(zf, zb, lbp, *rest):
    tabs, (of_ref, ob_ref, st_ref) = _pairs(rest[:-3]), rest[-3:]
    w = BRANCH_W

    @pl.when(pl.program_id(0) == 0)
    def _():
        st_ref[...] = jnp.zeros_like(st_ref)

    sides = []
    for b in range(zf.shape[0]):
        for d, (z_ref, o_ref) in enumerate(((zf, of_ref), (zb, ob_ref))):
            f = z_ref[b, :, (2 + d) * w:(3 + d) * w]
            lb = lbp[d:d + 1, :]
            log_lb = lbp[2 + d:3 + d, :]
            log_1m = lbp[4 + d:5 + d, :]
            c = log_1m + _log_sigmoid(f)
            g = jnp.maximum(log_lb, c) + jnp.log1p(jnp.exp(-jnp.abs(log_lb - c)))
            k = (1.0 - lb) * _sigmoid(-f)
            q = _silu(z_ref[b, :, 0:w]) * (HEAD_W ** -0.5)
            sides.append((q, k, z_ref[b, :, w:2 * w], g, o_ref.at[b], d == 1))
    _gla_step(sides, tabs, st_ref)


def _mixer_specs(widths_cols, tile_fn):
    return [pl.BlockSpec((None, TM, w), lambda b, s, c=c: (b, tile_fn(s), c)) for w, c in widths_cols]


def _all_rows_spec(bsz, width, tile_fn):
    return pl.BlockSpec((bsz, TM, width), lambda s: (0, tile_fn(s), 0))


def _gla_call(z_gla, wlr, blr, tables, nc):
    bsz, n, _ = z_gla.shape
    nt = n // TM
    w = BRANCH_W
    fwd = lambda s: s
    bwd = lambda s: _bwd_tile(s, nc, nt)
    return pl.pallas_call(
        _gla_kernel,
        grid=(nt,),
        in_specs=[_all_rows_spec(bsz, W_GLA, fwd), _all_rows_spec(bsz, W_GLA, bwd),
                  _const_spec(wlr.shape), _const_spec(blr.shape)] + [_const_spec(t.shape) for t in tables],
        out_specs=[_all_rows_spec(bsz, w, fwd), _all_rows_spec(bsz, w, bwd)],
        out_shape=[jax.ShapeDtypeStruct((bsz, n, w), F32)] * 2,
        scratch_shapes=[pltpu.VMEM((2 * bsz, N_HEADS, HEAD_W, HEAD_W), F32)],
        compiler_params=_params(("arbitrary",)),
        name="gla",
    )(z_gla, z_gla, wlr, blr, *tables)


def _hg_call(z_hg, lbp, tables, nc):
    bsz, n, _ = z_hg.shape
    nt = n // TM
    w = BRANCH_W
    fwd = lambda s: s
    bwd = lambda s: _bwd_tile(s, nc, nt)
    return pl.pallas_call(
        _hg_kernel,
        grid=(nt,),
        in_specs=[_all_rows_spec(bsz, 4 * w, fwd), _all_rows_spec(bsz, 4 * w, bwd),
                  _const_spec(lbp.shape)] + [_const_spec(t.shape) for t in tables],
        out_specs=[_all_rows_spec(bsz, w, fwd), _all_rows_spec(bsz, w, bwd)],
        out_shape=[jax.ShapeDtypeStruct((bsz, n, w), F32)] * 2,
        scratch_shapes=[pltpu.VMEM((2 * bsz, N_HEADS, HEAD_W, HEAD_W), F32)],
        compiler_params=_params(("arbitrary",)),
        name="hgrn2",
    )(z_hg, z_hg, lbp, *tables)


def _cummax_rows(a, reverse):
    n = a.shape[0]
    row = lax.broadcasted_iota(jnp.int32, a.shape, 0)
    k = 1
    while k < n:
        if reverse:
            a_s, ok = pltpu.roll(a, n - k, 0), row < n - k
        else:
            a_s, ok = pltpu.roll(a, k, 0), row >= k
        a = jnp.maximum(a, jnp.where(ok, a_s, -jnp.inf))
        k *= 2
    return a


def _ml_chunk(q, k, v, gi, log_f, tri_sel, tri_mask, ct_ref, m_ref, slot, d, reverse):
    b = _mm_sel(tri_sel, log_f)
    a = gi - b
    m_prev = m_ref[slot:slot + 1, :]
    m_rel = jnp.maximum(_cummax_rows(a, reverse), m_prev)
    w_inter = jnp.exp(m_prev - m_rel)
    exp_neg_m = jnp.exp(-(b + m_rel))
    last = 0 if reverse else CHUNK - 1
    b_end = b[last:last + 1, :]
    m_new = b_end + m_rel[last:last + 1, :]
    w_end = jnp.exp(b_end - b + gi - m_new)
    decay = jnp.exp(b_end + m_prev - m_new)
    m_ref[slot:slot + 1, :] = m_new
    a_t = a.T
    ones = jnp.ones((CHUNK, HEAD_W), BF16)
    outs = []
    for h in range(N_HEADS):
        cols = slice(h * HEAD_W, (h + 1) * HEAD_W)
        qh, kh, vh = q[:, cols], k[:, cols], v[:, cols]
        c = d * N_HEADS + h
        log_w = jnp.where(tri_mask > 0.0, a_t[c:c + 1, :] - m_rel[:, c:c + 1], -jnp.inf)
        s = _mm_nt(qh, kh) * jnp.exp(log_w)
        v_ext = jnp.concatenate([vh.astype(BF16), ones], axis=-1)
        st = slot * N_HEADS + h
        ct = ct_ref[st]
        tot = _mm(s, v_ext) + w_inter[:, c:c + 1] * _mm_nt(qh, ct)
        num, den = tot[:, :HEAD_W], tot[:, HEAD_W:]
        outs.append(num / jnp.maximum(jnp.abs(den), exp_neg_m[:, c:c + 1]))
        ct_ref[st] = decay[:, c:c + 1] * ct + _mm_tn(v_ext, kh * w_end[:, c:c + 1])
    return jnp.concatenate(outs, axis=-1)


def _ml_kernel(zf, zf_p, zf_n, zb, zb_p, zb_n, cw, cb, gbias,
               tri_sel, tri_mask, of_ref, ob_ref, pad_ref, ct_ref, m_ref, *, nc, nt):
    s = pl.program_id(0)
    w = BRANCH_W
    n_chunks = TM // CHUNK

    @pl.when(s == 0)
    def _():
        ct_ref[...] = jnp.zeros_like(ct_ref)
        m_ref[...] = jnp.zeros_like(m_ref)

    dirs = ((zf, zf_p, zf_n, of_ref, s), (zb, zb_p, zb_n, ob_ref, _bwd_tile(s, nc, nt)))
    sides = []
    for b in range(zf.shape[0]):
        for d, (z_ref, p_ref, nx_ref, o_ref, tile) in enumerate(dirs):
            slot = 2 * b + d
            qk = _silu(_conv_tile(z_ref.at[b, :, 0:2 * w], p_ref.at[b], nx_ref.at[b],
                                  pad_ref.at[slot], cw, cb, tile, nc, nt))
            gi = z_ref[b, :, 4 * w:4 * w + HEAD_W] + gbias[0:1, :]
            log_f = _log_sigmoid(z_ref[b, :, 4 * w + HEAD_W:4 * w + 2 * HEAD_W] + gbias[1:2, :])
            sides.append((qk[:, :w] * (HEAD_W ** -0.5), qk[:, w:], z_ref[b, :, 2 * w:3 * w],
                          gi, log_f, o_ref.at[b], slot, d))
    for c in range(n_chunks):
        for q, k, v, gi, log_f, o_ref, slot, d in sides:
            cc = n_chunks - 1 - c if d == 1 else c
            rows = slice(cc * CHUNK, (cc + 1) * CHUNK)
            o_ref[rows, :] = _ml_chunk(q[rows], k[rows], v[rows], gi[rows], log_f[rows],
                                       tri_sel[d], tri_mask[d], ct_ref, m_ref, slot, d, d == 1)


def _ml_call(z_ml, cw, cb, gbias, tri_sel, tri_mask, nc):
    bsz, n, _ = z_ml.shape
    nt = n // TM
    w = BRANCH_W
    fwd = lambda s: s
    bwd = lambda s: _bwd_tile(s, nc, nt)
    rows8 = TM // SUBLANES
    last8 = n // SUBLANES - 1

    def side(fn):
        return [_all_rows_spec(bsz, W_ML, fn),
                pl.BlockSpec((bsz, SUBLANES, 2 * w), lambda s: (0, jnp.maximum(fn(s) * rows8 - 1, 0), 0)),
                pl.BlockSpec((bsz, SUBLANES, 2 * w), lambda s: (0, jnp.minimum((fn(s) + 1) * rows8, last8), 0))]

    return pl.pallas_call(
        functools.partial(_ml_kernel, nc=nc, nt=nt),
        grid=(nt,),
        in_specs=side(fwd) + side(bwd) + [
            _const_spec((CONV_W, 2 * w)), _const_spec((1, 2 * w)), _const_spec((2, HEAD_W)),
            _const_spec(tri_sel.shape), _const_spec(tri_mask.shape)],
        out_specs=[_all_rows_spec(bsz, w, fwd), _all_rows_spec(bsz, w, bwd)],
        out_shape=[jax.ShapeDtypeStruct((bsz, n, w), F32)] * 2,
        scratch_shapes=[pltpu.VMEM((2 * bsz, TM + 2 * SUBLANES, 2 * w), F32),
                        pltpu.VMEM((2 * bsz * N_HEADS, 2 * HEAD_W, HEAD_W), F32),
                        pltpu.VMEM((2 * bsz, HEAD_W), F32)],
        compiler_params=_params(("arbitrary",)),
        name="mlstm",
    )(*([z_ml] * 6), cw, cb, gbias, tri_sel, tri_mask)


def _gelu_tanh(x):
    return 0.5 * x * (1.0 + jnp.tanh(0.7978845608028654 * (x + 0.044715 * (x * x * x))))


def _route(logits):
    col = lax.broadcasted_iota(jnp.int32, logits.shape, 1)
    colf = col.astype(F32)
    is_g = jnp.logical_and(col >= N_EXPERTS, col < N_EXPERTS + N_GROUPS)
    is_e = col < N_EXPERTS
    neg = -jnp.inf
    big = 1e9
    gl = jnp.where(is_g, logits, neg)
    g_max = jnp.max(gl, axis=-1, keepdims=True)
    grp = jnp.min(jnp.where(gl == g_max, colf, big), axis=-1, keepdims=True) - N_EXPERTS
    p_grp = 1.0 / jnp.sum(jnp.exp(gl - g_max), axis=-1, keepdims=True)
    col_grp = lax.shift_right_logical(col, EXPERTS_PER_GROUP.bit_length() - 1)
    in_grp = jnp.logical_and(is_e, col_grp.astype(F32) == grp)
    e1 = jnp.where(in_grp, logits, neg)
    top1 = jnp.max(e1, axis=-1, keepdims=True)
    idx1 = jnp.min(jnp.where(e1 == top1, colf, big), axis=-1, keepdims=True)
    e2 = jnp.where(colf == idx1, neg, e1)
    top2 = jnp.max(e2, axis=-1, keepdims=True)
    idx2 = jnp.min(jnp.where(e2 == top2, colf, big), axis=-1, keepdims=True)
    t = jnp.exp(top2 - top1)
    w1 = p_grp / (1.0 + t)
    w2 = p_grp * (t / (1.0 + t))
    out = jnp.where(col == ROUTE_IDX, idx1, jnp.where(col == ROUTE_IDX + 1, idx2, 0.0))
    return jnp.where(col == ROUTE_W, w1, jnp.where(col == ROUTE_W + 1, w2, out))


def _route_onehot(route):
    colf = lax.broadcasted_iota(jnp.int32, route.shape, 1).astype(F32)
    oh1 = colf == route[:, ROUTE_IDX:ROUTE_IDX + 1]
    oh2 = colf == route[:, ROUTE_IDX + 1:ROUTE_IDX + 2]
    return jnp.where(jnp.logical_or(oh1, oh2), 1.0, 0.0), oh1, oh2


def _stage_c_kernel(u_ref, h_ref, mod_ref, rg_y, rg_f, rg_b, gla_g, gla_f, gla_b, hg_g, hg_f, hg_b,
                    ml_o, ml_f, ml_b, gains, w_merge, b_merge, w_branch, w_out, ffn_g, w_route, b_route,
                    h_out, u2_out, wt_out, cnt_out):
    ys = (
        _gelu_tanh(rg_y[...]) * (rg_f[...] + rg_b[...]),
        _head_rmsnorm(gla_f[...] + gla_b[...], gains[0:1, :]) * _silu(gla_g[...]),
        _head_rmsnorm(hg_f[...] + hg_b[...], gains[1:2, :]) * _silu(hg_g[...]),
        _sigmoid(ml_o[...]) * _head_rmsnorm(ml_f[...] + ml_b[...], gains[2:3, :]),
    )
    u = u_ref[...]
    d = u.shape[-1]
    merged = None
    for kk, y in enumerate(ys):
        gate = _sigmoid(jnp.dot(u, w_merge[:, kk * d:(kk + 1) * d], preferred_element_type=F32)
                              + b_merge[:, kk * d:(kk + 1) * d])
        term = gate * _mm(y, w_branch[kk])
        merged = term if merged is None else merged + term
    mix = _mm(merged, w_out[...])
    h_new = h_ref[...] + mod_ref[2:3, :] * mix
    h_out[...] = h_new
    u2 = _rmsnorm_rows(h_new, ffn_g[...]) * (1.0 + mod_ref[4:5, :]) + mod_ref[3:4, :]
    _store_tiled(u2_out, u2)
    logits = jnp.dot(u2, w_route[...], precision=lax.Precision.HIGHEST,
                     preferred_element_type=F32) + b_route[...]
    route = _route(logits)
    wt_out[...] = route
    cnt = jnp.sum(_route_onehot(route)[0], axis=0, keepdims=True)
    cnt_out[...] = jnp.broadcast_to(cnt, cnt_out.shape)


def _stage_c_call(u, h, mod, z_rg, rg_f, rg_b, z_gla, gla_f, gla_b, z_hg, hg_f, hg_b, z_ml, ml_f, ml_b,
                  gains, w_merge, b_merge, w_branch, w_out, ffn_g, w_route, b_route, nc, t0):
    bsz, n, d = h.shape
    nt = n // TM
    w = BRANCH_W

    n_out = n - t0 * TM

    def tok(width, col=0):
        return pl.BlockSpec((None, TM, width), lambda b, t: (b, t + t0, col))

    def out_tok(width):
        return pl.BlockSpec((None, TM, width), lambda b, t: (b, t, 0))

    mod_spec = pl.BlockSpec((None, 6, d), lambda b, t: (jnp.where(t + t0 < nc, bsz, b), 0, 0))
    return pl.pallas_call(
        _stage_c_kernel,
        grid=(bsz, nt - t0),
        in_specs=[tok(d), tok(d), mod_spec,
                  tok(w, 1), tok(w), tok(w),
                  tok(w, 3), tok(w), tok(w),
                  tok(w, 4), tok(w), tok(w),
                  tok(w, 3), tok(w), tok(w),
                  _const_spec(gains.shape), _const_spec(w_merge.shape), _const_spec(b_merge.shape),
                  _const_spec(w_branch.shape), _const_spec(w_out.shape), _const_spec(ffn_g.shape),
                  _const_spec(w_route.shape), _const_spec(b_route.shape)],
        out_specs=[out_tok(d),
                   pl.BlockSpec((None, TM * SUBLANES, HEAD_W), lambda b, t: (b, t, 0)),
                   out_tok(HEAD_W),
                   pl.BlockSpec((None, None, SUBLANES, HEAD_W), lambda b, t: (b, t, 0, 0))],
        out_shape=[jax.ShapeDtypeStruct((bsz, n_out, d), F32),
                   jax.ShapeDtypeStruct((bsz, n_out * SUBLANES, HEAD_W), F32),
                   jax.ShapeDtypeStruct((bsz, n_out, HEAD_W), F32),
                   jax.ShapeDtypeStruct((bsz, nt - t0, SUBLANES, HEAD_W), F32)],
        compiler_params=_params(("parallel", "parallel")),
        name="stage_c",
    )(u, h, mod, z_rg, rg_f, rg_b, z_gla, gla_f, gla_b, z_hg, hg_f, hg_b, z_ml, ml_f, ml_b,
      gains, w_merge, b_merge, w_branch, w_out, ffn_g, w_route, b_route)


def _slots_to_smem(slots_vmem, slots_smem, sem):
    cp = pltpu.make_async_copy(slots_vmem, slots_smem, sem)
    cp.start()
    cp.wait()


def _row_copy(src, src_row, dst, dst_row, sem):
    def tile(ref, row):
        return ref.at[pl.ds(pl.multiple_of(row * SUBLANES, SUBLANES), SUBLANES), :]
    return pltpu.make_async_copy(tile(src, src_row), tile(dst, dst_row), sem)


def _to_tiles(x):
    return [x[:, k * HEAD_W:(k + 1) * HEAD_W] for k in range(SUBLANES)]


def _store_tiled(ref, x):
    rows = x.shape[0]
    for k, piece in enumerate(_to_tiles(x)):
        ref[pl.ds(k, rows, stride=SUBLANES), :] = piece


def _load_tiled(ref, rows):
    return jnp.concatenate([ref[pl.ds(k, rows, stride=SUBLANES), :] for k in range(SUBLANES)], axis=-1)


def _dispatch_kernel(route_ref, base_ref, tri_ref, x_ref, xs_in, xs_hbm, slots_ref,
                     slots_smem, sem_s, sem_r):
    del xs_in
    route = route_ref[...]
    onehot, oh1, oh2 = _route_onehot(route)
    rank = jnp.dot(tri_ref[...], onehot.astype(BF16), preferred_element_type=F32)
    pos = base_ref[...] + rank
    slot_a = jnp.sum(jnp.where(oh1, pos, 0.0), axis=-1, keepdims=True)
    slot_b = jnp.sum(jnp.where(oh2, pos, 0.0), axis=-1, keepdims=True)
    col = lax.broadcasted_iota(jnp.int32, route.shape, 1)
    both = jnp.where(col == 0, slot_a, jnp.where(col == 1, slot_b, 0.0))
    slots_ref[...] = both.T[0:SUBLANES, :].astype(jnp.int32)
    _slots_to_smem(slots_ref, slots_smem, sem_s)

    def start(t, carry):
        for j in range(2):
            _row_copy(x_ref, t, xs_hbm, slots_smem[j, t], sem_r).start(priority=j)
        return carry

    def wait(t, carry):
        for j in range(2):
            _row_copy(x_ref, 0, xs_hbm, 0, sem_r).wait()
        return carry

    lax.fori_loop(0, TM, start, 0, unroll=8)
    lax.fori_loop(0, TM, wait, 0, unroll=8)


def _experts_kernel(blk_expert, n_used, x_ref, w1, w3, w2, o_ref):
    del blk_expert

    @pl.when(pl.program_id(0) < n_used[0])
    def _():
        x = _load_tiled(x_ref, MOE_BLK)
        h1 = _mm(x, w1[...])
        h3 = _mm(x, w3[...])
        _store_tiled(o_ref, _mm(_silu(h1) * h3, w2[...]))

    @pl.when(pl.program_id(0) >= n_used[0])
    def _():
        o_ref[...] = jnp.zeros_like(o_ref)


def _combine_kernel(slots_ref, route_ref, h_ref, mod_ref, fin_g, ys_hbm, o_ref,
                    slots_smem, buf, sem_s, sem_r, *, final):
    _slots_to_smem(slots_ref, slots_smem, sem_s)

    def start(t, carry):
        for j in range(2):
            _row_copy(ys_hbm, slots_smem[j, t], buf.at[j], t, sem_r).start(priority=j)
        return carry

    def wait(t, carry):
        for j in range(2):
            _row_copy(ys_hbm, 0, buf.at[j], 0, sem_r).wait()
        return carry

    lax.fori_loop(0, TM, start, 0, unroll=8)
    lax.fori_loop(0, TM, wait, 0, unroll=8)
    route = route_ref[...]
    y = (route[:, ROUTE_W:ROUTE_W + 1] * _load_tiled(buf.at[0], TM)
         + route[:, ROUTE_W + 1:ROUTE_W + 2] * _load_tiled(buf.at[1], TM))
    out = h_ref[...] + mod_ref[5:6, :] * y
    if final:
        out = _rmsnorm_rows(out, fin_g[...])
    o_ref[...] = out


def _moe_call(u2, route, cnt, h, mod, fin_g, w1, w3, w2, nc, t0, final):
    bsz, n_out, d = h.shape
    de = w1.shape[-1]
    rows = bsz * n_out
    tiles = rows // TM
    tiles_per_row = n_out // TM
    n_blocks = (2 * rows + N_EXPERTS * (MOE_BLK - 1)) // MOE_BLK + 1

    counts = cnt[:, :, 0, :N_EXPERTS].reshape(tiles, N_EXPERTS)
    total = jnp.sum(counts, axis=0)
    padded = jnp.ceil(total / MOE_BLK) * MOE_BLK
    seg_end = jnp.cumsum(padded)
    base = (seg_end - padded)[None, :] + jnp.cumsum(counts, axis=0) - counts
    base = _pad_cols(base, HEAD_W).reshape(tiles, 1, HEAD_W)
    blk_start = jnp.arange(n_blocks, dtype=F32) * MOE_BLK
    blk_expert = jnp.minimum(jnp.sum(blk_start[:, None] >= seg_end[None, :], axis=1), N_EXPERTS - 1)
    n_used = (seg_end[-1] / MOE_BLK).astype(jnp.int32).reshape(1)
    tri = jnp.asarray(np.tril(np.ones((TM, TM), np.float32), -1), BF16)

    assert d == SUBLANES * HEAD_W
    slot_rows = n_blocks * MOE_BLK * SUBLANES
    xs, slots = pl.pallas_call(
        _dispatch_kernel,
        grid=(tiles,),
        in_specs=[pl.BlockSpec((TM, HEAD_W), lambda t: (t, 0)),
                  pl.BlockSpec((None, 1, HEAD_W), lambda t: (t, 0, 0)),
                  _const_spec((TM, TM)),
                  pl.BlockSpec((TM * SUBLANES, HEAD_W), lambda t: (t, 0)),
                  pl.BlockSpec(memory_space=pl.ANY)],
        out_specs=[pl.BlockSpec(memory_space=pl.ANY),
                   pl.BlockSpec((SUBLANES, TM), lambda t: (t, 0))],
        out_shape=[jax.ShapeDtypeStruct((slot_rows, HEAD_W), F32),
                   jax.ShapeDtypeStruct((tiles * SUBLANES, TM), jnp.int32)],
        scratch_shapes=[pltpu.SMEM((SUBLANES, TM), jnp.int32),
                        pltpu.SemaphoreType.DMA, pltpu.SemaphoreType.DMA],
        input_output_aliases={4: 0},
        compiler_params=_params(("arbitrary",)),
        name="moe_dispatch",
    )(route.reshape(rows, HEAD_W), base, tri, u2.reshape(rows * SUBLANES, HEAD_W),
      jnp.zeros((slot_rows, HEAD_W), F32))

    blk_spec = pl.BlockSpec((MOE_BLK * SUBLANES, HEAD_W), lambda b, be, nu: (b, 0))
    ys = pl.pallas_call(
        _experts_kernel,
        grid_spec=pltpu.PrefetchScalarGridSpec(
            num_scalar_prefetch=2,
            grid=(n_blocks,),
            in_specs=[blk_spec,
                      pl.BlockSpec((None, d, de), lambda b, be, nu: (be[b], 0, 0)),
                      pl.BlockSpec((None, d, de), lambda b, be, nu: (be[b], 0, 0)),
                      pl.BlockSpec((None, de, d), lambda b, be, nu: (be[b], 0, 0))],
            out_specs=blk_spec),
        out_shape=jax.ShapeDtypeStruct((slot_rows, HEAD_W), F32),
        compiler_params=_params(("arbitrary",)),
        name="moe_experts",
    )(blk_expert.astype(jnp.int32), n_used, xs, w1, w3, w2)

    def mod_index(t):
        b, pos = t // tiles_per_row, t % tiles_per_row
        return (jnp.where(pos + t0 < nc, bsz, b), 0, 0)

    out = pl.pallas_call(
        functools.partial(_combine_kernel, final=final),
        grid=(tiles,),
        in_specs=[pl.BlockSpec((SUBLANES, TM), lambda t: (t, 0)),
                  pl.BlockSpec((TM, HEAD_W), lambda t: (t, 0)),
                  pl.BlockSpec((TM, d), lambda t: (t, 0)),
                  pl.BlockSpec((None, 6, d), mod_index),
                  _const_spec((1, d)),
                  pl.BlockSpec(memory_space=pl.ANY)],
        out_specs=pl.BlockSpec((TM, d), lambda t: (t, 0)),
        out_shape=jax.ShapeDtypeStruct((rows, d), F32),
        scratch_shapes=[pltpu.SMEM((SUBLANES, TM), jnp.int32),
                        pltpu.VMEM((2, TM * SUBLANES, HEAD_W), F32),
                        pltpu.SemaphoreType.DMA, pltpu.SemaphoreType.DMA],
        compiler_params=_params(("arbitrary",)),
        name="moe_combine",
    )(slots, route.reshape(rows, HEAD_W), h.reshape(rows, d), mod, fin_g, ys)
    return out.reshape(bsz, n_out, d)


def _pad_heads(w, dk):
    lead = w.shape[:-1]
    w = w.reshape(lead + (N_HEADS, dk))
    w = jnp.pad(w, [(0, 0)] * len(lead) + [(0, 0), (0, HEAD_W - dk)])
    return w.reshape(lead + (N_HEADS * HEAD_W,))


def _pad_cols(w, width):
    return jnp.pad(w, [(0, 0)] * (w.ndim - 1) + [(0, width - w.shape[-1])])


def _split_w_in(w_in):
    bw = BRANCH_W
    sizes = [bw, bw, N_HEADS * GLA_DK, N_HEADS * GLA_DK, bw, bw, 2 * GLA_RANK,
             bw, bw, 2 * bw, bw, bw, bw, bw, bw, 16]
    offs = np.cumsum([0] + sizes)
    p = [w_in[:, offs[i]:offs[i + 1]] for i in range(len(sizes))]
    w_rg = jnp.concatenate([p[0], p[1]], axis=1)
    w_gla = jnp.concatenate([_pad_heads(p[2], GLA_DK), _pad_heads(p[3], GLA_DK), p[4], p[5],
                             _pad_cols(p[6], HEAD_W)], axis=1)
    w_hg = jnp.concatenate([p[7], p[8], p[9], p[10]], axis=1)
    gates = p[15].reshape(-1, 2, 2, N_HEADS)
    w_gi = _pad_cols(gates[:, :, 0].reshape(-1, 2 * N_HEADS), HEAD_W)
    w_gf = _pad_cols(gates[:, :, 1].reshape(-1, 2 * N_HEADS), HEAD_W)
    w_ml = jnp.concatenate([p[11], p[12], p[13], p[14], w_gi, w_gf], axis=1)
    return [w.astype(BF16) for w in (w_rg, w_gla, w_hg, w_ml)]


def _block_diag(w):
    k, n = w.shape[-3], w.shape[-1]
    eye = jnp.eye(k, dtype=w.dtype)
    full = jnp.einsum('...kij,kl->...kilj', w, eye)
    return full.reshape(w.shape[:-3] + (k * n, k * n))


def kernel(x, c, ctx, c_ctx, ada_w, ada_b, norm_mix_g, norm_ffn_g, w_in, rg_conv_w, rg_conv_b, rg_gate_w, rg_gate_b, rg_lambda, gla_w_lr, gla_b_lr, gla_norm_g, hgrn_lb_logits, hgrn_norm_g, ml_conv_w, ml_conv_b, ml_gate_b, ml_norm_g, w_branch, w_merge, b_merge, w_out, moe_w_group, moe_b_group, moe_w_expert, moe_b_expert, moe_w1, moe_w3, moe_w2, final_norm_g):
    bsz, seq, d = x.shape
    n_ctx = ctx.shape[1]
    depth = ada_w.shape[0]
    assert n_ctx % TM == 0 and seq % TM == 0 and d == 2 * BRANCH_W
    nc = n_ctx // TM

    h = jnp.concatenate([ctx, x], axis=1)
    cvec = jnp.zeros((SUBLANES, d), F32).at[:bsz].set(c).at[bsz].set(c_ctx)
    mod_all = _mod_call(cvec, ada_w, ada_b).reshape(depth, SUBLANES, 6, d)[:, :bsz + 1]

    lb_cum = jnp.cumsum(jax.nn.softmax(hgrn_lb_logits.astype(F32), axis=0), axis=0)
    hgrn_lb = lb_cum - lb_cum[:1]
    tables = [t for levels in SPLIT_LEVELS for t in _decay_tables(levels)]
    tri_sel = tables[0][:, :CHUNK, :]
    tri_mask = tables[1][:, 0]

    out = None
    for l in range(depth):
        last = l == depth - 1
        mod = mod_all[l]
        w_rg, w_gla, w_hg, w_ml = _split_w_in(w_in[l])
        z_rg, z_gla, z_hg, z_ml, u = _stage_a_call(h, mod, norm_mix_g[l], w_rg, w_gla, w_hg, w_ml, nc)

        rg_f, rg_b = _rg_call(z_rg, rg_conv_w[l], rg_conv_b[l].reshape(1, -1),
                              _block_diag(rg_gate_w[l]).astype(BF16),
                              rg_gate_b[l].reshape(4, -1), rg_lambda[l], nc)

        wlr = jnp.zeros((2, HEAD_W, N_HEADS * HEAD_W), F32)
        wlr_p = _pad_heads(gla_w_lr[l], GLA_DK)
        wlr = wlr.at[0, :GLA_RANK].set(wlr_p[0]).at[1, GLA_RANK:2 * GLA_RANK].set(wlr_p[1])
        gla_f, gla_b = _gla_call(z_gla, wlr.astype(BF16), _pad_heads(gla_b_lr[l], GLA_DK), tables, nc)

        lb = hgrn_lb[l]
        lbp = jnp.concatenate([lb, jnp.log(lb), jnp.log1p(-lb), jnp.zeros((2, lb.shape[-1]), F32)], axis=0)
        hg_f, hg_b = _hg_call(z_hg, lbp, tables, nc)

        gbias = _pad_cols(ml_gate_b[l].transpose(1, 0, 2).reshape(2, -1), HEAD_W)
        ml_f, ml_b = _ml_call(z_ml, ml_conv_w[l], ml_conv_b[l].reshape(1, -1), gbias, tri_sel, tri_mask, nc)

        gains = jnp.zeros((SUBLANES, HEAD_W), F32).at[0].set(gla_norm_g[l]).at[1].set(hgrn_norm_g[l]).at[2].set(ml_norm_g[l])
        w_route = _pad_cols(jnp.concatenate([moe_w_expert[l], moe_w_group[l]], axis=1), HEAD_W)
        b_route = _pad_cols(jnp.concatenate([moe_b_expert[l], moe_b_group[l]]).reshape(1, -1), HEAD_W)
        t0 = nc if last else 0
        h_mid, u2, route, cnt = _stage_c_call(
            u, h, mod, z_rg, rg_f, rg_b, z_gla, gla_f, gla_b, z_hg, hg_f, hg_b, z_ml, ml_f, ml_b,
            gains, w_merge[l].astype(BF16), b_merge[l].reshape(1, -1), w_branch[l].astype(BF16),
            w_out[l].astype(BF16), norm_ffn_g[l].reshape(1, -1), w_route, b_route, nc, t0)

        res = _moe_call(u2, route, cnt, h_mid, mod, final_norm_g.reshape(1, -1), moe_w1[l],
                        moe_w3[l], moe_w2[l], nc, t0, last)
        if last:
            out = res
        else:
            h = res
    return out
```

```python
import functools

import numpy as np
import jax
import jax.numpy as jnp
from jax import lax
from jax.experimental import pallas as pl
from jax.experimental.pallas import tpu as pltpu

F32 = jnp.float32
BF16 = jnp.bfloat16

EPS = 1e-6
TM = 256
CHUNK = 64
SUBLANES = 8
N_HEADS = 4
HEAD_W = 128
BRANCH_W = 512
CONV_W = 4
CONV_LEFT = 2
RG_C = 8.0
GLA_DK = 64
GLA_RANK = 16
GLA_GATE_NORM = 16.0
N_GROUPS = 4
EXPERTS_PER_GROUP = 4
N_EXPERTS = 16
MOE_BLK = 256
ROUTE_IDX = 16
ROUTE_W = 18
N_LEVELS = 6
MILD_DECAY = -40.0
VMEM_LIMIT = 56 * 1024 * 1024

W_RG = 2 * BRANCH_W
W_GLA = 4 * BRANCH_W + HEAD_W
W_HG = 5 * BRANCH_W
W_ML = 4 * BRANCH_W + 2 * HEAD_W
SPLIT_LEVELS = (0, 2, N_LEVELS)


def _mm(a, b):
    return jnp.dot(a.astype(BF16), b.astype(BF16), preferred_element_type=F32)


def _mm_nt(a, b):
    return lax.dot_general(a.astype(BF16), b.astype(BF16), (((1,), (1,)), ((), ())),
                           preferred_element_type=F32)


def _mm_tn(a, b):
    return lax.dot_general(a.astype(BF16), b.astype(BF16), (((0,), (0,)), ((), ())),
                           preferred_element_type=F32)


def _mm_sel(sel, x):
    x1 = x.astype(BF16)
    x2 = (x - x1.astype(F32)).astype(BF16)
    dot = functools.partial(jnp.dot, preferred_element_type=F32)
    return dot(sel, x1) + dot(sel, x2)


def _split_cols(w):
    hi = w.astype(BF16)
    return jnp.concatenate([hi, (w - hi.astype(F32)).astype(BF16)], axis=1)


def _mm_split(x, w_split):
    n = w_split.shape[1] // 2
    x1 = x.astype(BF16)
    x2 = (x - x1.astype(F32)).astype(BF16)
    p = jnp.dot(x1, w_split, preferred_element_type=F32)
    return p[:, :n] + p[:, n:] + jnp.dot(x2, w_split[:, :n], preferred_element_type=F32)


def _log_sigmoid(x):
    return jnp.minimum(x, 0.0) - jnp.log1p(jnp.exp(-jnp.abs(x)))


def _sigmoid(x):
    return 0.5 * jnp.tanh(0.5 * x) + 0.5


def _silu(x):
    return x * _sigmoid(x)


def _rmsnorm_rows(x, g):
    return x * lax.rsqrt(jnp.mean(x * x, axis=-1, keepdims=True) + EPS) * g


def _head_rmsnorm(o, g):
    parts = [_rmsnorm_rows(o[:, h * HEAD_W:(h + 1) * HEAD_W], g) for h in range(N_HEADS)]
    return jnp.concatenate(parts, axis=-1)


def _bwd_tile(s, nc, nt):
    return jnp.where(s < nc, nc - 1 - s, nt - 1 - (s - nc))


def _const_spec(shape):
    nd = len(shape)
    return pl.BlockSpec(shape, lambda *_: (0,) * nd)


def _params(sem):
    return pltpu.CompilerParams(dimension_semantics=sem, vmem_limit_bytes=VMEM_LIMIT)


def _mod_kernel(c_ref, w_ref, b_ref, o_ref):
    cv = _silu(c_ref[...])
    o_ref[...] = jnp.dot(cv, w_ref[...], precision=lax.Precision.HIGHEST,
                         preferred_element_type=F32) + b_ref[...]


def _mod_call(cvec, ada_w, ada_b):
    depth, d, six_d = ada_w.shape
    tn = 1024
    return pl.pallas_call(
        _mod_kernel,
        grid=(depth, six_d // tn),
        in_specs=[pl.BlockSpec((SUBLANES, d), lambda l, j: (0, 0)),
                  pl.BlockSpec((None, d, tn), lambda l, j: (l, 0, j)),
                  pl.BlockSpec((None, 1, tn), lambda l, j: (l, 0, j))],
        out_specs=pl.BlockSpec((None, SUBLANES, tn), lambda l, j: (l, 0, j)),
        out_shape=jax.ShapeDtypeStruct((depth, SUBLANES, six_d), F32),
        compiler_params=_params(("parallel", "parallel")),
        name="adaln_mod",
    )(cvec, ada_w, ada_b.reshape(depth, 1, six_d))


def _stage_a_kernel(h_ref, mod_ref, g_ref, w_rg, w_gla, w_hg, w_ml,
                    z_rg, z_gla, z_hg, z_ml, u_ref):
    x = h_ref[...]
    u = _rmsnorm_rows(x, g_ref[...]) * (1.0 + mod_ref[1:2, :]) + mod_ref[0:1, :]
    ub = u.astype(BF16)
    u_ref[...] = ub
    z_rg[...] = jnp.dot(ub, w_rg[...], preferred_element_type=F32)
    z_gla[...] = jnp.dot(ub, w_gla[...], preferred_element_type=F32)
    z_hg[...] = jnp.dot(ub, w_hg[...], preferred_element_type=F32)
    z_ml[...] = jnp.dot(ub, w_ml[...], preferred_element_type=F32)


def _stage_a_call(h, mod, norm_g, w_rg, w_gla, w_hg, w_ml, nc):
    bsz, n, d = h.shape
    nt = n // TM
    tm = TM // 2

    def tok(w):
        return pl.BlockSpec((None, tm, w), lambda b, t: (b, t, 0))

    return pl.pallas_call(
        _stage_a_kernel,
        grid=(bsz, n // tm),
        in_specs=[tok(d),
                  pl.BlockSpec((None, 6, d), lambda b, t: (jnp.where(t * tm < nc * TM, bsz, b), 0, 0)),
                  _const_spec((1, d)),
                  _const_spec((d, W_RG)), _const_spec((d, W_GLA)),
                  _const_spec((d, W_HG)), _const_spec((d, W_ML))],
        out_specs=[tok(W_RG), tok(W_GLA), tok(W_HG), tok(W_ML), tok(d)],
        out_shape=[jax.ShapeDtypeStruct((bsz, n, W_RG), F32),
                   jax.ShapeDtypeStruct((bsz, n, W_GLA), F32),
                   jax.ShapeDtypeStruct((bsz, n, W_HG), F32),
                   jax.ShapeDtypeStruct((bsz, n, W_ML), F32),
                   jax.ShapeDtypeStruct((bsz, n, d), BF16)],
        compiler_params=_params(("parallel", "parallel")),
        name="stage_a",
    )(h, mod, norm_g.reshape(1, d), w_rg, w_gla, w_hg, w_ml)


def _conv_tile(x_ref, prev_ref, next_ref, pad_ref, cw_ref, cb_ref, tile, nc, nt):
    prev_ok = jnp.logical_and(tile != 0, tile != nc)
    next_ok = jnp.logical_and(tile != nc - 1, tile != nt - 1)
    pad_ref[0:SUBLANES, :] = jnp.where(prev_ok, prev_ref[...], 0.0)
    pad_ref[SUBLANES:SUBLANES + TM, :] = x_ref[...]
    pad_ref[SUBLANES + TM:2 * SUBLANES + TM, :] = jnp.where(next_ok, next_ref[...], 0.0)
    acc = cb_ref[...]
    for j in range(CONV_W):
        off = SUBLANES - CONV_LEFT + j
        acc = acc + cw_ref[j:j + 1, :] * pad_ref[off:off + TM, :]
    return acc


def _halo_specs(width, col_block, tile_fn, n):
    rows = TM // SUBLANES
    last = n // SUBLANES - 1
    cur = pl.BlockSpec((None, TM, width), lambda b, s: (b, tile_fn(s), col_block))
    prev = pl.BlockSpec((None, SUBLANES, width),
                        lambda b, s: (b, jnp.maximum(tile_fn(s) * rows - 1, 0), col_block))
    nxt = pl.BlockSpec((None, SUBLANES, width),
                       lambda b, s: (b, jnp.minimum((tile_fn(s) + 1) * rows, last), col_block))
    return [cur, prev, nxt]


def _scan_rows8(a, b, reverse):
    n = a.shape[0]
    pos = jnp.bitwise_and(lax.broadcasted_iota(jnp.int32, a.shape, 0), SUBLANES - 1)
    k = 1
    while k < SUBLANES:
        if reverse:
            a_s, b_s, ok = pltpu.roll(a, n - k, 0), pltpu.roll(b, n - k, 0), pos < SUBLANES - k
        else:
            a_s, b_s, ok = pltpu.roll(a, k, 0), pltpu.roll(b, k, 0), pos >= k
        b = b + a * jnp.where(ok, b_s, 0.0)
        a = a * jnp.where(ok, a_s, 1.0)
        k *= 2
    return a, b


def _scan_tile(a, b, carry, o_ref, reverse):
    a8, b8 = _scan_rows8(a, b, reverse)
    groups = a.shape[0] // SUBLANES
    order = range(groups - 1, -1, -1) if reverse else range(groups)
    for r in order:
        rows = slice(r * SUBLANES, (r + 1) * SUBLANES)
        h = b8[rows] + a8[rows] * carry
        o_ref[rows, :] = h
        carry = h[0:1, :] if reverse else h[SUBLANES - 1:SUBLANES, :]
    return carry


def _rg_kernel(xf, xf_p, xf_n, xb, xb_p, xb_n, cw, cb, gw, gb, lam,
               hf_ref, hb_ref, pad_ref, carry_ref, *, nc, nt):
    s = pl.program_id(1)

    @pl.when(s == 0)
    def _():
        carry_ref[...] = jnp.zeros_like(carry_ref)

    dirs = ((xf, xf_p, xf_n, hf_ref, s, False),
            (xb, xb_p, xb_n, hb_ref, _bwd_tile(s, nc, nt), True))
    for d, (x_ref, p_ref, n_ref, o_ref, tile, reverse) in enumerate(dirs):
        x = _conv_tile(x_ref, p_ref, n_ref, pad_ref, cw, cb, tile, nc, nt)
        r = _sigmoid(_mm(x, gw[d, 0]) + gb[2 * d:2 * d + 1, :])
        i = _sigmoid(_mm(x, gw[d, 1]) + gb[2 * d + 1:2 * d + 2, :])
        lam_d = lam[d:d + 1, :]
        softplus = jnp.maximum(-lam_d, 0.0) + jnp.log1p(jnp.exp(-jnp.abs(lam_d)))
        log_a = -RG_C * r * softplus
        a = jnp.exp(log_a)
        t = jnp.tanh(log_a)
        bt = jnp.sqrt(-2.0 * t / (1.0 - t)) * (i * x)
        carry_ref[d:d + 1, :] = _scan_tile(a, bt, carry_ref[d:d + 1, :], o_ref, reverse)


def _rg_call(z_rg, cw, cb, gw, gb, lam, nc):
    bsz, n, _ = z_rg.shape
    nt = n // TM
    w = BRANCH_W
    fwd = lambda s: s
    bwd = lambda s: _bwd_tile(s, nc, nt)
    out = lambda fn: pl.BlockSpec((None, TM, w), lambda b, s: (b, fn(s), 0))
    return pl.pallas_call(
        functools.partial(_rg_kernel, nc=nc, nt=nt),
        grid=(bsz, nt),
        in_specs=_halo_specs(w, 0, fwd, n) + _halo_specs(w, 0, bwd, n) + [
            _const_spec((CONV_W, w)), _const_spec((1, w)),
            _const_spec((2, 2, w, w)), _const_spec((4, w)), _const_spec((2, w))],
        out_specs=[out(fwd), out(bwd)],
        out_shape=[jax.ShapeDtypeStruct((bsz, n, w), F32)] * 2,
        scratch_shapes=[pltpu.VMEM((TM + 2 * SUBLANES, w), F32), pltpu.VMEM((2, w), F32)],
        compiler_params=_params(("parallel", "arbitrary")),
        name="rglru",
    )(z_rg, z_rg, z_rg, z_rg, z_rg, z_rg, cw, cb, gw, gb, lam)


def _decay_tables(levels):
    n = CHUNK
    blk = n >> levels
    sel = np.zeros((levels + 3, n, n), np.float32)
    masks = np.zeros((levels + 1, n, n), np.float32)
    for lvl in range(levels):
        half = n >> (lvl + 1)
        for r in range(n):
            start = (r // (2 * half)) * 2 * half
            ref = start + half - 1
            if r - start >= half:
                sel[lvl, r, ref + 1:r + 1] = 1.0
                masks[lvl, r, start:start + half] = 1.0
            else:
                sel[lvl, r, r + 1:ref + 1] = 1.0
    sel[levels] = np.tril(np.ones((n, n), np.float32))
    sel[levels + 1] = 1.0 - sel[levels]
    for r in range(n):
        start = (r // blk) * blk
        sel[levels + 2, r, r + 1:start + blk] = 1.0
        masks[levels, r, start:r + 1] = 1.0
    if levels == 0 or levels == N_LEVELS:
        sel = sel[:levels + 2]
    sel_b = sel[:, ::-1, ::-1].reshape(-1, n)
    masks_b = masks[:, ::-1, ::-1]
    return (jnp.asarray(np.stack([sel.reshape(-1, n), sel_b]), BF16),
            jnp.asarray(np.stack([masks, masks_b]), F32))


def _ref_rows(levels, reverse):
    def runs(block, offset):
        return [(s + offset, block) for s in range(0, CHUNK, block)]

    out = [runs(CHUNK >> lvl, (CHUNK >> (lvl + 1)) - 1) for lvl in range(levels)]
    out.append(runs(CHUNK, CHUNK - 1))
    out.append(runs(CHUNK >> levels, (CHUNK >> levels) - 1))
    if reverse:
        out = [[(CHUNK - 1 - r, n) for r, n in reversed(segs)] for segs in out]
    return out


def _rows_bcast(b, segs):
    return jnp.concatenate([jnp.broadcast_to(b[r:r + 1, :], (n, b.shape[-1])) for r, n in segs], axis=0)


def _gla_chunk(q, k, v, g, sel, masks, st_ref, d, reverse, levels):
    if levels == N_LEVELS:
        x = jnp.exp(_mm_sel(sel, g))
        x_lvl = [x[lvl * CHUNK:(lvl + 1) * CHUNK] for lvl in range(levels)]
        x_cum = x[levels * CHUNK:(levels + 1) * CHUNK]
        x_rest = x[(levels + 1) * CHUNK:(levels + 2) * CHUNK]
        xk_blk = xq_blk = None
    else:
        b = _mm_sel(sel[levels * CHUNK:(levels + 1) * CHUNK], g)
        refs = _ref_rows(levels, reverse)
        x_cum = jnp.exp(b)
        x_lvl = [jnp.exp(-jnp.abs(b - _rows_bcast(b, segs))) for segs in refs[:levels]]
        x_rest = jnp.exp(-jnp.abs(b - _rows_bcast(b, refs[levels])))
        if levels == 0:
            xk_blk, xq_blk = x_rest, jnp.exp(jnp.abs(b - _rows_bcast(b, refs[levels])))
        else:
            d_blk = jnp.abs(b - _rows_bcast(b, refs[levels + 1]))
            xk_blk, xq_blk = jnp.exp(-d_blk), jnp.exp(d_blk)
    outs = []
    for h in range(N_HEADS):
        cols = slice(h * HEAD_W, (h + 1) * HEAD_W)
        qh, kh, vh = q[:, cols], k[:, cols], v[:, cols]
        kx = kh * x_rest[:, cols]
        if xk_blk is None:
            sc = _mm_nt(qh, kh)
        elif levels == 0:
            sc = _mm_nt(qh * xq_blk[:, cols], kx)
        else:
            sc = _mm_nt(qh * xq_blk[:, cols], kh * xk_blk[:, cols])
        sc = jnp.where(masks[levels] > 0.0, sc, 0.0)
        for lvl in range(levels):
            xl = x_lvl[lvl][:, cols]
            sc = sc + jnp.where(masks[lvl] > 0.0, _mm_nt(qh * xl, kh * xl), 0.0)
        st = st_ref[d, h]
        outs.append(_mm(sc, vh) + _mm_nt(qh * x_cum[:, cols], st))
        x_end = x_cum[0:1, cols] if reverse else x_cum[CHUNK - 1:CHUNK, cols]
        st_ref[d, h] = st * x_end + _mm_tn(vh, kx)
    return jnp.concatenate(outs, axis=-1)


def _gla_step(sides, tabs, st_ref):
    n_chunks = TM // CHUNK

    def min_block_sum(block):
        mins = [jnp.min(jnp.sum(side[3].reshape(TM // block, block, side[3].shape[-1]), axis=1))
                for side in sides]
        return functools.reduce(jnp.minimum, mins)

    conds = []
    taken = None
    for levels in SPLIT_LEVELS[:-1]:
        ok = min_block_sum(CHUNK >> levels) > MILD_DECAY
        conds.append(ok if taken is None else jnp.logical_and(ok, jnp.logical_not(taken)))
        taken = ok if taken is None else jnp.logical_or(taken, ok)
    conds.append(jnp.logical_not(taken))

    for levels, cond, (sel_ref, mask_ref) in zip(SPLIT_LEVELS, conds, tabs):
        @pl.when(cond)
        def _(levels=levels, sel_ref=sel_ref, mask_ref=mask_ref):
            for c in range(n_chunks):
                for i, (q, k, v, g, o_ref, reverse) in enumerate(sides):
                    cc = n_chunks - 1 - c if reverse else c
                    rows = slice(cc * CHUNK, (cc + 1) * CHUNK)
                    o_ref[rows, :] = _gla_chunk(q[rows], k[rows], v[rows], g[rows],
                                                sel_ref[int(reverse)], mask_ref[int(reverse)],
                                                st_ref, i, reverse, levels)


def _pairs(refs):
    return tuple(zip(refs[0::2], refs[1::2]))


def _gla_kernel(zf, zb, wlr, blr, *rest):
    tabs, (of_ref, ob_ref, st_ref) = _pairs(rest[:-3]), rest[-3:]
    w = BRANCH_W

    @pl.when(pl.program_id(0) == 0)
    def _():
        st_ref[...] = jnp.zeros_like(st_ref)

    sides = []
    for b in range(zf.shape[0]):
        for d, (z_ref, o_ref) in enumerate(((zf, of_ref), (zb, ob_ref))):
            pre = _mm(z_ref[b, :, 4 * w:4 * w + HEAD_W], wlr[d]) + blr[d:d + 1, :]
            g = _log_sigmoid(pre) * (1.0 / GLA_GATE_NORM)
            q = z_ref[b, :, 0:w] * (GLA_DK ** -0.5)
            sides.append((q, z_ref[b, :, w:2 * w], z_ref[b, :, 2 * w:3 * w], g, o_ref.at[b], d == 1))
    _gla_step(sides, tabs, st_ref)


def _hg_kernel(zf, zb, lbp, *rest):
    tabs, (of_ref, ob_ref, st_ref) = _pairs(rest[:-3]), rest[-3:]
    w = BRANCH_W

    @pl.when(pl.program_id(0) == 0)
    def _():
        st_ref[...] = jnp.zeros_like(st_ref)

    sides = []
    for b in range(zf.shape[0]):
        for d, (z_ref, o_ref) in enumerate(((zf, of_ref), (zb, ob_ref))):
            f = z_ref[b, :, (2 + d) * w:(3 + d) * w]
            lb = lbp[d:d + 1, :]
            log_lb = lbp[2 + d:3 + d, :]
            log_1m = lbp[4 + d:5 + d, :]
            c = log_1m + _log_sigmoid(f)
            g = jnp.maximum(log_lb, c) + jnp.log1p(jnp.exp(-jnp.abs(log_lb - c)))
            k = (1.0 - lb) * _sigmoid(-f)
            q = _silu(z_ref[b, :, 0:w]) * (HEAD_W ** -0.5)
            sides.append((q, k, z_ref[b, :, w:2 * w], g, o_ref.at[b], d == 1))
    _gla_step(sides, tabs, st_ref)


def _mixer_specs(widths_cols, tile_fn):
    return [pl.BlockSpec((None, TM, w), lambda b, s, c=c: (b, tile_fn(s), c)) for w, c in widths_cols]


def _all_rows_spec(bsz, width, tile_fn):
    return pl.BlockSpec((bsz, TM, width), lambda s: (0, tile_fn(s), 0))


def _gla_call(z_gla, wlr, blr, tables, nc):
    bsz, n, _ = z_gla.shape
    nt = n // TM
    w = BRANCH_W
    fwd = lambda s: s
    bwd = lambda s: _bwd_tile(s, nc, nt)
    return pl.pallas_call(
        _gla_kernel,
        grid=(nt,),
        in_specs=[_all_rows_spec(bsz, W_GLA, fwd), _all_rows_spec(bsz, W_GLA, bwd),
                  _const_spec(wlr.shape), _const_spec(blr.shape)] + [_const_spec(t.shape) for t in tables],
        out_specs=[_all_rows_spec(bsz, w, fwd), _all_rows_spec(bsz, w, bwd)],
        out_shape=[jax.ShapeDtypeStruct((bsz, n, w), F32)] * 2,
        scratch_shapes=[pltpu.VMEM((2 * bsz, N_HEADS, HEAD_W, HEAD_W), F32)],
        compiler_params=_params(("arbitrary",)),
        name="gla",
    )(z_gla, z_gla, wlr, blr, *tables)


def _hg_call(z_hg, lbp, tables, nc):
    bsz, n, _ = z_hg.shape
    nt = n // TM
    w = BRANCH_W
    fwd = lambda s: s
    bwd = lambda s: _bwd_tile(s, nc, nt)
    return pl.pallas_call(
        _hg_kernel,
        grid=(nt,),
        in_specs=[_all_rows_spec(bsz, 4 * w, fwd), _all_rows_spec(bsz, 4 * w, bwd),
                  _const_spec(lbp.shape)] + [_const_spec(t.shape) for t in tables],
        out_specs=[_all_rows_spec(bsz, w, fwd), _all_rows_spec(bsz, w, bwd)],
        out_shape=[jax.ShapeDtypeStruct((bsz, n, w), F32)] * 2,
        scratch_shapes=[pltpu.VMEM((2 * bsz, N_HEADS, HEAD_W, HEAD_W), F32)],
        compiler_params=_params(("arbitrary",)),
        name="hgrn2",
    )(z_hg, z_hg, lbp, *tables)


def _cummax_rows(a, reverse):
    n = a.shape[0]
    row = lax.broadcasted_iota(jnp.int32, a.shape, 0)
    k = 1
    while k < n:
        if reverse:
            a_s, ok = pltpu.roll(a, n - k, 0), row < n - k
        else:
            a_s, ok = pltpu.roll(a, k, 0), row >= k
        a = jnp.maximum(a, jnp.where(ok, a_s, -jnp.inf))
        k *= 2
    return a


def _ml_chunk(q, k, v, gi, log_f, tri_sel, tri_mask, ct_ref, m_ref, slot, d, reverse):
    b = _mm_sel(tri_sel, log_f)
    a = gi - b
    m_prev = m_ref[slot:slot + 1, :]
    m_rel = jnp.maximum(_cummax_rows(a, reverse), m_prev)
    w_inter = jnp.exp(m_prev - m_rel)
    exp_neg_m = jnp.exp(-(b + m_rel))
    last = 0 if reverse else CHUNK - 1
    b_end = b[last:last + 1, :]
    m_new = b_end + m_rel[last:last + 1, :]
    w_end = jnp.exp(b_end - b + gi - m_new)
    decay = jnp.exp(b_end + m_prev - m_new)
    m_ref[slot:slot + 1, :] = m_new
    a_t = a.T
    ones = jnp.ones((CHUNK, HEAD_W), BF16)
    outs = []
    for h in range(N_HEADS):
        cols = slice(h * HEAD_W, (h + 1) * HEAD_W)
        qh, kh, vh = q[:, cols], k[:, cols], v[:, cols]
        c = d * N_HEADS + h
        log_w = jnp.where(tri_mask > 0.0, a_t[c:c + 1, :] - m_rel[:, c:c + 1], -jnp.inf)
        s = _mm_nt(qh, kh) * jnp.exp(log_w)
        v_ext = jnp.concatenate([vh.astype(BF16), ones], axis=-1)
        st = slot * N_HEADS + h
        ct = ct_ref[st]
        tot = _mm(s, v_ext) + w_inter[:, c:c + 1] * _mm_nt(qh, ct)
        num, den = tot[:, :HEAD_W], tot[:, HEAD_W:]
        outs.append(num / jnp.maximum(jnp.abs(den), exp_neg_m[:, c:c + 1]))
        ct_ref[st] = decay[:, c:c + 1] * ct + _mm_tn(v_ext, kh * w_end[:, c:c + 1])
    return jnp.concatenate(outs, axis=-1)


def _ml_kernel(zf, zf_p, zf_n, zb, zb_p, zb_n, cw, cb, gbias,
               tri_sel, tri_mask, of_ref, ob_ref, pad_ref, ct_ref, m_ref, *, nc, nt):
    s = pl.program_id(0)
    w = BRANCH_W
    n_chunks = TM // CHUNK

    @pl.when(s == 0)
    def _():
        ct_ref[...] = jnp.zeros_like(ct_ref)
        m_ref[...] = jnp.zeros_like(m_ref)

    dirs = ((zf, zf_p, zf_n, of_ref, s), (zb, zb_p, zb_n, ob_ref, _bwd_tile(s, nc, nt)))
    sides = []
    for b in range(zf.shape[0]):
        for d, (z_ref, p_ref, nx_ref, o_ref, tile) in enumerate(dirs):
            slot = 2 * b + d
            qk = _silu(_conv_tile(z_ref.at[b, :, 0:2 * w], p_ref.at[b], nx_ref.at[b],
                                  pad_ref.at[slot], cw, cb, tile, nc, nt))
            gi = z_ref[b, :, 4 * w:4 * w + HEAD_W] + gbias[0:1, :]
            log_f = _log_sigmoid(z_ref[b, :, 4 * w + HEAD_W:4 * w + 2 * HEAD_W] + gbias[1:2, :])
            sides.append((qk[:, :w] * (HEAD_W ** -0.5), qk[:, w:], z_ref[b, :, 2 * w:3 * w],
                          gi, log_f, o_ref.at[b], slot, d))
    for c in range(n_chunks):
        for q, k, v, gi, log_f, o_ref, slot, d in sides:
            cc = n_chunks - 1 - c if d == 1 else c
            rows = slice(cc * CHUNK, (cc + 1) * CHUNK)
            o_ref[rows, :] = _ml_chunk(q[rows], k[rows], v[rows], gi[rows], log_f[rows],
                                       tri_sel[d], tri_mask[d], ct_ref, m_ref, slot, d, d == 1)


def _ml_call(z_ml, cw, cb, gbias, tri_sel, tri_mask, nc):
    bsz, n, _ = z_ml.shape
    nt = n // TM
    w = BRANCH_W
    fwd = lambda s: s
    bwd = lambda s: _bwd_tile(s, nc, nt)
    rows8 = TM // SUBLANES
    last8 = n // SUBLANES - 1

    def side(fn):
        return [_all_rows_spec(bsz, W_ML, fn),
                pl.BlockSpec((bsz, SUBLANES, 2 * w), lambda s: (0, jnp.maximum(fn(s) * rows8 - 1, 0), 0)),
                pl.BlockSpec((bsz, SUBLANES, 2 * w), lambda s: (0, jnp.minimum((fn(s) + 1) * rows8, last8), 0))]

    return pl.pallas_call(
        functools.partial(_ml_kernel, nc=nc, nt=nt),
        grid=(nt,),
        in_specs=side(fwd) + side(bwd) + [
            _const_spec((CONV_W, 2 * w)), _const_spec((1, 2 * w)), _const_spec((2, HEAD_W)),
            _const_spec(tri_sel.shape), _const_spec(tri_mask.shape)],
        out_specs=[_all_rows_spec(bsz, w, fwd), _all_rows_spec(bsz, w, bwd)],
        out_shape=[jax.ShapeDtypeStruct((bsz, n, w), F32)] * 2,
        scratch_shapes=[pltpu.VMEM((2 * bsz, TM + 2 * SUBLANES, 2 * w), F32),
                        pltpu.VMEM((2 * bsz * N_HEADS, 2 * HEAD_W, HEAD_W), F32),
                        pltpu.VMEM((2 * bsz, HEAD_W), F32)],
        compiler_params=_params(("arbitrary",)),
        name="mlstm",
    )(*([z_ml] * 6), cw, cb, gbias, tri_sel, tri_mask)


def _gelu_tanh(x):
    return 0.5 * x * (1.0 + jnp.tanh(0.7978845608028654 * (x + 0.044715 * (x * x * x))))


def _route(logits):
    col = lax.broadcasted_iota(jnp.int32, logits.shape, 1)
    colf = col.astype(F32)
    is_g = jnp.logical_and(col >= N_EXPERTS, col < N_EXPERTS + N_GROUPS)
    is_e = col < N_EXPERTS
    neg = -jnp.inf
    big = 1e9
    gl = jnp.where(is_g, logits, neg)
    g_max = jnp.max(gl, axis=-1, keepdims=True)
    grp = jnp.min(jnp.where(gl == g_max, colf, big), axis=-1, keepdims=True) - N_EXPERTS
    p_grp = 1.0 / jnp.sum(jnp.exp(gl - g_max), axis=-1, keepdims=True)
    col_grp = lax.shift_right_logical(col, EXPERTS_PER_GROUP.bit_length() - 1)
    in_grp = jnp.logical_and(is_e, col_grp.astype(F32) == grp)
    e1 = jnp.where(in_grp, logits, neg)
    top1 = jnp.max(e1, axis=-1, keepdims=True)
    idx1 = jnp.min(jnp.where(e1 == top1, colf, big), axis=-1, keepdims=True)
    e2 = jnp.where(colf == idx1, neg, e1)
    top2 = jnp.max(e2, axis=-1, keepdims=True)
    idx2 = jnp.min(jnp.where(e2 == top2, colf, big), axis=-1, keepdims=True)
    t = jnp.exp(top2 - top1)
    w1 = p_grp / (1.0 + t)
    w2 = p_grp * (t / (1.0 + t))
    out = jnp.where(col == ROUTE_IDX, idx1, jnp.where(col == ROUTE_IDX + 1, idx2, 0.0))
    return jnp.where(col == ROUTE_W, w1, jnp.where(col == ROUTE_W + 1, w2, out))


def _route_onehot(route):
    colf = lax.broadcasted_iota(jnp.int32, route.shape, 1).astype(F32)
    oh1 = colf == route[:, ROUTE_IDX:ROUTE_IDX + 1]
    oh2 = colf == route[:, ROUTE_IDX + 1:ROUTE_IDX + 2]
    return jnp.where(jnp.logical_or(oh1, oh2), 1.0, 0.0), oh1, oh2


def _stage_c_kernel(u_ref, h_ref, mod_ref, rg_y, rg_f, rg_b, gla_g, gla_f, gla_b, hg_g, hg_f, hg_b,
                    ml_o, ml_f, ml_b, gains, w_merge, b_merge, w_branch, w_out, ffn_g, w_route, b_route,
                    h_out, u2_out, wt_out, cnt_out):
    ys = (
        _gelu_tanh(rg_y[...]) * (rg_f[...] + rg_b[...]),
        _head_rmsnorm(gla_f[...] + gla_b[...], gains[0:1, :]) * _silu(gla_g[...]),
        _head_rmsnorm(hg_f[...] + hg_b[...], gains[1:2, :]) * _silu(hg_g[...]),
        _sigmoid(ml_o[...]) * _head_rmsnorm(ml_f[...] + ml_b[...], gains[2:3, :]),
    )
    u = u_ref[...]
    d = u.shape[-1]
    merged = None
    for kk, y in enumerate(ys):
        gate = _sigmoid(jnp.dot(u, w_merge[:, kk * d:(kk + 1) * d], preferred_element_type=F32)
                              + b_merge[:, kk * d:(kk + 1) * d])
        term = gate * _mm(y, w_branch[kk])
        merged = term if merged is None else merged + term
    mix = _mm(merged, w_out[...])
    h_new = h_ref[...] + mod_ref[2:3, :] * mix
    h_out[...] = h_new
    u2 = _rmsnorm_rows(h_new, ffn_g[...]) * (1.0 + mod_ref[4:5, :]) + mod_ref[3:4, :]
    _store_tiled(u2_out, u2)
    logits = _mm_split(u2, w_route[...]) + b_route[...]
    route = _route(logits)
    wt_out[...] = route
    cnt = jnp.sum(_route_onehot(route)[0], axis=0, keepdims=True)
    cnt_out[...] = jnp.broadcast_to(cnt, cnt_out.shape)


def _stage_c_call(u, h, mod, z_rg, rg_f, rg_b, z_gla, gla_f, gla_b, z_hg, hg_f, hg_b, z_ml, ml_f, ml_b,
                  gains, w_merge, b_merge, w_branch, w_out, ffn_g, w_route, b_route, nc, t0):
    bsz, n, d = h.shape
    nt = n // TM
    w = BRANCH_W

    n_out = n - t0 * TM

    def tok(width, col=0):
        return pl.BlockSpec((None, TM, width), lambda b, t: (b, t + t0, col))

    def out_tok(width):
        return pl.BlockSpec((None, TM, width), lambda b, t: (b, t, 0))

    mod_spec = pl.BlockSpec((None, 6, d), lambda b, t: (jnp.where(t + t0 < nc, bsz, b), 0, 0))
    return pl.pallas_call(
        _stage_c_kernel,
        grid=(bsz, nt - t0),
        in_specs=[tok(d), tok(d), mod_spec,
                  tok(w, 1), tok(w), tok(w),
                  tok(w, 3), tok(w), tok(w),
                  tok(w, 4), tok(w), tok(w),
                  tok(w, 3), tok(w), tok(w),
                  _const_spec(gains.shape), _const_spec(w_merge.shape), _const_spec(b_merge.shape),
                  _const_spec(w_branch.shape), _const_spec(w_out.shape), _const_spec(ffn_g.shape),
                  _const_spec(w_route.shape), _const_spec(b_route.shape)],
        out_specs=[out_tok(d),
                   pl.BlockSpec((None, TM * SUBLANES, HEAD_W), lambda b, t: (b, t, 0)),
                   out_tok(HEAD_W),
                   pl.BlockSpec((None, None, SUBLANES, HEAD_W), lambda b, t: (b, t, 0, 0))],
        out_shape=[jax.ShapeDtypeStruct((bsz, n_out, d), F32),
                   jax.ShapeDtypeStruct((bsz, n_out * SUBLANES, HEAD_W), F32),
                   jax.ShapeDtypeStruct((bsz, n_out, HEAD_W), F32),
                   jax.ShapeDtypeStruct((bsz, nt - t0, SUBLANES, HEAD_W), F32)],
        compiler_params=_params(("parallel", "parallel")),
        name="stage_c",
    )(u, h, mod, z_rg, rg_f, rg_b, z_gla, gla_f, gla_b, z_hg, hg_f, hg_b, z_ml, ml_f, ml_b,
      gains, w_merge, b_merge, w_branch, w_out, ffn_g, w_route, b_route)


def _slots_to_smem(slots_vmem, slots_smem, sem):
    cp = pltpu.make_async_copy(slots_vmem, slots_smem, sem)
    cp.start()
    cp.wait()


def _row_copy(src, src_row, dst, dst_row, sem):
    def tile(ref, row):
        return ref.at[pl.ds(pl.multiple_of(row * SUBLANES, SUBLANES), SUBLANES), :]
    return pltpu.make_async_copy(tile(src, src_row), tile(dst, dst_row), sem)


def _to_tiles(x):
    return [x[:, k * HEAD_W:(k + 1) * HEAD_W] for k in range(SUBLANES)]


def _store_tiled(ref, x):
    rows = x.shape[0]
    for k, piece in enumerate(_to_tiles(x)):
        ref[pl.ds(k, rows, stride=SUBLANES), :] = piece


def _load_tiled(ref, rows):
    return jnp.concatenate([ref[pl.ds(k, rows, stride=SUBLANES), :] for k in range(SUBLANES)], axis=-1)


def _dispatch_kernel(seg_ref, route_ref, base_ref, tri_ref, x_ref, xs_hbm, slots_ref,
                     slots_smem, zero_buf, sem_s, sem_r, sem_z):
    @pl.when(pl.program_id(0) == 0)
    def _():
        zero_buf[...] = jnp.zeros_like(zero_buf)

        def fill(e):
            start_row = pl.multiple_of((seg_ref[e] - MOE_BLK) * SUBLANES, SUBLANES)
            return pltpu.make_async_copy(
                zero_buf, xs_hbm.at[pl.ds(start_row, MOE_BLK * SUBLANES), :], sem_z)

        for e in range(N_EXPERTS):
            @pl.when(seg_ref[N_EXPERTS + e] > 0)
            def _(e=e):
                fill(e).start()
        for e in range(N_EXPERTS):
            @pl.when(seg_ref[N_EXPERTS + e] > 0)
            def _(e=e):
                fill(e).wait()

        blk_rows = MOE_BLK * SUBLANES
        used = seg_ref[N_EXPERTS - 1] // MOE_BLK

        def tail(b):
            return pltpu.make_async_copy(
                zero_buf, xs_hbm.at[pl.ds(pl.multiple_of(b * blk_rows, blk_rows), blk_rows), :], sem_z)

        def tail_start(b, carry):
            tail(b).start()
            return carry

        def tail_wait(b, carry):
            tail(b).wait()
            return carry

        n_blk = xs_hbm.shape[0] // blk_rows
        lax.fori_loop(used, n_blk, tail_start, 0)
        lax.fori_loop(used, n_blk, tail_wait, 0)

    route = route_ref[...]
    onehot, oh1, oh2 = _route_onehot(route)
    rank = jnp.dot(tri_ref[...], onehot.astype(BF16), preferred_element_type=F32)
    pos = base_ref[...] + rank
    slot_a = jnp.sum(jnp.where(oh1, pos, 0.0), axis=-1, keepdims=True)
    slot_b = jnp.sum(jnp.where(oh2, pos, 0.0), axis=-1, keepdims=True)
    col = lax.broadcasted_iota(jnp.int32, route.shape, 1)
    both = jnp.where(col == 0, slot_a, jnp.where(col == 1, slot_b, 0.0))
    slots_ref[...] = both.T[0:SUBLANES, :].astype(jnp.int32)
    _slots_to_smem(slots_ref, slots_smem, sem_s)

    def start(t, carry):
        for j in range(2):
            _row_copy(x_ref, t, xs_hbm, slots_smem[j, t], sem_r).start(priority=j)
        return carry

    def wait(t, carry):
        for j in range(2):
            _row_copy(x_ref, 0, xs_hbm, 0, sem_r).wait()
        return carry

    lax.fori_loop(0, TM, start, 0, unroll=8)
    lax.fori_loop(0, TM, wait, 0, unroll=8)


def _experts_kernel(blk_expert, n_used, x_ref, w1, w3, w2, o_ref):
    del blk_expert

    @pl.when(pl.program_id(0) < n_used[0])
    def _():
        x = _load_tiled(x_ref, MOE_BLK)
        h1 = _mm(x, w1[...])
        h3 = _mm(x, w3[...])
        _store_tiled(o_ref, _mm(_silu(h1) * h3, w2[...]))

    @pl.when(pl.program_id(0) >= n_used[0])
    def _():
        o_ref[...] = jnp.zeros_like(o_ref)


def _combine_kernel(slots_ref, route_ref, h_ref, mod_ref, fin_g, ys_hbm, o_ref,
                    slots_smem, buf, sem_s, sem_r, *, final):
    _slots_to_smem(slots_ref, slots_smem, sem_s)

    def start(t, carry):
        for j in range(2):
            _row_copy(ys_hbm, slots_smem[j, t], buf.at[j], t, sem_r).start(priority=j)
        return carry

    def wait(t, carry):
        for j in range(2):
            _row_copy(ys_hbm, 0, buf.at[j], 0, sem_r).wait()
        return carry

    lax.fori_loop(0, TM, start, 0, unroll=8)
    lax.fori_loop(0, TM, wait, 0, unroll=8)
    route = route_ref[...]
    y = (route[:, ROUTE_W:ROUTE_W + 1] * _load_tiled(buf.at[0], TM)
         + route[:, ROUTE_W + 1:ROUTE_W + 2] * _load_tiled(buf.at[1], TM))
    out = h_ref[...] + mod_ref[5:6, :] * y
    if final:
        out = _rmsnorm_rows(out, fin_g[...])
    o_ref[...] = out


def _moe_call(u2, route, cnt, h, mod, fin_g, w1, w3, w2, layer, nc, t0, final):
    bsz, n_out, d = h.shape
    de = w1.shape[-1]
    rows = bsz * n_out
    tiles = rows // TM
    tiles_per_row = n_out // TM
    n_blocks = (2 * rows + N_EXPERTS * (MOE_BLK - 1)) // MOE_BLK + 1

    counts = cnt[:, :, 0, :N_EXPERTS].reshape(tiles, N_EXPERTS)
    total = jnp.sum(counts, axis=0)
    padded = jnp.ceil(total / MOE_BLK) * MOE_BLK
    seg_end = jnp.cumsum(padded)
    base = (seg_end - padded)[None, :] + jnp.cumsum(counts, axis=0) - counts
    base = _pad_cols(base, HEAD_W).reshape(tiles, 1, HEAD_W)
    blk_start = jnp.arange(n_blocks, dtype=F32) * MOE_BLK
    blk_expert = jnp.minimum(jnp.sum(blk_start[:, None] >= seg_end[None, :], axis=1), N_EXPERTS - 1)
    n_used = (seg_end[-1] / MOE_BLK).astype(jnp.int32).reshape(1)
    tri = jnp.asarray(np.tril(np.ones((TM, TM), np.float32), -1), BF16)

    assert d == SUBLANES * HEAD_W
    slot_rows = n_blocks * MOE_BLK * SUBLANES
    seg = jnp.concatenate([seg_end, padded]).astype(jnp.int32)
    xs, slots = pl.pallas_call(
        _dispatch_kernel,
        grid_spec=pltpu.PrefetchScalarGridSpec(
            num_scalar_prefetch=1,
            grid=(tiles,),
            in_specs=[pl.BlockSpec((TM, HEAD_W), lambda t, sg: (t, 0)),
                      pl.BlockSpec((None, 1, HEAD_W), lambda t, sg: (t, 0, 0)),
                      pl.BlockSpec((TM, TM), lambda t, sg: (0, 0)),
                      pl.BlockSpec((TM * SUBLANES, HEAD_W), lambda t, sg: (t, 0))],
            out_specs=[pl.BlockSpec(memory_space=pl.ANY),
                       pl.BlockSpec((SUBLANES, TM), lambda t, sg: (t, 0))],
            scratch_shapes=[pltpu.SMEM((SUBLANES, TM), jnp.int32),
                            pltpu.VMEM((MOE_BLK * SUBLANES, HEAD_W), F32),
                            pltpu.SemaphoreType.DMA, pltpu.SemaphoreType.DMA,
                            pltpu.SemaphoreType.DMA]),
        out_shape=[jax.ShapeDtypeStruct((slot_rows, HEAD_W), F32),
                   jax.ShapeDtypeStruct((tiles * SUBLANES, TM), jnp.int32)],
        compiler_params=_params(("arbitrary",)),
        name="moe_dispatch",
    )(seg, route.reshape(rows, HEAD_W), base, tri, u2.reshape(rows * SUBLANES, HEAD_W))

    blk_spec = pl.BlockSpec((MOE_BLK * SUBLANES, HEAD_W), lambda b, be, nu: (b, 0))
    used_blk_spec = pl.BlockSpec((MOE_BLK * SUBLANES, HEAD_W),
                                 lambda b, be, nu: (jnp.minimum(b, nu[0] - 1), 0))
    ys = pl.pallas_call(
        _experts_kernel,
        grid_spec=pltpu.PrefetchScalarGridSpec(
            num_scalar_prefetch=2,
            grid=(n_blocks,),
            in_specs=[used_blk_spec,
                      pl.BlockSpec((None, None, d, de), lambda b, be, nu: (layer, be[b], 0, 0)),
                      pl.BlockSpec((None, None, d, de), lambda b, be, nu: (layer, be[b], 0, 0)),
                      pl.BlockSpec((None, None, de, d), lambda b, be, nu: (layer, be[b], 0, 0))],
            out_specs=blk_spec),
        out_shape=jax.ShapeDtypeStruct((slot_rows, HEAD_W), F32),
        compiler_params=_params(("arbitrary",)),
        name="moe_experts",
    )(blk_expert.astype(jnp.int32), n_used, xs, w1, w3, w2)

    def mod_index(t):
        b, pos = t // tiles_per_row, t % tiles_per_row
        return (jnp.where(pos + t0 < nc, bsz, b), 0, 0)

    out = pl.pallas_call(
        functools.partial(_combine_kernel, final=final),
        grid=(tiles,),
        in_specs=[pl.BlockSpec((SUBLANES, TM), lambda t: (t, 0)),
                  pl.BlockSpec((TM, HEAD_W), lambda t: (t, 0)),
                  pl.BlockSpec((TM, d), lambda t: (t, 0)),
                  pl.BlockSpec((None, 6, d), mod_index),
                  _const_spec((1, d)),
                  pl.BlockSpec(memory_space=pl.ANY)],
        out_specs=pl.BlockSpec((TM, d), lambda t: (t, 0)),
        out_shape=jax.ShapeDtypeStruct((rows, d), F32),
        scratch_shapes=[pltpu.SMEM((SUBLANES, TM), jnp.int32),
                        pltpu.VMEM((2, TM * SUBLANES, HEAD_W), F32),
                        pltpu.SemaphoreType.DMA, pltpu.SemaphoreType.DMA],
        compiler_params=_params(("arbitrary",)),
        name="moe_combine",
    )(slots, route.reshape(rows, HEAD_W), h.reshape(rows, d), mod, fin_g, ys)
    return out.reshape(bsz, n_out, d)


def _pad_heads(w, dk):
    lead = w.shape[:-1]
    w = w.reshape(lead + (N_HEADS, dk))
    w = jnp.pad(w, [(0, 0)] * len(lead) + [(0, 0), (0, HEAD_W - dk)])
    return w.reshape(lead + (N_HEADS * HEAD_W,))


def _pad_cols(w, width):
    return jnp.pad(w, [(0, 0)] * (w.ndim - 1) + [(0, width - w.shape[-1])])


def _split_w_in(w_in):
    bw = BRANCH_W
    sizes = [bw, bw, N_HEADS * GLA_DK, N_HEADS * GLA_DK, bw, bw, 2 * GLA_RANK,
             bw, bw, 2 * bw, bw, bw, bw, bw, bw, 16]
    offs = np.cumsum([0] + sizes)
    p = [w_in[:, offs[i]:offs[i + 1]] for i in range(len(sizes))]
    w_rg = jnp.concatenate([p[0], p[1]], axis=1)
    w_gla = jnp.concatenate([_pad_heads(p[2], GLA_DK), _pad_heads(p[3], GLA_DK), p[4], p[5],
                             _pad_cols(p[6], HEAD_W)], axis=1)
    w_hg = jnp.concatenate([p[7], p[8], p[9], p[10]], axis=1)
    gates = p[15].reshape(-1, 2, 2, N_HEADS)
    w_gi = _pad_cols(gates[:, :, 0].reshape(-1, 2 * N_HEADS), HEAD_W)
    w_gf = _pad_cols(gates[:, :, 1].reshape(-1, 2 * N_HEADS), HEAD_W)
    w_ml = jnp.concatenate([p[11], p[12], p[13], p[14], w_gi, w_gf], axis=1)
    return [w.astype(BF16) for w in (w_rg, w_gla, w_hg, w_ml)]


def _block_diag(w):
    k, n = w.shape[-3], w.shape[-1]
    eye = jnp.eye(k, dtype=w.dtype)
    full = jnp.einsum('...kij,kl->...kilj', w, eye)
    return full.reshape(w.shape[:-3] + (k * n, k * n))


def kernel(x, c, ctx, c_ctx, ada_w, ada_b, norm_mix_g, norm_ffn_g, w_in, rg_conv_w, rg_conv_b, rg_gate_w, rg_gate_b, rg_lambda, gla_w_lr, gla_b_lr, gla_norm_g, hgrn_lb_logits, hgrn_norm_g, ml_conv_w, ml_conv_b, ml_gate_b, ml_norm_g, w_branch, w_merge, b_merge, w_out, moe_w_group, moe_b_group, moe_w_expert, moe_b_expert, moe_w1, moe_w3, moe_w2, final_norm_g):
    bsz, seq, d = x.shape
    n_ctx = ctx.shape[1]
    depth = ada_w.shape[0]
    assert n_ctx % TM == 0 and seq % TM == 0 and d == 2 * BRANCH_W
    nc = n_ctx // TM

    h = jnp.concatenate([ctx, x], axis=1)
    cvec = jnp.zeros((SUBLANES, d), F32).at[:bsz].set(c).at[bsz].set(c_ctx)
    mod_all = _mod_call(cvec, ada_w, ada_b).reshape(depth, SUBLANES, 6, d)[:, :bsz + 1]

    lb_cum = jnp.cumsum(jax.nn.softmax(hgrn_lb_logits.astype(F32), axis=0), axis=0)
    hgrn_lb = lb_cum - lb_cum[:1]
    tables = [t for levels in SPLIT_LEVELS for t in _decay_tables(levels)]
    tri_sel = tables[0][:, :CHUNK, :]
    tri_mask = tables[1][:, 0]

    out = None
    for l in range(depth):
        last = l == depth - 1
        mod = mod_all[l]
        w_rg, w_gla, w_hg, w_ml = _split_w_in(w_in[l])
        z_rg, z_gla, z_hg, z_ml, u = _stage_a_call(h, mod, norm_mix_g[l], w_rg, w_gla, w_hg, w_ml, nc)

        rg_f, rg_b = _rg_call(z_rg, rg_conv_w[l], rg_conv_b[l].reshape(1, -1),
                              _block_diag(rg_gate_w[l]).astype(BF16),
                              rg_gate_b[l].reshape(4, -1), rg_lambda[l], nc)

        wlr = jnp.zeros((2, HEAD_W, N_HEADS * HEAD_W), F32)
        wlr_p = _pad_heads(gla_w_lr[l], GLA_DK)
        wlr = wlr.at[0, :GLA_RANK].set(wlr_p[0]).at[1, GLA_RANK:2 * GLA_RANK].set(wlr_p[1])
        gla_f, gla_b = _gla_call(z_gla, wlr.astype(BF16), _pad_heads(gla_b_lr[l], GLA_DK), tables, nc)

        lb = hgrn_lb[l]
        lbp = jnp.concatenate([lb, jnp.log(lb), jnp.log1p(-lb), jnp.zeros((2, lb.shape[-1]), F32)], axis=0)
        hg_f, hg_b = _hg_call(z_hg, lbp, tables, nc)

        gbias = _pad_cols(ml_gate_b[l].transpose(1, 0, 2).reshape(2, -1), HEAD_W)
        ml_f, ml_b = _ml_call(z_ml, ml_conv_w[l], ml_conv_b[l].reshape(1, -1), gbias, tri_sel, tri_mask, nc)

        gains = jnp.zeros((SUBLANES, HEAD_W), F32).at[0].set(gla_norm_g[l]).at[1].set(hgrn_norm_g[l]).at[2].set(ml_norm_g[l])
        w_route = _split_cols(_pad_cols(jnp.concatenate([moe_w_expert[l], moe_w_group[l]], axis=1), HEAD_W))
        b_route = _pad_cols(jnp.concatenate([moe_b_expert[l], moe_b_group[l]]).reshape(1, -1), HEAD_W)
        t0 = nc if last else 0
        h_mid, u2, route, cnt = _stage_c_call(
            u, h, mod, z_rg, rg_f, rg_b, z_gla, gla_f, gla_b, z_hg, hg_f, hg_b, z_ml, ml_f, ml_b,
            gains, w_merge[l].astype(BF16), b_merge[l].reshape(1, -1), w_branch[l].astype(BF16),
            w_out[l].astype(BF16), norm_ffn_g[l].reshape(1, -1), w_route, b_route, nc, t0)

        res = _moe_call(u2, route, cnt, h_mid, mod, final_norm_g.reshape(1, -1), moe_w1,
                        moe_w3, moe_w2, l, nc, t0, last)
        if last:
            out = res
        else:
            h = res
    return out
```

```python
import functools

import numpy as np
import jax
import jax.numpy as jnp
from jax import lax
from jax.experimental import pallas as pl
from jax.experimental.pallas import tpu as pltpu

F32 = jnp.float32
BF16 = jnp.bfloat16

EPS = 1e-6
TM = 256
CHUNK = 64
SUBLANES = 8
N_HEADS = 4
HEAD_W = 128
BRANCH_W = 512
CONV_W = 4
CONV_LEFT = 2
RG_C = 8.0
GLA_DK = 64
GLA_RANK = 16
GLA_GATE_NORM = 16.0
N_GROUPS = 4
EXPERTS_PER_GROUP = 4
N_EXPERTS = 16
MOE_BLK = 256
ROUTE_IDX = 16
ROUTE_W = 18
N_LEVELS = 6
MILD_DECAY = -40.0
VMEM_LIMIT = 56 * 1024 * 1024

W_RG = 2 * BRANCH_W
W_GLA = 4 * BRANCH_W + HEAD_W
W_HG = 5 * BRANCH_W
W_ML = 4 * BRANCH_W + 2 * HEAD_W
SPLIT_LEVELS = (0, 2, N_LEVELS)


def _mm(a, b):
    return jnp.dot(a.astype(BF16), b.astype(BF16), preferred_element_type=F32)


def _mm_nt(a, b):
    return lax.dot_general(a.astype(BF16), b.astype(BF16), (((1,), (1,)), ((), ())),
                           preferred_element_type=F32)


def _mm_tn(a, b):
    return lax.dot_general(a.astype(BF16), b.astype(BF16), (((0,), (0,)), ((), ())),
                           preferred_element_type=F32)


def _mm_sel(sel, x):
    x1 = x.astype(BF16)
    x2 = (x - x1.astype(F32)).astype(BF16)
    dot = functools.partial(jnp.dot, preferred_element_type=F32)
    return dot(sel, x1) + dot(sel, x2)


def _split_cols(w):
    hi = w.astype(BF16)
    return jnp.concatenate([hi, (w - hi.astype(F32)).astype(BF16)], axis=1)


def _mm_split(x, w_split):
    n = w_split.shape[1] // 2
    x1 = x.astype(BF16)
    x2 = (x - x1.astype(F32)).astype(BF16)
    p = jnp.dot(x1, w_split, preferred_element_type=F32)
    return p[:, :n] + p[:, n:] + jnp.dot(x2, w_split[:, :n], preferred_element_type=F32)


def _log_sigmoid(x):
    return jnp.minimum(x, 0.0) - jnp.log1p(jnp.exp(-jnp.abs(x)))


def _sigmoid(x):
    return 0.5 * jnp.tanh(0.5 * x) + 0.5


def _silu(x):
    return x * _sigmoid(x)


def _rmsnorm_rows(x, g):
    return x * lax.rsqrt(jnp.mean(x * x, axis=-1, keepdims=True) + EPS) * g


def _head_rmsnorm(o, g):
    parts = [_rmsnorm_rows(o[:, h * HEAD_W:(h + 1) * HEAD_W], g) for h in range(N_HEADS)]
    return jnp.concatenate(parts, axis=-1)


def _bwd_tile(s, nc, nt):
    return jnp.where(s < nc, nc - 1 - s, nt - 1 - (s - nc))


def _const_spec(shape):
    nd = len(shape)
    return pl.BlockSpec(shape, lambda *_: (0,) * nd)


def _params(sem):
    return pltpu.CompilerParams(dimension_semantics=sem, vmem_limit_bytes=VMEM_LIMIT)


def _mod_kernel(c_ref, w_ref, b_ref, o_ref):
    cv = _silu(c_ref[...])
    o_ref[...] = jnp.dot(cv, w_ref[...], precision=lax.Precision.HIGHEST,
                         preferred_element_type=F32) + b_ref[...]


def _mod_call(cvec, ada_w, ada_b):
    depth, d, six_d = ada_w.shape
    tn = 1024
    return pl.pallas_call(
        _mod_kernel,
        grid=(depth, six_d // tn),
        in_specs=[pl.BlockSpec((SUBLANES, d), lambda l, j: (0, 0)),
                  pl.BlockSpec((None, d, tn), lambda l, j: (l, 0, j)),
                  pl.BlockSpec((None, 1, tn), lambda l, j: (l, 0, j))],
        out_specs=pl.BlockSpec((None, SUBLANES, tn), lambda l, j: (l, 0, j)),
        out_shape=jax.ShapeDtypeStruct((depth, SUBLANES, six_d), F32),
        compiler_params=_params(("parallel", "parallel")),
        name="adaln_mod",
    )(cvec, ada_w, ada_b.reshape(depth, 1, six_d))


def _stage_a_kernel(h_ref, mod_ref, g_ref, w_rg, w_gla, w_hg, w_ml,
                    z_rg, z_gla, z_hg, z_ml, u_ref):
    x = h_ref[...]
    u = _rmsnorm_rows(x, g_ref[...]) * (1.0 + mod_ref[1:2, :]) + mod_ref[0:1, :]
    ub = u.astype(BF16)
    u_ref[...] = ub
    z_rg[...] = jnp.dot(ub, w_rg[...], preferred_element_type=F32)
    z_gla[...] = jnp.dot(ub, w_gla[...], preferred_element_type=F32)
    z_hg[...] = jnp.dot(ub, w_hg[...], preferred_element_type=F32)
    z_ml[...] = jnp.dot(ub, w_ml[...], preferred_element_type=F32)


def _stage_a_call(h, mod, norm_g, w_rg, w_gla, w_hg, w_ml, nc):
    bsz, n, d = h.shape
    nt = n // TM
    tm = TM // 2

    def tok(w):
        return pl.BlockSpec((None, tm, w), lambda b, t: (b, t, 0))

    return pl.pallas_call(
        _stage_a_kernel,
        grid=(bsz, n // tm),
        in_specs=[tok(d),
                  pl.BlockSpec((None, 6, d), lambda b, t: (jnp.where(t * tm < nc * TM, bsz, b), 0, 0)),
                  _const_spec((1, d)),
                  _const_spec((d, W_RG)), _const_spec((d, W_GLA)),
                  _const_spec((d, W_HG)), _const_spec((d, W_ML))],
        out_specs=[tok(W_RG), tok(W_GLA), tok(W_HG), tok(W_ML), tok(d)],
        out_shape=[jax.ShapeDtypeStruct((bsz, n, W_RG), F32),
                   jax.ShapeDtypeStruct((bsz, n, W_GLA), F32),
                   jax.ShapeDtypeStruct((bsz, n, W_HG), F32),
                   jax.ShapeDtypeStruct((bsz, n, W_ML), F32),
                   jax.ShapeDtypeStruct((bsz, n, d), BF16)],
        compiler_params=_params(("parallel", "parallel")),
        name="stage_a",
    )(h, mod, norm_g.reshape(1, d), w_rg, w_gla, w_hg, w_ml)


def _conv_tile(x_ref, prev_ref, next_ref, pad_ref, cw_ref, cb_ref, tile, nc, nt):
    prev_ok = jnp.logical_and(tile != 0, tile != nc)
    next_ok = jnp.logical_and(tile != nc - 1, tile != nt - 1)
    pad_ref[0:SUBLANES, :] = jnp.where(prev_ok, prev_ref[...], 0.0)
    pad_ref[SUBLANES:SUBLANES + TM, :] = x_ref[...]
    pad_ref[SUBLANES + TM:2 * SUBLANES + TM, :] = jnp.where(next_ok, next_ref[...], 0.0)
    acc = cb_ref[...]
    for j in range(CONV_W):
        off = SUBLANES - CONV_LEFT + j
        acc = acc + cw_ref[j:j + 1, :] * pad_ref[off:off + TM, :]
    return acc


def _halo_specs(width, col_block, tile_fn, n):
    rows = TM // SUBLANES
    last = n // SUBLANES - 1
    cur = pl.BlockSpec((None, TM, width), lambda b, s: (b, tile_fn(s), col_block))
    prev = pl.BlockSpec((None, SUBLANES, width),
                        lambda b, s: (b, jnp.maximum(tile_fn(s) * rows - 1, 0), col_block))
    nxt = pl.BlockSpec((None, SUBLANES, width),
                       lambda b, s: (b, jnp.minimum((tile_fn(s) + 1) * rows, last), col_block))
    return [cur, prev, nxt]


def _scan_rows8(a, b, reverse):
    n = a.shape[0]
    pos = jnp.bitwise_and(lax.broadcasted_iota(jnp.int32, a.shape, 0), SUBLANES - 1)
    k = 1
    while k < SUBLANES:
        if reverse:
            a_s, b_s, ok = pltpu.roll(a, n - k, 0), pltpu.roll(b, n - k, 0), pos < SUBLANES - k
        else:
            a_s, b_s, ok = pltpu.roll(a, k, 0), pltpu.roll(b, k, 0), pos >= k
        b = b + a * jnp.where(ok, b_s, 0.0)
        a = a * jnp.where(ok, a_s, 1.0)
        k *= 2
    return a, b


def _scan_tile(a, b, carry, o_ref, reverse):
    a8, b8 = _scan_rows8(a, b, reverse)
    groups = a.shape[0] // SUBLANES
    order = range(groups - 1, -1, -1) if reverse else range(groups)
    for r in order:
        rows = slice(r * SUBLANES, (r + 1) * SUBLANES)
        h = b8[rows] + a8[rows] * carry
        o_ref[rows, :] = h
        carry = h[0:1, :] if reverse else h[SUBLANES - 1:SUBLANES, :]
    return carry


def _rg_kernel(xf, xf_p, xf_n, xb, xb_p, xb_n, cw, cb, gw, gb, lam,
               hf_ref, hb_ref, pad_ref, carry_ref, *, nc, nt):
    s = pl.program_id(1)

    @pl.when(s == 0)
    def _():
        carry_ref[...] = jnp.zeros_like(carry_ref)

    dirs = ((xf, xf_p, xf_n, hf_ref, s, False),
            (xb, xb_p, xb_n, hb_ref, _bwd_tile(s, nc, nt), True))
    for d, (x_ref, p_ref, n_ref, o_ref, tile, reverse) in enumerate(dirs):
        x = _conv_tile(x_ref, p_ref, n_ref, pad_ref, cw, cb, tile, nc, nt)
        r = _sigmoid(_mm(x, gw[d, 0]) + gb[2 * d:2 * d + 1, :])
        i = _sigmoid(_mm(x, gw[d, 1]) + gb[2 * d + 1:2 * d + 2, :])
        lam_d = lam[d:d + 1, :]
        softplus = jnp.maximum(-lam_d, 0.0) + jnp.log1p(jnp.exp(-jnp.abs(lam_d)))
        log_a = -RG_C * r * softplus
        a = jnp.exp(log_a)
        t = jnp.tanh(log_a)
        bt = jnp.sqrt(-2.0 * t / (1.0 - t)) * (i * x)
        carry_ref[d:d + 1, :] = _scan_tile(a, bt, carry_ref[d:d + 1, :], o_ref, reverse)


def _rg_call(z_rg, cw, cb, gw, gb, lam, nc):
    bsz, n, _ = z_rg.shape
    nt = n // TM
    w = BRANCH_W
    fwd = lambda s: s
    bwd = lambda s: _bwd_tile(s, nc, nt)
    out = lambda fn: pl.BlockSpec((None, TM, w), lambda b, s: (b, fn(s), 0))
    return pl.pallas_call(
        functools.partial(_rg_kernel, nc=nc, nt=nt),
        grid=(bsz, nt),
        in_specs=_halo_specs(w, 0, fwd, n) + _halo_specs(w, 0, bwd, n) + [
            _const_spec((CONV_W, w)), _const_spec((1, w)),
            _const_spec((2, 2, w, w)), _const_spec((4, w)), _const_spec((2, w))],
        out_specs=[out(fwd), out(bwd)],
        out_shape=[jax.ShapeDtypeStruct((bsz, n, w), F32)] * 2,
        scratch_shapes=[pltpu.VMEM((TM + 2 * SUBLANES, w), F32), pltpu.VMEM((2, w), F32)],
        compiler_params=_params(("parallel", "arbitrary")),
        name="rglru",
    )(z_rg, z_rg, z_rg, z_rg, z_rg, z_rg, cw, cb, gw, gb, lam)


def _decay_tables(levels):
    n = CHUNK
    blk = n >> levels
    sel = np.zeros((levels + 3, n, n), np.float32)
    masks = np.zeros((levels + 1, n, n), np.float32)
    for lvl in range(levels):
        half = n >> (lvl + 1)
        for r in range(n):
            start = (r // (2 * half)) * 2 * half
            ref = start + half - 1
            if r - start >= half:
                sel[lvl, r, ref + 1:r + 1] = 1.0
                masks[lvl, r, start:start + half] = 1.0
            else:
                sel[lvl, r, r + 1:ref + 1] = 1.0
    sel[levels] = np.tril(np.ones((n, n), np.float32))
    sel[levels + 1] = 1.0 - sel[levels]
    for r in range(n):
        start = (r // blk) * blk
        sel[levels + 2, r, r + 1:start + blk] = 1.0
        masks[levels, r, start:r + 1] = 1.0
    if levels == 0 or levels == N_LEVELS:
        sel = sel[:levels + 2]
    sel_b = sel[:, ::-1, ::-1].reshape(-1, n)
    masks_b = masks[:, ::-1, ::-1]
    return (jnp.asarray(np.stack([sel.reshape(-1, n), sel_b]), BF16),
            jnp.asarray(np.stack([masks, masks_b]), F32))


def _ref_rows(levels, reverse):
    def runs(block, offset):
        return [(s + offset, block) for s in range(0, CHUNK, block)]

    out = [runs(CHUNK >> lvl, (CHUNK >> (lvl + 1)) - 1) for lvl in range(levels)]
    out.append(runs(CHUNK, CHUNK - 1))
    out.append(runs(CHUNK >> levels, (CHUNK >> levels) - 1))
    if reverse:
        out = [[(CHUNK - 1 - r, n) for r, n in reversed(segs)] for segs in out]
    return out


def _rows_bcast(b, segs):
    return jnp.concatenate([jnp.broadcast_to(b[r:r + 1, :], (n, b.shape[-1])) for r, n in segs], axis=0)


def _gla_chunk(q, k, v, g, sel, masks, st_ref, d, reverse, levels):
    if levels == N_LEVELS:
        x = jnp.exp(_mm_sel(sel, g))
        x_lvl = [x[lvl * CHUNK:(lvl + 1) * CHUNK] for lvl in range(levels)]
        x_cum = x[levels * CHUNK:(levels + 1) * CHUNK]
        x_rest = x[(levels + 1) * CHUNK:(levels + 2) * CHUNK]
        xk_blk = xq_blk = None
    else:
        b = _mm_sel(sel[levels * CHUNK:(levels + 1) * CHUNK], g)
        refs = _ref_rows(levels, reverse)
        x_cum = jnp.exp(b)
        x_lvl = [jnp.exp(-jnp.abs(b - _rows_bcast(b, segs))) for segs in refs[:levels]]
        x_rest = jnp.exp(-jnp.abs(b - _rows_bcast(b, refs[levels])))
        if levels == 0:
            xk_blk, xq_blk = x_rest, jnp.exp(jnp.abs(b - _rows_bcast(b, refs[levels])))
        else:
            d_blk = jnp.abs(b - _rows_bcast(b, refs[levels + 1]))
            xk_blk, xq_blk = jnp.exp(-d_blk), jnp.exp(d_blk)
    outs = []
    for h in range(N_HEADS):
        cols = slice(h * HEAD_W, (h + 1) * HEAD_W)
        qh, kh, vh = q[:, cols], k[:, cols], v[:, cols]
        kx = kh * x_rest[:, cols]
        if xk_blk is None:
            sc = _mm_nt(qh, kh)
        elif levels == 0:
            sc = _mm_nt(qh * xq_blk[:, cols], kx)
        else:
            sc = _mm_nt(qh * xq_blk[:, cols], kh * xk_blk[:, cols])
        sc = jnp.where(masks[levels] > 0.0, sc, 0.0)
        for lvl in range(levels):
            xl = x_lvl[lvl][:, cols]
            sc = sc + jnp.where(masks[lvl] > 0.0, _mm_nt(qh * xl, kh * xl), 0.0)
        st = st_ref[d, h]
        outs.append(_mm(sc, vh) + _mm_nt(qh * x_cum[:, cols], st))
        x_end = x_cum[0:1, cols] if reverse else x_cum[CHUNK - 1:CHUNK, cols]
        st_ref[d, h] = st * x_end + _mm_tn(vh, kx)
    return jnp.concatenate(outs, axis=-1)


def _gla_step(sides, tabs, st_ref):
    n_chunks = TM // CHUNK

    def min_block_sum(block):
        mins = [jnp.min(jnp.sum(side[3].reshape(TM // block, block, side[3].shape[-1]), axis=1))
                for side in sides]
        return functools.reduce(jnp.minimum, mins)

    conds = []
    taken = None
    for levels in SPLIT_LEVELS[:-1]:
        ok = min_block_sum(CHUNK >> levels) > MILD_DECAY
        conds.append(ok if taken is None else jnp.logical_and(ok, jnp.logical_not(taken)))
        taken = ok if taken is None else jnp.logical_or(taken, ok)
    conds.append(jnp.logical_not(taken))

    for levels, cond, (sel_ref, mask_ref) in zip(SPLIT_LEVELS, conds, tabs):
        @pl.when(cond)
        def _(levels=levels, sel_ref=sel_ref, mask_ref=mask_ref):
            for c in range(n_chunks):
                for i, (q, k, v, g, o_ref, reverse) in enumerate(sides):
                    cc = n_chunks - 1 - c if reverse else c
                    rows = slice(cc * CHUNK, (cc + 1) * CHUNK)
                    o_ref[rows, :] = _gla_chunk(q[rows], k[rows], v[rows], g[rows],
                                                sel_ref[int(reverse)], mask_ref[int(reverse)],
                                                st_ref, i, reverse, levels)


def _pairs(refs):
    return tuple(zip(refs[0::2], refs[1::2]))


def _gla_kernel(zf, zb, wlr, blr, *rest):
    tabs, (of_ref, ob_ref, st_ref) = _pairs(rest[:-3]), rest[-3:]
    w = BRANCH_W

    @pl.when(pl.program_id(0) == 0)
    def _():
        st_ref[...] = jnp.zeros_like(st_ref)

    sides = []
    for b in range(zf.shape[0]):
        for d, (z_ref, o_ref) in enumerate(((zf, of_ref), (zb, ob_ref))):
            pre = _mm(z_ref[b, :, 4 * w:4 * w + HEAD_W], wlr[d]) + blr[d:d + 1, :]
            g = _log_sigmoid(pre) * (1.0 / GLA_GATE_NORM)
            q = z_ref[b, :, 0:w] * (GLA_DK ** -0.5)
            sides.append((q, z_ref[b, :, w:2 * w], z_ref[b, :, 2 * w:3 * w], g, o_ref.at[b], d == 1))
    _gla_step(sides, tabs, st_ref)


def _hg_kernel(zf, zb, lbp, *rest):
    tabs, (of_ref, ob_ref, st_ref) = _pairs(rest[:-3]), rest[-3:]
    w = BRANCH_W

    @pl.when(pl.program_id(0) == 0)
    def _():
        st_ref[...] = jnp.zeros_like(st_ref)

    sides = []
    for b in range(zf.shape[0]):
        for d, (z_ref, o_ref) in enumerate(((zf, of_ref), (zb, ob_ref))):
            f = z_ref[b, :, (2 + d) * w:(3 + d) * w]
            lb = lbp[d:d + 1, :]
            log_lb = lbp[2 + d:3 + d, :]
            log_1m = lbp[4 + d:5 + d, :]
            c = log_1m + _log_sigmoid(f)
            g = jnp.maximum(log_lb, c) + jnp.log1p(jnp.exp(-jnp.abs(log_lb - c)))
            k = (1.0 - lb) * _sigmoid(-f)
            q = _silu(z_ref[b, :, 0:w]) * (HEAD_W ** -0.5)
            sides.append((q, k, z_ref[b, :, w:2 * w], g, o_ref.at[b], d == 1))
    _gla_step(sides, tabs, st_ref)


def _mixer_specs(widths_cols, tile_fn):
    return [pl.BlockSpec((None, TM, w), lambda b, s, c=c: (b, tile_fn(s), c)) for w, c in widths_cols]


def _all_rows_spec(bsz, width, tile_fn):
    return pl.BlockSpec((bsz, TM, width), lambda s: (0, tile_fn(s), 0))


def _gla_call(z_gla, wlr, blr, tables, nc):
    bsz, n, _ = z_gla.shape
    nt = n // TM
    w = BRANCH_W
    fwd = lambda s: s
    bwd = lambda s: _bwd_tile(s, nc, nt)
    return pl.pallas_call(
        _gla_kernel,
        grid=(nt,),
        in_specs=[_all_rows_spec(bsz, W_GLA, fwd), _all_rows_spec(bsz, W_GLA, bwd),
                  _const_spec(wlr.shape), _const_spec(blr.shape)] + [_const_spec(t.shape) for t in tables],
        out_specs=[_all_rows_spec(bsz, w, fwd), _all_rows_spec(bsz, w, bwd)],
        out_shape=[jax.ShapeDtypeStruct((bsz, n, w), F32)] * 2,
        scratch_shapes=[pltpu.VMEM((2 * bsz, N_HEADS, HEAD_W, HEAD_W), F32)],
        compiler_params=_params(("arbitrary",)),
        name="gla",
    )(z_gla, z_gla, wlr, blr, *tables)


def _hg_call(z_hg, lbp, tables, nc):
    bsz, n, _ = z_hg.shape
    nt = n // TM
    w = BRANCH_W
    fwd = lambda s: s
    bwd = lambda s: _bwd_tile(s, nc, nt)
    return pl.pallas_call(
        _hg_kernel,
        grid=(nt,),
        in_specs=[_all_rows_spec(bsz, 4 * w, fwd), _all_rows_spec(bsz, 4 * w, bwd),
                  _const_spec(lbp.shape)] + [_const_spec(t.shape) for t in tables],
        out_specs=[_all_rows_spec(bsz, w, fwd), _all_rows_spec(bsz, w, bwd)],
        out_shape=[jax.ShapeDtypeStruct((bsz, n, w), F32)] * 2,
        scratch_shapes=[pltpu.VMEM((2 * bsz, N_HEADS, HEAD_W, HEAD_W), F32)],
        compiler_params=_params(("arbitrary",)),
        name="hgrn2",
    )(z_hg, z_hg, lbp, *tables)


def _cummax_rows(a, reverse):
    n = a.shape[0]
    row = lax.broadcasted_iota(jnp.int32, a.shape, 0)
    k = 1
    while k < n:
        if reverse:
            a_s, ok = pltpu.roll(a, n - k, 0), row < n - k
        else:
            a_s, ok = pltpu.roll(a, k, 0), row >= k
        a = jnp.maximum(a, jnp.where(ok, a_s, -jnp.inf))
        k *= 2
    return a


def _ml_chunk(q, k, v, gi, log_f, tri_sel, tri_mask, ct_ref, m_ref, slot, d, reverse):
    b = _mm_sel(tri_sel, log_f)
    a = gi - b
    m_prev = m_ref[slot:slot + 1, :]
    m_rel = jnp.maximum(_cummax_rows(a, reverse), m_prev)
    w_inter = jnp.exp(m_prev - m_rel)
    exp_neg_m = jnp.exp(-(b + m_rel))
    last = 0 if reverse else CHUNK - 1
    b_end = b[last:last + 1, :]
    m_new = b_end + m_rel[last:last + 1, :]
    w_end = jnp.exp(b_end - b + gi - m_new)
    decay = jnp.exp(b_end + m_prev - m_new)
    m_ref[slot:slot + 1, :] = m_new
    a_t = a.T
    ones = jnp.ones((CHUNK, HEAD_W), BF16)
    outs = []
    for h in range(N_HEADS):
        cols = slice(h * HEAD_W, (h + 1) * HEAD_W)
        qh, kh, vh = q[:, cols], k[:, cols], v[:, cols]
        c = d * N_HEADS + h
        log_w = jnp.where(tri_mask > 0.0, a_t[c:c + 1, :] - m_rel[:, c:c + 1], -jnp.inf)
        s = _mm_nt(qh, kh) * jnp.exp(log_w)
        v_ext = jnp.concatenate([vh.astype(BF16), ones], axis=-1)
        st = slot * N_HEADS + h
        ct = ct_ref[st]
        tot = _mm(s, v_ext) + w_inter[:, c:c + 1] * _mm_nt(qh, ct)
        num, den = tot[:, :HEAD_W], tot[:, HEAD_W:]
        outs.append(num / jnp.maximum(jnp.abs(den), exp_neg_m[:, c:c + 1]))
        ct_ref[st] = decay[:, c:c + 1] * ct + _mm_tn(v_ext, kh * w_end[:, c:c + 1])
    return jnp.concatenate(outs, axis=-1)


def _ml_kernel(zf, zf_p, zf_n, zb, zb_p, zb_n, cw, cb, gbias,
               tri_sel, tri_mask, of_ref, ob_ref, pad_ref, ct_ref, m_ref, *, nc, nt):
    s = pl.program_id(0)
    w = BRANCH_W
    n_chunks = TM // CHUNK

    @pl.when(s == 0)
    def _():
        ct_ref[...] = jnp.zeros_like(ct_ref)
        m_ref[...] = jnp.zeros_like(m_ref)

    dirs = ((zf, zf_p, zf_n, of_ref, s), (zb, zb_p, zb_n, ob_ref, _bwd_tile(s, nc, nt)))
    sides = []
    for b in range(zf.shape[0]):
        for d, (z_ref, p_ref, nx_ref, o_ref, tile) in enumerate(dirs):
            slot = 2 * b + d
            qk = _silu(_conv_tile(z_ref.at[b, :, 0:2 * w], p_ref.at[b], nx_ref.at[b],
                                  pad_ref.at[slot], cw, cb, tile, nc, nt))
            gi = z_ref[b, :, 4 * w:4 * w + HEAD_W] + gbias[0:1, :]
            log_f = _log_sigmoid(z_ref[b, :, 4 * w + HEAD_W:4 * w + 2 * HEAD_W] + gbias[1:2, :])
            sides.append((qk[:, :w] * (HEAD_W ** -0.5), qk[:, w:], z_ref[b, :, 2 * w:3 * w],
                          gi, log_f, o_ref.at[b], slot, d))
    for c in range(n_chunks):
        for q, k, v, gi, log_f, o_ref, slot, d in sides:
            cc = n_chunks - 1 - c if d == 1 else c
            rows = slice(cc * CHUNK, (cc + 1) * CHUNK)
            o_ref[rows, :] = _ml_chunk(q[rows], k[rows], v[rows], gi[rows], log_f[rows],
                                       tri_sel[d], tri_mask[d], ct_ref, m_ref, slot, d, d == 1)


def _ml_call(z_ml, cw, cb, gbias, tri_sel, tri_mask, nc):
    bsz, n, _ = z_ml.shape
    nt = n // TM
    w = BRANCH_W
    fwd = lambda s: s
    bwd = lambda s: _bwd_tile(s, nc, nt)
    rows8 = TM // SUBLANES
    last8 = n // SUBLANES - 1

    def side(fn):
        return [_all_rows_spec(bsz, W_ML, fn),
                pl.BlockSpec((bsz, SUBLANES, 2 * w), lambda s: (0, jnp.maximum(fn(s) * rows8 - 1, 0), 0)),
                pl.BlockSpec((bsz, SUBLANES, 2 * w), lambda s: (0, jnp.minimum((fn(s) + 1) * rows8, last8), 0))]

    return pl.pallas_call(
        functools.partial(_ml_kernel, nc=nc, nt=nt),
        grid=(nt,),
        in_specs=side(fwd) + side(bwd) + [
            _const_spec((CONV_W, 2 * w)), _const_spec((1, 2 * w)), _const_spec((2, HEAD_W)),
            _const_spec(tri_sel.shape), _const_spec(tri_mask.shape)],
        out_specs=[_all_rows_spec(bsz, w, fwd), _all_rows_spec(bsz, w, bwd)],
        out_shape=[jax.ShapeDtypeStruct((bsz, n, w), F32)] * 2,
        scratch_shapes=[pltpu.VMEM((2 * bsz, TM + 2 * SUBLANES, 2 * w), F32),
                        pltpu.VMEM((2 * bsz * N_HEADS, 2 * HEAD_W, HEAD_W), F32),
                        pltpu.VMEM((2 * bsz, HEAD_W), F32)],
        compiler_params=_params(("arbitrary",)),
        name="mlstm",
    )(*([z_ml] * 6), cw, cb, gbias, tri_sel, tri_mask)


def _gelu_tanh(x):
    return 0.5 * x * (1.0 + jnp.tanh(0.7978845608028654 * (x + 0.044715 * (x * x * x))))


def _route(logits):
    col = lax.broadcasted_iota(jnp.int32, logits.shape, 1)
    colf = col.astype(F32)
    is_g = jnp.logical_and(col >= N_EXPERTS, col < N_EXPERTS + N_GROUPS)
    is_e = col < N_EXPERTS
    neg = -jnp.inf
    big = 1e9
    gl = jnp.where(is_g, logits, neg)
    g_max = jnp.max(gl, axis=-1, keepdims=True)
    grp = jnp.min(jnp.where(gl == g_max, colf, big), axis=-1, keepdims=True) - N_EXPERTS
    p_grp = 1.0 / jnp.sum(jnp.exp(gl - g_max), axis=-1, keepdims=True)
    col_grp = lax.shift_right_logical(col, EXPERTS_PER_GROUP.bit_length() - 1)
    in_grp = jnp.logical_and(is_e, col_grp.astype(F32) == grp)
    e1 = jnp.where(in_grp, logits, neg)
    top1 = jnp.max(e1, axis=-1, keepdims=True)
    idx1 = jnp.min(jnp.where(e1 == top1, colf, big), axis=-1, keepdims=True)
    e2 = jnp.where(colf == idx1, neg, e1)
    top2 = jnp.max(e2, axis=-1, keepdims=True)
    idx2 = jnp.min(jnp.where(e2 == top2, colf, big), axis=-1, keepdims=True)
    t = jnp.exp(top2 - top1)
    w1 = p_grp / (1.0 + t)
    w2 = p_grp * (t / (1.0 + t))
    out = jnp.where(col == ROUTE_IDX, idx1, jnp.where(col == ROUTE_IDX + 1, idx2, 0.0))
    return jnp.where(col == ROUTE_W, w1, jnp.where(col == ROUTE_W + 1, w2, out))


def _route_onehot(route):
    colf = lax.broadcasted_iota(jnp.int32, route.shape, 1).astype(F32)
    oh1 = colf == route[:, ROUTE_IDX:ROUTE_IDX + 1]
    oh2 = colf == route[:, ROUTE_IDX + 1:ROUTE_IDX + 2]
    return jnp.where(jnp.logical_or(oh1, oh2), 1.0, 0.0), oh1, oh2


def _stage_c_kernel(u_ref, h_ref, mod_ref, rg_y, rg_f, rg_b, gla_g, gla_f, gla_b, hg_g, hg_f, hg_b,
                    ml_o, ml_f, ml_b, gains, w_merge, b_merge, w_branch, w_out, ffn_g, w_route, b_route,
                    h_out, u2_out, wt_out, cnt_out):
    ys = (
        _gelu_tanh(rg_y[...]) * (rg_f[...] + rg_b[...]),
        _head_rmsnorm(gla_f[...] + gla_b[...], gains[0:1, :]) * _silu(gla_g[...]),
        _head_rmsnorm(hg_f[...] + hg_b[...], gains[1:2, :]) * _silu(hg_g[...]),
        _sigmoid(ml_o[...]) * _head_rmsnorm(ml_f[...] + ml_b[...], gains[2:3, :]),
    )
    u = u_ref[...]
    d = u.shape[-1]
    merged = None
    for kk, y in enumerate(ys):
        gate = _sigmoid(jnp.dot(u, w_merge[:, kk * d:(kk + 1) * d], preferred_element_type=F32)
                              + b_merge[:, kk * d:(kk + 1) * d])
        term = gate * _mm(y, w_branch[kk])
        merged = term if merged is None else merged + term
    mix = _mm(merged, w_out[...])
    h_new = h_ref[...] + mod_ref[2:3, :] * mix
    h_out[...] = h_new
    u2 = _rmsnorm_rows(h_new, ffn_g[...]) * (1.0 + mod_ref[4:5, :]) + mod_ref[3:4, :]
    _store_tiled(u2_out, u2)
    logits = _mm_split(u2, w_route[...]) + b_route[...]
    route = _route(logits)
    wt_out[...] = route
    cnt = jnp.sum(_route_onehot(route)[0], axis=0, keepdims=True)
    cnt_out[...] = jnp.broadcast_to(cnt, cnt_out.shape)


def _stage_c_call(u, h, mod, z_rg, rg_f, rg_b, z_gla, gla_f, gla_b, z_hg, hg_f, hg_b, z_ml, ml_f, ml_b,
                  gains, w_merge, b_merge, w_branch, w_out, ffn_g, w_route, b_route, nc, t0):
    bsz, n, d = h.shape
    nt = n // TM
    w = BRANCH_W

    n_out = n - t0 * TM

    def tok(width, col=0):
        return pl.BlockSpec((None, TM, width), lambda b, t: (b, t + t0, col))

    def out_tok(width):
        return pl.BlockSpec((None, TM, width), lambda b, t: (b, t, 0))

    mod_spec = pl.BlockSpec((None, 6, d), lambda b, t: (jnp.where(t + t0 < nc, bsz, b), 0, 0))
    return pl.pallas_call(
        _stage_c_kernel,
        grid=(bsz, nt - t0),
        in_specs=[tok(d), tok(d), mod_spec,
                  tok(w, 1), tok(w), tok(w),
                  tok(w, 3), tok(w), tok(w),
                  tok(w, 4), tok(w), tok(w),
                  tok(w, 3), tok(w), tok(w),
                  _const_spec(gains.shape), _const_spec(w_merge.shape), _const_spec(b_merge.shape),
                  _const_spec(w_branch.shape), _const_spec(w_out.shape), _const_spec(ffn_g.shape),
                  _const_spec(w_route.shape), _const_spec(b_route.shape)],
        out_specs=[out_tok(d),
                   pl.BlockSpec((None, TM * SUBLANES, HEAD_W), lambda b, t: (b, t, 0)),
                   out_tok(HEAD_W),
                   pl.BlockSpec((None, None, SUBLANES, HEAD_W), lambda b, t: (b, t, 0, 0))],
        out_shape=[jax.ShapeDtypeStruct((bsz, n_out, d), F32),
                   jax.ShapeDtypeStruct((bsz, n_out * SUBLANES, HEAD_W), F32),
                   jax.ShapeDtypeStruct((bsz, n_out, HEAD_W), F32),
                   jax.ShapeDtypeStruct((bsz, nt - t0, SUBLANES, HEAD_W), F32)],
        compiler_params=_params(("parallel", "parallel")),
        name="stage_c",
    )(u, h, mod, z_rg, rg_f, rg_b, z_gla, gla_f, gla_b, z_hg, hg_f, hg_b, z_ml, ml_f, ml_b,
      gains, w_merge, b_merge, w_branch, w_out, ffn_g, w_route, b_route)


def _slots_to_smem(slots_vmem, slots_smem, sem):
    cp = pltpu.make_async_copy(slots_vmem, slots_smem, sem)
    cp.start()
    cp.wait()


def _row_copy(src, src_row, dst, dst_row, sem):
    def tile(ref, row):
        return ref.at[pl.ds(pl.multiple_of(row * SUBLANES, SUBLANES), SUBLANES), :]
    return pltpu.make_async_copy(tile(src, src_row), tile(dst, dst_row), sem)


def _to_tiles(x):
    return [x[:, k * HEAD_W:(k + 1) * HEAD_W] for k in range(SUBLANES)]


def _store_tiled(ref, x):
    rows = x.shape[0]
    for k, piece in enumerate(_to_tiles(x)):
        ref[pl.ds(k, rows, stride=SUBLANES), :] = piece


def _load_tiled(ref, rows):
    return jnp.concatenate([ref[pl.ds(k, rows, stride=SUBLANES), :] for k in range(SUBLANES)], axis=-1)


def _plan_kernel(route_ref, base_ref, tri_ref, slots_ref, token_hbm, slots_smem, token_smem,
                 sem_s, sem_t):
    step = pl.program_id(0)

    @pl.when(step == 0)
    def _():
        def clear(i, carry):
            token_smem[i] = 0
            return carry
        lax.fori_loop(0, token_smem.shape[0], clear, 0, unroll=8)

    route = route_ref[...]
    onehot, oh1, oh2 = _route_onehot(route)
    rank = jnp.dot(tri_ref[...], onehot.astype(BF16), preferred_element_type=F32)
    pos = base_ref[...] + rank
    slot_a = jnp.sum(jnp.where(oh1, pos, 0.0), axis=-1, keepdims=True)
    slot_b = jnp.sum(jnp.where(oh2, pos, 0.0), axis=-1, keepdims=True)
    col = lax.broadcasted_iota(jnp.int32, route.shape, 1)
    both = jnp.where(col == 0, slot_a, jnp.where(col == 1, slot_b, 0.0))
    slots_ref[...] = both.T[0:SUBLANES, :].astype(jnp.int32)
    _slots_to_smem(slots_ref, slots_smem, sem_s)

    row0 = step * TM

    def record(t, carry):
        for j in range(2):
            token_smem[slots_smem[j, t]] = row0 + t
        return carry

    lax.fori_loop(0, TM, record, 0, unroll=8)

    @pl.when(step == pl.num_programs(0) - 1)
    def _():
        cp = pltpu.make_async_copy(token_smem, token_hbm, sem_t)
        cp.start()
        cp.wait()


def _experts_kernel(blk_expert, token_of_slot, x_hbm, w1, w3, w2, o_ref, buf, sem):
    del blk_expert
    b = pl.program_id(0)
    n_blk = pl.num_programs(0)

    def gather(blk, half):
        for r in range(MOE_BLK):
            _row_copy(x_hbm, token_of_slot[blk * MOE_BLK + r], buf.at[half], r,
                      sem.at[half]).start(priority=r % 2)

    def gather_wait(half):
        def wait(r, carry):
            _row_copy(x_hbm, 0, buf.at[half], 0, sem.at[half]).wait()
            return carry
        lax.fori_loop(0, MOE_BLK, wait, 0, unroll=8)

    @pl.when(b == 0)
    def _():
        gather(0, 0)

    cur = lax.rem(b, 2)
    gather_wait(cur)
    gather(jnp.minimum(b + 1, n_blk - 1), 1 - cur)
    x = _load_tiled(buf.at[cur], MOE_BLK)
    h1 = _mm(x, w1[...])
    h3 = _mm(x, w3[...])
    _store_tiled(o_ref, _mm(_silu(h1) * h3, w2[...]))

    @pl.when(b == n_blk - 1)
    def _():
        gather_wait(1 - cur)


def _combine_kernel(slots_ref, route_ref, h_ref, mod_ref, fin_g, ys_hbm, o_ref,
                    slots_smem, buf, sem_s, sem_r, *, final):
    _slots_to_smem(slots_ref, slots_smem, sem_s)

    def start(t, carry):
        for j in range(2):
            _row_copy(ys_hbm, slots_smem[j, t], buf.at[j], t, sem_r).start(priority=j)
        return carry

    def wait(t, carry):
        for j in range(2):
            _row_copy(ys_hbm, 0, buf.at[j], 0, sem_r).wait()
        return carry

    lax.fori_loop(0, TM, start, 0, unroll=8)
    lax.fori_loop(0, TM, wait, 0, unroll=8)
    route = route_ref[...]
    y = (route[:, ROUTE_W:ROUTE_W + 1] * _load_tiled(buf.at[0], TM)
         + route[:, ROUTE_W + 1:ROUTE_W + 2] * _load_tiled(buf.at[1], TM))
    out = h_ref[...] + mod_ref[5:6, :] * y
    if final:
        out = _rmsnorm_rows(out, fin_g[...])
    o_ref[...] = out


def _moe_call(u2, route, cnt, h, mod, fin_g, w1, w3, w2, layer, nc, t0, final):
    bsz, n_out, d = h.shape
    de = w1.shape[-1]
    rows = bsz * n_out
    tiles = rows // TM
    tiles_per_row = n_out // TM
    n_blocks = (2 * rows + N_EXPERTS * (MOE_BLK - 1)) // MOE_BLK + 1

    counts = cnt[:, :, 0, :N_EXPERTS].reshape(tiles, N_EXPERTS)
    total = jnp.sum(counts, axis=0)
    padded = jnp.ceil(total / MOE_BLK) * MOE_BLK
    seg_end = jnp.cumsum(padded)
    base = (seg_end - padded)[None, :] + jnp.cumsum(counts, axis=0) - counts
    base = _pad_cols(base, HEAD_W).reshape(tiles, 1, HEAD_W)
    blk_start = jnp.arange(n_blocks, dtype=F32) * MOE_BLK
    blk_expert = jnp.minimum(jnp.sum(blk_start[:, None] >= seg_end[None, :], axis=1), N_EXPERTS - 1)
    tri = jnp.asarray(np.tril(np.ones((TM, TM), np.float32), -1), BF16)

    assert d == SUBLANES * HEAD_W
    n_slots = n_blocks * MOE_BLK
    slots, token_of_slot = pl.pallas_call(
        _plan_kernel,
        grid=(tiles,),
        in_specs=[pl.BlockSpec((TM, HEAD_W), lambda t: (t, 0)),
                  pl.BlockSpec((None, 1, HEAD_W), lambda t: (t, 0, 0)),
                  _const_spec((TM, TM))],
        out_specs=[pl.BlockSpec((SUBLANES, TM), lambda t: (t, 0)),
                   pl.BlockSpec(memory_space=pl.ANY)],
        out_shape=[jax.ShapeDtypeStruct((tiles * SUBLANES, TM), jnp.int32),
                   jax.ShapeDtypeStruct((n_slots,), jnp.int32)],
        scratch_shapes=[pltpu.SMEM((SUBLANES, TM), jnp.int32),
                        pltpu.SMEM((n_slots,), jnp.int32),
                        pltpu.SemaphoreType.DMA, pltpu.SemaphoreType.DMA],
        compiler_params=_params(("arbitrary",)),
        name="moe_plan",
    )(route.reshape(rows, HEAD_W), base, tri)

    ys = pl.pallas_call(
        _experts_kernel,
        grid_spec=pltpu.PrefetchScalarGridSpec(
            num_scalar_prefetch=2,
            grid=(n_blocks,),
            in_specs=[pl.BlockSpec(memory_space=pl.ANY),
                      pl.BlockSpec((None, None, d, de), lambda b, be, tk: (layer, be[b], 0, 0)),
                      pl.BlockSpec((None, None, d, de), lambda b, be, tk: (layer, be[b], 0, 0)),
                      pl.BlockSpec((None, None, de, d), lambda b, be, tk: (layer, be[b], 0, 0))],
            out_specs=pl.BlockSpec((MOE_BLK * SUBLANES, HEAD_W), lambda b, be, tk: (b, 0)),
            scratch_shapes=[pltpu.VMEM((2, MOE_BLK * SUBLANES, HEAD_W), F32),
                            pltpu.SemaphoreType.DMA((2,))]),
        out_shape=jax.ShapeDtypeStruct((n_slots * SUBLANES, HEAD_W), F32),
        compiler_params=_params(("arbitrary",)),
        name="moe_experts",
    )(blk_expert.astype(jnp.int32), token_of_slot, u2.reshape(rows * SUBLANES, HEAD_W), w1, w3, w2)

    def mod_index(t):
        b, pos = t // tiles_per_row, t % tiles_per_row
        return (jnp.where(pos + t0 < nc, bsz, b), 0, 0)

    out = pl.pallas_call(
        functools.partial(_combine_kernel, final=final),
        grid=(tiles,),
        in_specs=[pl.BlockSpec((SUBLANES, TM), lambda t: (t, 0)),
                  pl.BlockSpec((TM, HEAD_W), lambda t: (t, 0)),
                  pl.BlockSpec((TM, d), lambda t: (t, 0)),
                  pl.BlockSpec((None, 6, d), mod_index),
                  _const_spec((1, d)),
                  pl.BlockSpec(memory_space=pl.ANY)],
        out_specs=pl.BlockSpec((TM, d), lambda t: (t, 0)),
        out_shape=jax.ShapeDtypeStruct((rows, d), F32),
        scratch_shapes=[pltpu.SMEM((SUBLANES, TM), jnp.int32),
                        pltpu.VMEM((2, TM * SUBLANES, HEAD_W), F32),
                        pltpu.SemaphoreType.DMA, pltpu.SemaphoreType.DMA],
        compiler_params=_params(("arbitrary",)),
        name="moe_combine",
    )(slots, route.reshape(rows, HEAD_W), h.reshape(rows, d), mod, fin_g, ys)
    return out.reshape(bsz, n_out, d)


def _pad_heads(w, dk):
    lead = w.shape[:-1]
    w = w.reshape(lead + (N_HEADS, dk))
    w = jnp.pad(w, [(0, 0)] * len(lead) + [(0, 0), (0, HEAD_W - dk)])
    return w.reshape(lead + (N_HEADS * HEAD_W,))


def _pad_cols(w, width):
    return jnp.pad(w, [(0, 0)] * (w.ndim - 1) + [(0, width - w.shape[-1])])


def _split_w_in(w_in):
    bw = BRANCH_W
    sizes = [bw, bw, N_HEADS * GLA_DK, N_HEADS * GLA_DK, bw, bw, 2 * GLA_RANK,
             bw, bw, 2 * bw, bw, bw, bw, bw, bw, 16]
    offs = np.cumsum([0] + sizes)
    p = [w_in[:, offs[i]:offs[i + 1]] for i in range(len(sizes))]
    w_rg = jnp.concatenate([p[0], p[1]], axis=1)
    w_gla = jnp.concatenate([_pad_heads(p[2], GLA_DK), _pad_heads(p[3], GLA_DK), p[4], p[5],
                             _pad_cols(p[6], HEAD_W)], axis=1)
    w_hg = jnp.concatenate([p[7], p[8], p[9], p[10]], axis=1)
    gates = p[15].reshape(-1, 2, 2, N_HEADS)
    w_gi = _pad_cols(gates[:, :, 0].reshape(-1, 2 * N_HEADS), HEAD_W)
    w_gf = _pad_cols(gates[:, :, 1].reshape(-1, 2 * N_HEADS), HEAD_W)
    w_ml = jnp.concatenate([p[11], p[12], p[13], p[14], w_gi, w_gf], axis=1)
    return [w.astype(BF16) for w in (w_rg, w_gla, w_hg, w_ml)]


def _block_diag(w):
    k, n = w.shape[-3], w.shape[-1]
    eye = jnp.eye(k, dtype=w.dtype)
    full = jnp.einsum('...kij,kl->...kilj', w, eye)
    return full.reshape(w.shape[:-3] + (k * n, k * n))


def kernel(x, c, ctx, c_ctx, ada_w, ada_b, norm_mix_g, norm_ffn_g, w_in, rg_conv_w, rg_conv_b, rg_gate_w, rg_gate_b, rg_lambda, gla_w_lr, gla_b_lr, gla_norm_g, hgrn_lb_logits, hgrn_norm_g, ml_conv_w, ml_conv_b, ml_gate_b, ml_norm_g, w_branch, w_merge, b_merge, w_out, moe_w_group, moe_b_group, moe_w_expert, moe_b_expert, moe_w1, moe_w3, moe_w2, final_norm_g):
    bsz, seq, d = x.shape
    n_ctx = ctx.shape[1]
    depth = ada_w.shape[0]
    assert n_ctx % TM == 0 and seq % TM == 0 and d == 2 * BRANCH_W
    nc = n_ctx // TM

    h = jnp.concatenate([ctx, x], axis=1)
    cvec = jnp.zeros((SUBLANES, d), F32).at[:bsz].set(c).at[bsz].set(c_ctx)
    mod_all = _mod_call(cvec, ada_w, ada_b).reshape(depth, SUBLANES, 6, d)[:, :bsz + 1]

    lb_cum = jnp.cumsum(jax.nn.softmax(hgrn_lb_logits.astype(F32), axis=0), axis=0)
    hgrn_lb = lb_cum - lb_cum[:1]
    tables = [t for levels in SPLIT_LEVELS for t in _decay_tables(levels)]
    tri_sel = tables[0][:, :CHUNK, :]
    tri_mask = tables[1][:, 0]

    out = None
    for l in range(depth):
        last = l == depth - 1
        mod = mod_all[l]
        w_rg, w_gla, w_hg, w_ml = _split_w_in(w_in[l])
        z_rg, z_gla, z_hg, z_ml, u = _stage_a_call(h, mod, norm_mix_g[l], w_rg, w_gla, w_hg, w_ml, nc)

        rg_f, rg_b = _rg_call(z_rg, rg_conv_w[l], rg_conv_b[l].reshape(1, -1),
                              _block_diag(rg_gate_w[l]).astype(BF16),
                              rg_gate_b[l].reshape(4, -1), rg_lambda[l], nc)

        wlr = jnp.zeros((2, HEAD_W, N_HEADS * HEAD_W), F32)
        wlr_p = _pad_heads(gla_w_lr[l], GLA_DK)
        wlr = wlr.at[0, :GLA_RANK].set(wlr_p[0]).at[1, GLA_RANK:2 * GLA_RANK].set(wlr_p[1])
        gla_f, gla_b = _gla_call(z_gla, wlr.astype(BF16), _pad_heads(gla_b_lr[l], GLA_DK), tables, nc)

        lb = hgrn_lb[l]
        lbp = jnp.concatenate([lb, jnp.log(lb), jnp.log1p(-lb), jnp.zeros((2, lb.shape[-1]), F32)], axis=0)
        hg_f, hg_b = _hg_call(z_hg, lbp, tables, nc)

        gbias = _pad_cols(ml_gate_b[l].transpose(1, 0, 2).reshape(2, -1), HEAD_W)
        ml_f, ml_b = _ml_call(z_ml, ml_conv_w[l], ml_conv_b[l].reshape(1, -1), gbias, tri_sel, tri_mask, nc)

        gains = jnp.zeros((SUBLANES, HEAD_W), F32).at[0].set(gla_norm_g[l]).at[1].set(hgrn_norm_g[l]).at[2].set(ml_norm_g[l])
        w_route = _split_cols(_pad_cols(jnp.concatenate([moe_w_expert[l], moe_w_group[l]], axis=1), HEAD_W))
        b_route = _pad_cols(jnp.concatenate([moe_b_expert[l], moe_b_group[l]]).reshape(1, -1), HEAD_W)
        t0 = nc if last else 0
        h_mid, u2, route, cnt = _stage_c_call(
            u, h, mod, z_rg, rg_f, rg_b, z_gla, gla_f, gla_b, z_hg, hg_f, hg_b, z_ml, ml_f, ml_b,
            gains, w_merge[l].astype(BF16), b_merge[l].reshape(1, -1), w_branch[l].astype(BF16),
            w_out[l].astype(BF16), norm_ffn_g[l].reshape(1, -1), w_route, b_route, nc, t0)

        res = _moe_call(u2, route, cnt, h_mid, mod, final_norm_g.reshape(1, -1), moe_w1,
                        moe_w3, moe_w2, l, nc, t0, last)
        if last:
            out = res
        else:
            h = res
    return out
```

```python
import functools

import numpy as np
import jax
import jax.numpy as jnp
from jax import lax
from jax.experimental import pallas as pl
from jax.experimental.pallas import tpu as pltpu

F32 = jnp.float32
BF16 = jnp.bfloat16

EPS = 1e-6
TM = 256
CHUNK = 64
SUBLANES = 8
N_HEADS = 4
HEAD_W = 128
BRANCH_W = 512
CONV_W = 4
CONV_LEFT = 2
RG_C = 8.0
GLA_DK = 64
GLA_RANK = 16
GLA_GATE_NORM = 16.0
N_GROUPS = 4
EXPERTS_PER_GROUP = 4
N_EXPERTS = 16
MOE_BLK = 256
ROUTE_IDX = 16
ROUTE_W = 18
N_LEVELS = 6
MILD_DECAY = -40.0
VMEM_LIMIT = 56 * 1024 * 1024

W_RG = 2 * BRANCH_W
W_GLA = 4 * BRANCH_W + HEAD_W
W_HG = 5 * BRANCH_W
W_ML = 4 * BRANCH_W + 2 * HEAD_W
SPLIT_LEVELS = (0, 2, N_LEVELS)


def _mm(a, b):
    return jnp.dot(a.astype(BF16), b.astype(BF16), preferred_element_type=F32)


def _mm_nt(a, b):
    return lax.dot_general(a.astype(BF16), b.astype(BF16), (((1,), (1,)), ((), ())),
                           preferred_element_type=F32)


def _mm_tn(a, b):
    return lax.dot_general(a.astype(BF16), b.astype(BF16), (((0,), (0,)), ((), ())),
                           preferred_element_type=F32)


def _mm_sel(sel, x):
    x1 = x.astype(BF16)
    x2 = (x - x1.astype(F32)).astype(BF16)
    dot = functools.partial(jnp.dot, preferred_element_type=F32)
    return dot(sel, x1) + dot(sel, x2)


def _split_cols(w):
    hi = w.astype(BF16)
    return jnp.concatenate([hi, (w - hi.astype(F32)).astype(BF16)], axis=1)


def _mm_split(x, w_split):
    n = w_split.shape[1] // 2
    x1 = x.astype(BF16)
    x2 = (x - x1.astype(F32)).astype(BF16)
    p = jnp.dot(x1, w_split, preferred_element_type=F32)
    return p[:, :n] + p[:, n:] + jnp.dot(x2, w_split[:, :n], preferred_element_type=F32)


def _log_sigmoid(x):
    return jnp.minimum(x, 0.0) - jnp.log1p(jnp.exp(-jnp.abs(x)))


def _sigmoid(x):
    return 0.5 * jnp.tanh(0.5 * x) + 0.5


def _silu(x):
    return x * _sigmoid(x)


def _rmsnorm_rows(x, g):
    return x * lax.rsqrt(jnp.mean(x * x, axis=-1, keepdims=True) + EPS) * g


def _head_rmsnorm(o, g):
    parts = [_rmsnorm_rows(o[:, h * HEAD_W:(h + 1) * HEAD_W], g) for h in range(N_HEADS)]
    return jnp.concatenate(parts, axis=-1)


def _bwd_tile(s, nc, nt):
    return jnp.where(s < nc, nc - 1 - s, nt - 1 - (s - nc))


def _const_spec(shape):
    nd = len(shape)
    return pl.BlockSpec(shape, lambda *_: (0,) * nd)


def _params(sem):
    return pltpu.CompilerParams(dimension_semantics=sem, vmem_limit_bytes=VMEM_LIMIT)


def _mod_kernel(c_ref, w_ref, b_ref, o_ref):
    cv = _silu(c_ref[...])
    o_ref[...] = jnp.dot(cv, w_ref[...], precision=lax.Precision.HIGHEST,
                         preferred_element_type=F32) + b_ref[...]


def _mod_call(cvec, ada_w, ada_b):
    depth, d, six_d = ada_w.shape
    tn = 1024
    return pl.pallas_call(
        _mod_kernel,
        grid=(depth, six_d // tn),
        in_specs=[pl.BlockSpec((SUBLANES, d), lambda l, j: (0, 0)),
                  pl.BlockSpec((None, d, tn), lambda l, j: (l, 0, j)),
                  pl.BlockSpec((None, 1, tn), lambda l, j: (l, 0, j))],
        out_specs=pl.BlockSpec((None, SUBLANES, tn), lambda l, j: (l, 0, j)),
        out_shape=jax.ShapeDtypeStruct((depth, SUBLANES, six_d), F32),
        compiler_params=_params(("parallel", "parallel")),
        name="adaln_mod",
    )(cvec, ada_w, ada_b.reshape(depth, 1, six_d))


def _stage_a_kernel(h_ref, mod_ref, g_ref, w_rg, w_gla, w_hg, w_ml,
                    z_rg, z_gla, z_hg, z_ml, u_ref):
    x = h_ref[...]
    u = _rmsnorm_rows(x, g_ref[...]) * (1.0 + mod_ref[1:2, :]) + mod_ref[0:1, :]
    ub = u.astype(BF16)
    u_ref[...] = ub
    z_rg[...] = jnp.dot(ub, w_rg[...], preferred_element_type=F32)
    z_gla[...] = jnp.dot(ub, w_gla[...], preferred_element_type=F32)
    z_hg[...] = jnp.dot(ub, w_hg[...], preferred_element_type=F32)
    z_ml[...] = jnp.dot(ub, w_ml[...], preferred_element_type=F32)


def _stage_a_call(h, mod, norm_g, w_rg, w_gla, w_hg, w_ml, nc):
    bsz, n, d = h.shape
    tm = TM

    def tok(w):
        return pl.BlockSpec((None, tm, w), lambda b, t: (b, t, 0))

    def weight(w):
        return pl.BlockSpec((d, w), lambda b, t: (0, 0), pipeline_mode=pl.Buffered(1))

    return pl.pallas_call(
        _stage_a_kernel,
        grid=(bsz, n // tm),
        in_specs=[tok(d),
                  pl.BlockSpec((None, 6, d), lambda b, t: (jnp.where(t < nc, bsz, b), 0, 0)),
                  _const_spec((1, d)),
                  weight(W_RG), weight(W_GLA), weight(W_HG), weight(W_ML)],
        out_specs=[tok(W_RG), tok(W_GLA), tok(W_HG), tok(W_ML), tok(d)],
        out_shape=[jax.ShapeDtypeStruct((bsz, n, W_RG), F32),
                   jax.ShapeDtypeStruct((bsz, n, W_GLA), F32),
                   jax.ShapeDtypeStruct((bsz, n, W_HG), F32),
                   jax.ShapeDtypeStruct((bsz, n, W_ML), F32),
                   jax.ShapeDtypeStruct((bsz, n, d), BF16)],
        compiler_params=_params(("parallel", "parallel")),
        name="stage_a",
    )(h, mod, norm_g.reshape(1, d), w_rg, w_gla, w_hg, w_ml)


def _conv_tile(x_ref, prev_ref, next_ref, pad_ref, cw_ref, cb_ref, tile, nc, nt):
    prev_ok = jnp.logical_and(tile != 0, tile != nc)
    next_ok = jnp.logical_and(tile != nc - 1, tile != nt - 1)
    pad_ref[0:SUBLANES, :] = jnp.where(prev_ok, prev_ref[...], 0.0)
    pad_ref[SUBLANES:SUBLANES + TM, :] = x_ref[...]
    pad_ref[SUBLANES + TM:2 * SUBLANES + TM, :] = jnp.where(next_ok, next_ref[...], 0.0)
    acc = cb_ref[...]
    for j in range(CONV_W):
        off = SUBLANES - CONV_LEFT + j
        acc = acc + cw_ref[j:j + 1, :] * pad_ref[off:off + TM, :]
    return acc


def _halo_specs(width, col_block, tile_fn, n):
    rows = TM // SUBLANES
    last = n // SUBLANES - 1
    cur = pl.BlockSpec((None, TM, width), lambda b, s: (b, tile_fn(s), col_block))
    prev = pl.BlockSpec((None, SUBLANES, width),
                        lambda b, s: (b, jnp.maximum(tile_fn(s) * rows - 1, 0), col_block))
    nxt = pl.BlockSpec((None, SUBLANES, width),
                       lambda b, s: (b, jnp.minimum((tile_fn(s) + 1) * rows, last), col_block))
    return [cur, prev, nxt]


def _scan_rows8(a, b, reverse):
    n = a.shape[0]
    pos = jnp.bitwise_and(lax.broadcasted_iota(jnp.int32, a.shape, 0), SUBLANES - 1)
    k = 1
    while k < SUBLANES:
        if reverse:
            a_s, b_s, ok = pltpu.roll(a, n - k, 0), pltpu.roll(b, n - k, 0), pos < SUBLANES - k
        else:
            a_s, b_s, ok = pltpu.roll(a, k, 0), pltpu.roll(b, k, 0), pos >= k
        b = b + a * jnp.where(ok, b_s, 0.0)
        a = a * jnp.where(ok, a_s, 1.0)
        k *= 2
    return a, b


def _scan_tile(a, b, carry, o_ref, reverse):
    a8, b8 = _scan_rows8(a, b, reverse)
    groups = a.shape[0] // SUBLANES
    order = range(groups - 1, -1, -1) if reverse else range(groups)
    for r in order:
        rows = slice(r * SUBLANES, (r + 1) * SUBLANES)
        h = b8[rows] + a8[rows] * carry
        o_ref[rows, :] = h
        carry = h[0:1, :] if reverse else h[SUBLANES - 1:SUBLANES, :]
    return carry


def _rg_kernel(xf, xf_p, xf_n, xb, xb_p, xb_n, cw, cb, gw, gb, lam,
               hf_ref, hb_ref, pad_ref, carry_ref, *, nc, nt):
    s = pl.program_id(1)

    @pl.when(s == 0)
    def _():
        carry_ref[...] = jnp.zeros_like(carry_ref)

    dirs = ((xf, xf_p, xf_n, hf_ref, s, False),
            (xb, xb_p, xb_n, hb_ref, _bwd_tile(s, nc, nt), True))
    for d, (x_ref, p_ref, n_ref, o_ref, tile, reverse) in enumerate(dirs):
        x = _conv_tile(x_ref, p_ref, n_ref, pad_ref, cw, cb, tile, nc, nt)
        r = _sigmoid(_mm(x, gw[d, 0]) + gb[2 * d:2 * d + 1, :])
        i = _sigmoid(_mm(x, gw[d, 1]) + gb[2 * d + 1:2 * d + 2, :])
        lam_d = lam[d:d + 1, :]
        softplus = jnp.maximum(-lam_d, 0.0) + jnp.log1p(jnp.exp(-jnp.abs(lam_d)))
        log_a = -RG_C * r * softplus
        a = jnp.exp(log_a)
        t = jnp.tanh(log_a)
        bt = jnp.sqrt(-2.0 * t / (1.0 - t)) * (i * x)
        carry_ref[d:d + 1, :] = _scan_tile(a, bt, carry_ref[d:d + 1, :], o_ref, reverse)


def _rg_call(z_rg, cw, cb, gw, gb, lam, nc):
    bsz, n, _ = z_rg.shape
    nt = n // TM
    w = BRANCH_W
    fwd = lambda s: s
    bwd = lambda s: _bwd_tile(s, nc, nt)
    out = lambda fn: pl.BlockSpec((None, TM, w), lambda b, s: (b, fn(s), 0))
    return pl.pallas_call(
        functools.partial(_rg_kernel, nc=nc, nt=nt),
        grid=(bsz, nt),
        in_specs=_halo_specs(w, 0, fwd, n) + _halo_specs(w, 0, bwd, n) + [
            _const_spec((CONV_W, w)), _const_spec((1, w)),
            _const_spec((2, 2, w, w)), _const_spec((4, w)), _const_spec((2, w))],
        out_specs=[out(fwd), out(bwd)],
        out_shape=[jax.ShapeDtypeStruct((bsz, n, w), F32)] * 2,
        scratch_shapes=[pltpu.VMEM((TM + 2 * SUBLANES, w), F32), pltpu.VMEM((2, w), F32)],
        compiler_params=_params(("parallel", "arbitrary")),
        name="rglru",
    )(z_rg, z_rg, z_rg, z_rg, z_rg, z_rg, cw, cb, gw, gb, lam)


def _decay_tables(levels):
    n = CHUNK
    blk = n >> levels
    sel = np.zeros((levels + 3, n, n), np.float32)
    masks = np.zeros((levels + 1, n, n), np.float32)
    for lvl in range(levels):
        half = n >> (lvl + 1)
        for r in range(n):
            start = (r // (2 * half)) * 2 * half
            ref = start + half - 1
            if r - start >= half:
                sel[lvl, r, ref + 1:r + 1] = 1.0
                masks[lvl, r, start:start + half] = 1.0
            else:
                sel[lvl, r, r + 1:ref + 1] = 1.0
    sel[levels] = np.tril(np.ones((n, n), np.float32))
    sel[levels + 1] = 1.0 - sel[levels]
    for r in range(n):
        start = (r // blk) * blk
        sel[levels + 2, r, r + 1:start + blk] = 1.0
        masks[levels, r, start:r + 1] = 1.0
    if levels == 0 or levels == N_LEVELS:
        sel = sel[:levels + 2]
    sel_b = sel[:, ::-1, ::-1].reshape(-1, n)
    masks_b = masks[:, ::-1, ::-1]
    return (jnp.asarray(np.stack([sel.reshape(-1, n), sel_b]), BF16),
            jnp.asarray(np.stack([masks, masks_b]), F32))


def _ref_rows(levels, reverse):
    def runs(block, offset):
        return [(s + offset, block) for s in range(0, CHUNK, block)]

    out = [runs(CHUNK >> lvl, (CHUNK >> (lvl + 1)) - 1) for lvl in range(levels)]
    out.append(runs(CHUNK, CHUNK - 1))
    out.append(runs(CHUNK >> levels, (CHUNK >> levels) - 1))
    if reverse:
        out = [[(CHUNK - 1 - r, n) for r, n in reversed(segs)] for segs in out]
    return out


def _rows_bcast(b, segs):
    return jnp.concatenate([jnp.broadcast_to(b[r:r + 1, :], (n, b.shape[-1])) for r, n in segs], axis=0)


def _gla_chunk(q, k, v, g, sel, masks, st_ref, d, reverse, levels):
    if levels == N_LEVELS:
        x = jnp.exp(_mm_sel(sel, g))
        x_lvl = [x[lvl * CHUNK:(lvl + 1) * CHUNK] for lvl in range(levels)]
        x_cum = x[levels * CHUNK:(levels + 1) * CHUNK]
        x_rest = x[(levels + 1) * CHUNK:(levels + 2) * CHUNK]
        xk_blk = xq_blk = None
    else:
        b = _mm_sel(sel[levels * CHUNK:(levels + 1) * CHUNK], g)
        refs = _ref_rows(levels, reverse)
        x_cum = jnp.exp(b)
        x_lvl = [jnp.exp(-jnp.abs(b - _rows_bcast(b, segs))) for segs in refs[:levels]]
        x_rest = jnp.exp(-jnp.abs(b - _rows_bcast(b, refs[levels])))
        if levels == 0:
            xk_blk, xq_blk = x_rest, jnp.exp(jnp.abs(b - _rows_bcast(b, refs[levels])))
        else:
            d_blk = jnp.abs(b - _rows_bcast(b, refs[levels + 1]))
            xk_blk, xq_blk = jnp.exp(-d_blk), jnp.exp(d_blk)
    outs = []
    for h in range(N_HEADS):
        cols = slice(h * HEAD_W, (h + 1) * HEAD_W)
        qh, kh, vh = q[:, cols], k[:, cols], v[:, cols]
        kx = kh * x_rest[:, cols]
        if xk_blk is None:
            sc = _mm_nt(qh, kh)
        elif levels == 0:
            sc = _mm_nt(qh * xq_blk[:, cols], kx)
        else:
            sc = _mm_nt(qh * xq_blk[:, cols], kh * xk_blk[:, cols])
        sc = jnp.where(masks[levels] > 0.0, sc, 0.0)
        for lvl in range(levels):
            xl = x_lvl[lvl][:, cols]
            sc = sc + jnp.where(masks[lvl] > 0.0, _mm_nt(qh * xl, kh * xl), 0.0)
        st = st_ref[d, h]
        outs.append(_mm(sc, vh) + _mm_nt(qh * x_cum[:, cols], st))
        x_end = x_cum[0:1, cols] if reverse else x_cum[CHUNK - 1:CHUNK, cols]
        st_ref[d, h] = st * x_end + _mm_tn(vh, kx)
    return jnp.concatenate(outs, axis=-1)


def _gla_step(sides, tabs, st_ref):
    n_chunks = TM // CHUNK

    def min_block_sum(block):
        mins = [jnp.min(jnp.sum(side[3].reshape(TM // block, block, side[3].shape[-1]), axis=1))
                for side in sides]
        return functools.reduce(jnp.minimum, mins)

    conds = []
    taken = None
    for levels in SPLIT_LEVELS[:-1]:
        ok = min_block_sum(CHUNK >> levels) > MILD_DECAY
        conds.append(ok if taken is None else jnp.logical_and(ok, jnp.logical_not(taken)))
        taken = ok if taken is None else jnp.logical_or(taken, ok)
    conds.append(jnp.logical_not(taken))

    for levels, cond, (sel_ref, mask_ref) in zip(SPLIT_LEVELS, conds, tabs):
        @pl.when(cond)
        def _(levels=levels, sel_ref=sel_ref, mask_ref=mask_ref):
            for c in range(n_chunks):
                for i, (q, k, v, g, o_ref, reverse) in enumerate(sides):
                    cc = n_chunks - 1 - c if reverse else c
                    rows = slice(cc * CHUNK, (cc + 1) * CHUNK)
                    o_ref[rows, :] = _gla_chunk(q[rows], k[rows], v[rows], g[rows],
                                                sel_ref[int(reverse)], mask_ref[int(reverse)],
                                                st_ref, i, reverse, levels)


def _pairs(refs):
    return tuple(zip(refs[0::2], refs[1::2]))


def _gla_kernel(zf, zb, wlr, blr, *rest):
    tabs, (of_ref, ob_ref, st_ref) = _pairs(rest[:-3]), rest[-3:]
    w = BRANCH_W

    @pl.when(pl.program_id(0) == 0)
    def _():
        st_ref[...] = jnp.zeros_like(st_ref)

    sides = []
    for b in range(zf.shape[0]):
        for d, (z_ref, o_ref) in enumerate(((zf, of_ref), (zb, ob_ref))):
            pre = _mm(z_ref[b, :, 4 * w:4 * w + HEAD_W], wlr[d]) + blr[d:d + 1, :]
            g = _log_sigmoid(pre) * (1.0 / GLA_GATE_NORM)
            q = z_ref[b, :, 0:w] * (GLA_DK ** -0.5)
            sides.append((q, z_ref[b, :, w:2 * w], z_ref[b, :, 2 * w:3 * w], g, o_ref.at[b], d == 1))
    _gla_step(sides, tabs, st_ref)


def _hg_kernel(zf, zb, lbp, *rest):
    tabs, (of_ref, ob_ref, st_ref) = _pairs(rest[:-3]), rest[-3:]
    w = BRANCH_W

    @pl.when(pl.program_id(0) == 0)
    def _():
        st_ref[...] = jnp.zeros_like(st_ref)

    sides = []
    for b in range(zf.shape[0]):
        for d, (z_ref, o_ref) in enumerate(((zf, of_ref), (zb, ob_ref))):
            f = z_ref[b, :, (2 + d) * w:(3 + d) * w]
            lb = lbp[d:d + 1, :]
            log_lb = lbp[2 + d:3 + d, :]
            log_1m = lbp[4 + d:5 + d, :]
            c = log_1m + _log_sigmoid(f)
            g = jnp.maximum(log_lb, c) + jnp.log1p(jnp.exp(-jnp.abs(log_lb - c)))
            k = (1.0 - lb) * _sigmoid(-f)
            q = _silu(z_ref[b, :, 0:w]) * (HEAD_W ** -0.5)
            sides.append((q, k, z_ref[b, :, w:2 * w], g, o_ref.at[b], d == 1))
    _gla_step(sides, tabs, st_ref)


def _mixer_specs(widths_cols, tile_fn):
    return [pl.BlockSpec((None, TM, w), lambda b, s, c=c: (b, tile_fn(s), c)) for w, c in widths_cols]


def _all_rows_spec(bsz, width, tile_fn):
    return pl.BlockSpec((bsz, TM, width), lambda s: (0, tile_fn(s), 0))


def _gla_call(z_gla, wlr, blr, tables, nc):
    bsz, n, _ = z_gla.shape
    nt = n // TM
    w = BRANCH_W
    fwd = lambda s: s
    bwd = lambda s: _bwd_tile(s, nc, nt)
    return pl.pallas_call(
        _gla_kernel,
        grid=(nt,),
        in_specs=[_all_rows_spec(bsz, W_GLA, fwd), _all_rows_spec(bsz, W_GLA, bwd),
                  _const_spec(wlr.shape), _const_spec(blr.shape)] + [_const_spec(t.shape) for t in tables],
        out_specs=[_all_rows_spec(bsz, w, fwd), _all_rows_spec(bsz, w, bwd)],
        out_shape=[jax.ShapeDtypeStruct((bsz, n, w), F32)] * 2,
        scratch_shapes=[pltpu.VMEM((2 * bsz, N_HEADS, HEAD_W, HEAD_W), F32)],
        compiler_params=_params(("arbitrary",)),
        name="gla",
    )(z_gla, z_gla, wlr, blr, *tables)


def _hg_call(z_hg, lbp, tables, nc):
    bsz, n, _ = z_hg.shape
    nt = n // TM
    w = BRANCH_W
    fwd = lambda s: s
    bwd = lambda s: _bwd_tile(s, nc, nt)
    return pl.pallas_call(
        _hg_kernel,
        grid=(nt,),
        in_specs=[_all_rows_spec(bsz, 4 * w, fwd), _all_rows_spec(bsz, 4 * w, bwd),
                  _const_spec(lbp.shape)] + [_const_spec(t.shape) for t in tables],
        out_specs=[_all_rows_spec(bsz, w, fwd), _all_rows_spec(bsz, w, bwd)],
        out_shape=[jax.ShapeDtypeStruct((bsz, n, w), F32)] * 2,
        scratch_shapes=[pltpu.VMEM((2 * bsz, N_HEADS, HEAD_W, HEAD_W), F32)],
        compiler_params=_params(("arbitrary",)),
        name="hgrn2",
    )(z_hg, z_hg, lbp, *tables)


def _cummax_rows(a, reverse):
    n = a.shape[0]
    row = lax.broadcasted_iota(jnp.int32, a.shape, 0)
    k = 1
    while k < n:
        if reverse:
            a_s, ok = pltpu.roll(a, n - k, 0), row < n - k
        else:
            a_s, ok = pltpu.roll(a, k, 0), row >= k
        a = jnp.maximum(a, jnp.where(ok, a_s, -jnp.inf))
        k *= 2
    return a


def _ml_chunk(q, k, v, gi, log_f, tri_sel, tri_mask, ct_ref, m_ref, slot, d, reverse):
    b = _mm_sel(tri_sel, log_f)
    a = gi - b
    m_prev = m_ref[slot:slot + 1, :]
    m_rel = jnp.maximum(_cummax_rows(a, reverse), m_prev)
    w_inter = jnp.exp(m_prev - m_rel)
    exp_neg_m = jnp.exp(-(b + m_rel))
    last = 0 if reverse else CHUNK - 1
    b_end = b[last:last + 1, :]
    m_new = b_end + m_rel[last:last + 1, :]
    w_end = jnp.exp(b_end - b + gi - m_new)
    decay = jnp.exp(b_end + m_prev - m_new)
    m_ref[slot:slot + 1, :] = m_new
    a_t = a.T
    ones = jnp.ones((CHUNK, HEAD_W), BF16)
    outs = []
    for h in range(N_HEADS):
        cols = slice(h * HEAD_W, (h + 1) * HEAD_W)
        qh, kh, vh = q[:, cols], k[:, cols], v[:, cols]
        c = d * N_HEADS + h
        log_w = jnp.where(tri_mask > 0.0, a_t[c:c + 1, :] - m_rel[:, c:c + 1], -jnp.inf)
        s = _mm_nt(qh, kh) * jnp.exp(log_w)
        v_ext = jnp.concatenate([vh.astype(BF16), ones], axis=-1)
        st = slot * N_HEADS + h
        ct = ct_ref[st]
        tot = _mm(s, v_ext) + w_inter[:, c:c + 1] * _mm_nt(qh, ct)
        num, den = tot[:, :HEAD_W], tot[:, HEAD_W:]
        outs.append(num / jnp.maximum(jnp.abs(den), exp_neg_m[:, c:c + 1]))
        ct_ref[st] = decay[:, c:c + 1] * ct + _mm_tn(v_ext, kh * w_end[:, c:c + 1])
    return jnp.concatenate(outs, axis=-1)


def _ml_kernel(zf, zf_p, zf_n, zb, zb_p, zb_n, cw, cb, gbias,
               tri_sel, tri_mask, of_ref, ob_ref, pad_ref, ct_ref, m_ref, *, nc, nt):
    s = pl.program_id(0)
    w = BRANCH_W
    n_chunks = TM // CHUNK

    @pl.when(s == 0)
    def _():
        ct_ref[...] = jnp.zeros_like(ct_ref)
        m_ref[...] = jnp.zeros_like(m_ref)

    dirs = ((zf, zf_p, zf_n, of_ref, s), (zb, zb_p, zb_n, ob_ref, _bwd_tile(s, nc, nt)))
    sides = []
    for b in range(zf.shape[0]):
        for d, (z_ref, p_ref, nx_ref, o_ref, tile) in enumerate(dirs):
            slot = 2 * b + d
            qk = _silu(_conv_tile(z_ref.at[b, :, 0:2 * w], p_ref.at[b], nx_ref.at[b],
                                  pad_ref.at[slot], cw, cb, tile, nc, nt))
            gi = z_ref[b, :, 4 * w:4 * w + HEAD_W] + gbias[0:1, :]
            log_f = _log_sigmoid(z_ref[b, :, 4 * w + HEAD_W:4 * w + 2 * HEAD_W] + gbias[1:2, :])
            sides.append((qk[:, :w] * (HEAD_W ** -0.5), qk[:, w:], z_ref[b, :, 2 * w:3 * w],
                          gi, log_f, o_ref.at[b], slot, d))
    for c in range(n_chunks):
        for q, k, v, gi, log_f, o_ref, slot, d in sides:
            cc = n_chunks - 1 - c if d == 1 else c
            rows = slice(cc * CHUNK, (cc + 1) * CHUNK)
            o_ref[rows, :] = _ml_chunk(q[rows], k[rows], v[rows], gi[rows], log_f[rows],
                                       tri_sel[d], tri_mask[d], ct_ref, m_ref, slot, d, d == 1)


def _ml_call(z_ml, cw, cb, gbias, tri_sel, tri_mask, nc):
    bsz, n, _ = z_ml.shape
    nt = n // TM
    w = BRANCH_W
    fwd = lambda s: s
    bwd = lambda s: _bwd_tile(s, nc, nt)
    rows8 = TM // SUBLANES
    last8 = n // SUBLANES - 1

    def side(fn):
        return [_all_rows_spec(bsz, W_ML, fn),
                pl.BlockSpec((bsz, SUBLANES, 2 * w), lambda s: (0, jnp.maximum(fn(s) * rows8 - 1, 0), 0)),
                pl.BlockSpec((bsz, SUBLANES, 2 * w), lambda s: (0, jnp.minimum((fn(s) + 1) * rows8, last8), 0))]

    return pl.pallas_call(
        functools.partial(_ml_kernel, nc=nc, nt=nt),
        grid=(nt,),
        in_specs=side(fwd) + side(bwd) + [
            _const_spec((CONV_W, 2 * w)), _const_spec((1, 2 * w)), _const_spec((2, HEAD_W)),
            _const_spec(tri_sel.shape), _const_spec(tri_mask.shape)],
        out_specs=[_all_rows_spec(bsz, w, fwd), _all_rows_spec(bsz, w, bwd)],
        out_shape=[jax.ShapeDtypeStruct((bsz, n, w), F32)] * 2,
        scratch_shapes=[pltpu.VMEM((2 * bsz, TM + 2 * SUBLANES, 2 * w), F32),
                        pltpu.VMEM((2 * bsz * N_HEADS, 2 * HEAD_W, HEAD_W), F32),
                        pltpu.VMEM((2 * bsz, HEAD_W), F32)],
        compiler_params=_params(("arbitrary",)),
        name="mlstm",
    )(*([z_ml] * 6), cw, cb, gbias, tri_sel, tri_mask)


def _gelu_tanh(x):
    return 0.5 * x * (1.0 + jnp.tanh(0.7978845608028654 * (x + 0.044715 * (x * x * x))))


def _route(logits):
    col = lax.broadcasted_iota(jnp.int32, logits.shape, 1)
    colf = col.astype(F32)
    is_g = jnp.logical_and(col >= N_EXPERTS, col < N_EXPERTS + N_GROUPS)
    is_e = col < N_EXPERTS
    neg = -jnp.inf
    big = 1e9
    gl = jnp.where(is_g, logits, neg)
    g_max = jnp.max(gl, axis=-1, keepdims=True)
    grp = jnp.min(jnp.where(gl == g_max, colf, big), axis=-1, keepdims=True) - N_EXPERTS
    p_grp = 1.0 / jnp.sum(jnp.exp(gl - g_max), axis=-1, keepdims=True)
    col_grp = lax.shift_right_logical(col, EXPERTS_PER_GROUP.bit_length() - 1)
    in_grp = jnp.logical_and(is_e, col_grp.astype(F32) == grp)
    e1 = jnp.where(in_grp, logits, neg)
    top1 = jnp.max(e1, axis=-1, keepdims=True)
    idx1 = jnp.min(jnp.where(e1 == top1, colf, big), axis=-1, keepdims=True)
    e2 = jnp.where(colf == idx1, neg, e1)
    top2 = jnp.max(e2, axis=-1, keepdims=True)
    idx2 = jnp.min(jnp.where(e2 == top2, colf, big), axis=-1, keepdims=True)
    t = jnp.exp(top2 - top1)
    w1 = p_grp / (1.0 + t)
    w2 = p_grp * (t / (1.0 + t))
    out = jnp.where(col == ROUTE_IDX, idx1, jnp.where(col == ROUTE_IDX + 1, idx2, 0.0))
    return jnp.where(col == ROUTE_W, w1, jnp.where(col == ROUTE_W + 1, w2, out))


def _route_onehot(route):
    colf = lax.broadcasted_iota(jnp.int32, route.shape, 1).astype(F32)
    oh1 = colf == route[:, ROUTE_IDX:ROUTE_IDX + 1]
    oh2 = colf == route[:, ROUTE_IDX + 1:ROUTE_IDX + 2]
    return jnp.where(jnp.logical_or(oh1, oh2), 1.0, 0.0), oh1, oh2


def _stage_c_kernel(u_ref, h_ref, mod_ref, rg_y, rg_f, rg_b, gla_g, gla_f, gla_b, hg_g, hg_f, hg_b,
                    ml_o, ml_f, ml_b, gains, w_merge, b_merge, w_branch, w_out, ffn_g, w_route, b_route,
                    h_out, u2_out, wt_out, cnt_out):
    ys = (
        _gelu_tanh(rg_y[...]) * (rg_f[...] + rg_b[...]),
        _head_rmsnorm(gla_f[...] + gla_b[...], gains[0:1, :]) * _silu(gla_g[...]),
        _head_rmsnorm(hg_f[...] + hg_b[...], gains[1:2, :]) * _silu(hg_g[...]),
        _sigmoid(ml_o[...]) * _head_rmsnorm(ml_f[...] + ml_b[...], gains[2:3, :]),
    )
    u = u_ref[...]
    d = u.shape[-1]
    merged = None
    for kk, y in enumerate(ys):
        gate = _sigmoid(jnp.dot(u, w_merge[:, kk * d:(kk + 1) * d], preferred_element_type=F32)
                              + b_merge[:, kk * d:(kk + 1) * d])
        term = gate * _mm(y, w_branch[kk])
        merged = term if merged is None else merged + term
    mix = _mm(merged, w_out[...])
    h_new = h_ref[...] + mod_ref[2:3, :] * mix
    h_out[...] = h_new
    u2 = _rmsnorm_rows(h_new, ffn_g[...]) * (1.0 + mod_ref[4:5, :]) + mod_ref[3:4, :]
    _store_tiled(u2_out, u2)
    logits = _mm_split(u2, w_route[...]) + b_route[...]
    route = _route(logits)
    wt_out[...] = route
    cnt = jnp.sum(_route_onehot(route)[0], axis=0, keepdims=True)
    cnt_out[...] = jnp.broadcast_to(cnt, cnt_out.shape)


def _stage_c_call(u, h, mod, z_rg, rg_f, rg_b, z_gla, gla_f, gla_b, z_hg, hg_f, hg_b, z_ml, ml_f, ml_b,
                  gains, w_merge, b_merge, w_branch, w_out, ffn_g, w_route, b_route, nc, t0):
    bsz, n, d = h.shape
    nt = n // TM
    w = BRANCH_W

    n_out = n - t0 * TM

    def tok(width, col=0):
        return pl.BlockSpec((None, TM, width), lambda b, t: (b, t + t0, col))

    def out_tok(width):
        return pl.BlockSpec((None, TM, width), lambda b, t: (b, t, 0))

    mod_spec = pl.BlockSpec((None, 6, d), lambda b, t: (jnp.where(t + t0 < nc, bsz, b), 0, 0))
    return pl.pallas_call(
        _stage_c_kernel,
        grid=(bsz, nt - t0),
        in_specs=[tok(d), tok(d), mod_spec,
                  tok(w, 1), tok(w), tok(w),
                  tok(w, 3), tok(w), tok(w),
                  tok(w, 4), tok(w), tok(w),
                  tok(w, 3), tok(w), tok(w),
                  _const_spec(gains.shape), _const_spec(w_merge.shape), _const_spec(b_merge.shape),
                  _const_spec(w_branch.shape), _const_spec(w_out.shape), _const_spec(ffn_g.shape),
                  _const_spec(w_route.shape), _const_spec(b_route.shape)],
        out_specs=[out_tok(d),
                   pl.BlockSpec((None, TM * SUBLANES, HEAD_W), lambda b, t: (b, t, 0)),
                   out_tok(HEAD_W),
                   pl.BlockSpec((None, None, SUBLANES, HEAD_W), lambda b, t: (b, t, 0, 0))],
        out_shape=[jax.ShapeDtypeStruct((bsz, n_out, d), F32),
                   jax.ShapeDtypeStruct((bsz, n_out * SUBLANES, HEAD_W), F32),
                   jax.ShapeDtypeStruct((bsz, n_out, HEAD_W), F32),
                   jax.ShapeDtypeStruct((bsz, nt - t0, SUBLANES, HEAD_W), F32)],
        compiler_params=_params(("parallel", "parallel")),
        name="stage_c",
    )(u, h, mod, z_rg, rg_f, rg_b, z_gla, gla_f, gla_b, z_hg, hg_f, hg_b, z_ml, ml_f, ml_b,
      gains, w_merge, b_merge, w_branch, w_out, ffn_g, w_route, b_route)


def _slots_to_smem(slots_vmem, slots_smem, sem):
    cp = pltpu.make_async_copy(slots_vmem, slots_smem, sem)
    cp.start()
    cp.wait()


def _row_copy(src, src_row, dst, dst_row, sem):
    def tile(ref, row):
        return ref.at[pl.ds(pl.multiple_of(row * SUBLANES, SUBLANES), SUBLANES), :]
    return pltpu.make_async_copy(tile(src, src_row), tile(dst, dst_row), sem)


def _to_tiles(x):
    return [x[:, k * HEAD_W:(k + 1) * HEAD_W] for k in range(SUBLANES)]


def _store_tiled(ref, x):
    rows = x.shape[0]
    for k, piece in enumerate(_to_tiles(x)):
        ref[pl.ds(k, rows, stride=SUBLANES), :] = piece


def _load_tiled(ref, rows):
    return jnp.concatenate([ref[pl.ds(k, rows, stride=SUBLANES), :] for k in range(SUBLANES)], axis=-1)


def _dispatch_kernel(seg_ref, route_ref, base_ref, tri_ref, x_ref, xs_hbm, slots_ref,
                     slots_smem, zero_buf, sem_s, sem_r, sem_z):
    @pl.when(pl.program_id(0) == 0)
    def _():
        zero_buf[...] = jnp.zeros_like(zero_buf)

        def fill(e):
            start_row = pl.multiple_of((seg_ref[e] - MOE_BLK) * SUBLANES, SUBLANES)
            return pltpu.make_async_copy(
                zero_buf, xs_hbm.at[pl.ds(start_row, MOE_BLK * SUBLANES), :], sem_z)

        for e in range(N_EXPERTS):
            @pl.when(seg_ref[N_EXPERTS + e] > 0)
            def _(e=e):
                fill(e).start()
        for e in range(N_EXPERTS):
            @pl.when(seg_ref[N_EXPERTS + e] > 0)
            def _(e=e):
                fill(e).wait()

        blk_rows = MOE_BLK * SUBLANES
        used = seg_ref[N_EXPERTS - 1] // MOE_BLK

        def tail(b):
            return pltpu.make_async_copy(
                zero_buf, xs_hbm.at[pl.ds(pl.multiple_of(b * blk_rows, blk_rows), blk_rows), :], sem_z)

        def tail_start(b, carry):
            tail(b).start()
            return carry

        def tail_wait(b, carry):
            tail(b).wait()
            return carry

        n_blk = xs_hbm.shape[0] // blk_rows
        lax.fori_loop(used, n_blk, tail_start, 0)
        lax.fori_loop(used, n_blk, tail_wait, 0)

    route = route_ref[...]
    onehot, oh1, oh2 = _route_onehot(route)
    rank = jnp.dot(tri_ref[...], onehot.astype(BF16), preferred_element_type=F32)
    pos = base_ref[...] + rank
    slot_a = jnp.sum(jnp.where(oh1, pos, 0.0), axis=-1, keepdims=True)
    slot_b = jnp.sum(jnp.where(oh2, pos, 0.0), axis=-1, keepdims=True)
    col = lax.broadcasted_iota(jnp.int32, route.shape, 1)
    both = jnp.where(col == 0, slot_a, jnp.where(col == 1, slot_b, 0.0))
    slots_ref[...] = both.T[0:SUBLANES, :].astype(jnp.int32)
    _slots_to_smem(slots_ref, slots_smem, sem_s)

    def start(t, carry):
        for j in range(2):
            _row_copy(x_ref, t, xs_hbm, slots_smem[j, t], sem_r).start(priority=j)
        return carry

    def wait(t, carry):
        for j in range(2):
            _row_copy(x_ref, 0, xs_hbm, 0, sem_r).wait()
        return carry

    lax.fori_loop(0, TM, start, 0, unroll=8)
    lax.fori_loop(0, TM, wait, 0, unroll=8)


def _experts_kernel(blk_expert, n_used, x_ref, w1, w3, w2, o_ref):
    del blk_expert

    @pl.when(pl.program_id(0) < n_used[0])
    def _():
        x = _load_tiled(x_ref, MOE_BLK)
        h1 = _mm(x, w1[...])
        h3 = _mm(x, w3[...])
        _store_tiled(o_ref, _mm(_silu(h1) * h3, w2[...]))

    @pl.when(pl.program_id(0) >= n_used[0])
    def _():
        o_ref[...] = jnp.zeros_like(o_ref)


def _combine_kernel(slots_smem, route_ref, h_ref, mod_ref, fin_g, ys_hbm, o_ref, buf, sem_r, *, final):
    def start(t, carry):
        for j in range(2):
            _row_copy(ys_hbm, slots_smem[j, t], buf.at[j], t, sem_r).start(priority=j)
        return carry

    def wait(t, carry):
        for j in range(2):
            _row_copy(ys_hbm, 0, buf.at[j], 0, sem_r).wait()
        return carry

    lax.fori_loop(0, TM, start, 0, unroll=8)
    lax.fori_loop(0, TM, wait, 0, unroll=8)
    route = route_ref[...]
    y = (route[:, ROUTE_W:ROUTE_W + 1] * _load_tiled(buf.at[0], TM)
         + route[:, ROUTE_W + 1:ROUTE_W + 2] * _load_tiled(buf.at[1], TM))
    out = h_ref[...] + mod_ref[5:6, :] * y
    if final:
        out = _rmsnorm_rows(out, fin_g[...])
    o_ref[...] = out


def _moe_call(u2, route, cnt, h, mod, fin_g, w1, w3, w2, layer, nc, t0, final):
    bsz, n_out, d = h.shape
    de = w1.shape[-1]
    rows = bsz * n_out
    tiles = rows // TM
    tiles_per_row = n_out // TM
    n_blocks = (2 * rows + N_EXPERTS * (MOE_BLK - 1)) // MOE_BLK + 1

    counts = cnt[:, :, 0, :N_EXPERTS].reshape(tiles, N_EXPERTS)
    total = jnp.sum(counts, axis=0)
    padded = jnp.ceil(total / MOE_BLK) * MOE_BLK
    seg_end = jnp.cumsum(padded)
    base = (seg_end - padded)[None, :] + jnp.cumsum(counts, axis=0) - counts
    base = _pad_cols(base, HEAD_W).reshape(tiles, 1, HEAD_W)
    blk_start = jnp.arange(n_blocks, dtype=F32) * MOE_BLK
    blk_expert = jnp.minimum(jnp.sum(blk_start[:, None] >= seg_end[None, :], axis=1), N_EXPERTS - 1)
    n_used = (seg_end[-1] / MOE_BLK).astype(jnp.int32).reshape(1)
    tri = jnp.asarray(np.tril(np.ones((TM, TM), np.float32), -1), BF16)

    assert d == SUBLANES * HEAD_W
    slot_rows = n_blocks * MOE_BLK * SUBLANES
    seg = jnp.concatenate([seg_end, padded]).astype(jnp.int32)
    xs, slots = pl.pallas_call(
        _dispatch_kernel,
        grid_spec=pltpu.PrefetchScalarGridSpec(
            num_scalar_prefetch=1,
            grid=(tiles,),
            in_specs=[pl.BlockSpec((TM, HEAD_W), lambda t, sg: (t, 0)),
                      pl.BlockSpec((None, 1, HEAD_W), lambda t, sg: (t, 0, 0)),
                      pl.BlockSpec((TM, TM), lambda t, sg: (0, 0)),
                      pl.BlockSpec((TM * SUBLANES, HEAD_W), lambda t, sg: (t, 0))],
            out_specs=[pl.BlockSpec(memory_space=pl.ANY),
                       pl.BlockSpec((SUBLANES, TM), lambda t, sg: (t, 0))],
            scratch_shapes=[pltpu.SMEM((SUBLANES, TM), jnp.int32),
                            pltpu.VMEM((MOE_BLK * SUBLANES, HEAD_W), F32),
                            pltpu.SemaphoreType.DMA, pltpu.SemaphoreType.DMA,
                            pltpu.SemaphoreType.DMA]),
        out_shape=[jax.ShapeDtypeStruct((slot_rows, HEAD_W), F32),
                   jax.ShapeDtypeStruct((tiles * SUBLANES, TM), jnp.int32)],
        compiler_params=_params(("arbitrary",)),
        name="moe_dispatch",
    )(seg, route.reshape(rows, HEAD_W), base, tri, u2.reshape(rows * SUBLANES, HEAD_W))

    blk_spec = pl.BlockSpec((MOE_BLK * SUBLANES, HEAD_W), lambda b, be, nu: (b, 0))
    used_blk_spec = pl.BlockSpec((MOE_BLK * SUBLANES, HEAD_W),
                                 lambda b, be, nu: (jnp.minimum(b, nu[0] - 1), 0))
    ys = pl.pallas_call(
        _experts_kernel,
        grid_spec=pltpu.PrefetchScalarGridSpec(
            num_scalar_prefetch=2,
            grid=(n_blocks,),
            in_specs=[used_blk_spec,
                      pl.BlockSpec((None, None, d, de), lambda b, be, nu: (layer, be[b], 0, 0)),
                      pl.BlockSpec((None, None, d, de), lambda b, be, nu: (layer, be[b], 0, 0)),
                      pl.BlockSpec((None, None, de, d), lambda b, be, nu: (layer, be[b], 0, 0))],
            out_specs=blk_spec),
        out_shape=jax.ShapeDtypeStruct((slot_rows, HEAD_W), F32),
        compiler_params=_params(("arbitrary",)),
        name="moe_experts",
    )(blk_expert.astype(jnp.int32), n_used, xs, w1, w3, w2)

    def mod_index(t):
        b, pos = t // tiles_per_row, t % tiles_per_row
        return (jnp.where(pos + t0 < nc, bsz, b), 0, 0)

    out = pl.pallas_call(
        functools.partial(_combine_kernel, final=final),
        grid=(tiles,),
        in_specs=[pl.BlockSpec((None, SUBLANES, TM), lambda t: (t, 0, 0), memory_space=pltpu.SMEM),
                  pl.BlockSpec((TM, HEAD_W), lambda t: (t, 0)),
                  pl.BlockSpec((TM, d), lambda t: (t, 0)),
                  pl.BlockSpec((None, 6, d), mod_index),
                  _const_spec((1, d)),
                  pl.BlockSpec(memory_space=pl.ANY)],
        out_specs=pl.BlockSpec((TM, d), lambda t: (t, 0)),
        out_shape=jax.ShapeDtypeStruct((rows, d), F32),
        scratch_shapes=[pltpu.VMEM((2, TM * SUBLANES, HEAD_W), F32),
                        pltpu.SemaphoreType.DMA],
        compiler_params=_params(("arbitrary",)),
        name="moe_combine",
    )(slots.reshape(tiles, SUBLANES, TM), route.reshape(rows, HEAD_W), h.reshape(rows, d), mod, fin_g, ys)
    return out.reshape(bsz, n_out, d)


def _pad_heads(w, dk):
    lead = w.shape[:-1]
    w = w.reshape(lead + (N_HEADS, dk))
    w = jnp.pad(w, [(0, 0)] * len(lead) + [(0, 0), (0, HEAD_W - dk)])
    return w.reshape(lead + (N_HEADS * HEAD_W,))


def _pad_cols(w, width):
    return jnp.pad(w, [(0, 0)] * (w.ndim - 1) + [(0, width - w.shape[-1])])


def _split_w_in(w_in):
    bw = BRANCH_W
    sizes = [bw, bw, N_HEADS * GLA_DK, N_HEADS * GLA_DK, bw, bw, 2 * GLA_RANK,
             bw, bw, 2 * bw, bw, bw, bw, bw, bw, 16]
    offs = np.cumsum([0] + sizes)
    p = [w_in[:, offs[i]:offs[i + 1]] for i in range(len(sizes))]
    w_rg = jnp.concatenate([p[0], p[1]], axis=1)
    w_gla = jnp.concatenate([_pad_heads(p[2], GLA_DK), _pad_heads(p[3], GLA_DK), p[4], p[5],
                             _pad_cols(p[6], HEAD_W)], axis=1)
    w_hg = jnp.concatenate([p[7], p[8], p[9], p[10]], axis=1)
    gates = p[15].reshape(-1, 2, 2, N_HEADS)
    w_gi = _pad_cols(gates[:, :, 0].reshape(-1, 2 * N_HEADS), HEAD_W)
    w_gf = _pad_cols(gates[:, :, 1].reshape(-1, 2 * N_HEADS), HEAD_W)
    w_ml = jnp.concatenate([p[11], p[12], p[13], p[14], w_gi, w_gf], axis=1)
    return [w.astype(BF16) for w in (w_rg, w_gla, w_hg, w_ml)]


def _block_diag(w):
    k, n = w.shape[-3], w.shape[-1]
    eye = jnp.eye(k, dtype=w.dtype)
    full = jnp.einsum('...kij,kl->...kilj', w, eye)
    return full.reshape(w.shape[:-3] + (k * n, k * n))


def kernel(x, c, ctx, c_ctx, ada_w, ada_b, norm_mix_g, norm_ffn_g, w_in, rg_conv_w, rg_conv_b, rg_gate_w, rg_gate_b, rg_lambda, gla_w_lr, gla_b_lr, gla_norm_g, hgrn_lb_logits, hgrn_norm_g, ml_conv_w, ml_conv_b, ml_gate_b, ml_norm_g, w_branch, w_merge, b_merge, w_out, moe_w_group, moe_b_group, moe_w_expert, moe_b_expert, moe_w1, moe_w3, moe_w2, final_norm_g):
    bsz, seq, d = x.shape
    n_ctx = ctx.shape[1]
    depth = ada_w.shape[0]
    assert n_ctx % TM == 0 and seq % TM == 0 and d == 2 * BRANCH_W
    nc = n_ctx // TM

    h = jnp.concatenate([ctx, x], axis=1)
    cvec = jnp.zeros((SUBLANES, d), F32).at[:bsz].set(c).at[bsz].set(c_ctx)
    mod_all = _mod_call(cvec, ada_w, ada_b).reshape(depth, SUBLANES, 6, d)[:, :bsz + 1]

    lb_cum = jnp.cumsum(jax.nn.softmax(hgrn_lb_logits.astype(F32), axis=0), axis=0)
    hgrn_lb = lb_cum - lb_cum[:1]
    tables = [t for levels in SPLIT_LEVELS for t in _decay_tables(levels)]
    tri_sel = tables[0][:, :CHUNK, :]
    tri_mask = tables[1][:, 0]

    out = None
    for l in range(depth):
        last = l == depth - 1
        mod = mod_all[l]
        w_rg, w_gla, w_hg, w_ml = _split_w_in(w_in[l])
        z_rg, z_gla, z_hg, z_ml, u = _stage_a_call(h, mod, norm_mix_g[l], w_rg, w_gla, w_hg, w_ml, nc)

        rg_f, rg_b = _rg_call(z_rg, rg_conv_w[l], rg_conv_b[l].reshape(1, -1),
                              _block_diag(rg_gate_w[l]).astype(BF16),
                              rg_gate_b[l].reshape(4, -1), rg_lambda[l], nc)

        wlr = jnp.zeros((2, HEAD_W, N_HEADS * HEAD_W), F32)
        wlr_p = _pad_heads(gla_w_lr[l], GLA_DK)
        wlr = wlr.at[0, :GLA_RANK].set(wlr_p[0]).at[1, GLA_RANK:2 * GLA_RANK].set(wlr_p[1])
        gla_f, gla_b = _gla_call(z_gla, wlr.astype(BF16), _pad_heads(gla_b_lr[l], GLA_DK), tables, nc)

        lb = hgrn_lb[l]
        lbp = jnp.concatenate([lb, jnp.log(lb), jnp.log1p(-lb), jnp.zeros((2, lb.shape[-1]), F32)], axis=0)
        hg_f, hg_b = _hg_call(z_hg, lbp, tables, nc)

        gbias = _pad_cols(ml_gate_b[l].transpose(1, 0, 2).reshape(2, -1), HEAD_W)
        ml_f, ml_b = _ml_call(z_ml, ml_conv_w[l], ml_conv_b[l].reshape(1, -1), gbias, tri_sel, tri_mask, nc)

        gains = jnp.zeros((SUBLANES, HEAD_W), F32).at[0].set(gla_norm_g[l]).at[1].set(hgrn_norm_g[l]).at[2].set(ml_norm_g[l])
        w_route = _split_cols(_pad_cols(jnp.concatenate([moe_w_expert[l], moe_w_group[l]], axis=1), HEAD_W))
        b_route = _pad_cols(jnp.concatenate([moe_b_expert[l], moe_b_group[l]]).reshape(1, -1), HEAD_W)
        t0 = nc if last else 0
        h_mid, u2, route, cnt = _stage_c_call(
            u, h, mod, z_rg, rg_f, rg_b, z_gla, gla_f, gla_b, z_hg, hg_f, hg_b, z_ml, ml_f, ml_b,
            gains, w_merge[l].astype(BF16), b_merge[l].reshape(1, -1), w_branch[l].astype(BF16),
            w_out[l].astype(BF16), norm_ffn_g[l].reshape(1, -1), w_route, b_route, nc, t0)

        res = _moe_call(u2, route, cnt, h_mid, mod, final_norm_g.reshape(1, -1), moe_w1,
                        moe_w3, moe_w2, l, nc, t0, last)
        if last:
            out = res
        else:
            h = res
    return out
```

```python
import functools

import numpy as np
import jax
import jax.numpy as jnp
from jax import lax
from jax.experimental import pallas as pl
from jax.experimental.pallas import tpu as pltpu

F32 = jnp.float32
BF16 = jnp.bfloat16

EPS = 1e-6
TM = 256
CHUNK = 64
SUBLANES = 8
N_HEADS = 4
HEAD_W = 128
BRANCH_W = 512
CONV_W = 4
CONV_LEFT = 2
RG_C = 8.0
GLA_DK = 64
GLA_RANK = 16
GLA_GATE_NORM = 16.0
N_GROUPS = 4
EXPERTS_PER_GROUP = 4
N_EXPERTS = 16
MOE_BLK = 256
ROUTE_IDX = 16
ROUTE_W = 18
N_LEVELS = 6
MILD_DECAY = -40.0
VMEM_LIMIT = 56 * 1024 * 1024

W_RG = 2 * BRANCH_W
W_GLA = 4 * BRANCH_W + HEAD_W
W_HG = 5 * BRANCH_W
W_ML = 4 * BRANCH_W + 2 * HEAD_W
SPLIT_LEVELS = (0, 2, N_LEVELS)


def _mm(a, b):
    return jnp.dot(a.astype(BF16), b.astype(BF16), preferred_element_type=F32)


def _mm_nt(a, b):
    return lax.dot_general(a.astype(BF16), b.astype(BF16), (((1,), (1,)), ((), ())),
                           preferred_element_type=F32)


def _mm_tn(a, b):
    return lax.dot_general(a.astype(BF16), b.astype(BF16), (((0,), (0,)), ((), ())),
                           preferred_element_type=F32)


def _mm_sel(sel, x):
    x1 = x.astype(BF16)
    x2 = (x - x1.astype(F32)).astype(BF16)
    dot = functools.partial(jnp.dot, preferred_element_type=F32)
    return dot(sel, x1) + dot(sel, x2)


def _split_cols(w):
    hi = w.astype(BF16)
    return jnp.concatenate([hi, (w - hi.astype(F32)).astype(BF16)], axis=1)


def _mm_split(x, w_split):
    n = w_split.shape[1] // 2
    x1 = x.astype(BF16)
    x2 = (x - x1.astype(F32)).astype(BF16)
    p = jnp.dot(x1, w_split, preferred_element_type=F32)
    return p[:, :n] + p[:, n:] + jnp.dot(x2, w_split[:, :n], preferred_element_type=F32)


def _softplus_neg_abs(x):
    return jnp.log(1.0 + jnp.exp(-jnp.abs(x)))


def _log_sigmoid(x):
    return jnp.minimum(x, 0.0) - _softplus_neg_abs(x)


def _sigmoid(x):
    return 0.5 * jnp.tanh(0.5 * x) + 0.5


def _silu(x):
    return x * _sigmoid(x)


def _rmsnorm_rows(x, g):
    return x * lax.rsqrt(jnp.mean(x * x, axis=-1, keepdims=True) + EPS) * g


def _head_rmsnorm(o, g):
    parts = [_rmsnorm_rows(o[:, h * HEAD_W:(h + 1) * HEAD_W], g) for h in range(N_HEADS)]
    return jnp.concatenate(parts, axis=-1)


def _bwd_tile(s, nc, nt):
    return jnp.where(s < nc, nc - 1 - s, nt - 1 - (s - nc))


def _const_spec(shape):
    nd = len(shape)
    return pl.BlockSpec(shape, lambda *_: (0,) * nd)


def _params(sem):
    return pltpu.CompilerParams(dimension_semantics=sem, vmem_limit_bytes=VMEM_LIMIT)


def _mod_kernel(c_ref, w_ref, b_ref, o_ref):
    cv = _silu(c_ref[...])
    o_ref[...] = jnp.dot(cv, w_ref[...], precision=lax.Precision.HIGHEST,
                         preferred_element_type=F32) + b_ref[...]


def _mod_call(cvec, ada_w, ada_b):
    depth, d, six_d = ada_w.shape
    tn = 1024
    return pl.pallas_call(
        _mod_kernel,
        grid=(depth, six_d // tn),
        in_specs=[pl.BlockSpec((SUBLANES, d), lambda l, j: (0, 0)),
                  pl.BlockSpec((None, d, tn), lambda l, j: (l, 0, j)),
                  pl.BlockSpec((None, 1, tn), lambda l, j: (l, 0, j))],
        out_specs=pl.BlockSpec((None, SUBLANES, tn), lambda l, j: (l, 0, j)),
        out_shape=jax.ShapeDtypeStruct((depth, SUBLANES, six_d), F32),
        compiler_params=_params(("parallel", "parallel")),
        name="adaln_mod",
    )(cvec, ada_w, ada_b.reshape(depth, 1, six_d))


def _stage_a_kernel(h_ref, mod_ref, g_ref, w_rg, w_gla, w_hg, w_ml,
                    z_rg, z_gla, z_hg, z_ml, u_ref):
    x = h_ref[...]
    u = _rmsnorm_rows(x, g_ref[...]) * (1.0 + mod_ref[1:2, :]) + mod_ref[0:1, :]
    ub = u.astype(BF16)
    u_ref[...] = ub
    z_rg[...] = jnp.dot(ub, w_rg[...], preferred_element_type=F32)
    z_gla[...] = jnp.dot(ub, w_gla[...], preferred_element_type=F32)
    z_hg[...] = jnp.dot(ub, w_hg[...], preferred_element_type=F32)
    z_ml[...] = jnp.dot(ub, w_ml[...], preferred_element_type=F32)


def _stage_a_call(h, mod, norm_g, w_rg, w_gla, w_hg, w_ml, nc):
    bsz, n, d = h.shape
    tm = TM

    def tok(w):
        return pl.BlockSpec((None, tm, w), lambda b, t: (b, t, 0))

    def weight(w):
        return pl.BlockSpec((d, w), lambda b, t: (0, 0), pipeline_mode=pl.Buffered(1))

    return pl.pallas_call(
        _stage_a_kernel,
        grid=(bsz, n // tm),
        in_specs=[tok(d),
                  pl.BlockSpec((None, 6, d), lambda b, t: (jnp.where(t < nc, bsz, b), 0, 0)),
                  _const_spec((1, d)),
                  weight(W_RG), weight(W_GLA), weight(W_HG), weight(W_ML)],
        out_specs=[tok(W_RG), tok(W_GLA), tok(W_HG), tok(W_ML), tok(d)],
        out_shape=[jax.ShapeDtypeStruct((bsz, n, W_RG), F32),
                   jax.ShapeDtypeStruct((bsz, n, W_GLA), F32),
                   jax.ShapeDtypeStruct((bsz, n, W_HG), F32),
                   jax.ShapeDtypeStruct((bsz, n, W_ML), F32),
                   jax.ShapeDtypeStruct((bsz, n, d), BF16)],
        compiler_params=_params(("parallel", "parallel")),
        name="stage_a",
    )(h, mod, norm_g.reshape(1, d), w_rg, w_gla, w_hg, w_ml)


def _conv_tile(x_ref, prev_ref, next_ref, pad_ref, cw_ref, cb_ref, tile, nc, nt):
    prev_ok = jnp.logical_and(tile != 0, tile != nc)
    next_ok = jnp.logical_and(tile != nc - 1, tile != nt - 1)
    pad_ref[0:SUBLANES, :] = jnp.where(prev_ok, prev_ref[...], 0.0)
    pad_ref[SUBLANES:SUBLANES + TM, :] = x_ref[...]
    pad_ref[SUBLANES + TM:2 * SUBLANES + TM, :] = jnp.where(next_ok, next_ref[...], 0.0)
    acc = cb_ref[...]
    for j in range(CONV_W):
        off = SUBLANES - CONV_LEFT + j
        acc = acc + cw_ref[j:j + 1, :] * pad_ref[off:off + TM, :]
    return acc


def _halo_specs(width, col_block, tile_fn, n):
    rows = TM // SUBLANES
    last = n // SUBLANES - 1
    cur = pl.BlockSpec((None, TM, width), lambda b, s: (b, tile_fn(s), col_block))
    prev = pl.BlockSpec((None, SUBLANES, width),
                        lambda b, s: (b, jnp.maximum(tile_fn(s) * rows - 1, 0), col_block))
    nxt = pl.BlockSpec((None, SUBLANES, width),
                       lambda b, s: (b, jnp.minimum((tile_fn(s) + 1) * rows, last), col_block))
    return [cur, prev, nxt]


def _scan_rows8(a, b, reverse):
    n = a.shape[0]
    pos = jnp.bitwise_and(lax.broadcasted_iota(jnp.int32, a.shape, 0), SUBLANES - 1)
    k = 1
    while k < SUBLANES:
        if reverse:
            a_s, b_s, ok = pltpu.roll(a, n - k, 0), pltpu.roll(b, n - k, 0), pos < SUBLANES - k
        else:
            a_s, b_s, ok = pltpu.roll(a, k, 0), pltpu.roll(b, k, 0), pos >= k
        b = b + a * jnp.where(ok, b_s, 0.0)
        a = a * jnp.where(ok, a_s, 1.0)
        k *= 2
    return a, b


def _scan_tile(a, b, carry, o_ref, reverse):
    a8, b8 = _scan_rows8(a, b, reverse)
    groups = a.shape[0] // SUBLANES
    order = range(groups - 1, -1, -1) if reverse else range(groups)
    for r in order:
        rows = slice(r * SUBLANES, (r + 1) * SUBLANES)
        h = b8[rows] + a8[rows] * carry
        o_ref[rows, :] = h
        carry = h[0:1, :] if reverse else h[SUBLANES - 1:SUBLANES, :]
    return carry


def _rg_kernel(xf, xf_p, xf_n, xb, xb_p, xb_n, cw, cb, gw, gb, lam,
               hf_ref, hb_ref, pad_ref, carry_ref, *, nc, nt):
    s = pl.program_id(1)

    @pl.when(s == 0)
    def _():
        carry_ref[...] = jnp.zeros_like(carry_ref)

    dirs = ((xf, xf_p, xf_n, hf_ref, s, False),
            (xb, xb_p, xb_n, hb_ref, _bwd_tile(s, nc, nt), True))
    for d, (x_ref, p_ref, n_ref, o_ref, tile, reverse) in enumerate(dirs):
        x = _conv_tile(x_ref, p_ref, n_ref, pad_ref, cw, cb, tile, nc, nt)
        r = _sigmoid(_mm(x, gw[d, 0]) + gb[2 * d:2 * d + 1, :])
        i = _sigmoid(_mm(x, gw[d, 1]) + gb[2 * d + 1:2 * d + 2, :])
        lam_d = lam[d:d + 1, :]
        softplus = jnp.maximum(-lam_d, 0.0) + jnp.log1p(jnp.exp(-jnp.abs(lam_d)))
        log_a = -RG_C * r * softplus
        a = jnp.exp(log_a)
        t = jnp.tanh(log_a)
        bt = jnp.sqrt(-2.0 * t / (1.0 - t)) * (i * x)
        carry_ref[d:d + 1, :] = _scan_tile(a, bt, carry_ref[d:d + 1, :], o_ref, reverse)


def _rg_call(z_rg, cw, cb, gw, gb, lam, nc):
    bsz, n, _ = z_rg.shape
    nt = n // TM
    w = BRANCH_W
    fwd = lambda s: s
    bwd = lambda s: _bwd_tile(s, nc, nt)
    out = lambda fn: pl.BlockSpec((None, TM, w), lambda b, s: (b, fn(s), 0))
    return pl.pallas_call(
        functools.partial(_rg_kernel, nc=nc, nt=nt),
        grid=(bsz, nt),
        in_specs=_halo_specs(w, 0, fwd, n) + _halo_specs(w, 0, bwd, n) + [
            _const_spec((CONV_W, w)), _const_spec((1, w)),
            _const_spec((2, 2, w, w)), _const_spec((4, w)), _const_spec((2, w))],
        out_specs=[out(fwd), out(bwd)],
        out_shape=[jax.ShapeDtypeStruct((bsz, n, w), F32)] * 2,
        scratch_shapes=[pltpu.VMEM((TM + 2 * SUBLANES, w), F32), pltpu.VMEM((2, w), F32)],
        compiler_params=_params(("parallel", "arbitrary")),
        name="rglru",
    )(z_rg, z_rg, z_rg, z_rg, z_rg, z_rg, cw, cb, gw, gb, lam)


def _decay_tables(levels):
    n = CHUNK
    blk = n >> levels
    sel = np.zeros((levels + 3, n, n), np.float32)
    masks = np.zeros((levels + 1, n, n), np.float32)
    for lvl in range(levels):
        half = n >> (lvl + 1)
        for r in range(n):
            start = (r // (2 * half)) * 2 * half
            ref = start + half - 1
            if r - start >= half:
                sel[lvl, r, ref + 1:r + 1] = 1.0
                masks[lvl, r, start:start + half] = 1.0
            else:
                sel[lvl, r, r + 1:ref + 1] = 1.0
    sel[levels] = np.tril(np.ones((n, n), np.float32))
    sel[levels + 1] = 1.0 - sel[levels]
    for r in range(n):
        start = (r // blk) * blk
        sel[levels + 2, r, r + 1:start + blk] = 1.0
        masks[levels, r, start:r + 1] = 1.0
    if levels == 0 or levels == N_LEVELS:
        sel = sel[:levels + 2]
    sel_b = sel[:, ::-1, ::-1].reshape(-1, n)
    masks_b = masks[:, ::-1, ::-1]
    return (jnp.asarray(np.stack([sel.reshape(-1, n), sel_b]), BF16),
            jnp.asarray(np.stack([masks, masks_b]), F32))


def _ref_rows(levels, reverse):
    def runs(block, offset):
        return [(s + offset, block) for s in range(0, CHUNK, block)]

    out = [runs(CHUNK >> lvl, (CHUNK >> (lvl + 1)) - 1) for lvl in range(levels)]
    out.append(runs(CHUNK, CHUNK - 1))
    out.append(runs(CHUNK >> levels, (CHUNK >> levels) - 1))
    if reverse:
        out = [[(CHUNK - 1 - r, n) for r, n in reversed(segs)] for segs in out]
    return out


def _rows_bcast(b, segs):
    return jnp.concatenate([jnp.broadcast_to(b[r:r + 1, :], (n, b.shape[-1])) for r, n in segs], axis=0)


def _gla_chunk(q, k, v, g, sel, masks, st_ref, d, reverse, levels):
    if levels == N_LEVELS:
        x = jnp.exp(_mm_sel(sel, g))
        x_lvl = [x[lvl * CHUNK:(lvl + 1) * CHUNK] for lvl in range(levels)]
        x_cum = x[levels * CHUNK:(levels + 1) * CHUNK]
        x_rest = x[(levels + 1) * CHUNK:(levels + 2) * CHUNK]
        xk_blk = xq_blk = None
    else:
        b = _mm_sel(sel[levels * CHUNK:(levels + 1) * CHUNK], g)
        refs = _ref_rows(levels, reverse)
        x_cum = jnp.exp(b)
        x_lvl = [jnp.exp(-jnp.abs(b - _rows_bcast(b, segs))) for segs in refs[:levels]]
        x_rest = jnp.exp(-jnp.abs(b - _rows_bcast(b, refs[levels])))
        if levels == 0:
            xk_blk, xq_blk = x_rest, jnp.exp(jnp.abs(b - _rows_bcast(b, refs[levels])))
        else:
            d_blk = jnp.abs(b - _rows_bcast(b, refs[levels + 1]))
            xk_blk, xq_blk = jnp.exp(-d_blk), jnp.exp(d_blk)
    outs = []
    for h in range(N_HEADS):
        cols = slice(h * HEAD_W, (h + 1) * HEAD_W)
        qh, kh, vh = q[:, cols], k[:, cols], v[:, cols]
        kx = kh * x_rest[:, cols]
        if xk_blk is None:
            sc = _mm_nt(qh, kh)
        elif levels == 0:
            sc = _mm_nt(qh * xq_blk[:, cols], kx)
        else:
            sc = _mm_nt(qh * xq_blk[:, cols], kh * xk_blk[:, cols])
        sc = jnp.where(masks[levels] > 0.0, sc, 0.0)
        for lvl in range(levels):
            xl = x_lvl[lvl][:, cols]
            sc = sc + jnp.where(masks[lvl] > 0.0, _mm_nt(qh * xl, kh * xl), 0.0)
        st = st_ref[d, h]
        outs.append(_mm(sc, vh) + _mm_nt(qh * x_cum[:, cols], st))
        x_end = x_cum[0:1, cols] if reverse else x_cum[CHUNK - 1:CHUNK, cols]
        st_ref[d, h] = st * x_end + _mm_tn(vh, kx)
    return jnp.concatenate(outs, axis=-1)


def _gla_step(sides, tabs, st_ref):
    n_chunks = TM // CHUNK

    def min_block_sum(block):
        mins = [jnp.min(jnp.sum(side[3].reshape(TM // block, block, side[3].shape[-1]), axis=1))
                for side in sides]
        return functools.reduce(jnp.minimum, mins)

    conds = []
    taken = None
    for levels in SPLIT_LEVELS[:-1]:
        ok = min_block_sum(CHUNK >> levels) > MILD_DECAY
        conds.append(ok if taken is None else jnp.logical_and(ok, jnp.logical_not(taken)))
        taken = ok if taken is None else jnp.logical_or(taken, ok)
    conds.append(jnp.logical_not(taken))

    for levels, cond, (sel_ref, mask_ref) in zip(SPLIT_LEVELS, conds, tabs):
        @pl.when(cond)
        def _(levels=levels, sel_ref=sel_ref, mask_ref=mask_ref):
            for c in range(n_chunks):
                for i, (q, k, v, g, o_ref, reverse) in enumerate(sides):
                    cc = n_chunks - 1 - c if reverse else c
                    rows = slice(cc * CHUNK, (cc + 1) * CHUNK)
                    o_ref[rows, :] = _gla_chunk(q[rows], k[rows], v[rows], g[rows],
                                                sel_ref[int(reverse)], mask_ref[int(reverse)],
                                                st_ref, i, reverse, levels)


def _pairs(refs):
    return tuple(zip(refs[0::2], refs[1::2]))


def _gla_kernel(zf, zb, wlr, blr, *rest):
    tabs, (of_ref, ob_ref, st_ref) = _pairs(rest[:-3]), rest[-3:]
    w = BRANCH_W

    @pl.when(pl.program_id(0) == 0)
    def _():
        st_ref[...] = jnp.zeros_like(st_ref)

    sides = []
    for b in range(zf.shape[0]):
        for d, (z_ref, o_ref) in enumerate(((zf, of_ref), (zb, ob_ref))):
            pre = _mm(z_ref[b, :, 4 * w:4 * w + HEAD_W], wlr[d]) + blr[d:d + 1, :]
            g = _log_sigmoid(pre) * (1.0 / GLA_GATE_NORM)
            q = z_ref[b, :, 0:w] * (GLA_DK ** -0.5)
            sides.append((q, z_ref[b, :, w:2 * w], z_ref[b, :, 2 * w:3 * w], g, o_ref.at[b], d == 1))
    _gla_step(sides, tabs, st_ref)


def _hg_kernel(zf, zb, lbp, *rest):
    tabs, (of_ref, ob_ref, st_ref) = _pairs(rest[:-3]), rest[-3:]
    w = BRANCH_W

    @pl.when(pl.program_id(0) == 0)
    def _():
        st_ref[...] = jnp.zeros_like(st_ref)

    sides = []
    for b in range(zf.shape[0]):
        for d, (z_ref, o_ref) in enumerate(((zf, of_ref), (zb, ob_ref))):
            f = z_ref[b, :, (2 + d) * w:(3 + d) * w]
            lb = lbp[d:d + 1, :]
            log_lb = lbp[2 + d:3 + d, :]
            log_1m = lbp[4 + d:5 + d, :]
            c = log_1m + _log_sigmoid(f)
            g = jnp.maximum(log_lb, c) + _softplus_neg_abs(log_lb - c)
            k = (1.0 - lb) * _sigmoid(-f)
            q = _silu(z_ref[b, :, 0:w]) * (HEAD_W ** -0.5)
            sides.append((q, k, z_ref[b, :, w:2 * w], g, o_ref.at[b], d == 1))
    _gla_step(sides, tabs, st_ref)


def _mixer_specs(widths_cols, tile_fn):
    return [pl.BlockSpec((None, TM, w), lambda b, s, c=c: (b, tile_fn(s), c)) for w, c in widths_cols]


def _all_rows_spec(bsz, width, tile_fn):
    return pl.BlockSpec((bsz, TM, width), lambda s: (0, tile_fn(s), 0))


def _gla_call(z_gla, wlr, blr, tables, nc):
    bsz, n, _ = z_gla.shape
    nt = n // TM
    w = BRANCH_W
    fwd = lambda s: s
    bwd = lambda s: _bwd_tile(s, nc, nt)
    return pl.pallas_call(
        _gla_kernel,
        grid=(nt,),
        in_specs=[_all_rows_spec(bsz, W_GLA, fwd), _all_rows_spec(bsz, W_GLA, bwd),
                  _const_spec(wlr.shape), _const_spec(blr.shape)] + [_const_spec(t.shape) for t in tables],
        out_specs=[_all_rows_spec(bsz, w, fwd), _all_rows_spec(bsz, w, bwd)],
        out_shape=[jax.ShapeDtypeStruct((bsz, n, w), F32)] * 2,
        scratch_shapes=[pltpu.VMEM((2 * bsz, N_HEADS, HEAD_W, HEAD_W), F32)],
        compiler_params=_params(("arbitrary",)),
        name="gla",
    )(z_gla, z_gla, wlr, blr, *tables)


def _hg_call(z_hg, lbp, tables, nc):
    bsz, n, _ = z_hg.shape
    nt = n // TM
    w = BRANCH_W
    fwd = lambda s: s
    bwd = lambda s: _bwd_tile(s, nc, nt)
    return pl.pallas_call(
        _hg_kernel,
        grid=(nt,),
        in_specs=[_all_rows_spec(bsz, 4 * w, fwd), _all_rows_spec(bsz, 4 * w, bwd),
                  _const_spec(lbp.shape)] + [_const_spec(t.shape) for t in tables],
        out_specs=[_all_rows_spec(bsz, w, fwd), _all_rows_spec(bsz, w, bwd)],
        out_shape=[jax.ShapeDtypeStruct((bsz, n, w), F32)] * 2,
        scratch_shapes=[pltpu.VMEM((2 * bsz, N_HEADS, HEAD_W, HEAD_W), F32)],
        compiler_params=_params(("arbitrary",)),
        name="hgrn2",
    )(z_hg, z_hg, lbp, *tables)


def _cummax_rows(a, reverse):
    n = a.shape[0]
    row = lax.broadcasted_iota(jnp.int32, a.shape, 0)
    k = 1
    while k < n:
        if reverse:
            a_s, ok = pltpu.roll(a, n - k, 0), row < n - k
        else:
            a_s, ok = pltpu.roll(a, k, 0), row >= k
        a = jnp.maximum(a, jnp.where(ok, a_s, -jnp.inf))
        k *= 2
    return a


def _ml_chunk(q, k, v, gi, log_f, tri_sel, tri_mask, ct_ref, m_ref, slot, d, reverse):
    b = _mm_sel(tri_sel, log_f)
    a = gi - b
    m_prev = m_ref[slot:slot + 1, :]
    m_rel = jnp.maximum(_cummax_rows(a, reverse), m_prev)
    w_inter = jnp.exp(m_prev - m_rel)
    exp_neg_m = jnp.exp(-(b + m_rel))
    last = 0 if reverse else CHUNK - 1
    b_end = b[last:last + 1, :]
    m_new = b_end + m_rel[last:last + 1, :]
    w_end = jnp.exp(b_end - b + gi - m_new)
    decay = jnp.exp(b_end + m_prev - m_new)
    m_ref[slot:slot + 1, :] = m_new
    a_t = a.T
    ones = jnp.ones((CHUNK, HEAD_W), BF16)
    outs = []
    for h in range(N_HEADS):
        cols = slice(h * HEAD_W, (h + 1) * HEAD_W)
        qh, kh, vh = q[:, cols], k[:, cols], v[:, cols]
        c = d * N_HEADS + h
        log_w = jnp.where(tri_mask > 0.0, a_t[c:c + 1, :] - m_rel[:, c:c + 1], -jnp.inf)
        s = _mm_nt(qh, kh) * jnp.exp(log_w)
        v_ext = jnp.concatenate([vh.astype(BF16), ones], axis=-1)
        st = slot * N_HEADS + h
        ct = ct_ref[st]
        tot = _mm(s, v_ext) + w_inter[:, c:c + 1] * _mm_nt(qh, ct)
        num, den = tot[:, :HEAD_W], tot[:, HEAD_W:]
        outs.append(num / jnp.maximum(jnp.abs(den), exp_neg_m[:, c:c + 1]))
        ct_ref[st] = decay[:, c:c + 1] * ct + _mm_tn(v_ext, kh * w_end[:, c:c + 1])
    return jnp.concatenate(outs, axis=-1)


def _ml_kernel(zf, zf_p, zf_n, zb, zb_p, zb_n, cw, cb, gbias,
               tri_sel, tri_mask, of_ref, ob_ref, pad_ref, ct_ref, m_ref, *, nc, nt):
    s = pl.program_id(0)
    w = BRANCH_W
    n_chunks = TM // CHUNK

    @pl.when(s == 0)
    def _():
        ct_ref[...] = jnp.zeros_like(ct_ref)
        m_ref[...] = jnp.zeros_like(m_ref)

    dirs = ((zf, zf_p, zf_n, of_ref, s), (zb, zb_p, zb_n, ob_ref, _bwd_tile(s, nc, nt)))
    sides = []
    for b in range(zf.shape[0]):
        for d, (z_ref, p_ref, nx_ref, o_ref, tile) in enumerate(dirs):
            slot = 2 * b + d
            qk = _silu(_conv_tile(z_ref.at[b, :, 0:2 * w], p_ref.at[b], nx_ref.at[b],
                                  pad_ref.at[slot], cw, cb, tile, nc, nt))
            gi = z_ref[b, :, 4 * w:4 * w + HEAD_W] + gbias[0:1, :]
            log_f = _log_sigmoid(z_ref[b, :, 4 * w + HEAD_W:4 * w + 2 * HEAD_W] + gbias[1:2, :])
            sides.append((qk[:, :w] * (HEAD_W ** -0.5), qk[:, w:], z_ref[b, :, 2 * w:3 * w],
                          gi, log_f, o_ref.at[b], slot, d))
    for c in range(n_chunks):
        for q, k, v, gi, log_f, o_ref, slot, d in sides:
            cc = n_chunks - 1 - c if d == 1 else c
            rows = slice(cc * CHUNK, (cc + 1) * CHUNK)
            o_ref[rows, :] = _ml_chunk(q[rows], k[rows], v[rows], gi[rows], log_f[rows],
                                       tri_sel[d], tri_mask[d], ct_ref, m_ref, slot, d, d == 1)


def _ml_call(z_ml, cw, cb, gbias, tri_sel, tri_mask, nc):
    bsz, n, _ = z_ml.shape
    nt = n // TM
    w = BRANCH_W
    fwd = lambda s: s
    bwd = lambda s: _bwd_tile(s, nc, nt)
    rows8 = TM // SUBLANES
    last8 = n // SUBLANES - 1

    def side(fn):
        return [_all_rows_spec(bsz, W_ML, fn),
                pl.BlockSpec((bsz, SUBLANES, 2 * w), lambda s: (0, jnp.maximum(fn(s) * rows8 - 1, 0), 0)),
                pl.BlockSpec((bsz, SUBLANES, 2 * w), lambda s: (0, jnp.minimum((fn(s) + 1) * rows8, last8), 0))]

    return pl.pallas_call(
        functools.partial(_ml_kernel, nc=nc, nt=nt),
        grid=(nt,),
        in_specs=side(fwd) + side(bwd) + [
            _const_spec((CONV_W, 2 * w)), _const_spec((1, 2 * w)), _const_spec((2, HEAD_W)),
            _const_spec(tri_sel.shape), _const_spec(tri_mask.shape)],
        out_specs=[_all_rows_spec(bsz, w, fwd), _all_rows_spec(bsz, w, bwd)],
        out_shape=[jax.ShapeDtypeStruct((bsz, n, w), F32)] * 2,
        scratch_shapes=[pltpu.VMEM((2 * bsz, TM + 2 * SUBLANES, 2 * w), F32),
                        pltpu.VMEM((2 * bsz * N_HEADS, 2 * HEAD_W, HEAD_W), F32),
                        pltpu.VMEM((2 * bsz, HEAD_W), F32)],
        compiler_params=_params(("arbitrary",)),
        name="mlstm",
    )(*([z_ml] * 6), cw, cb, gbias, tri_sel, tri_mask)


def _gelu_tanh(x):
    return 0.5 * x * (1.0 + jnp.tanh(0.7978845608028654 * (x + 0.044715 * (x * x * x))))


def _route(logits):
    col = lax.broadcasted_iota(jnp.int32, logits.shape, 1)
    colf = col.astype(F32)
    is_g = jnp.logical_and(col >= N_EXPERTS, col < N_EXPERTS + N_GROUPS)
    is_e = col < N_EXPERTS
    neg = -jnp.inf
    big = 1e9
    gl = jnp.where(is_g, logits, neg)
    g_max = jnp.max(gl, axis=-1, keepdims=True)
    grp = jnp.min(jnp.where(gl == g_max, colf, big), axis=-1, keepdims=True) - N_EXPERTS
    p_grp = 1.0 / jnp.sum(jnp.exp(gl - g_max), axis=-1, keepdims=True)
    col_grp = lax.shift_right_logical(col, EXPERTS_PER_GROUP.bit_length() - 1)
    in_grp = jnp.logical_and(is_e, col_grp.astype(F32) == grp)
    e1 = jnp.where(in_grp, logits, neg)
    top1 = jnp.max(e1, axis=-1, keepdims=True)
    idx1 = jnp.min(jnp.where(e1 == top1, colf, big), axis=-1, keepdims=True)
    e2 = jnp.where(colf == idx1, neg, e1)
    top2 = jnp.max(e2, axis=-1, keepdims=True)
    idx2 = jnp.min(jnp.where(e2 == top2, colf, big), axis=-1, keepdims=True)
    t = jnp.exp(top2 - top1)
    w1 = p_grp / (1.0 + t)
    w2 = p_grp * (t / (1.0 + t))
    out = jnp.where(col == ROUTE_IDX, idx1, jnp.where(col == ROUTE_IDX + 1, idx2, 0.0))
    return jnp.where(col == ROUTE_W, w1, jnp.where(col == ROUTE_W + 1, w2, out))


def _route_onehot(route):
    colf = lax.broadcasted_iota(jnp.int32, route.shape, 1).astype(F32)
    oh1 = colf == route[:, ROUTE_IDX:ROUTE_IDX + 1]
    oh2 = colf == route[:, ROUTE_IDX + 1:ROUTE_IDX + 2]
    return jnp.where(jnp.logical_or(oh1, oh2), 1.0, 0.0), oh1, oh2


def _stage_c_kernel(u_ref, h_ref, mod_ref, rg_y, rg_f, rg_b, gla_g, gla_f, gla_b, hg_g, hg_f, hg_b,
                    ml_o, ml_f, ml_b, gains, w_merge, b_merge, w_branch, w_out, ffn_g, w_route, b_route,
                    h_out, u2_out, wt_out, cnt_out):
    ys = (
        _gelu_tanh(rg_y[...]) * (rg_f[...] + rg_b[...]),
        _head_rmsnorm(gla_f[...] + gla_b[...], gains[0:1, :]) * _silu(gla_g[...]),
        _head_rmsnorm(hg_f[...] + hg_b[...], gains[1:2, :]) * _silu(hg_g[...]),
        _sigmoid(ml_o[...]) * _head_rmsnorm(ml_f[...] + ml_b[...], gains[2:3, :]),
    )
    u = u_ref[...]
    d = u.shape[-1]
    merged = None
    for kk, y in enumerate(ys):
        gate = _sigmoid(jnp.dot(u, w_merge[:, kk * d:(kk + 1) * d], preferred_element_type=F32)
                              + b_merge[:, kk * d:(kk + 1) * d])
        term = gate * _mm(y, w_branch[kk])
        merged = term if merged is None else merged + term
    mix = _mm(merged, w_out[...])
    h_new = h_ref[...] + mod_ref[2:3, :] * mix
    h_out[...] = h_new
    u2 = _rmsnorm_rows(h_new, ffn_g[...]) * (1.0 + mod_ref[4:5, :]) + mod_ref[3:4, :]
    _store_tiled(u2_out, u2)
    logits = _mm_split(u2, w_route[...]) + b_route[...]
    route = _route(logits)
    wt_out[...] = route
    cnt = jnp.sum(_route_onehot(route)[0], axis=0, keepdims=True)
    cnt_out[...] = jnp.broadcast_to(cnt, cnt_out.shape)


def _stage_c_call(u, h, mod, z_rg, rg_f, rg_b, z_gla, gla_f, gla_b, z_hg, hg_f, hg_b, z_ml, ml_f, ml_b,
                  gains, w_merge, b_merge, w_branch, w_out, ffn_g, w_route, b_route, nc, t0):
    bsz, n, d = h.shape
    nt = n // TM
    w = BRANCH_W

    n_out = n - t0 * TM

    def tok(width, col=0):
        return pl.BlockSpec((None, TM, width), lambda b, t: (b, t + t0, col))

    def out_tok(width):
        return pl.BlockSpec((None, TM, width), lambda b, t: (b, t, 0))

    mod_spec = pl.BlockSpec((None, 6, d), lambda b, t: (jnp.where(t + t0 < nc, bsz, b), 0, 0))
    return pl.pallas_call(
        _stage_c_kernel,
        grid=(bsz, nt - t0),
        in_specs=[tok(d), tok(d), mod_spec,
                  tok(w, 1), tok(w), tok(w),
                  tok(w, 3), tok(w), tok(w),
                  tok(w, 4), tok(w), tok(w),
                  tok(w, 3), tok(w), tok(w),
                  _const_spec(gains.shape), _const_spec(w_merge.shape), _const_spec(b_merge.shape),
                  _const_spec(w_branch.shape), _const_spec(w_out.shape), _const_spec(ffn_g.shape),
                  _const_spec(w_route.shape), _const_spec(b_route.shape)],
        out_specs=[out_tok(d),
                   pl.BlockSpec((None, TM * SUBLANES, HEAD_W), lambda b, t: (b, t, 0)),
                   out_tok(HEAD_W),
                   pl.BlockSpec((None, None, SUBLANES, HEAD_W), lambda b, t: (b, t, 0, 0))],
        out_shape=[jax.ShapeDtypeStruct((bsz, n_out, d), F32),
                   jax.ShapeDtypeStruct((bsz, n_out * SUBLANES, HEAD_W), F32),
                   jax.ShapeDtypeStruct((bsz, n_out, HEAD_W), F32),
                   jax.ShapeDtypeStruct((bsz, nt - t0, SUBLANES, HEAD_W), F32)],
        compiler_params=_params(("parallel", "parallel")),
        name="stage_c",
    )(u, h, mod, z_rg, rg_f, rg_b, z_gla, gla_f, gla_b, z_hg, hg_f, hg_b, z_ml, ml_f, ml_b,
      gains, w_merge, b_merge, w_branch, w_out, ffn_g, w_route, b_route)


def _slots_to_smem(slots_vmem, slots_smem, sem):
    cp = pltpu.make_async_copy(slots_vmem, slots_smem, sem)
    cp.start()
    cp.wait()


def _row_copy(src, src_row, dst, dst_row, sem):
    def tile(ref, row):
        return ref.at[pl.ds(pl.multiple_of(row * SUBLANES, SUBLANES), SUBLANES), :]
    return pltpu.make_async_copy(tile(src, src_row), tile(dst, dst_row), sem)


def _to_tiles(x):
    return [x[:, k * HEAD_W:(k + 1) * HEAD_W] for k in range(SUBLANES)]


def _store_tiled(ref, x):
    rows = x.shape[0]
    for k, piece in enumerate(_to_tiles(x)):
        ref[pl.ds(k, rows, stride=SUBLANES), :] = piece


def _load_tiled(ref, rows):
    return jnp.concatenate([ref[pl.ds(k, rows, stride=SUBLANES), :] for k in range(SUBLANES)], axis=-1)


def _dispatch_kernel(seg_ref, route_ref, base_ref, tri_ref, x_ref, xs_hbm, slots_ref,
                     slots_smem, zero_buf, sem_s, sem_r, sem_z):
    @pl.when(pl.program_id(0) == 0)
    def _():
        zero_buf[...] = jnp.zeros_like(zero_buf)

        def fill(e):
            start_row = pl.multiple_of((seg_ref[e] - MOE_BLK) * SUBLANES, SUBLANES)
            return pltpu.make_async_copy(
                zero_buf, xs_hbm.at[pl.ds(start_row, MOE_BLK * SUBLANES), :], sem_z)

        for e in range(N_EXPERTS):
            @pl.when(seg_ref[N_EXPERTS + e] > 0)
            def _(e=e):
                fill(e).start()
        for e in range(N_EXPERTS):
            @pl.when(seg_ref[N_EXPERTS + e] > 0)
            def _(e=e):
                fill(e).wait()

        blk_rows = MOE_BLK * SUBLANES
        used = seg_ref[N_EXPERTS - 1] // MOE_BLK

        def tail(b):
            return pltpu.make_async_copy(
                zero_buf, xs_hbm.at[pl.ds(pl.multiple_of(b * blk_rows, blk_rows), blk_rows), :], sem_z)

        def tail_start(b, carry):
            tail(b).start()
            return carry

        def tail_wait(b, carry):
            tail(b).wait()
            return carry

        n_blk = xs_hbm.shape[0] // blk_rows
        lax.fori_loop(used, n_blk, tail_start, 0)
        lax.fori_loop(used, n_blk, tail_wait, 0)

    route = route_ref[...]
    onehot, oh1, oh2 = _route_onehot(route)
    rank = jnp.dot(tri_ref[...], onehot.astype(BF16), preferred_element_type=F32)
    pos = base_ref[...] + rank
    slot_a = jnp.sum(jnp.where(oh1, pos, 0.0), axis=-1, keepdims=True)
    slot_b = jnp.sum(jnp.where(oh2, pos, 0.0), axis=-1, keepdims=True)
    col = lax.broadcasted_iota(jnp.int32, route.shape, 1)
    both = jnp.where(col == 0, slot_a, jnp.where(col == 1, slot_b, 0.0))
    slots_ref[...] = both.T[0:SUBLANES, :].astype(jnp.int32)
    _slots_to_smem(slots_ref, slots_smem, sem_s)

    def start(t, carry):
        for j in range(2):
            _row_copy(x_ref, t, xs_hbm, slots_smem[j, t], sem_r).start(priority=j)
        return carry

    def wait(t, carry):
        for j in range(2):
            _row_copy(x_ref, 0, xs_hbm, 0, sem_r).wait()
        return carry

    lax.fori_loop(0, TM, start, 0, unroll=8)
    lax.fori_loop(0, TM, wait, 0, unroll=8)


def _experts_kernel(blk_expert, n_used, x_ref, w1, w3, w2, o_ref):
    del blk_expert

    @pl.when(pl.program_id(0) < n_used[0])
    def _():
        x = _load_tiled(x_ref, MOE_BLK)
        h1 = _mm(x, w1[...])
        h3 = _mm(x, w3[...])
        _store_tiled(o_ref, _mm(_silu(h1) * h3, w2[...]))

    @pl.when(pl.program_id(0) >= n_used[0])
    def _():
        o_ref[...] = jnp.zeros_like(o_ref)


def _combine_kernel(slots_smem, route_ref, h_ref, mod_ref, fin_g, ys_hbm, o_ref, buf, sem_r, *, final):
    def start(t, carry):
        for j in range(2):
            _row_copy(ys_hbm, slots_smem[j, t], buf.at[j], t, sem_r).start(priority=j)
        return carry

    def wait(t, carry):
        for j in range(2):
            _row_copy(ys_hbm, 0, buf.at[j], 0, sem_r).wait()
        return carry

    lax.fori_loop(0, TM, start, 0, unroll=8)
    lax.fori_loop(0, TM, wait, 0, unroll=8)
    route = route_ref[...]
    y = (route[:, ROUTE_W:ROUTE_W + 1] * _load_tiled(buf.at[0], TM)
         + route[:, ROUTE_W + 1:ROUTE_W + 2] * _load_tiled(buf.at[1], TM))
    out = h_ref[...] + mod_ref[5:6, :] * y
    if final:
        out = _rmsnorm_rows(out, fin_g[...])
    o_ref[...] = out


def _moe_call(u2, route, cnt, h, mod, fin_g, w1, w3, w2, layer, nc, t0, final):
    bsz, n_out, d = h.shape
    de = w1.shape[-1]
    rows = bsz * n_out
    tiles = rows // TM
    tiles_per_row = n_out // TM
    n_blocks = (2 * rows + N_EXPERTS * (MOE_BLK - 1)) // MOE_BLK + 1

    counts = cnt[:, :, 0, :N_EXPERTS].reshape(tiles, N_EXPERTS)
    total = jnp.sum(counts, axis=0)
    padded = jnp.ceil(total / MOE_BLK) * MOE_BLK
    seg_end = jnp.cumsum(padded)
    base = (seg_end - padded)[None, :] + jnp.cumsum(counts, axis=0) - counts
    base = _pad_cols(base, HEAD_W).reshape(tiles, 1, HEAD_W)
    blk_start = jnp.arange(n_blocks, dtype=F32) * MOE_BLK
    blk_expert = jnp.minimum(jnp.sum(blk_start[:, None] >= seg_end[None, :], axis=1), N_EXPERTS - 1)
    n_used = (seg_end[-1] / MOE_BLK).astype(jnp.int32).reshape(1)
    tri = jnp.asarray(np.tril(np.ones((TM, TM), np.float32), -1), BF16)

    assert d == SUBLANES * HEAD_W
    slot_rows = n_blocks * MOE_BLK * SUBLANES
    seg = jnp.concatenate([seg_end, padded]).astype(jnp.int32)
    xs, slots = pl.pallas_call(
        _dispatch_kernel,
        grid_spec=pltpu.PrefetchScalarGridSpec(
            num_scalar_prefetch=1,
            grid=(tiles,),
            in_specs=[pl.BlockSpec((TM, HEAD_W), lambda t, sg: (t, 0)),
                      pl.BlockSpec((None, 1, HEAD_W), lambda t, sg: (t, 0, 0)),
                      pl.BlockSpec((TM, TM), lambda t, sg: (0, 0)),
                      pl.BlockSpec((TM * SUBLANES, HEAD_W), lambda t, sg: (t, 0))],
            out_specs=[pl.BlockSpec(memory_space=pl.ANY),
                       pl.BlockSpec((SUBLANES, TM), lambda t, sg: (t, 0))],
            scratch_shapes=[pltpu.SMEM((SUBLANES, TM), jnp.int32),
                            pltpu.VMEM((MOE_BLK * SUBLANES, HEAD_W), F32),
                            pltpu.SemaphoreType.DMA, pltpu.SemaphoreType.DMA,
                            pltpu.SemaphoreType.DMA]),
        out_shape=[jax.ShapeDtypeStruct((slot_rows, HEAD_W), F32),
                   jax.ShapeDtypeStruct((tiles * SUBLANES, TM), jnp.int32)],
        compiler_params=_params(("arbitrary",)),
        name="moe_dispatch",
    )(seg, route.reshape(rows, HEAD_W), base, tri, u2.reshape(rows * SUBLANES, HEAD_W))

    blk_spec = pl.BlockSpec((MOE_BLK * SUBLANES, HEAD_W), lambda b, be, nu: (b, 0))
    used_blk_spec = pl.BlockSpec((MOE_BLK * SUBLANES, HEAD_W),
                                 lambda b, be, nu: (jnp.minimum(b, nu[0] - 1), 0))
    ys = pl.pallas_call(
        _experts_kernel,
        grid_spec=pltpu.PrefetchScalarGridSpec(
            num_scalar_prefetch=2,
            grid=(n_blocks,),
            in_specs=[used_blk_spec,
                      pl.BlockSpec((None, None, d, de), lambda b, be, nu: (layer, be[b], 0, 0)),
                      pl.BlockSpec((None, None, d, de), lambda b, be, nu: (layer, be[b], 0, 0)),
                      pl.BlockSpec((None, None, de, d), lambda b, be, nu: (layer, be[b], 0, 0))],
            out_specs=blk_spec),
        out_shape=jax.ShapeDtypeStruct((slot_rows, HEAD_W), F32),
        compiler_params=_params(("arbitrary",)),
        name="moe_experts",
    )(blk_expert.astype(jnp.int32), n_used, xs, w1, w3, w2)

    def mod_index(t):
        b, pos = t // tiles_per_row, t % tiles_per_row
        return (jnp.where(pos + t0 < nc, bsz, b), 0, 0)

    out = pl.pallas_call(
        functools.partial(_combine_kernel, final=final),
        grid=(tiles,),
        in_specs=[pl.BlockSpec((None, SUBLANES, TM), lambda t: (t, 0, 0), memory_space=pltpu.SMEM),
                  pl.BlockSpec((TM, HEAD_W), lambda t: (t, 0)),
                  pl.BlockSpec((TM, d), lambda t: (t, 0)),
                  pl.BlockSpec((None, 6, d), mod_index),
                  _const_spec((1, d)),
                  pl.BlockSpec(memory_space=pl.ANY)],
        out_specs=pl.BlockSpec((TM, d), lambda t: (t, 0)),
        out_shape=jax.ShapeDtypeStruct((rows, d), F32),
        scratch_shapes=[pltpu.VMEM((2, TM * SUBLANES, HEAD_W), F32),
                        pltpu.SemaphoreType.DMA],
        compiler_params=_params(("arbitrary",)),
        name="moe_combine",
    )(slots.reshape(tiles, SUBLANES, TM), route.reshape(rows, HEAD_W), h.reshape(rows, d), mod, fin_g, ys)
    return out.reshape(bsz, n_out, d)


def _pad_heads(w, dk):
    lead = w.shape[:-1]
    w = w.reshape(lead + (N_HEADS, dk))
    w = jnp.pad(w, [(0, 0)] * len(lead) + [(0, 0), (0, HEAD_W - dk)])
    return w.reshape(lead + (N_HEADS * HEAD_W,))


def _pad_cols(w, width):
    return jnp.pad(w, [(0, 0)] * (w.ndim - 1) + [(0, width - w.shape[-1])])


def _split_w_in(w_in):
    bw = BRANCH_W
    sizes = [bw, bw, N_HEADS * GLA_DK, N_HEADS * GLA_DK, bw, bw, 2 * GLA_RANK,
             bw, bw, 2 * bw, bw, bw, bw, bw, bw, 16]
    offs = np.cumsum([0] + sizes)
    p = [w_in[:, offs[i]:offs[i + 1]] for i in range(len(sizes))]
    w_rg = jnp.concatenate([p[0], p[1]], axis=1)
    w_gla = jnp.concatenate([_pad_heads(p[2], GLA_DK), _pad_heads(p[3], GLA_DK), p[4], p[5],
                             _pad_cols(p[6], HEAD_W)], axis=1)
    w_hg = jnp.concatenate([p[7], p[8], p[9], p[10]], axis=1)
    gates = p[15].reshape(-1, 2, 2, N_HEADS)
    w_gi = _pad_cols(gates[:, :, 0].reshape(-1, 2 * N_HEADS), HEAD_W)
    w_gf = _pad_cols(gates[:, :, 1].reshape(-1, 2 * N_HEADS), HEAD_W)
    w_ml = jnp.concatenate([p[11], p[12], p[13], p[14], w_gi, w_gf], axis=1)
    return [w.astype(BF16) for w in (w_rg, w_gla, w_hg, w_ml)]


def _block_diag(w):
    k, n = w.shape[-3], w.shape[-1]
    eye = jnp.eye(k, dtype=w.dtype)
    full = jnp.einsum('...kij,kl->...kilj', w, eye)
    return full.reshape(w.shape[:-3] + (k * n, k * n))


def kernel(x, c, ctx, c_ctx, ada_w, ada_b, norm_mix_g, norm_ffn_g, w_in, rg_conv_w, rg_conv_b, rg_gate_w, rg_gate_b, rg_lambda, gla_w_lr, gla_b_lr, gla_norm_g, hgrn_lb_logits, hgrn_norm_g, ml_conv_w, ml_conv_b, ml_gate_b, ml_norm_g, w_branch, w_merge, b_merge, w_out, moe_w_group, moe_b_group, moe_w_expert, moe_b_expert, moe_w1, moe_w3, moe_w2, final_norm_g):
    bsz, seq, d = x.shape
    n_ctx = ctx.shape[1]
    depth = ada_w.shape[0]
    assert n_ctx % TM == 0 and seq % TM == 0 and d == 2 * BRANCH_W
    nc = n_ctx // TM

    h = jnp.concatenate([ctx, x], axis=1)
    cvec = jnp.zeros((SUBLANES, d), F32).at[:bsz].set(c).at[bsz].set(c_ctx)
    mod_all = _mod_call(cvec, ada_w, ada_b).reshape(depth, SUBLANES, 6, d)[:, :bsz + 1]

    lb_cum = jnp.cumsum(jax.nn.softmax(hgrn_lb_logits.astype(F32), axis=0), axis=0)
    hgrn_lb = lb_cum - lb_cum[:1]
    tables = [t for levels in SPLIT_LEVELS for t in _decay_tables(levels)]
    tri_sel = tables[0][:, :CHUNK, :]
    tri_mask = tables[1][:, 0]

    out = None
    for l in range(depth):
        last = l == depth - 1
        mod = mod_all[l]
        w_rg, w_gla, w_hg, w_ml = _split_w_in(w_in[l])
        z_rg, z_gla, z_hg, z_ml, u = _stage_a_call(h, mod, norm_mix_g[l], w_rg, w_gla, w_hg, w_ml, nc)

        rg_f, rg_b = _rg_call(z_rg, rg_conv_w[l], rg_conv_b[l].reshape(1, -1),
                              _block_diag(rg_gate_w[l]).astype(BF16),
                              rg_gate_b[l].reshape(4, -1), rg_lambda[l], nc)

        wlr = jnp.zeros((2, HEAD_W, N_HEADS * HEAD_W), F32)
        wlr_p = _pad_heads(gla_w_lr[l], GLA_DK)
        wlr = wlr.at[0, :GLA_RANK].set(wlr_p[0]).at[1, GLA_RANK:2 * GLA_RANK].set(wlr_p[1])
        gla_f, gla_b = _gla_call(z_gla, wlr.astype(BF16), _pad_heads(gla_b_lr[l], GLA_DK), tables, nc)

        lb = hgrn_lb[l]
        lbp = jnp.concatenate([lb, jnp.log(lb), jnp.log1p(-lb), jnp.zeros((2, lb.shape[-1]), F32)], axis=0)
        hg_f, hg_b = _hg_call(z_hg, lbp, tables, nc)

        gbias = _pad_cols(ml_gate_b[l].transpose(1, 0, 2).reshape(2, -1), HEAD_W)
        ml_f, ml_b = _ml_call(z_ml, ml_conv_w[l], ml_conv_b[l].reshape(1, -1), gbias, tri_sel, tri_mask, nc)

        gains = jnp.zeros((SUBLANES, HEAD_W), F32).at[0].set(gla_norm_g[l]).at[1].set(hgrn_norm_g[l]).at[2].set(ml_norm_g[l])
        w_route = _split_cols(_pad_cols(jnp.concatenate([moe_w_expert[l], moe_w_group[l]], axis=1), HEAD_W))
        b_route = _pad_cols(jnp.concatenate([moe_b_expert[l], moe_b_group[l]]).reshape(1, -1), HEAD_W)
        t0 = nc if last else 0
        h_mid, u2, route, cnt = _stage_c_call(
            u, h, mod, z_rg, rg_f, rg_b, z_gla, gla_f, gla_b, z_hg, hg_f, hg_b, z_ml, ml_f, ml_b,
            gains, w_merge[l].astype(BF16), b_merge[l].reshape(1, -1), w_branch[l].astype(BF16),
            w_out[l].astype(BF16), norm_ffn_g[l].reshape(1, -1), w_route, b_route, nc, t0)

        res = _moe_call(u2, route, cnt, h_mid, mod, final_norm_g.reshape(1, -1), moe_w1,
                        moe_w3, moe_w2, l, nc, t0, last)
        if last:
            out = res
        else:
            h = res
    return out
```

```python
import functools

import numpy as np
import jax
import jax.numpy as jnp
from jax import lax
from jax.experimental import pallas as pl
from jax.experimental.pallas import tpu as pltpu

F32 = jnp.float32
BF16 = jnp.bfloat16

EPS = 1e-6
TM = 256
CHUNK = 64
SUBLANES = 8
N_HEADS = 4
HEAD_W = 128
BRANCH_W = 512
CONV_W = 4
CONV_LEFT = 2
RG_C = 8.0
GLA_DK = 64
GLA_RANK = 16
GLA_GATE_NORM = 16.0
N_GROUPS = 4
EXPERTS_PER_GROUP = 4
N_EXPERTS = 16
MOE_BLK = 256
ROUTE_IDX = 16
ROUTE_W = 18
N_LEVELS = 6
MILD_DECAY = -40.0
VMEM_LIMIT = 56 * 1024 * 1024

W_RG = 2 * BRANCH_W
W_GLA = 4 * BRANCH_W + HEAD_W
W_HG = 5 * BRANCH_W
W_ML = 4 * BRANCH_W + 2 * HEAD_W
SPLIT_LEVELS = (0, 2, N_LEVELS)


def _mm(a, b):
    return jnp.dot(a.astype(BF16), b.astype(BF16), preferred_element_type=F32)


def _mm_nt(a, b):
    return lax.dot_general(a.astype(BF16), b.astype(BF16), (((1,), (1,)), ((), ())),
                           preferred_element_type=F32)


def _mm_tn(a, b):
    return lax.dot_general(a.astype(BF16), b.astype(BF16), (((0,), (0,)), ((), ())),
                           preferred_element_type=F32)


def _mm_sel(sel, x):
    x1 = x.astype(BF16)
    x2 = (x - x1.astype(F32)).astype(BF16)
    dot = functools.partial(jnp.dot, preferred_element_type=F32)
    return dot(sel, x1) + dot(sel, x2)


def _split_cols(w):
    hi = w.astype(BF16)
    return jnp.concatenate([hi, (w - hi.astype(F32)).astype(BF16)], axis=1)


def _mm_split(x, w_split):
    n = w_split.shape[1] // 2
    x1 = x.astype(BF16)
    x2 = (x - x1.astype(F32)).astype(BF16)
    p = jnp.dot(x1, w_split, preferred_element_type=F32)
    return p[:, :n] + p[:, n:] + jnp.dot(x2, w_split[:, :n], preferred_element_type=F32)


def _softplus_neg_abs(x):
    return jnp.log(1.0 + jnp.exp(-jnp.abs(x)))


def _log_sigmoid(x):
    return jnp.minimum(x, 0.0) - _softplus_neg_abs(x)


def _sigmoid(x):
    return 0.5 * jnp.tanh(0.5 * x) + 0.5


def _silu(x):
    return x * _sigmoid(x)


def _rmsnorm_rows(x, g):
    return x * lax.rsqrt(jnp.mean(x * x, axis=-1, keepdims=True) + EPS) * g


def _head_rmsnorm(o, g):
    parts = [_rmsnorm_rows(o[:, h * HEAD_W:(h + 1) * HEAD_W], g) for h in range(N_HEADS)]
    return jnp.concatenate(parts, axis=-1)


def _bwd_tile(s, nc, nt):
    return jnp.where(s < nc, nc - 1 - s, nt - 1 - (s - nc))


def _const_spec(shape):
    nd = len(shape)
    return pl.BlockSpec(shape, lambda *_: (0,) * nd)


def _params(sem):
    return pltpu.CompilerParams(dimension_semantics=sem, vmem_limit_bytes=VMEM_LIMIT)


def _mod_kernel(c_ref, w_ref, b_ref, o_ref):
    cv = _silu(c_ref[...])
    o_ref[...] = jnp.dot(cv, w_ref[...], precision=lax.Precision.HIGHEST,
                         preferred_element_type=F32) + b_ref[...]


def _mod_call(cvec, ada_w, ada_b):
    depth, d, six_d = ada_w.shape
    tn = 1024
    return pl.pallas_call(
        _mod_kernel,
        grid=(depth, six_d // tn),
        in_specs=[pl.BlockSpec((SUBLANES, d), lambda l, j: (0, 0)),
                  pl.BlockSpec((None, d, tn), lambda l, j: (l, 0, j)),
                  pl.BlockSpec((None, 1, tn), lambda l, j: (l, 0, j))],
        out_specs=pl.BlockSpec((None, SUBLANES, tn), lambda l, j: (l, 0, j)),
        out_shape=jax.ShapeDtypeStruct((depth, SUBLANES, six_d), F32),
        compiler_params=_params(("parallel", "parallel")),
        name="adaln_mod",
    )(cvec, ada_w, ada_b.reshape(depth, 1, six_d))


def _stage_a_kernel(h_ref, hp_ref, hn_ref, mod_ref, g_ref, w_rg, w_gla, w_hg, w_ml,
                    rg_cw, rg_cb, ml_cw, ml_cb,
                    z_rg, z_gla, z_hg, z_ml, u_ref, pad_rg, pad_ml, *, nc, nt):
    def modulated(x):
        return (_rmsnorm_rows(x, g_ref[...]) * (1.0 + mod_ref[1:2, :]) + mod_ref[0:1, :]).astype(BF16)

    ub = modulated(h_ref[...])
    u_ref[...] = ub
    u_halo = jnp.concatenate([modulated(hp_ref[...]), ub, modulated(hn_ref[...])], axis=0)
    w = BRANCH_W
    tile = pl.program_id(1)
    prev_ok = jnp.logical_and(tile != 0, tile != nc)
    next_ok = jnp.logical_and(tile != nc - 1, tile != nt - 1)
    row = lax.broadcasted_iota(jnp.int32, (TM + 2 * SUBLANES, 1), 0)
    keep = jnp.logical_and(jnp.logical_or(row >= SUBLANES, prev_ok),
                           jnp.logical_or(row < SUBLANES + TM, next_ok))

    def conv(pad_ref, w_ref, width, cw_ref, cb_ref):
        z = jnp.dot(u_halo, w_ref[:, 0:width], preferred_element_type=F32)
        pad_ref[...] = jnp.where(keep, z, 0.0)
        acc = cb_ref[...]
        for j in range(CONV_W):
            off = SUBLANES - CONV_LEFT + j
            acc = acc + cw_ref[j:j + 1, :] * pad_ref[off:off + TM, :]
        return acc

    z_rg[:, 0:w] = conv(pad_rg, w_rg, w, rg_cw, rg_cb)
    z_rg[:, w:] = jnp.dot(ub, w_rg[:, w:], preferred_element_type=F32)
    z_ml[:, 0:2 * w] = conv(pad_ml, w_ml, 2 * w, ml_cw, ml_cb)
    z_ml[:, 2 * w:] = jnp.dot(ub, w_ml[:, 2 * w:], preferred_element_type=F32)
    z_gla[...] = jnp.dot(ub, w_gla[...], preferred_element_type=F32)
    z_hg[...] = jnp.dot(ub, w_hg[...], preferred_element_type=F32)


def _stage_a_call(h, mod, norm_g, w_rg, w_gla, w_hg, w_ml, rg_cw, rg_cb, ml_cw, ml_cb, nc):
    bsz, n, d = h.shape
    tm = TM
    nt = n // tm
    w = BRANCH_W

    def tok(width):
        return pl.BlockSpec((None, tm, width), lambda b, t: (b, t, 0))

    def weight(width):
        return pl.BlockSpec((d, width), lambda b, t: (0, 0), pipeline_mode=pl.Buffered(1))

    return pl.pallas_call(
        functools.partial(_stage_a_kernel, nc=nc, nt=nt),
        grid=(bsz, nt),
        in_specs=_halo_specs(d, 0, lambda t: t, n) + [
            pl.BlockSpec((None, 6, d), lambda b, t: (jnp.where(t < nc, bsz, b), 0, 0)),
            _const_spec((1, d)),
            weight(W_RG), weight(W_GLA), weight(W_HG), weight(W_ML),
            _const_spec((CONV_W, w)), _const_spec((1, w)),
            _const_spec((CONV_W, 2 * w)), _const_spec((1, 2 * w))],
        out_specs=[tok(W_RG), tok(W_GLA), tok(W_HG), tok(W_ML), tok(d)],
        out_shape=[jax.ShapeDtypeStruct((bsz, n, W_RG), F32),
                   jax.ShapeDtypeStruct((bsz, n, W_GLA), F32),
                   jax.ShapeDtypeStruct((bsz, n, W_HG), F32),
                   jax.ShapeDtypeStruct((bsz, n, W_ML), F32),
                   jax.ShapeDtypeStruct((bsz, n, d), BF16)],
        scratch_shapes=[pltpu.VMEM((tm + 2 * SUBLANES, w), F32),
                        pltpu.VMEM((tm + 2 * SUBLANES, 2 * w), F32)],
        compiler_params=_params(("parallel", "parallel")),
        name="stage_a",
    )(h, h, h, mod, norm_g.reshape(1, d), w_rg, w_gla, w_hg, w_ml, rg_cw, rg_cb, ml_cw, ml_cb)


def _halo_specs(width, col_block, tile_fn, n):
    rows = TM // SUBLANES
    last = n // SUBLANES - 1
    cur = pl.BlockSpec((None, TM, width), lambda b, s: (b, tile_fn(s), col_block))
    prev = pl.BlockSpec((None, SUBLANES, width),
                        lambda b, s: (b, jnp.maximum(tile_fn(s) * rows - 1, 0), col_block))
    nxt = pl.BlockSpec((None, SUBLANES, width),
                       lambda b, s: (b, jnp.minimum((tile_fn(s) + 1) * rows, last), col_block))
    return [cur, prev, nxt]


def _scan_rows8(a, b, reverse):
    n = a.shape[0]
    pos = jnp.bitwise_and(lax.broadcasted_iota(jnp.int32, a.shape, 0), SUBLANES - 1)
    k = 1
    while k < SUBLANES:
        if reverse:
            a_s, b_s, ok = pltpu.roll(a, n - k, 0), pltpu.roll(b, n - k, 0), pos < SUBLANES - k
        else:
            a_s, b_s, ok = pltpu.roll(a, k, 0), pltpu.roll(b, k, 0), pos >= k
        b = b + a * jnp.where(ok, b_s, 0.0)
        a = a * jnp.where(ok, a_s, 1.0)
        k *= 2
    return a, b


def _scan_tile(a, b, carry, o_ref, reverse):
    a8, b8 = _scan_rows8(a, b, reverse)
    groups = a.shape[0] // SUBLANES
    order = range(groups - 1, -1, -1) if reverse else range(groups)
    for r in order:
        rows = slice(r * SUBLANES, (r + 1) * SUBLANES)
        h = b8[rows] + a8[rows] * carry
        o_ref[rows, :] = h
        carry = h[0:1, :] if reverse else h[SUBLANES - 1:SUBLANES, :]
    return carry


def _rg_kernel(xf, xb, gw, gb, lam, hf_ref, hb_ref, carry_ref):
    @pl.when(pl.program_id(1) == 0)
    def _():
        carry_ref[...] = jnp.zeros_like(carry_ref)

    for d, (x_ref, o_ref, reverse) in enumerate(((xf, hf_ref, False), (xb, hb_ref, True))):
        x = x_ref[...]
        r = _sigmoid(_mm(x, gw[d, 0]) + gb[2 * d:2 * d + 1, :])
        i = _sigmoid(_mm(x, gw[d, 1]) + gb[2 * d + 1:2 * d + 2, :])
        lam_d = lam[d:d + 1, :]
        softplus = jnp.maximum(-lam_d, 0.0) + jnp.log1p(jnp.exp(-jnp.abs(lam_d)))
        log_a = -RG_C * r * softplus
        a = jnp.exp(log_a)
        t = jnp.tanh(log_a)
        bt = jnp.sqrt(-2.0 * t / (1.0 - t)) * (i * x)
        carry_ref[d:d + 1, :] = _scan_tile(a, bt, carry_ref[d:d + 1, :], o_ref, reverse)


def _rg_call(z_rg, gw, gb, lam, nc):
    bsz, n, _ = z_rg.shape
    nt = n // TM
    w = BRANCH_W
    fwd = lambda s: s
    bwd = lambda s: _bwd_tile(s, nc, nt)
    blk = lambda fn: pl.BlockSpec((None, TM, w), lambda b, s: (b, fn(s), 0))
    return pl.pallas_call(
        _rg_kernel,
        grid=(bsz, nt),
        in_specs=[blk(fwd), blk(bwd),
                  _const_spec((2, 2, w, w)), _const_spec((4, w)), _const_spec((2, w))],
        out_specs=[blk(fwd), blk(bwd)],
        out_shape=[jax.ShapeDtypeStruct((bsz, n, w), F32)] * 2,
        scratch_shapes=[pltpu.VMEM((2, w), F32)],
        compiler_params=_params(("parallel", "arbitrary")),
        name="rglru",
    )(z_rg, z_rg, gw, gb, lam)


def _decay_tables(levels):
    n = CHUNK
    blk = n >> levels
    sel = np.zeros((levels + 3, n, n), np.float32)
    masks = np.zeros((levels + 1, n, n), np.float32)
    for lvl in range(levels):
        half = n >> (lvl + 1)
        for r in range(n):
            start = (r // (2 * half)) * 2 * half
            ref = start + half - 1
            if r - start >= half:
                sel[lvl, r, ref + 1:r + 1] = 1.0
                masks[lvl, r, start:start + half] = 1.0
            else:
                sel[lvl, r, r + 1:ref + 1] = 1.0
    sel[levels] = np.tril(np.ones((n, n), np.float32))
    sel[levels + 1] = 1.0 - sel[levels]
    for r in range(n):
        start = (r // blk) * blk
        sel[levels + 2, r, r + 1:start + blk] = 1.0
        masks[levels, r, start:r + 1] = 1.0
    if levels == 0 or levels == N_LEVELS:
        sel = sel[:levels + 2]
    sel_b = sel[:, ::-1, ::-1].reshape(-1, n)
    masks_b = masks[:, ::-1, ::-1]
    return (jnp.asarray(np.stack([sel.reshape(-1, n), sel_b]), BF16),
            jnp.asarray(np.stack([masks, masks_b]), F32))


def _ref_rows(levels, reverse):
    def runs(block, offset):
        return [(s + offset, block) for s in range(0, CHUNK, block)]

    out = [runs(CHUNK >> lvl, (CHUNK >> (lvl + 1)) - 1) for lvl in range(levels)]
    out.append(runs(CHUNK, CHUNK - 1))
    out.append(runs(CHUNK >> levels, (CHUNK >> levels) - 1))
    if reverse:
        out = [[(CHUNK - 1 - r, n) for r, n in reversed(segs)] for segs in out]
    return out


def _rows_bcast(b, segs):
    return jnp.concatenate([jnp.broadcast_to(b[r:r + 1, :], (n, b.shape[-1])) for r, n in segs], axis=0)


def _gla_chunk(q, k, v, g, sel, masks, st_ref, d, reverse, levels):
    if levels == N_LEVELS:
        x = jnp.exp(_mm_sel(sel, g))
        x_lvl = [x[lvl * CHUNK:(lvl + 1) * CHUNK] for lvl in range(levels)]
        x_cum = x[levels * CHUNK:(levels + 1) * CHUNK]
        x_rest = x[(levels + 1) * CHUNK:(levels + 2) * CHUNK]
        xk_blk = xq_blk = None
    else:
        b = _mm_sel(sel[levels * CHUNK:(levels + 1) * CHUNK], g)
        refs = _ref_rows(levels, reverse)
        x_cum = jnp.exp(b)
        x_lvl = [jnp.exp(-jnp.abs(b - _rows_bcast(b, segs))) for segs in refs[:levels]]
        x_rest = jnp.exp(-jnp.abs(b - _rows_bcast(b, refs[levels])))
        if levels == 0:
            xk_blk, xq_blk = x_rest, jnp.exp(jnp.abs(b - _rows_bcast(b, refs[levels])))
        else:
            d_blk = jnp.abs(b - _rows_bcast(b, refs[levels + 1]))
            xk_blk, xq_blk = jnp.exp(-d_blk), jnp.exp(d_blk)
    outs = []
    for h in range(N_HEADS):
        cols = slice(h * HEAD_W, (h + 1) * HEAD_W)
        qh, kh, vh = q[:, cols], k[:, cols], v[:, cols]
        kx = kh * x_rest[:, cols]
        if xk_blk is None:
            sc = _mm_nt(qh, kh)
        elif levels == 0:
            sc = _mm_nt(qh * xq_blk[:, cols], kx)
        else:
            sc = _mm_nt(qh * xq_blk[:, cols], kh * xk_blk[:, cols])
        sc = jnp.where(masks[levels] > 0.0, sc, 0.0)
        for lvl in range(levels):
            xl = x_lvl[lvl][:, cols]
            sc = sc + jnp.where(masks[lvl] > 0.0, _mm_nt(qh * xl, kh * xl), 0.0)
        st = st_ref[d, h]
        outs.append(_mm(sc, vh) + _mm_nt(qh * x_cum[:, cols], st))
        x_end = x_cum[0:1, cols] if reverse else x_cum[CHUNK - 1:CHUNK, cols]
        st_ref[d, h] = st * x_end + _mm_tn(vh, kx)
    return jnp.concatenate(outs, axis=-1)


def _gla_step(sides, tabs, st_ref):
    n_chunks = TM // CHUNK

    def min_block_sum(block):
        mins = [jnp.min(jnp.sum(side[3].reshape(TM // block, block, side[3].shape[-1]), axis=1))
                for side in sides]
        return functools.reduce(jnp.minimum, mins)

    conds = []
    taken = None
    for levels in SPLIT_LEVELS[:-1]:
        ok = min_block_sum(CHUNK >> levels) > MILD_DECAY
        conds.append(ok if taken is None else jnp.logical_and(ok, jnp.logical_not(taken)))
        taken = ok if taken is None else jnp.logical_or(taken, ok)
    conds.append(jnp.logical_not(taken))

    for levels, cond, (sel_ref, mask_ref) in zip(SPLIT_LEVELS, conds, tabs):
        @pl.when(cond)
        def _(levels=levels, sel_ref=sel_ref, mask_ref=mask_ref):
            for c in range(n_chunks):
                for i, (q, k, v, g, o_ref, reverse) in enumerate(sides):
                    cc = n_chunks - 1 - c if reverse else c
                    rows = slice(cc * CHUNK, (cc + 1) * CHUNK)
                    o_ref[rows, :] = _gla_chunk(q[rows], k[rows], v[rows], g[rows],
                                                sel_ref[int(reverse)], mask_ref[int(reverse)],
                                                st_ref, i, reverse, levels)


def _pairs(refs):
    return tuple(zip(refs[0::2], refs[1::2]))


def _gla_kernel(zf, zb, wlr, blr, *rest):
    tabs, (of_ref, ob_ref, st_ref) = _pairs(rest[:-3]), rest[-3:]
    w = BRANCH_W

    @pl.when(pl.program_id(0) == 0)
    def _():
        st_ref[...] = jnp.zeros_like(st_ref)

    sides = []
    for b in range(zf.shape[0]):
        for d, (z_ref, o_ref) in enumerate(((zf, of_ref), (zb, ob_ref))):
            pre = _mm(z_ref[b, :, 4 * w:4 * w + HEAD_W], wlr[d]) + blr[d:d + 1, :]
            g = _log_sigmoid(pre) * (1.0 / GLA_GATE_NORM)
            q = z_ref[b, :, 0:w] * (GLA_DK ** -0.5)
            sides.append((q, z_ref[b, :, w:2 * w], z_ref[b, :, 2 * w:3 * w], g, o_ref.at[b], d == 1))
    _gla_step(sides, tabs, st_ref)


def _hg_kernel(zf, zb, lbp, *rest):
    tabs, (of_ref, ob_ref, st_ref) = _pairs(rest[:-3]), rest[-3:]
    w = BRANCH_W

    @pl.when(pl.program_id(0) == 0)
    def _():
        st_ref[...] = jnp.zeros_like(st_ref)

    sides = []
    for b in range(zf.shape[0]):
        for d, (z_ref, o_ref) in enumerate(((zf, of_ref), (zb, ob_ref))):
            f = z_ref[b, :, (2 + d) * w:(3 + d) * w]
            lb = lbp[d:d + 1, :]
            log_lb = lbp[2 + d:3 + d, :]
            log_1m = lbp[4 + d:5 + d, :]
            c = log_1m + _log_sigmoid(f)
            g = jnp.maximum(log_lb, c) + _softplus_neg_abs(log_lb - c)
            k = (1.0 - lb) * _sigmoid(-f)
            q = _silu(z_ref[b, :, 0:w]) * (HEAD_W ** -0.5)
            sides.append((q, k, z_ref[b, :, w:2 * w], g, o_ref.at[b], d == 1))
    _gla_step(sides, tabs, st_ref)


def _mixer_specs(widths_cols, tile_fn):
    return [pl.BlockSpec((None, TM, w), lambda b, s, c=c: (b, tile_fn(s), c)) for w, c in widths_cols]


def _all_rows_spec(bsz, width, tile_fn):
    return pl.BlockSpec((bsz, TM, width), lambda s: (0, tile_fn(s), 0))


def _gla_call(z_gla, wlr, blr, tables, nc):
    bsz, n, _ = z_gla.shape
    nt = n // TM
    w = BRANCH_W
    fwd = lambda s: s
    bwd = lambda s: _bwd_tile(s, nc, nt)
    return pl.pallas_call(
        _gla_kernel,
        grid=(nt,),
        in_specs=[_all_rows_spec(bsz, W_GLA, fwd), _all_rows_spec(bsz, W_GLA, bwd),
                  _const_spec(wlr.shape), _const_spec(blr.shape)] + [_const_spec(t.shape) for t in tables],
        out_specs=[_all_rows_spec(bsz, w, fwd), _all_rows_spec(bsz, w, bwd)],
        out_shape=[jax.ShapeDtypeStruct((bsz, n, w), F32)] * 2,
        scratch_shapes=[pltpu.VMEM((2 * bsz, N_HEADS, HEAD_W, HEAD_W), F32)],
        compiler_params=_params(("arbitrary",)),
        name="gla",
    )(z_gla, z_gla, wlr, blr, *tables)


def _hg_call(z_hg, lbp, tables, nc):
    bsz, n, _ = z_hg.shape
    nt = n // TM
    w = BRANCH_W
    fwd = lambda s: s
    bwd = lambda s: _bwd_tile(s, nc, nt)
    return pl.pallas_call(
        _hg_kernel,
        grid=(nt,),
        in_specs=[_all_rows_spec(bsz, 4 * w, fwd), _all_rows_spec(bsz, 4 * w, bwd),
                  _const_spec(lbp.shape)] + [_const_spec(t.shape) for t in tables],
        out_specs=[_all_rows_spec(bsz, w, fwd), _all_rows_spec(bsz, w, bwd)],
        out_shape=[jax.ShapeDtypeStruct((bsz, n, w), F32)] * 2,
        scratch_shapes=[pltpu.VMEM((2 * bsz, N_HEADS, HEAD_W, HEAD_W), F32)],
        compiler_params=_params(("arbitrary",)),
        name="hgrn2",
    )(z_hg, z_hg, lbp, *tables)


def _cummax_rows(a, reverse):
    n = a.shape[0]
    row = lax.broadcasted_iota(jnp.int32, a.shape, 0)
    k = 1
    while k < n:
        if reverse:
            a_s, ok = pltpu.roll(a, n - k, 0), row < n - k
        else:
            a_s, ok = pltpu.roll(a, k, 0), row >= k
        a = jnp.maximum(a, jnp.where(ok, a_s, -jnp.inf))
        k *= 2
    return a


def _ml_chunk(q, k, v, gi, log_f, tri_sel, tri_mask, ct_ref, m_ref, slot, d, reverse):
    b = _mm_sel(tri_sel, log_f)
    a = gi - b
    m_prev = m_ref[slot:slot + 1, :]
    m_rel = jnp.maximum(_cummax_rows(a, reverse), m_prev)
    w_inter = jnp.exp(m_prev - m_rel)
    exp_neg_m = jnp.exp(-(b + m_rel))
    last = 0 if reverse else CHUNK - 1
    b_end = b[last:last + 1, :]
    m_new = b_end + m_rel[last:last + 1, :]
    w_end = jnp.exp(b_end - b + gi - m_new)
    decay = jnp.exp(b_end + m_prev - m_new)
    m_ref[slot:slot + 1, :] = m_new
    a_t = a.T
    ones = jnp.ones((CHUNK, HEAD_W), BF16)
    outs = []
    for h in range(N_HEADS):
        cols = slice(h * HEAD_W, (h + 1) * HEAD_W)
        qh, kh, vh = q[:, cols], k[:, cols], v[:, cols]
        c = d * N_HEADS + h
        log_w = jnp.where(tri_mask > 0.0, a_t[c:c + 1, :] - m_rel[:, c:c + 1], -jnp.inf)
        s = _mm_nt(qh, kh) * jnp.exp(log_w)
        v_ext = jnp.concatenate([vh.astype(BF16), ones], axis=-1)
        st = slot * N_HEADS + h
        ct = ct_ref[st]
        tot = _mm(s, v_ext) + w_inter[:, c:c + 1] * _mm_nt(qh, ct)
        num, den = tot[:, :HEAD_W], tot[:, HEAD_W:]
        outs.append(num / jnp.maximum(jnp.abs(den), exp_neg_m[:, c:c + 1]))
        ct_ref[st] = decay[:, c:c + 1] * ct + _mm_tn(v_ext, kh * w_end[:, c:c + 1])
    return jnp.concatenate(outs, axis=-1)


def _ml_kernel(zf, zb, gbias, tri_sel, tri_mask, of_ref, ob_ref, ct_ref, m_ref):
    w = BRANCH_W
    n_chunks = TM // CHUNK

    @pl.when(pl.program_id(0) == 0)
    def _():
        ct_ref[...] = jnp.zeros_like(ct_ref)
        m_ref[...] = jnp.zeros_like(m_ref)

    sides = []
    for b in range(zf.shape[0]):
        for d, (z_ref, o_ref) in enumerate(((zf, of_ref), (zb, ob_ref))):
            slot = 2 * b + d
            qk = _silu(z_ref[b, :, 0:2 * w])
            gi = z_ref[b, :, 4 * w:4 * w + HEAD_W] + gbias[0:1, :]
            log_f = _log_sigmoid(z_ref[b, :, 4 * w + HEAD_W:4 * w + 2 * HEAD_W] + gbias[1:2, :])
            sides.append((qk[:, :w] * (HEAD_W ** -0.5), qk[:, w:], z_ref[b, :, 2 * w:3 * w],
                          gi, log_f, o_ref.at[b], slot, d))
    for c in range(n_chunks):
        for q, k, v, gi, log_f, o_ref, slot, d in sides:
            cc = n_chunks - 1 - c if d == 1 else c
            rows = slice(cc * CHUNK, (cc + 1) * CHUNK)
            o_ref[rows, :] = _ml_chunk(q[rows], k[rows], v[rows], gi[rows], log_f[rows],
                                       tri_sel[d], tri_mask[d], ct_ref, m_ref, slot, d, d == 1)


def _ml_call(z_ml, gbias, tri_sel, tri_mask, nc):
    bsz, n, _ = z_ml.shape
    nt = n // TM
    w = BRANCH_W
    fwd = lambda s: s
    bwd = lambda s: _bwd_tile(s, nc, nt)
    return pl.pallas_call(
        _ml_kernel,
        grid=(nt,),
        in_specs=[_all_rows_spec(bsz, W_ML, fwd), _all_rows_spec(bsz, W_ML, bwd),
                  _const_spec((2, HEAD_W)), _const_spec(tri_sel.shape), _const_spec(tri_mask.shape)],
        out_specs=[_all_rows_spec(bsz, w, fwd), _all_rows_spec(bsz, w, bwd)],
        out_shape=[jax.ShapeDtypeStruct((bsz, n, w), F32)] * 2,
        scratch_shapes=[pltpu.VMEM((2 * bsz * N_HEADS, 2 * HEAD_W, HEAD_W), F32),
                        pltpu.VMEM((2 * bsz, HEAD_W), F32)],
        compiler_params=_params(("arbitrary",)),
        name="mlstm",
    )(z_ml, z_ml, gbias, tri_sel, tri_mask)


def _gelu_tanh(x):
    return 0.5 * x * (1.0 + jnp.tanh(0.7978845608028654 * (x + 0.044715 * (x * x * x))))


def _route(logits):
    col = lax.broadcasted_iota(jnp.int32, logits.shape, 1)
    colf = col.astype(F32)
    is_g = jnp.logical_and(col >= N_EXPERTS, col < N_EXPERTS + N_GROUPS)
    is_e = col < N_EXPERTS
    neg = -jnp.inf
    big = 1e9
    gl = jnp.where(is_g, logits, neg)
    g_max = jnp.max(gl, axis=-1, keepdims=True)
    grp = jnp.min(jnp.where(gl == g_max, colf, big), axis=-1, keepdims=True) - N_EXPERTS
    p_grp = 1.0 / jnp.sum(jnp.exp(gl - g_max), axis=-1, keepdims=True)
    col_grp = lax.shift_right_logical(col, EXPERTS_PER_GROUP.bit_length() - 1)
    in_grp = jnp.logical_and(is_e, col_grp.astype(F32) == grp)
    e1 = jnp.where(in_grp, logits, neg)
    top1 = jnp.max(e1, axis=-1, keepdims=True)
    idx1 = jnp.min(jnp.where(e1 == top1, colf, big), axis=-1, keepdims=True)
    e2 = jnp.where(colf == idx1, neg, e1)
    top2 = jnp.max(e2, axis=-1, keepdims=True)
    idx2 = jnp.min(jnp.where(e2 == top2, colf, big), axis=-1, keepdims=True)
    t = jnp.exp(top2 - top1)
    w1 = p_grp / (1.0 + t)
    w2 = p_grp * (t / (1.0 + t))
    out = jnp.where(col == ROUTE_IDX, idx1, jnp.where(col == ROUTE_IDX + 1, idx2, 0.0))
    return jnp.where(col == ROUTE_W, w1, jnp.where(col == ROUTE_W + 1, w2, out))


def _route_onehot(route):
    colf = lax.broadcasted_iota(jnp.int32, route.shape, 1).astype(F32)
    oh1 = colf == route[:, ROUTE_IDX:ROUTE_IDX + 1]
    oh2 = colf == route[:, ROUTE_IDX + 1:ROUTE_IDX + 2]
    return jnp.where(jnp.logical_or(oh1, oh2), 1.0, 0.0), oh1, oh2


def _stage_c_kernel(u_ref, h_ref, mod_ref, rg_y, rg_f, rg_b, gla_g, gla_f, gla_b, hg_g, hg_f, hg_b,
                    ml_o, ml_f, ml_b, gains, w_merge, b_merge, w_branch, w_out, ffn_g, w_route, b_route,
                    h_out, u2_out, wt_out, cnt_out):
    ys = (
        _gelu_tanh(rg_y[...]) * (rg_f[...] + rg_b[...]),
        _head_rmsnorm(gla_f[...] + gla_b[...], gains[0:1, :]) * _silu(gla_g[...]),
        _head_rmsnorm(hg_f[...] + hg_b[...], gains[1:2, :]) * _silu(hg_g[...]),
        _sigmoid(ml_o[...]) * _head_rmsnorm(ml_f[...] + ml_b[...], gains[2:3, :]),
    )
    u = u_ref[...]
    d = u.shape[-1]
    merged = None
    for kk, y in enumerate(ys):
        gate = _sigmoid(jnp.dot(u, w_merge[:, kk * d:(kk + 1) * d], preferred_element_type=F32)
                              + b_merge[:, kk * d:(kk + 1) * d])
        term = gate * _mm(y, w_branch[kk])
        merged = term if merged is None else merged + term
    mix = _mm(merged, w_out[...])
    h_new = h_ref[...] + mod_ref[2:3, :] * mix
    h_out[...] = h_new
    u2 = _rmsnorm_rows(h_new, ffn_g[...]) * (1.0 + mod_ref[4:5, :]) + mod_ref[3:4, :]
    _store_tiled(u2_out, u2)
    logits = _mm_split(u2, w_route[...]) + b_route[...]
    route = _route(logits)
    wt_out[...] = route
    cnt = jnp.sum(_route_onehot(route)[0], axis=0, keepdims=True)
    cnt_out[...] = jnp.broadcast_to(cnt, cnt_out.shape)


def _stage_c_call(u, h, mod, z_rg, rg_f, rg_b, z_gla, gla_f, gla_b, z_hg, hg_f, hg_b, z_ml, ml_f, ml_b,
                  gains, w_merge, b_merge, w_branch, w_out, ffn_g, w_route, b_route, nc, t0):
    bsz, n, d = h.shape
    nt = n // TM
    w = BRANCH_W

    n_out = n - t0 * TM

    def tok(width, col=0):
        return pl.BlockSpec((None, TM, width), lambda b, t: (b, t + t0, col))

    def out_tok(width):
        return pl.BlockSpec((None, TM, width), lambda b, t: (b, t, 0))

    mod_spec = pl.BlockSpec((None, 6, d), lambda b, t: (jnp.where(t + t0 < nc, bsz, b), 0, 0))
    return pl.pallas_call(
        _stage_c_kernel,
        grid=(bsz, nt - t0),
        in_specs=[tok(d), tok(d), mod_spec,
                  tok(w, 1), tok(w), tok(w),
                  tok(w, 3), tok(w), tok(w),
                  tok(w, 4), tok(w), tok(w),
                  tok(w, 3), tok(w), tok(w),
                  _const_spec(gains.shape), _const_spec(w_merge.shape), _const_spec(b_merge.shape),
                  _const_spec(w_branch.shape), _const_spec(w_out.shape), _const_spec(ffn_g.shape),
                  _const_spec(w_route.shape), _const_spec(b_route.shape)],
        out_specs=[out_tok(d),
                   pl.BlockSpec((None, TM * SUBLANES, HEAD_W), lambda b, t: (b, t, 0)),
                   out_tok(HEAD_W),
                   pl.BlockSpec((None, None, SUBLANES, HEAD_W), lambda b, t: (b, t, 0, 0))],
        out_shape=[jax.ShapeDtypeStruct((bsz, n_out, d), F32),
                   jax.ShapeDtypeStruct((bsz, n_out * SUBLANES, HEAD_W), F32),
                   jax.ShapeDtypeStruct((bsz, n_out, HEAD_W), F32),
                   jax.ShapeDtypeStruct((bsz, nt - t0, SUBLANES, HEAD_W), F32)],
        compiler_params=_params(("parallel", "parallel")),
        name="stage_c",
    )(u, h, mod, z_rg, rg_f, rg_b, z_gla, gla_f, gla_b, z_hg, hg_f, hg_b, z_ml, ml_f, ml_b,
      gains, w_merge, b_merge, w_branch, w_out, ffn_g, w_route, b_route)


def _slots_to_smem(slots_vmem, slots_smem, sem):
    cp = pltpu.make_async_copy(slots_vmem, slots_smem, sem)
    cp.start()
    cp.wait()


def _row_copy(src, src_row, dst, dst_row, sem):
    def tile(ref, row):
        return ref.at[pl.ds(pl.multiple_of(row * SUBLANES, SUBLANES), SUBLANES), :]
    return pltpu.make_async_copy(tile(src, src_row), tile(dst, dst_row), sem)


def _to_tiles(x):
    return [x[:, k * HEAD_W:(k + 1) * HEAD_W] for k in range(SUBLANES)]


def _store_tiled(ref, x):
    rows = x.shape[0]
    for k, piece in enumerate(_to_tiles(x)):
        ref[pl.ds(k, rows, stride=SUBLANES), :] = piece


def _load_tiled(ref, rows):
    return jnp.concatenate([ref[pl.ds(k, rows, stride=SUBLANES), :] for k in range(SUBLANES)], axis=-1)


def _dispatch_kernel(seg_ref, route_ref, base_ref, tri_ref, x_ref, xs_hbm, slots_ref,
                     slots_smem, zero_buf, sem_s, sem_r, sem_z):
    @pl.when(pl.program_id(0) == 0)
    def _():
        zero_buf[...] = jnp.zeros_like(zero_buf)

        def fill(e):
            start_row = pl.multiple_of((seg_ref[e] - MOE_BLK) * SUBLANES, SUBLANES)
            return pltpu.make_async_copy(
                zero_buf, xs_hbm.at[pl.ds(start_row, MOE_BLK * SUBLANES), :], sem_z)

        for e in range(N_EXPERTS):
            @pl.when(seg_ref[N_EXPERTS + e] > 0)
            def _(e=e):
                fill(e).start()
        for e in range(N_EXPERTS):
            @pl.when(seg_ref[N_EXPERTS + e] > 0)
            def _(e=e):
                fill(e).wait()

        blk_rows = MOE_BLK * SUBLANES
        used = seg_ref[N_EXPERTS - 1] // MOE_BLK

        def tail(b):
            return pltpu.make_async_copy(
                zero_buf, xs_hbm.at[pl.ds(pl.multiple_of(b * blk_rows, blk_rows), blk_rows), :], sem_z)

        def tail_start(b, carry):
            tail(b).start()
            return carry

        def tail_wait(b, carry):
            tail(b).wait()
            return carry

        n_blk = xs_hbm.shape[0] // blk_rows
        lax.fori_loop(used, n_blk, tail_start, 0)
        lax.fori_loop(used, n_blk, tail_wait, 0)

    route = route_ref[...]
    onehot, oh1, oh2 = _route_onehot(route)
    rank = jnp.dot(tri_ref[...], onehot.astype(BF16), preferred_element_type=F32)
    pos = base_ref[...] + rank
    slot_a = jnp.sum(jnp.where(oh1, pos, 0.0), axis=-1, keepdims=True)
    slot_b = jnp.sum(jnp.where(oh2, pos, 0.0), axis=-1, keepdims=True)
    col = lax.broadcasted_iota(jnp.int32, route.shape, 1)
    both = jnp.where(col == 0, slot_a, jnp.where(col == 1, slot_b, 0.0))
    slots_ref[...] = both.T[0:SUBLANES, :].astype(jnp.int32)
    _slots_to_smem(slots_ref, slots_smem, sem_s)

    def start(t, carry):
        for j in range(2):
            _row_copy(x_ref, t, xs_hbm, slots_smem[j, t], sem_r).start(priority=j)
        return carry

    def wait(t, carry):
        for j in range(2):
            _row_copy(x_ref, 0, xs_hbm, 0, sem_r).wait()
        return carry

    lax.fori_loop(0, TM, start, 0, unroll=8)
    lax.fori_loop(0, TM, wait, 0, unroll=8)


def _experts_kernel(blk_expert, n_used, x_ref, w1, w3, w2, o_ref):
    del blk_expert

    @pl.when(pl.program_id(0) < n_used[0])
    def _():
        x = _load_tiled(x_ref, MOE_BLK)
        h1 = _mm(x, w1[...])
        h3 = _mm(x, w3[...])
        _store_tiled(o_ref, _mm(_silu(h1) * h3, w2[...]))

    @pl.when(pl.program_id(0) >= n_used[0])
    def _():
        o_ref[...] = jnp.zeros_like(o_ref)


def _combine_kernel(slots_smem, route_ref, h_ref, mod_ref, fin_g, ys_hbm, o_ref, buf, sem_r, *, final):
    def start(t, carry):
        for j in range(2):
            _row_copy(ys_hbm, slots_smem[j, t], buf.at[j], t, sem_r).start(priority=j)
        return carry

    def wait(t, carry):
        for j in range(2):
            _row_copy(ys_hbm, 0, buf.at[j], 0, sem_r).wait()
        return carry

    lax.fori_loop(0, TM, start, 0, unroll=8)
    lax.fori_loop(0, TM, wait, 0, unroll=8)
    route = route_ref[...]
    y = (route[:, ROUTE_W:ROUTE_W + 1] * _load_tiled(buf.at[0], TM)
         + route[:, ROUTE_W + 1:ROUTE_W + 2] * _load_tiled(buf.at[1], TM))
    out = h_ref[...] + mod_ref[5:6, :] * y
    if final:
        out = _rmsnorm_rows(out, fin_g[...])
    o_ref[...] = out


def _moe_call(u2, route, cnt, h, mod, fin_g, w1, w3, w2, layer, nc, t0, final):
    bsz, n_out, d = h.shape
    de = w1.shape[-1]
    rows = bsz * n_out
    tiles = rows // TM
    tiles_per_row = n_out // TM
    n_blocks = (2 * rows + N_EXPERTS * (MOE_BLK - 1)) // MOE_BLK + 1

    counts = cnt[:, :, 0, :N_EXPERTS].reshape(tiles, N_EXPERTS)
    total = jnp.sum(counts, axis=0)
    padded = jnp.ceil(total / MOE_BLK) * MOE_BLK
    seg_end = jnp.cumsum(padded)
    base = (seg_end - padded)[None, :] + jnp.cumsum(counts, axis=0) - counts
    base = _pad_cols(base, HEAD_W).reshape(tiles, 1, HEAD_W)
    blk_start = jnp.arange(n_blocks, dtype=F32) * MOE_BLK
    blk_expert = jnp.minimum(jnp.sum(blk_start[:, None] >= seg_end[None, :], axis=1), N_EXPERTS - 1)
    n_used = (seg_end[-1] / MOE_BLK).astype(jnp.int32).reshape(1)
    tri = jnp.asarray(np.tril(np.ones((TM, TM), np.float32), -1), BF16)

    assert d == SUBLANES * HEAD_W
    slot_rows = n_blocks * MOE_BLK * SUBLANES
    seg = jnp.concatenate([seg_end, padded]).astype(jnp.int32)
    xs, slots = pl.pallas_call(
        _dispatch_kernel,
        grid_spec=pltpu.PrefetchScalarGridSpec(
            num_scalar_prefetch=1,
            grid=(tiles,),
            in_specs=[pl.BlockSpec((TM, HEAD_W), lambda t, sg: (t, 0)),
                      pl.BlockSpec((None, 1, HEAD_W), lambda t, sg: (t, 0, 0)),
                      pl.BlockSpec((TM, TM), lambda t, sg: (0, 0)),
                      pl.BlockSpec((TM * SUBLANES, HEAD_W), lambda t, sg: (t, 0))],
            out_specs=[pl.BlockSpec(memory_space=pl.ANY),
                       pl.BlockSpec((SUBLANES, TM), lambda t, sg: (t, 0))],
            scratch_shapes=[pltpu.SMEM((SUBLANES, TM), jnp.int32),
                            pltpu.VMEM((MOE_BLK * SUBLANES, HEAD_W), F32),
                            pltpu.SemaphoreType.DMA, pltpu.SemaphoreType.DMA,
                            pltpu.SemaphoreType.DMA]),
        out_shape=[jax.ShapeDtypeStruct((slot_rows, HEAD_W), F32),
                   jax.ShapeDtypeStruct((tiles * SUBLANES, TM), jnp.int32)],
        compiler_params=_params(("arbitrary",)),
        name="moe_dispatch",
    )(seg, route.reshape(rows, HEAD_W), base, tri, u2.reshape(rows * SUBLANES, HEAD_W))

    blk_spec = pl.BlockSpec((MOE_BLK * SUBLANES, HEAD_W), lambda b, be, nu: (b, 0))
    used_blk_spec = pl.BlockSpec((MOE_BLK * SUBLANES, HEAD_W),
                                 lambda b, be, nu: (jnp.minimum(b, nu[0] - 1), 0))
    ys = pl.pallas_call(
        _experts_kernel,
        grid_spec=pltpu.PrefetchScalarGridSpec(
            num_scalar_prefetch=2,
            grid=(n_blocks,),
            in_specs=[used_blk_spec,
                      pl.BlockSpec((None, None, d, de), lambda b, be, nu: (layer, be[b], 0, 0)),
                      pl.BlockSpec((None, None, d, de), lambda b, be, nu: (layer, be[b], 0, 0)),
                      pl.BlockSpec((None, None, de, d), lambda b, be, nu: (layer, be[b], 0, 0))],
            out_specs=blk_spec),
        out_shape=jax.ShapeDtypeStruct((slot_rows, HEAD_W), F32),
        compiler_params=_params(("arbitrary",)),
        name="moe_experts",
    )(blk_expert.astype(jnp.int32), n_used, xs, w1, w3, w2)

    def mod_index(t):
        b, pos = t // tiles_per_row, t % tiles_per_row
        return (jnp.where(pos + t0 < nc, bsz, b), 0, 0)

    out = pl.pallas_call(
        functools.partial(_combine_kernel, final=final),
        grid=(tiles,),
        in_specs=[pl.BlockSpec((None, SUBLANES, TM), lambda t: (t, 0, 0), memory_space=pltpu.SMEM),
                  pl.BlockSpec((TM, HEAD_W), lambda t: (t, 0)),
                  pl.BlockSpec((TM, d), lambda t: (t, 0)),
                  pl.BlockSpec((None, 6, d), mod_index),
                  _const_spec((1, d)),
                  pl.BlockSpec(memory_space=pl.ANY)],
        out_specs=pl.BlockSpec((TM, d), lambda t: (t, 0)),
        out_shape=jax.ShapeDtypeStruct((rows, d), F32),
        scratch_shapes=[pltpu.VMEM((2, TM * SUBLANES, HEAD_W), F32),
                        pltpu.SemaphoreType.DMA],
        compiler_params=_params(("arbitrary",)),
        name="moe_combine",
    )(slots.reshape(tiles, SUBLANES, TM), route.reshape(rows, HEAD_W), h.reshape(rows, d), mod, fin_g, ys)
    return out.reshape(bsz, n_out, d)


def _pad_heads(w, dk):
    lead = w.shape[:-1]
    w = w.reshape(lead + (N_HEADS, dk))
    w = jnp.pad(w, [(0, 0)] * len(lead) + [(0, 0), (0, HEAD_W - dk)])
    return w.reshape(lead + (N_HEADS * HEAD_W,))


def _pad_cols(w, width):
    return jnp.pad(w, [(0, 0)] * (w.ndim - 1) + [(0, width - w.shape[-1])])


def _split_w_in(w_in):
    bw = BRANCH_W
    sizes = [bw, bw, N_HEADS * GLA_DK, N_HEADS * GLA_DK, bw, bw, 2 * GLA_RANK,
             bw, bw, 2 * bw, bw, bw, bw, bw, bw, 16]
    offs = np.cumsum([0] + sizes)
    p = [w_in[:, offs[i]:offs[i + 1]] for i in range(len(sizes))]
    w_rg = jnp.concatenate([p[0], p[1]], axis=1)
    w_gla = jnp.concatenate([_pad_heads(p[2], GLA_DK), _pad_heads(p[3], GLA_DK), p[4], p[5],
                             _pad_cols(p[6], HEAD_W)], axis=1)
    w_hg = jnp.concatenate([p[7], p[8], p[9], p[10]], axis=1)
    gates = p[15].reshape(-1, 2, 2, N_HEADS)
    w_gi = _pad_cols(gates[:, :, 0].reshape(-1, 2 * N_HEADS), HEAD_W)
    w_gf = _pad_cols(gates[:, :, 1].reshape(-1, 2 * N_HEADS), HEAD_W)
    w_ml = jnp.concatenate([p[11], p[12], p[13], p[14], w_gi, w_gf], axis=1)
    return [w.astype(BF16) for w in (w_rg, w_gla, w_hg, w_ml)]


def _block_diag(w):
    k, n = w.shape[-3], w.shape[-1]
    eye = jnp.eye(k, dtype=w.dtype)
    full = jnp.einsum('...kij,kl->...kilj', w, eye)
    return full.reshape(w.shape[:-3] + (k * n, k * n))


def kernel(x, c, ctx, c_ctx, ada_w, ada_b, norm_mix_g, norm_ffn_g, w_in, rg_conv_w, rg_conv_b, rg_gate_w, rg_gate_b, rg_lambda, gla_w_lr, gla_b_lr, gla_norm_g, hgrn_lb_logits, hgrn_norm_g, ml_conv_w, ml_conv_b, ml_gate_b, ml_norm_g, w_branch, w_merge, b_merge, w_out, moe_w_group, moe_b_group, moe_w_expert, moe_b_expert, moe_w1, moe_w3, moe_w2, final_norm_g):
    bsz, seq, d = x.shape
    n_ctx = ctx.shape[1]
    depth = ada_w.shape[0]
    assert n_ctx % TM == 0 and seq % TM == 0 and d == 2 * BRANCH_W
    nc = n_ctx // TM

    h = jnp.concatenate([ctx, x], axis=1)
    cvec = jnp.zeros((SUBLANES, d), F32).at[:bsz].set(c).at[bsz].set(c_ctx)
    mod_all = _mod_call(cvec, ada_w, ada_b).reshape(depth, SUBLANES, 6, d)[:, :bsz + 1]

    lb_cum = jnp.cumsum(jax.nn.softmax(hgrn_lb_logits.astype(F32), axis=0), axis=0)
    hgrn_lb = lb_cum - lb_cum[:1]
    tables = [t for levels in SPLIT_LEVELS for t in _decay_tables(levels)]
    tri_sel = tables[0][:, :CHUNK, :]
    tri_mask = tables[1][:, 0]

    out = None
    for l in range(depth):
        last = l == depth - 1
        mod = mod_all[l]
        w_rg, w_gla, w_hg, w_ml = _split_w_in(w_in[l])
        z_rg, z_gla, z_hg, z_ml, u = _stage_a_call(
            h, mod, norm_mix_g[l], w_rg, w_gla, w_hg, w_ml, rg_conv_w[l], rg_conv_b[l].reshape(1, -1),
            ml_conv_w[l], ml_conv_b[l].reshape(1, -1), nc)

        rg_f, rg_b = _rg_call(z_rg, _block_diag(rg_gate_w[l]).astype(BF16),
                              rg_gate_b[l].reshape(4, -1), rg_lambda[l], nc)

        wlr = jnp.zeros((2, HEAD_W, N_HEADS * HEAD_W), F32)
        wlr_p = _pad_heads(gla_w_lr[l], GLA_DK)
        wlr = wlr.at[0, :GLA_RANK].set(wlr_p[0]).at[1, GLA_RANK:2 * GLA_RANK].set(wlr_p[1])
        gla_f, gla_b = _gla_call(z_gla, wlr.astype(BF16), _pad_heads(gla_b_lr[l], GLA_DK), tables, nc)

        lb = hgrn_lb[l]
        lbp = jnp.concatenate([lb, jnp.log(lb), jnp.log1p(-lb), jnp.zeros((2, lb.shape[-1]), F32)], axis=0)
        hg_f, hg_b = _hg_call(z_hg, lbp, tables, nc)

        gbias = _pad_cols(ml_gate_b[l].transpose(1, 0, 2).reshape(2, -1), HEAD_W)
        ml_f, ml_b = _ml_call(z_ml, gbias, tri_sel, tri_mask, nc)

        gains = jnp.zeros((SUBLANES, HEAD_W), F32).at[0].set(gla_norm_g[l]).at[1].set(hgrn_norm_g[l]).at[2].set(ml_norm_g[l])
        w_route = _split_cols(_pad_cols(jnp.concatenate([moe_w_expert[l], moe_w_group[l]], axis=1), HEAD_W))
        b_route = _pad_cols(jnp.concatenate([moe_b_expert[l], moe_b_group[l]]).reshape(1, -1), HEAD_W)
        t0 = nc if last else 0
        h_mid, u2, route, cnt = _stage_c_call(
            u, h, mod, z_rg, rg_f, rg_b, z_gla, gla_f, gla_b, z_hg, hg_f, hg_b, z_ml, ml_f, ml_b,
            gains, w_merge[l].astype(BF16), b_merge[l].reshape(1, -1), w_branch[l].astype(BF16),
            w_out[l].astype(BF16), norm_ffn_g[l].reshape(1, -1), w_route, b_route, nc, t0)

        res = _moe_call(u2, route, cnt, h_mid, mod, final_norm_g.reshape(1, -1), moe_w1,
                        moe_w3, moe_w2, l, nc, t0, last)
        if last:
            out = res
        else:
            h = res
    return out
```

```python
import functools

import numpy as np
import jax
import jax.numpy as jnp
from jax import lax
from jax.experimental import pallas as pl
from jax.experimental.pallas import tpu as pltpu

F32 = jnp.float32
BF16 = jnp.bfloat16

EPS = 1e-6
TM = 256
CHUNK = 64
SUBLANES = 8
N_HEADS = 4
HEAD_W = 128
BRANCH_W = 512
CONV_W = 4
CONV_LEFT = 2
RG_C = 8.0
GLA_DK = 64
GLA_RANK = 16
GLA_GATE_NORM = 16.0
N_GROUPS = 4
EXPERTS_PER_GROUP = 4
N_EXPERTS = 16
MOE_BLK = 512
ROUTE_IDX = 16
ROUTE_W = 18
N_LEVELS = 6
MILD_DECAY = -40.0
VMEM_LIMIT = 56 * 1024 * 1024

W_RG = 2 * BRANCH_W
W_GLA = 4 * BRANCH_W + HEAD_W
W_HG = 5 * BRANCH_W
W_ML = 4 * BRANCH_W + 2 * HEAD_W
SPLIT_LEVELS = (0, 2, N_LEVELS)


def _mm(a, b):
    return jnp.dot(a.astype(BF16), b.astype(BF16), preferred_element_type=F32)


def _mm_nt(a, b):
    return lax.dot_general(a.astype(BF16), b.astype(BF16), (((1,), (1,)), ((), ())),
                           preferred_element_type=F32)


def _mm_tn(a, b):
    return lax.dot_general(a.astype(BF16), b.astype(BF16), (((0,), (0,)), ((), ())),
                           preferred_element_type=F32)


def _mm_sel(sel, x):
    x1 = x.astype(BF16)
    x2 = (x - x1.astype(F32)).astype(BF16)
    dot = functools.partial(jnp.dot, preferred_element_type=F32)
    return dot(sel, x1) + dot(sel, x2)


def _split_cols(w):
    hi = w.astype(BF16)
    return jnp.concatenate([hi, (w - hi.astype(F32)).astype(BF16)], axis=1)


def _mm_split(x, w_split):
    n = w_split.shape[1] // 2
    x1 = x.astype(BF16)
    x2 = (x - x1.astype(F32)).astype(BF16)
    p = jnp.dot(x1, w_split, preferred_element_type=F32)
    return p[:, :n] + p[:, n:] + jnp.dot(x2, w_split[:, :n], preferred_element_type=F32)


def _softplus_neg_abs(x):
    return jnp.log(1.0 + jnp.exp(-jnp.abs(x)))


def _log_sigmoid(x):
    return jnp.minimum(x, 0.0) - _softplus_neg_abs(x)


def _sigmoid(x):
    return 0.5 * jnp.tanh(0.5 * x) + 0.5


def _silu(x):
    return x * _sigmoid(x)


def _rmsnorm_rows(x, g):
    return x * lax.rsqrt(jnp.mean(x * x, axis=-1, keepdims=True) + EPS) * g


def _head_rmsnorm(o, g):
    parts = [_rmsnorm_rows(o[:, h * HEAD_W:(h + 1) * HEAD_W], g) for h in range(N_HEADS)]
    return jnp.concatenate(parts, axis=-1)


def _bwd_tile(s, nc, nt):
    return jnp.where(s < nc, nc - 1 - s, nt - 1 - (s - nc))


def _const_spec(shape):
    nd = len(shape)
    return pl.BlockSpec(shape, lambda *_: (0,) * nd)


def _params(sem):
    return pltpu.CompilerParams(dimension_semantics=sem, vmem_limit_bytes=VMEM_LIMIT)


def _mod_kernel(c_ref, w_ref, b_ref, o_ref):
    cv = _silu(c_ref[...])
    o_ref[...] = jnp.dot(cv, w_ref[...], precision=lax.Precision.HIGHEST,
                         preferred_element_type=F32) + b_ref[...]


def _mod_call(cvec, ada_w, ada_b):
    depth, d, six_d = ada_w.shape
    tn = 1024
    return pl.pallas_call(
        _mod_kernel,
        grid=(depth, six_d // tn),
        in_specs=[pl.BlockSpec((SUBLANES, d), lambda l, j: (0, 0)),
                  pl.BlockSpec((None, d, tn), lambda l, j: (l, 0, j)),
                  pl.BlockSpec((None, 1, tn), lambda l, j: (l, 0, j))],
        out_specs=pl.BlockSpec((None, SUBLANES, tn), lambda l, j: (l, 0, j)),
        out_shape=jax.ShapeDtypeStruct((depth, SUBLANES, six_d), F32),
        compiler_params=_params(("parallel", "parallel")),
        name="adaln_mod",
    )(cvec, ada_w, ada_b.reshape(depth, 1, six_d))


def _stage_a_kernel(h_ref, mod_ref, g_ref, w_rg, w_gla, w_hg, w_ml,
                    z_rg, z_gla, z_hg, z_ml, u_ref):
    x = h_ref[...]
    u = _rmsnorm_rows(x, g_ref[...]) * (1.0 + mod_ref[1:2, :]) + mod_ref[0:1, :]
    ub = u.astype(BF16)
    u_ref[...] = ub
    z_rg[...] = jnp.dot(ub, w_rg[...], preferred_element_type=F32)
    z_gla[...] = jnp.dot(ub, w_gla[...], preferred_element_type=F32)
    z_hg[...] = jnp.dot(ub, w_hg[...], preferred_element_type=F32)
    z_ml[...] = jnp.dot(ub, w_ml[...], preferred_element_type=F32)


def _stage_a_call(h, mod, norm_g, w_rg, w_gla, w_hg, w_ml, nc):
    bsz, n, d = h.shape
    tm = TM

    def tok(w):
        return pl.BlockSpec((None, tm, w), lambda b, t: (b, t, 0))

    def weight(w):
        return pl.BlockSpec((d, w), lambda b, t: (0, 0), pipeline_mode=pl.Buffered(1))

    return pl.pallas_call(
        _stage_a_kernel,
        grid=(bsz, n // tm),
        in_specs=[tok(d),
                  pl.BlockSpec((None, 6, d), lambda b, t: (jnp.where(t < nc, bsz, b), 0, 0)),
                  _const_spec((1, d)),
                  weight(W_RG), weight(W_GLA), weight(W_HG), weight(W_ML)],
        out_specs=[tok(W_RG), tok(W_GLA), tok(W_HG), tok(W_ML), tok(d)],
        out_shape=[jax.ShapeDtypeStruct((bsz, n, W_RG), F32),
                   jax.ShapeDtypeStruct((bsz, n, W_GLA), F32),
                   jax.ShapeDtypeStruct((bsz, n, W_HG), F32),
                   jax.ShapeDtypeStruct((bsz, n, W_ML), F32),
                   jax.ShapeDtypeStruct((bsz, n, d), BF16)],
        compiler_params=_params(("parallel", "parallel")),
        name="stage_a",
    )(h, mod, norm_g.reshape(1, d), w_rg, w_gla, w_hg, w_ml)


def _conv_tile(x_ref, prev_ref, next_ref, pad_ref, cw_ref, cb_ref, tile, nc, nt):
    prev_ok = jnp.logical_and(tile != 0, tile != nc)
    next_ok = jnp.logical_and(tile != nc - 1, tile != nt - 1)
    pad_ref[0:SUBLANES, :] = jnp.where(prev_ok, prev_ref[...], 0.0)
    pad_ref[SUBLANES:SUBLANES + TM, :] = x_ref[...]
    pad_ref[SUBLANES + TM:2 * SUBLANES + TM, :] = jnp.where(next_ok, next_ref[...], 0.0)
    acc = cb_ref[...]
    for j in range(CONV_W):
        off = SUBLANES - CONV_LEFT + j
        acc = acc + cw_ref[j:j + 1, :] * pad_ref[off:off + TM, :]
    return acc


def _halo_specs(width, col_block, tile_fn, n):
    rows = TM // SUBLANES
    last = n // SUBLANES - 1
    cur = pl.BlockSpec((None, TM, width), lambda b, s: (b, tile_fn(s), col_block))
    prev = pl.BlockSpec((None, SUBLANES, width),
                        lambda b, s: (b, jnp.maximum(tile_fn(s) * rows - 1, 0), col_block))
    nxt = pl.BlockSpec((None, SUBLANES, width),
                       lambda b, s: (b, jnp.minimum((tile_fn(s) + 1) * rows, last), col_block))
    return [cur, prev, nxt]


def _scan_rows8(a, b, reverse):
    n = a.shape[0]
    pos = jnp.bitwise_and(lax.broadcasted_iota(jnp.int32, a.shape, 0), SUBLANES - 1)
    k = 1
    while k < SUBLANES:
        if reverse:
            a_s, b_s, ok = pltpu.roll(a, n - k, 0), pltpu.roll(b, n - k, 0), pos < SUBLANES - k
        else:
            a_s, b_s, ok = pltpu.roll(a, k, 0), pltpu.roll(b, k, 0), pos >= k
        b = b + a * jnp.where(ok, b_s, 0.0)
        a = a * jnp.where(ok, a_s, 1.0)
        k *= 2
    return a, b


def _scan_tile(a, b, carry, o_ref, reverse):
    a8, b8 = _scan_rows8(a, b, reverse)
    groups = a.shape[0] // SUBLANES
    order = range(groups - 1, -1, -1) if reverse else range(groups)
    for r in order:
        rows = slice(r * SUBLANES, (r + 1) * SUBLANES)
        h = b8[rows] + a8[rows] * carry
        o_ref[rows, :] = h
        carry = h[0:1, :] if reverse else h[SUBLANES - 1:SUBLANES, :]
    return carry


def _rg_kernel(xf, xf_p, xf_n, xb, xb_p, xb_n, cw, cb, gw, gb, lam,
               hf_ref, hb_ref, pad_ref, carry_ref, *, nc, nt):
    s = pl.program_id(1)

    @pl.when(s == 0)
    def _():
        carry_ref[...] = jnp.zeros_like(carry_ref)

    dirs = ((xf, xf_p, xf_n, hf_ref, s, False),
            (xb, xb_p, xb_n, hb_ref, _bwd_tile(s, nc, nt), True))
    for d, (x_ref, p_ref, n_ref, o_ref, tile, reverse) in enumerate(dirs):
        x = _conv_tile(x_ref, p_ref, n_ref, pad_ref, cw, cb, tile, nc, nt)
        r = _sigmoid(_mm(x, gw[d, 0]) + gb[2 * d:2 * d + 1, :])
        i = _sigmoid(_mm(x, gw[d, 1]) + gb[2 * d + 1:2 * d + 2, :])
        lam_d = lam[d:d + 1, :]
        softplus = jnp.maximum(-lam_d, 0.0) + jnp.log1p(jnp.exp(-jnp.abs(lam_d)))
        log_a = -RG_C * r * softplus
        a = jnp.exp(log_a)
        t = jnp.tanh(log_a)
        bt = jnp.sqrt(-2.0 * t / (1.0 - t)) * (i * x)
        carry_ref[d:d + 1, :] = _scan_tile(a, bt, carry_ref[d:d + 1, :], o_ref, reverse)


def _rg_call(z_rg, cw, cb, gw, gb, lam, nc):
    bsz, n, _ = z_rg.shape
    nt = n // TM
    w = BRANCH_W
    fwd = lambda s: s
    bwd = lambda s: _bwd_tile(s, nc, nt)
    out = lambda fn: pl.BlockSpec((None, TM, w), lambda b, s: (b, fn(s), 0))
    return pl.pallas_call(
        functools.partial(_rg_kernel, nc=nc, nt=nt),
        grid=(bsz, nt),
        in_specs=_halo_specs(w, 0, fwd, n) + _halo_specs(w, 0, bwd, n) + [
            _const_spec((CONV_W, w)), _const_spec((1, w)),
            _const_spec((2, 2, w, w)), _const_spec((4, w)), _const_spec((2, w))],
        out_specs=[out(fwd), out(bwd)],
        out_shape=[jax.ShapeDtypeStruct((bsz, n, w), F32)] * 2,
        scratch_shapes=[pltpu.VMEM((TM + 2 * SUBLANES, w), F32), pltpu.VMEM((2, w), F32)],
        compiler_params=_params(("parallel", "arbitrary")),
        name="rglru",
    )(z_rg, z_rg, z_rg, z_rg, z_rg, z_rg, cw, cb, gw, gb, lam)


def _decay_tables(levels):
    n = CHUNK
    blk = n >> levels
    sel = np.zeros((levels + 3, n, n), np.float32)
    masks = np.zeros((levels + 1, n, n), np.float32)
    for lvl in range(levels):
        half = n >> (lvl + 1)
        for r in range(n):
            start = (r // (2 * half)) * 2 * half
            ref = start + half - 1
            if r - start >= half:
                sel[lvl, r, ref + 1:r + 1] = 1.0
                masks[lvl, r, start:start + half] = 1.0
            else:
                sel[lvl, r, r + 1:ref + 1] = 1.0
    sel[levels] = np.tril(np.ones((n, n), np.float32))
    sel[levels + 1] = 1.0 - sel[levels]
    for r in range(n):
        start = (r // blk) * blk
        sel[levels + 2, r, r + 1:start + blk] = 1.0
        masks[levels, r, start:r + 1] = 1.0
    if levels == 0 or levels == N_LEVELS:
        sel = sel[:levels + 2]
    sel_b = sel[:, ::-1, ::-1].reshape(-1, n)
    masks_b = masks[:, ::-1, ::-1]
    return (jnp.asarray(np.stack([sel.reshape(-1, n), sel_b]), BF16),
            jnp.asarray(np.stack([masks, masks_b]), F32))


def _ref_rows(levels, reverse):
    def runs(block, offset):
        return [(s + offset, block) for s in range(0, CHUNK, block)]

    out = [runs(CHUNK >> lvl, (CHUNK >> (lvl + 1)) - 1) for lvl in range(levels)]
    out.append(runs(CHUNK, CHUNK - 1))
    out.append(runs(CHUNK >> levels, (CHUNK >> levels) - 1))
    if reverse:
        out = [[(CHUNK - 1 - r, n) for r, n in reversed(segs)] for segs in out]
    return out


def _rows_bcast(b, segs):
    return jnp.concatenate([jnp.broadcast_to(b[r:r + 1, :], (n, b.shape[-1])) for r, n in segs], axis=0)


def _gla_chunk(q, k, v, g, sel, masks, st_ref, d, reverse, levels):
    if levels == N_LEVELS:
        x = jnp.exp(_mm_sel(sel, g))
        x_lvl = [x[lvl * CHUNK:(lvl + 1) * CHUNK] for lvl in range(levels)]
        x_cum = x[levels * CHUNK:(levels + 1) * CHUNK]
        x_rest = x[(levels + 1) * CHUNK:(levels + 2) * CHUNK]
        xk_blk = xq_blk = None
    else:
        b = _mm_sel(sel[levels * CHUNK:(levels + 1) * CHUNK], g)
        refs = _ref_rows(levels, reverse)
        x_cum = jnp.exp(b)
        x_lvl = [jnp.exp(-jnp.abs(b - _rows_bcast(b, segs))) for segs in refs[:levels]]
        x_rest = jnp.exp(-jnp.abs(b - _rows_bcast(b, refs[levels])))
        if levels == 0:
            xk_blk, xq_blk = x_rest, jnp.exp(jnp.abs(b - _rows_bcast(b, refs[levels])))
        else:
            d_blk = jnp.abs(b - _rows_bcast(b, refs[levels + 1]))
            xk_blk, xq_blk = jnp.exp(-d_blk), jnp.exp(d_blk)
    outs = []
    for h in range(N_HEADS):
        cols = slice(h * HEAD_W, (h + 1) * HEAD_W)
        qh, kh, vh = q[:, cols], k[:, cols], v[:, cols]
        kx = kh * x_rest[:, cols]
        if xk_blk is None:
            sc = _mm_nt(qh, kh)
        elif levels == 0:
            sc = _mm_nt(qh * xq_blk[:, cols], kx)
        else:
            sc = _mm_nt(qh * xq_blk[:, cols], kh * xk_blk[:, cols])
        sc = jnp.where(masks[levels] > 0.0, sc, 0.0)
        for lvl in range(levels):
            xl = x_lvl[lvl][:, cols]
            sc = sc + jnp.where(masks[lvl] > 0.0, _mm_nt(qh * xl, kh * xl), 0.0)
        st = st_ref[d, h]
        outs.append(_mm(sc, vh) + _mm_nt(qh * x_cum[:, cols], st))
        x_end = x_cum[0:1, cols] if reverse else x_cum[CHUNK - 1:CHUNK, cols]
        st_ref[d, h] = st * x_end + _mm_tn(vh, kx)
    return jnp.concatenate(outs, axis=-1)


def _gla_step(sides, tabs, st_ref):
    n_chunks = TM // CHUNK

    def min_block_sum(block):
        mins = [jnp.min(jnp.sum(side[3].reshape(TM // block, block, side[3].shape[-1]), axis=1))
                for side in sides]
        return functools.reduce(jnp.minimum, mins)

    conds = []
    taken = None
    for levels in SPLIT_LEVELS[:-1]:
        ok = min_block_sum(CHUNK >> levels) > MILD_DECAY
        conds.append(ok if taken is None else jnp.logical_and(ok, jnp.logical_not(taken)))
        taken = ok if taken is None else jnp.logical_or(taken, ok)
    conds.append(jnp.logical_not(taken))

    for levels, cond, (sel_ref, mask_ref) in zip(SPLIT_LEVELS, conds, tabs):
        @pl.when(cond)
        def _(levels=levels, sel_ref=sel_ref, mask_ref=mask_ref):
            for c in range(n_chunks):
                for i, (q, k, v, g, o_ref, reverse) in enumerate(sides):
                    cc = n_chunks - 1 - c if reverse else c
                    rows = slice(cc * CHUNK, (cc + 1) * CHUNK)
                    o_ref[rows, :] = _gla_chunk(q[rows], k[rows], v[rows], g[rows],
                                                sel_ref[int(reverse)], mask_ref[int(reverse)],
                                                st_ref, i, reverse, levels)


def _pairs(refs):
    return tuple(zip(refs[0::2], refs[1::2]))


def _gla_kernel(zf, zb, wlr, blr, *rest):
    tabs, (of_ref, ob_ref, st_ref) = _pairs(rest[:-3]), rest[-3:]
    w = BRANCH_W

    @pl.when(pl.program_id(0) == 0)
    def _():
        st_ref[...] = jnp.zeros_like(st_ref)

    sides = []
    for b in range(zf.shape[0]):
        for d, (z_ref, o_ref) in enumerate(((zf, of_ref), (zb, ob_ref))):
            pre = _mm(z_ref[b, :, 4 * w:4 * w + HEAD_W], wlr[d]) + blr[d:d + 1, :]
            g = _log_sigmoid(pre) * (1.0 / GLA_GATE_NORM)
            q = z_ref[b, :, 0:w] * (GLA_DK ** -0.5)
            sides.append((q, z_ref[b, :, w:2 * w], z_ref[b, :, 2 * w:3 * w], g, o_ref.at[b], d == 1))
    _gla_step(sides, tabs, st_ref)


def _hg_kernel(zf, zb, lbp, *rest):
    tabs, (of_ref, ob_ref, st_ref) = _pairs(rest[:-3]), rest[-3:]
    w = BRANCH_W

    @pl.when(pl.program_id(0) == 0)
    def _():
        st_ref[...] = jnp.zeros_like(st_ref)

    sides = []
    for b in range(zf.shape[0]):
        for d, (z_ref, o_ref) in enumerate(((zf, of_ref), (zb, ob_ref))):
            f = z_ref[b, :, (2 + d) * w:(3 + d) * w]
            lb = lbp[d:d + 1, :]
            log_lb = lbp[2 + d:3 + d, :]
            log_1m = lbp[4 + d:5 + d, :]
            c = log_1m + _log_sigmoid(f)
            g = jnp.maximum(log_lb, c) + _softplus_neg_abs(log_lb - c)
            k = (1.0 - lb) * _sigmoid(-f)
            q = _silu(z_ref[b, :, 0:w]) * (HEAD_W ** -0.5)
            sides.append((q, k, z_ref[b, :, w:2 * w], g, o_ref.at[b], d == 1))
    _gla_step(sides, tabs, st_ref)


def _mixer_specs(widths_cols, tile_fn):
    return [pl.BlockSpec((None, TM, w), lambda b, s, c=c: (b, tile_fn(s), c)) for w, c in widths_cols]


def _all_rows_spec(bsz, width, tile_fn):
    return pl.BlockSpec((bsz, TM, width), lambda s: (0, tile_fn(s), 0))


def _gla_call(z_gla, wlr, blr, tables, nc):
    bsz, n, _ = z_gla.shape
    nt = n // TM
    w = BRANCH_W
    fwd = lambda s: s
    bwd = lambda s: _bwd_tile(s, nc, nt)
    return pl.pallas_call(
        _gla_kernel,
        grid=(nt,),
        in_specs=[_all_rows_spec(bsz, W_GLA, fwd), _all_rows_spec(bsz, W_GLA, bwd),
                  _const_spec(wlr.shape), _const_spec(blr.shape)] + [_const_spec(t.shape) for t in tables],
        out_specs=[_all_rows_spec(bsz, w, fwd), _all_rows_spec(bsz, w, bwd)],
        out_shape=[jax.ShapeDtypeStruct((bsz, n, w), F32)] * 2,
        scratch_shapes=[pltpu.VMEM((2 * bsz, N_HEADS, HEAD_W, HEAD_W), F32)],
        compiler_params=_params(("arbitrary",)),
        name="gla",
    )(z_gla, z_gla, wlr, blr, *tables)


def _hg_call(z_hg, lbp, tables, nc):
    bsz, n, _ = z_hg.shape
    nt = n // TM
    w = BRANCH_W
    fwd = lambda s: s
    bwd = lambda s: _bwd_tile(s, nc, nt)
    return pl.pallas_call(
        _hg_kernel,
        grid=(nt,),
        in_specs=[_all_rows_spec(bsz, 4 * w, fwd), _all_rows_spec(bsz, 4 * w, bwd),
                  _const_spec(lbp.shape)] + [_const_spec(t.shape) for t in tables],
        out_specs=[_all_rows_spec(bsz, w, fwd), _all_rows_spec(bsz, w, bwd)],
        out_shape=[jax.ShapeDtypeStruct((bsz, n, w), F32)] * 2,
        scratch_shapes=[pltpu.VMEM((2 * bsz, N_HEADS, HEAD_W, HEAD_W), F32)],
        compiler_params=_params(("arbitrary",)),
        name="hgrn2",
    )(z_hg, z_hg, lbp, *tables)


def _cummax_rows(a, reverse):
    n = a.shape[0]
    row = lax.broadcasted_iota(jnp.int32, a.shape, 0)
    k = 1
    while k < n:
        if reverse:
            a_s, ok = pltpu.roll(a, n - k, 0), row < n - k
        else:
            a_s, ok = pltpu.roll(a, k, 0), row >= k
        a = jnp.maximum(a, jnp.where(ok, a_s, -jnp.inf))
        k *= 2
    return a


def _ml_chunk(q, k, v, gi, log_f, tri_sel, tri_mask, ct_ref, m_ref, slot, d, reverse):
    b = _mm_sel(tri_sel, log_f)
    a = gi - b
    m_prev = m_ref[slot:slot + 1, :]
    m_rel = jnp.maximum(_cummax_rows(a, reverse), m_prev)
    w_inter = jnp.exp(m_prev - m_rel)
    exp_neg_m = jnp.exp(-(b + m_rel))
    last = 0 if reverse else CHUNK - 1
    b_end = b[last:last + 1, :]
    m_new = b_end + m_rel[last:last + 1, :]
    w_end = jnp.exp(b_end - b + gi - m_new)
    decay = jnp.exp(b_end + m_prev - m_new)
    m_ref[slot:slot + 1, :] = m_new
    a_t = a.T
    ones = jnp.ones((CHUNK, HEAD_W), BF16)
    outs = []
    for h in range(N_HEADS):
        cols = slice(h * HEAD_W, (h + 1) * HEAD_W)
        qh, kh, vh = q[:, cols], k[:, cols], v[:, cols]
        c = d * N_HEADS + h
        log_w = jnp.where(tri_mask > 0.0, a_t[c:c + 1, :] - m_rel[:, c:c + 1], -jnp.inf)
        s = _mm_nt(qh, kh) * jnp.exp(log_w)
        v_ext = jnp.concatenate([vh.astype(BF16), ones], axis=-1)
        st = slot * N_HEADS + h
        ct = ct_ref[st]
        tot = _mm(s, v_ext) + w_inter[:, c:c + 1] * _mm_nt(qh, ct)
        num, den = tot[:, :HEAD_W], tot[:, HEAD_W:]
        outs.append(num / jnp.maximum(jnp.abs(den), exp_neg_m[:, c:c + 1]))
        ct_ref[st] = decay[:, c:c + 1] * ct + _mm_tn(v_ext, kh * w_end[:, c:c + 1])
    return jnp.concatenate(outs, axis=-1)


def _ml_kernel(zf, zf_p, zf_n, zb, zb_p, zb_n, cw, cb, gbias,
               tri_sel, tri_mask, of_ref, ob_ref, pad_ref, ct_ref, m_ref, *, nc, nt):
    s = pl.program_id(0)
    w = BRANCH_W
    n_chunks = TM // CHUNK

    @pl.when(s == 0)
    def _():
        ct_ref[...] = jnp.zeros_like(ct_ref)
        m_ref[...] = jnp.zeros_like(m_ref)

    dirs = ((zf, zf_p, zf_n, of_ref, s), (zb, zb_p, zb_n, ob_ref, _bwd_tile(s, nc, nt)))
    sides = []
    for b in range(zf.shape[0]):
        for d, (z_ref, p_ref, nx_ref, o_ref, tile) in enumerate(dirs):
            slot = 2 * b + d
            qk = _silu(_conv_tile(z_ref.at[b, :, 0:2 * w], p_ref.at[b], nx_ref.at[b],
                                  pad_ref.at[slot], cw, cb, tile, nc, nt))
            gi = z_ref[b, :, 4 * w:4 * w + HEAD_W] + gbias[0:1, :]
            log_f = _log_sigmoid(z_ref[b, :, 4 * w + HEAD_W:4 * w + 2 * HEAD_W] + gbias[1:2, :])
            sides.append((qk[:, :w] * (HEAD_W ** -0.5), qk[:, w:], z_ref[b, :, 2 * w:3 * w],
                          gi, log_f, o_ref.at[b], slot, d))
    for c in range(n_chunks):
        for q, k, v, gi, log_f, o_ref, slot, d in sides:
            cc = n_chunks - 1 - c if d == 1 else c
            rows = slice(cc * CHUNK, (cc + 1) * CHUNK)
            o_ref[rows, :] = _ml_chunk(q[rows], k[rows], v[rows], gi[rows], log_f[rows],
                                       tri_sel[d], tri_mask[d], ct_ref, m_ref, slot, d, d == 1)


def _ml_call(z_ml, cw, cb, gbias, tri_sel, tri_mask, nc):
    bsz, n, _ = z_ml.shape
    nt = n // TM
    w = BRANCH_W
    fwd = lambda s: s
    bwd = lambda s: _bwd_tile(s, nc, nt)
    rows8 = TM // SUBLANES
    last8 = n // SUBLANES - 1

    def side(fn):
        return [_all_rows_spec(bsz, W_ML, fn),
                pl.BlockSpec((bsz, SUBLANES, 2 * w), lambda s: (0, jnp.maximum(fn(s) * rows8 - 1, 0), 0)),
                pl.BlockSpec((bsz, SUBLANES, 2 * w), lambda s: (0, jnp.minimum((fn(s) + 1) * rows8, last8), 0))]

    return pl.pallas_call(
        functools.partial(_ml_kernel, nc=nc, nt=nt),
        grid=(nt,),
        in_specs=side(fwd) + side(bwd) + [
            _const_spec((CONV_W, 2 * w)), _const_spec((1, 2 * w)), _const_spec((2, HEAD_W)),
            _const_spec(tri_sel.shape), _const_spec(tri_mask.shape)],
        out_specs=[_all_rows_spec(bsz, w, fwd), _all_rows_spec(bsz, w, bwd)],
        out_shape=[jax.ShapeDtypeStruct((bsz, n, w), F32)] * 2,
        scratch_shapes=[pltpu.VMEM((2 * bsz, TM + 2 * SUBLANES, 2 * w), F32),
                        pltpu.VMEM((2 * bsz * N_HEADS, 2 * HEAD_W, HEAD_W), F32),
                        pltpu.VMEM((2 * bsz, HEAD_W), F32)],
        compiler_params=_params(("arbitrary",)),
        name="mlstm",
    )(*([z_ml] * 6), cw, cb, gbias, tri_sel, tri_mask)


def _gelu_tanh(x):
    return 0.5 * x * (1.0 + jnp.tanh(0.7978845608028654 * (x + 0.044715 * (x * x * x))))


def _route(logits):
    col = lax.broadcasted_iota(jnp.int32, logits.shape, 1)
    colf = col.astype(F32)
    is_g = jnp.logical_and(col >= N_EXPERTS, col < N_EXPERTS + N_GROUPS)
    is_e = col < N_EXPERTS
    neg = -jnp.inf
    big = 1e9
    gl = jnp.where(is_g, logits, neg)
    g_max = jnp.max(gl, axis=-1, keepdims=True)
    grp = jnp.min(jnp.where(gl == g_max, colf, big), axis=-1, keepdims=True) - N_EXPERTS
    p_grp = 1.0 / jnp.sum(jnp.exp(gl - g_max), axis=-1, keepdims=True)
    col_grp = lax.shift_right_logical(col, EXPERTS_PER_GROUP.bit_length() - 1)
    in_grp = jnp.logical_and(is_e, col_grp.astype(F32) == grp)
    e1 = jnp.where(in_grp, logits, neg)
    top1 = jnp.max(e1, axis=-1, keepdims=True)
    idx1 = jnp.min(jnp.where(e1 == top1, colf, big), axis=-1, keepdims=True)
    e2 = jnp.where(colf == idx1, neg, e1)
    top2 = jnp.max(e2, axis=-1, keepdims=True)
    idx2 = jnp.min(jnp.where(e2 == top2, colf, big), axis=-1, keepdims=True)
    t = jnp.exp(top2 - top1)
    w1 = p_grp / (1.0 + t)
    w2 = p_grp * (t / (1.0 + t))
    out = jnp.where(col == ROUTE_IDX, idx1, jnp.where(col == ROUTE_IDX + 1, idx2, 0.0))
    return jnp.where(col == ROUTE_W, w1, jnp.where(col == ROUTE_W + 1, w2, out))


def _route_onehot(route):
    colf = lax.broadcasted_iota(jnp.int32, route.shape, 1).astype(F32)
    oh1 = colf == route[:, ROUTE_IDX:ROUTE_IDX + 1]
    oh2 = colf == route[:, ROUTE_IDX + 1:ROUTE_IDX + 2]
    return jnp.where(jnp.logical_or(oh1, oh2), 1.0, 0.0), oh1, oh2


def _stage_c_kernel(u_ref, h_ref, mod_ref, rg_y, rg_f, rg_b, gla_g, gla_f, gla_b, hg_g, hg_f, hg_b,
                    ml_o, ml_f, ml_b, gains, w_merge, b_merge, w_branch, w_out, ffn_g, w_route, b_route,
                    h_out, u2_out, wt_out, cnt_out):
    ys = (
        _gelu_tanh(rg_y[...]) * (rg_f[...] + rg_b[...]),
        _head_rmsnorm(gla_f[...] + gla_b[...], gains[0:1, :]) * _silu(gla_g[...]),
        _head_rmsnorm(hg_f[...] + hg_b[...], gains[1:2, :]) * _silu(hg_g[...]),
        _sigmoid(ml_o[...]) * _head_rmsnorm(ml_f[...] + ml_b[...], gains[2:3, :]),
    )
    u = u_ref[...]
    d = u.shape[-1]
    merged = None
    for kk, y in enumerate(ys):
        gate = _sigmoid(jnp.dot(u, w_merge[:, kk * d:(kk + 1) * d], preferred_element_type=F32)
                              + b_merge[:, kk * d:(kk + 1) * d])
        term = gate * _mm(y, w_branch[kk])
        merged = term if merged is None else merged + term
    mix = _mm(merged, w_out[...])
    h_new = h_ref[...] + mod_ref[2:3, :] * mix
    h_out[...] = h_new
    u2 = _rmsnorm_rows(h_new, ffn_g[...]) * (1.0 + mod_ref[4:5, :]) + mod_ref[3:4, :]
    _store_tiled(u2_out, u2)
    logits = _mm_split(u2, w_route[...]) + b_route[...]
    route = _route(logits)
    wt_out[...] = route
    cnt = jnp.sum(_route_onehot(route)[0], axis=0, keepdims=True)
    cnt_out[...] = jnp.broadcast_to(cnt, cnt_out.shape)


def _stage_c_call(u, h, mod, z_rg, rg_f, rg_b, z_gla, gla_f, gla_b, z_hg, hg_f, hg_b, z_ml, ml_f, ml_b,
                  gains, w_merge, b_merge, w_branch, w_out, ffn_g, w_route, b_route, nc, t0):
    bsz, n, d = h.shape
    nt = n // TM
    w = BRANCH_W

    n_out = n - t0 * TM

    def tok(width, col=0):
        return pl.BlockSpec((None, TM, width), lambda b, t: (b, t + t0, col))

    def out_tok(width):
        return pl.BlockSpec((None, TM, width), lambda b, t: (b, t, 0))

    mod_spec = pl.BlockSpec((None, 6, d), lambda b, t: (jnp.where(t + t0 < nc, bsz, b), 0, 0))
    return pl.pallas_call(
        _stage_c_kernel,
        grid=(bsz, nt - t0),
        in_specs=[tok(d), tok(d), mod_spec,
                  tok(w, 1), tok(w), tok(w),
                  tok(w, 3), tok(w), tok(w),
                  tok(w, 4), tok(w), tok(w),
                  tok(w, 3), tok(w), tok(w),
                  _const_spec(gains.shape), _const_spec(w_merge.shape), _const_spec(b_merge.shape),
                  _const_spec(w_branch.shape), _const_spec(w_out.shape), _const_spec(ffn_g.shape),
                  _const_spec(w_route.shape), _const_spec(b_route.shape)],
        out_specs=[out_tok(d),
                   pl.BlockSpec((None, TM * SUBLANES, HEAD_W), lambda b, t: (b, t, 0)),
                   out_tok(HEAD_W),
                   pl.BlockSpec((None, None, SUBLANES, HEAD_W), lambda b, t: (b, t, 0, 0))],
        out_shape=[jax.ShapeDtypeStruct((bsz, n_out, d), F32),
                   jax.ShapeDtypeStruct((bsz, n_out * SUBLANES, HEAD_W), F32),
                   jax.ShapeDtypeStruct((bsz, n_out, HEAD_W), F32),
                   jax.ShapeDtypeStruct((bsz, nt - t0, SUBLANES, HEAD_W), F32)],
        compiler_params=_params(("parallel", "parallel")),
        name="stage_c",
    )(u, h, mod, z_rg, rg_f, rg_b, z_gla, gla_f, gla_b, z_hg, hg_f, hg_b, z_ml, ml_f, ml_b,
      gains, w_merge, b_merge, w_branch, w_out, ffn_g, w_route, b_route)


def _slots_to_smem(slots_vmem, slots_smem, sem):
    cp = pltpu.make_async_copy(slots_vmem, slots_smem, sem)
    cp.start()
    cp.wait()


def _row_copy(src, src_row, dst, dst_row, sem):
    def tile(ref, row):
        return ref.at[pl.ds(pl.multiple_of(row * SUBLANES, SUBLANES), SUBLANES), :]
    return pltpu.make_async_copy(tile(src, src_row), tile(dst, dst_row), sem)


def _to_tiles(x):
    return [x[:, k * HEAD_W:(k + 1) * HEAD_W] for k in range(SUBLANES)]


def _store_tiled(ref, x):
    rows = x.shape[0]
    for k, piece in enumerate(_to_tiles(x)):
        ref[pl.ds(k, rows, stride=SUBLANES), :] = piece


def _load_tiled(ref, rows):
    return jnp.concatenate([ref[pl.ds(k, rows, stride=SUBLANES), :] for k in range(SUBLANES)], axis=-1)


def _dispatch_kernel(seg_ref, route_ref, base_ref, tri_ref, x_ref, xs_hbm, slots_ref,
                     slots_smem, zero_buf, sem_s, sem_r, sem_z):
    @pl.when(pl.program_id(0) == 0)
    def _():
        zero_buf[...] = jnp.zeros_like(zero_buf)

        def fill(e):
            start_row = pl.multiple_of((seg_ref[e] - MOE_BLK) * SUBLANES, SUBLANES)
            return pltpu.make_async_copy(
                zero_buf, xs_hbm.at[pl.ds(start_row, MOE_BLK * SUBLANES), :], sem_z)

        for e in range(N_EXPERTS):
            @pl.when(seg_ref[N_EXPERTS + e] > 0)
            def _(e=e):
                fill(e).start()
        for e in range(N_EXPERTS):
            @pl.when(seg_ref[N_EXPERTS + e] > 0)
            def _(e=e):
                fill(e).wait()

        blk_rows = MOE_BLK * SUBLANES
        used = seg_ref[N_EXPERTS - 1] // MOE_BLK

        def tail(b):
            return pltpu.make_async_copy(
                zero_buf, xs_hbm.at[pl.ds(pl.multiple_of(b * blk_rows, blk_rows), blk_rows), :], sem_z)

        def tail_start(b, carry):
            tail(b).start()
            return carry

        def tail_wait(b, carry):
            tail(b).wait()
            return carry

        n_blk = xs_hbm.shape[0] // blk_rows
        lax.fori_loop(used, n_blk, tail_start, 0)
        lax.fori_loop(used, n_blk, tail_wait, 0)

    route = route_ref[...]
    onehot, oh1, oh2 = _route_onehot(route)
    rank = jnp.dot(tri_ref[...], onehot.astype(BF16), preferred_element_type=F32)
    pos = base_ref[...] + rank
    slot_a = jnp.sum(jnp.where(oh1, pos, 0.0), axis=-1, keepdims=True)
    slot_b = jnp.sum(jnp.where(oh2, pos, 0.0), axis=-1, keepdims=True)
    col = lax.broadcasted_iota(jnp.int32, route.shape, 1)
    both = jnp.where(col == 0, slot_a, jnp.where(col == 1, slot_b, 0.0))
    slots_ref[...] = both.T[0:SUBLANES, :].astype(jnp.int32)
    _slots_to_smem(slots_ref, slots_smem, sem_s)

    def start(t, carry):
        for j in range(2):
            _row_copy(x_ref, t, xs_hbm, slots_smem[j, t], sem_r).start(priority=j)
        return carry

    def wait(t, carry):
        for j in range(2):
            _row_copy(x_ref, 0, xs_hbm, 0, sem_r).wait()
        return carry

    lax.fori_loop(0, TM, start, 0, unroll=8)
    lax.fori_loop(0, TM, wait, 0, unroll=8)


def _experts_kernel(blk_expert, n_used, x_ref, w1, w3, w2, o_ref):
    del blk_expert

    @pl.when(pl.program_id(0) < n_used[0])
    def _():
        x = _load_tiled(x_ref, MOE_BLK)
        h1 = _mm(x, w1[...])
        h3 = _mm(x, w3[...])
        _store_tiled(o_ref, _mm(_silu(h1) * h3, w2[...]))

    @pl.when(pl.program_id(0) >= n_used[0])
    def _():
        o_ref[...] = jnp.zeros_like(o_ref)


def _combine_kernel(slots_smem, route_ref, h_ref, mod_ref, fin_g, ys_hbm, o_ref, buf, sem_r, *, final):
    def start(t, carry):
        for j in range(2):
            _row_copy(ys_hbm, slots_smem[j, t], buf.at[j], t, sem_r).start(priority=j)
        return carry

    def wait(t, carry):
        for j in range(2):
            _row_copy(ys_hbm, 0, buf.at[j], 0, sem_r).wait()
        return carry

    lax.fori_loop(0, TM, start, 0, unroll=8)
    lax.fori_loop(0, TM, wait, 0, unroll=8)
    route = route_ref[...]
    y = (route[:, ROUTE_W:ROUTE_W + 1] * _load_tiled(buf.at[0], TM)
         + route[:, ROUTE_W + 1:ROUTE_W + 2] * _load_tiled(buf.at[1], TM))
    out = h_ref[...] + mod_ref[5:6, :] * y
    if final:
        out = _rmsnorm_rows(out, fin_g[...])
    o_ref[...] = out


def _moe_call(u2, route, cnt, h, mod, fin_g, w1, w3, w2, layer, nc, t0, final):
    bsz, n_out, d = h.shape
    de = w1.shape[-1]
    rows = bsz * n_out
    tiles = rows // TM
    tiles_per_row = n_out // TM
    n_blocks = (2 * rows + N_EXPERTS * (MOE_BLK - 1)) // MOE_BLK + 1

    counts = cnt[:, :, 0, :N_EXPERTS].reshape(tiles, N_EXPERTS)
    total = jnp.sum(counts, axis=0)
    padded = jnp.ceil(total / MOE_BLK) * MOE_BLK
    seg_end = jnp.cumsum(padded)
    base = (seg_end - padded)[None, :] + jnp.cumsum(counts, axis=0) - counts
    base = _pad_cols(base, HEAD_W).reshape(tiles, 1, HEAD_W)
    blk_start = jnp.arange(n_blocks, dtype=F32) * MOE_BLK
    blk_expert = jnp.minimum(jnp.sum(blk_start[:, None] >= seg_end[None, :], axis=1), N_EXPERTS - 1)
    n_used = (seg_end[-1] / MOE_BLK).astype(jnp.int32).reshape(1)
    tri = jnp.asarray(np.tril(np.ones((TM, TM), np.float32), -1), BF16)

    assert d == SUBLANES * HEAD_W
    slot_rows = n_blocks * MOE_BLK * SUBLANES
    seg = jnp.concatenate([seg_end, padded]).astype(jnp.int32)
    xs, slots = pl.pallas_call(
        _dispatch_kernel,
        grid_spec=pltpu.PrefetchScalarGridSpec(
            num_scalar_prefetch=1,
            grid=(tiles,),
            in_specs=[pl.BlockSpec((TM, HEAD_W), lambda t, sg: (t, 0)),
                      pl.BlockSpec((None, 1, HEAD_W), lambda t, sg: (t, 0, 0)),
                      pl.BlockSpec((TM, TM), lambda t, sg: (0, 0)),
                      pl.BlockSpec((TM * SUBLANES, HEAD_W), lambda t, sg: (t, 0))],
            out_specs=[pl.BlockSpec(memory_space=pl.ANY),
                       pl.BlockSpec((SUBLANES, TM), lambda t, sg: (t, 0))],
            scratch_shapes=[pltpu.SMEM((SUBLANES, TM), jnp.int32),
                            pltpu.VMEM((MOE_BLK * SUBLANES, HEAD_W), F32),
                            pltpu.SemaphoreType.DMA, pltpu.SemaphoreType.DMA,
                            pltpu.SemaphoreType.DMA]),
        out_shape=[jax.ShapeDtypeStruct((slot_rows, HEAD_W), F32),
                   jax.ShapeDtypeStruct((tiles * SUBLANES, TM), jnp.int32)],
        compiler_params=_params(("arbitrary",)),
        name="moe_dispatch",
    )(seg, route.reshape(rows, HEAD_W), base, tri, u2.reshape(rows * SUBLANES, HEAD_W))

    blk_spec = pl.BlockSpec((MOE_BLK * SUBLANES, HEAD_W), lambda b, be, nu: (b, 0))
    used_blk_spec = pl.BlockSpec((MOE_BLK * SUBLANES, HEAD_W),
                                 lambda b, be, nu: (jnp.minimum(b, nu[0] - 1), 0))
    ys = pl.pallas_call(
        _experts_kernel,
        grid_spec=pltpu.PrefetchScalarGridSpec(
            num_scalar_prefetch=2,
            grid=(n_blocks,),
            in_specs=[used_blk_spec,
                      pl.BlockSpec((None, None, d, de), lambda b, be, nu: (layer, be[b], 0, 0)),
                      pl.BlockSpec((None, None, d, de), lambda b, be, nu: (layer, be[b], 0, 0)),
                      pl.BlockSpec((None, None, de, d), lambda b, be, nu: (layer, be[b], 0, 0))],
            out_specs=blk_spec),
        out_shape=jax.ShapeDtypeStruct((slot_rows, HEAD_W), F32),
        compiler_params=_params(("arbitrary",)),
        name="moe_experts",
    )(blk_expert.astype(jnp.int32), n_used, xs, w1, w3, w2)

    def mod_index(t):
        b, pos = t // tiles_per_row, t % tiles_per_row
        return (jnp.where(pos + t0 < nc, bsz, b), 0, 0)

    out = pl.pallas_call(
        functools.partial(_combine_kernel, final=final),
        grid=(tiles,),
        in_specs=[pl.BlockSpec((None, SUBLANES, TM), lambda t: (t, 0, 0), memory_space=pltpu.SMEM),
                  pl.BlockSpec((TM, HEAD_W), lambda t: (t, 0)),
                  pl.BlockSpec((TM, d), lambda t: (t, 0)),
                  pl.BlockSpec((None, 6, d), mod_index),
                  _const_spec((1, d)),
                  pl.BlockSpec(memory_space=pl.ANY)],
        out_specs=pl.BlockSpec((TM, d), lambda t: (t, 0)),
        out_shape=jax.ShapeDtypeStruct((rows, d), F32),
        scratch_shapes=[pltpu.VMEM((2, TM * SUBLANES, HEAD_W), F32),
                        pltpu.SemaphoreType.DMA],
        compiler_params=_params(("arbitrary",)),
        name="moe_combine",
    )(slots.reshape(tiles, SUBLANES, TM), route.reshape(rows, HEAD_W), h.reshape(rows, d), mod, fin_g, ys)
    return out.reshape(bsz, n_out, d)


def _pad_heads(w, dk):
    lead = w.shape[:-1]
    w = w.reshape(lead + (N_HEADS, dk))
    w = jnp.pad(w, [(0, 0)] * len(lead) + [(0, 0), (0, HEAD_W - dk)])
    return w.reshape(lead + (N_HEADS * HEAD_W,))


def _pad_cols(w, width):
    return jnp.pad(w, [(0, 0)] * (w.ndim - 1) + [(0, width - w.shape[-1])])


def _split_w_in(w_in):
    bw = BRANCH_W
    sizes = [bw, bw, N_HEADS * GLA_DK, N_HEADS * GLA_DK, bw, bw, 2 * GLA_RANK,
             bw, bw, 2 * bw, bw, bw, bw, bw, bw, 16]
    offs = np.cumsum([0] + sizes)
    p = [w_in[:, offs[i]:offs[i + 1]] for i in range(len(sizes))]
    w_rg = jnp.concatenate([p[0], p[1]], axis=1)
    w_gla = jnp.concatenate([_pad_heads(p[2], GLA_DK), _pad_heads(p[3], GLA_DK), p[4], p[5],
                             _pad_cols(p[6], HEAD_W)], axis=1)
    w_hg = jnp.concatenate([p[7], p[8], p[9], p[10]], axis=1)
    gates = p[15].reshape(-1, 2, 2, N_HEADS)
    w_gi = _pad_cols(gates[:, :, 0].reshape(-1, 2 * N_HEADS), HEAD_W)
    w_gf = _pad_cols(gates[:, :, 1].reshape(-1, 2 * N_HEADS), HEAD_W)
    w_ml = jnp.concatenate([p[11], p[12], p[13], p[14], w_gi, w_gf], axis=1)
    return [w.astype(BF16) for w in (w_rg, w_gla, w_hg, w_ml)]


def _block_diag(w):
    k, n = w.shape[-3], w.shape[-1]
    eye = jnp.eye(k, dtype=w.dtype)
    full = jnp.einsum('...kij,kl->...kilj', w, eye)
    return full.reshape(w.shape[:-3] + (k * n, k * n))


def kernel(x, c, ctx, c_ctx, ada_w, ada_b, norm_mix_g, norm_ffn_g, w_in, rg_conv_w, rg_conv_b, rg_gate_w, rg_gate_b, rg_lambda, gla_w_lr, gla_b_lr, gla_norm_g, hgrn_lb_logits, hgrn_norm_g, ml_conv_w, ml_conv_b, ml_gate_b, ml_norm_g, w_branch, w_merge, b_merge, w_out, moe_w_group, moe_b_group, moe_w_expert, moe_b_expert, moe_w1, moe_w3, moe_w2, final_norm_g):
    bsz, seq, d = x.shape
    n_ctx = ctx.shape[1]
    depth = ada_w.shape[0]
    assert n_ctx % TM == 0 and seq % TM == 0 and d == 2 * BRANCH_W
    nc = n_ctx // TM

    h = jnp.concatenate([ctx, x], axis=1)
    cvec = jnp.zeros((SUBLANES, d), F32).at[:bsz].set(c).at[bsz].set(c_ctx)
    mod_all = _mod_call(cvec, ada_w, ada_b).reshape(depth, SUBLANES, 6, d)[:, :bsz + 1]

    lb_cum = jnp.cumsum(jax.nn.softmax(hgrn_lb_logits.astype(F32), axis=0), axis=0)
    hgrn_lb = lb_cum - lb_cum[:1]
    tables = [t for levels in SPLIT_LEVELS for t in _decay_tables(levels)]
    tri_sel = tables[0][:, :CHUNK, :]
    tri_mask = tables[1][:, 0]

    out = None
    for l in range(depth):
        last = l == depth - 1
        mod = mod_all[l]
        w_rg, w_gla, w_hg, w_ml = _split_w_in(w_in[l])
        z_rg, z_gla, z_hg, z_ml, u = _stage_a_call(h, mod, norm_mix_g[l], w_rg, w_gla, w_hg, w_ml, nc)

        rg_f, rg_b = _rg_call(z_rg, rg_conv_w[l], rg_conv_b[l].reshape(1, -1),
                              _block_diag(rg_gate_w[l]).astype(BF16),
                              rg_gate_b[l].reshape(4, -1), rg_lambda[l], nc)

        wlr = jnp.zeros((2, HEAD_W, N_HEADS * HEAD_W), F32)
        wlr_p = _pad_heads(gla_w_lr[l], GLA_DK)
        wlr = wlr.at[0, :GLA_RANK].set(wlr_p[0]).at[1, GLA_RANK:2 * GLA_RANK].set(wlr_p[1])
        gla_f, gla_b = _gla_call(z_gla, wlr.astype(BF16), _pad_heads(gla_b_lr[l], GLA_DK), tables, nc)

        lb = hgrn_lb[l]
        lbp = jnp.concatenate([lb, jnp.log(lb), jnp.log1p(-lb), jnp.zeros((2, lb.shape[-1]), F32)], axis=0)
        hg_f, hg_b = _hg_call(z_hg, lbp, tables, nc)

        gbias = _pad_cols(ml_gate_b[l].transpose(1, 0, 2).reshape(2, -1), HEAD_W)
        ml_f, ml_b = _ml_call(z_ml, ml_conv_w[l], ml_conv_b[l].reshape(1, -1), gbias, tri_sel, tri_mask, nc)

        gains = jnp.zeros((SUBLANES, HEAD_W), F32).at[0].set(gla_norm_g[l]).at[1].set(hgrn_norm_g[l]).at[2].set(ml_norm_g[l])
        w_route = _split_cols(_pad_cols(jnp.concatenate([moe_w_expert[l], moe_w_group[l]], axis=1), HEAD_W))
        b_route = _pad_cols(jnp.concatenate([moe_b_expert[l], moe_b_group[l]]).reshape(1, -1), HEAD_W)
        t0 = nc if last else 0
        h_mid, u2, route, cnt = _stage_c_call(
            u, h, mod, z_rg, rg_f, rg_b, z_gla, gla_f, gla_b, z_hg, hg_f, hg_b, z_ml, ml_f, ml_b,
            gains, w_merge[l].astype(BF16), b_merge[l].reshape(1, -1), w_branch[l].astype(BF16),
            w_out[l].astype(BF16), norm_ffn_g[l].reshape(1, -1), w_route, b_route, nc, t0)

        res = _moe_call(u2, route, cnt, h_mid, mod, final_norm_g.reshape(1, -1), moe_w1,
                        moe_w3, moe_w2, l, nc, t0, last)
        if last:
            out = res
        else:
            h = res
    return out
```

```python
import functools

import numpy as np
import jax
import jax.numpy as jnp
from jax import lax
from jax.experimental import pallas as pl
from jax.experimental.pallas import tpu as pltpu

F32 = jnp.float32
BF16 = jnp.bfloat16

EPS = 1e-6
TM = 256
CHUNK = 64
SUBLANES = 8
N_HEADS = 4
HEAD_W = 128
BRANCH_W = 512
CONV_W = 4
CONV_LEFT = 2
RG_C = 8.0
GLA_DK = 64
GLA_RANK = 16
GLA_GATE_NORM = 16.0
N_GROUPS = 4
EXPERTS_PER_GROUP = 4
N_EXPERTS = 16
MOE_BLK = 512
ROUTE_IDX = 16
ROUTE_W = 18
N_LEVELS = 6
MILD_DECAY = -64.0
VMEM_LIMIT = 56 * 1024 * 1024

W_RG = 2 * BRANCH_W
W_GLA = 4 * BRANCH_W + HEAD_W
W_HG = 5 * BRANCH_W
W_ML = 4 * BRANCH_W + 2 * HEAD_W
SPLIT_LEVELS = (1, 2, N_LEVELS)


def _mm(a, b):
    return jnp.dot(a.astype(BF16), b.astype(BF16), preferred_element_type=F32)


def _mm_nt(a, b):
    return lax.dot_general(a.astype(BF16), b.astype(BF16), (((1,), (1,)), ((), ())),
                           preferred_element_type=F32)


def _mm_tn(a, b):
    return lax.dot_general(a.astype(BF16), b.astype(BF16), (((0,), (0,)), ((), ())),
                           preferred_element_type=F32)


def _mm_sel(sel, x):
    x1 = x.astype(BF16)
    x2 = (x - x1.astype(F32)).astype(BF16)
    dot = functools.partial(jnp.dot, preferred_element_type=F32)
    return dot(sel, x1) + dot(sel, x2)


def _split_cols(w):
    hi = w.astype(BF16)
    return jnp.concatenate([hi, (w - hi.astype(F32)).astype(BF16)], axis=1)


def _mm_split(x, w_split):
    n = w_split.shape[1] // 2
    x1 = x.astype(BF16)
    x2 = (x - x1.astype(F32)).astype(BF16)
    p = jnp.dot(x1, w_split, preferred_element_type=F32)
    return p[:, :n] + p[:, n:] + jnp.dot(x2, w_split[:, :n], preferred_element_type=F32)


def _softplus_neg_abs(x):
    return jnp.log(1.0 + jnp.exp(-jnp.abs(x)))


def _log_sigmoid(x):
    return jnp.minimum(x, 0.0) - _softplus_neg_abs(x)


def _sigmoid(x):
    return 0.5 * jnp.tanh(0.5 * x) + 0.5


def _silu(x):
    return x * _sigmoid(x)


def _rmsnorm_rows(x, g):
    return x * lax.rsqrt(jnp.mean(x * x, axis=-1, keepdims=True) + EPS) * g


def _head_rmsnorm(o, g):
    parts = [_rmsnorm_rows(o[:, h * HEAD_W:(h + 1) * HEAD_W], g) for h in range(N_HEADS)]
    return jnp.concatenate(parts, axis=-1)


def _bwd_tile(s, nc, nt):
    return jnp.where(s < nc, nc - 1 - s, nt - 1 - (s - nc))


def _const_spec(shape):
    nd = len(shape)
    return pl.BlockSpec(shape, lambda *_: (0,) * nd)


def _params(sem):
    return pltpu.CompilerParams(dimension_semantics=sem, vmem_limit_bytes=VMEM_LIMIT)


def _mod_kernel(c_ref, w_ref, b_ref, o_ref):
    cv = _silu(c_ref[...])
    o_ref[...] = jnp.dot(cv, w_ref[...], precision=lax.Precision.HIGHEST,
                         preferred_element_type=F32) + b_ref[...]


def _mod_call(cvec, ada_w, ada_b):
    depth, d, six_d = ada_w.shape
    tn = 1024
    return pl.pallas_call(
        _mod_kernel,
        grid=(depth, six_d // tn),
        in_specs=[pl.BlockSpec((SUBLANES, d), lambda l, j: (0, 0)),
                  pl.BlockSpec((None, d, tn), lambda l, j: (l, 0, j)),
                  pl.BlockSpec((None, 1, tn), lambda l, j: (l, 0, j))],
        out_specs=pl.BlockSpec((None, SUBLANES, tn), lambda l, j: (l, 0, j)),
        out_shape=jax.ShapeDtypeStruct((depth, SUBLANES, six_d), F32),
        compiler_params=_params(("parallel", "parallel")),
        name="adaln_mod",
    )(cvec, ada_w, ada_b.reshape(depth, 1, six_d))


def _stage_a_kernel(h_ref, mod_ref, g_ref, w_rg, w_gla, w_hg, w_ml,
                    z_rg, z_gla, z_hg, z_ml, u_ref):
    x = h_ref[...]
    u = _rmsnorm_rows(x, g_ref[...]) * (1.0 + mod_ref[1:2, :]) + mod_ref[0:1, :]
    ub = u.astype(BF16)
    u_ref[...] = ub
    z_rg[...] = jnp.dot(ub, w_rg[...], preferred_element_type=F32)
    z_gla[...] = jnp.dot(ub, w_gla[...], preferred_element_type=F32)
    z_hg[...] = jnp.dot(ub, w_hg[...], preferred_element_type=F32)
    z_ml[...] = jnp.dot(ub, w_ml[...], preferred_element_type=F32)


def _stage_a_call(h, mod, norm_g, w_rg, w_gla, w_hg, w_ml, nc):
    bsz, n, d = h.shape
    tm = TM

    def tok(w):
        return pl.BlockSpec((None, tm, w), lambda b, t: (b, t, 0))

    def weight(w):
        return pl.BlockSpec((d, w), lambda b, t: (0, 0), pipeline_mode=pl.Buffered(1))

    return pl.pallas_call(
        _stage_a_kernel,
        grid=(bsz, n // tm),
        in_specs=[tok(d),
                  pl.BlockSpec((None, 6, d), lambda b, t: (jnp.where(t < nc, bsz, b), 0, 0)),
                  _const_spec((1, d)),
                  weight(W_RG), weight(W_GLA), weight(W_HG), weight(W_ML)],
        out_specs=[tok(W_RG), tok(W_GLA), tok(W_HG), tok(W_ML), tok(d)],
        out_shape=[jax.ShapeDtypeStruct((bsz, n, W_RG), F32),
                   jax.ShapeDtypeStruct((bsz, n, W_GLA), F32),
                   jax.ShapeDtypeStruct((bsz, n, W_HG), F32),
                   jax.ShapeDtypeStruct((bsz, n, W_ML), F32),
                   jax.ShapeDtypeStruct((bsz, n, d), BF16)],
        compiler_params=_params(("parallel", "parallel")),
        name="stage_a",
    )(h, mod, norm_g.reshape(1, d), w_rg, w_gla, w_hg, w_ml)


def _conv_tile(x_ref, prev_ref, next_ref, pad_ref, cw_ref, cb_ref, tile, nc, nt):
    prev_ok = jnp.logical_and(tile != 0, tile != nc)
    next_ok = jnp.logical_and(tile != nc - 1, tile != nt - 1)
    pad_ref[0:SUBLANES, :] = jnp.where(prev_ok, prev_ref[...], 0.0)
    pad_ref[SUBLANES:SUBLANES + TM, :] = x_ref[...]
    pad_ref[SUBLANES + TM:2 * SUBLANES + TM, :] = jnp.where(next_ok, next_ref[...], 0.0)
    acc = cb_ref[...]
    for j in range(CONV_W):
        off = SUBLANES - CONV_LEFT + j
        acc = acc + cw_ref[j:j + 1, :] * pad_ref[off:off + TM, :]
    return acc


def _halo_specs(width, col_block, tile_fn, n):
    rows = TM // SUBLANES
    last = n // SUBLANES - 1
    cur = pl.BlockSpec((None, TM, width), lambda b, s: (b, tile_fn(s), col_block))
    prev = pl.BlockSpec((None, SUBLANES, width),
                        lambda b, s: (b, jnp.maximum(tile_fn(s) * rows - 1, 0), col_block))
    nxt = pl.BlockSpec((None, SUBLANES, width),
                       lambda b, s: (b, jnp.minimum((tile_fn(s) + 1) * rows, last), col_block))
    return [cur, prev, nxt]


def _scan_rows8(a, b, reverse):
    n = a.shape[0]
    pos = jnp.bitwise_and(lax.broadcasted_iota(jnp.int32, a.shape, 0), SUBLANES - 1)
    k = 1
    while k < SUBLANES:
        if reverse:
            a_s, b_s, ok = pltpu.roll(a, n - k, 0), pltpu.roll(b, n - k, 0), pos < SUBLANES - k
        else:
            a_s, b_s, ok = pltpu.roll(a, k, 0), pltpu.roll(b, k, 0), pos >= k
        b = b + a * jnp.where(ok, b_s, 0.0)
        a = a * jnp.where(ok, a_s, 1.0)
        k *= 2
    return a, b


def _scan_tile(a, b, carry, o_ref, reverse):
    a8, b8 = _scan_rows8(a, b, reverse)
    groups = a.shape[0] // SUBLANES
    order = range(groups - 1, -1, -1) if reverse else range(groups)
    for r in order:
        rows = slice(r * SUBLANES, (r + 1) * SUBLANES)
        h = b8[rows] + a8[rows] * carry
        o_ref[rows, :] = h
        carry = h[0:1, :] if reverse else h[SUBLANES - 1:SUBLANES, :]
    return carry


def _rg_kernel(xf, xf_p, xf_n, xb, xb_p, xb_n, cw, cb, gw, gb, lam,
               hf_ref, hb_ref, pad_ref, carry_ref, *, nc, nt):
    s = pl.program_id(1)

    @pl.when(s == 0)
    def _():
        carry_ref[...] = jnp.zeros_like(carry_ref)

    dirs = ((xf, xf_p, xf_n, hf_ref, s, False),
            (xb, xb_p, xb_n, hb_ref, _bwd_tile(s, nc, nt), True))
    for d, (x_ref, p_ref, n_ref, o_ref, tile, reverse) in enumerate(dirs):
        x = _conv_tile(x_ref, p_ref, n_ref, pad_ref, cw, cb, tile, nc, nt)
        r = _sigmoid(_mm(x, gw[d, 0]) + gb[2 * d:2 * d + 1, :])
        i = _sigmoid(_mm(x, gw[d, 1]) + gb[2 * d + 1:2 * d + 2, :])
        lam_d = lam[d:d + 1, :]
        softplus = jnp.maximum(-lam_d, 0.0) + jnp.log1p(jnp.exp(-jnp.abs(lam_d)))
        log_a = -RG_C * r * softplus
        a = jnp.exp(log_a)
        t = jnp.tanh(log_a)
        bt = jnp.sqrt(-2.0 * t / (1.0 - t)) * (i * x)
        carry_ref[d:d + 1, :] = _scan_tile(a, bt, carry_ref[d:d + 1, :], o_ref, reverse)


def _rg_call(z_rg, cw, cb, gw, gb, lam, nc):
    bsz, n, _ = z_rg.shape
    nt = n // TM
    w = BRANCH_W
    fwd = lambda s: s
    bwd = lambda s: _bwd_tile(s, nc, nt)
    out = lambda fn: pl.BlockSpec((None, TM, w), lambda b, s: (b, fn(s), 0))
    return pl.pallas_call(
        functools.partial(_rg_kernel, nc=nc, nt=nt),
        grid=(bsz, nt),
        in_specs=_halo_specs(w, 0, fwd, n) + _halo_specs(w, 0, bwd, n) + [
            _const_spec((CONV_W, w)), _const_spec((1, w)),
            _const_spec((2, 2, w, w)), _const_spec((4, w)), _const_spec((2, w))],
        out_specs=[out(fwd), out(bwd)],
        out_shape=[jax.ShapeDtypeStruct((bsz, n, w), F32)] * 2,
        scratch_shapes=[pltpu.VMEM((TM + 2 * SUBLANES, w), F32), pltpu.VMEM((2, w), F32)],
        compiler_params=_params(("parallel", "arbitrary")),
        name="rglru",
    )(z_rg, z_rg, z_rg, z_rg, z_rg, z_rg, cw, cb, gw, gb, lam)


def _decay_tables(levels):
    n = CHUNK
    blk = n >> levels
    sel = np.zeros((levels + 3, n, n), np.float32)
    masks = np.zeros((levels + 1, n, n), np.float32)
    for lvl in range(levels):
        half = n >> (lvl + 1)
        for r in range(n):
            start = (r // (2 * half)) * 2 * half
            ref = start + half - 1
            if r - start >= half:
                sel[lvl, r, ref + 1:r + 1] = 1.0
                masks[lvl, r, start:start + half] = 1.0
            else:
                sel[lvl, r, r + 1:ref + 1] = 1.0
    sel[levels] = np.tril(np.ones((n, n), np.float32))
    sel[levels + 1] = 1.0 - sel[levels]
    for r in range(n):
        start = (r // blk) * blk
        sel[levels + 2, r, r + 1:start + blk] = 1.0
        masks[levels, r, start:r + 1] = 1.0
    if levels == 0 or levels == N_LEVELS:
        sel = sel[:levels + 2]
    sel_b = sel[:, ::-1, ::-1].reshape(-1, n)
    masks_b = masks[:, ::-1, ::-1]
    return (jnp.asarray(np.stack([sel.reshape(-1, n), sel_b]), BF16),
            jnp.asarray(np.stack([masks, masks_b]), F32))


def _ref_rows(levels, reverse):
    def runs(block, offset):
        return [(s + offset, block) for s in range(0, CHUNK, block)]

    out = [runs(CHUNK >> lvl, (CHUNK >> (lvl + 1)) - 1) for lvl in range(levels)]
    out.append(runs(CHUNK, CHUNK - 1))
    out.append(runs(CHUNK >> levels, (CHUNK >> levels) - 1))
    if reverse:
        out = [[(CHUNK - 1 - r, n) for r, n in reversed(segs)] for segs in out]
    return out


def _rows_bcast(b, segs):
    return jnp.concatenate([jnp.broadcast_to(b[r:r + 1, :], (n, b.shape[-1])) for r, n in segs], axis=0)


def _gla_chunk(q, k, v, g, sel, masks, st_ref, d, reverse, levels):
    if levels == N_LEVELS:
        x = jnp.exp(_mm_sel(sel, g))
        x_lvl = [x[lvl * CHUNK:(lvl + 1) * CHUNK] for lvl in range(levels)]
        x_cum = x[levels * CHUNK:(levels + 1) * CHUNK]
        x_rest = x[(levels + 1) * CHUNK:(levels + 2) * CHUNK]
        xk_blk = xq_blk = None
    else:
        b = _mm_sel(sel[levels * CHUNK:(levels + 1) * CHUNK], g)
        refs = _ref_rows(levels, reverse)
        x_cum = jnp.exp(b)
        x_lvl = [jnp.exp(-jnp.abs(b - _rows_bcast(b, segs))) for segs in refs[:levels]]
        x_rest = jnp.exp(-jnp.abs(b - _rows_bcast(b, refs[levels])))
        if levels == 0:
            xk_blk, xq_blk = x_rest, jnp.exp(jnp.abs(b - _rows_bcast(b, refs[levels])))
        else:
            d_blk = jnp.abs(b - _rows_bcast(b, refs[levels + 1]))
            xk_blk, xq_blk = jnp.exp(-d_blk), jnp.exp(d_blk)
    outs = []
    for h in range(N_HEADS):
        cols = slice(h * HEAD_W, (h + 1) * HEAD_W)
        qh, kh, vh = q[:, cols], k[:, cols], v[:, cols]
        kx = kh * x_rest[:, cols]
        if xk_blk is None:
            sc = _mm_nt(qh, kh)
        elif levels == 0:
            sc = _mm_nt(qh * xq_blk[:, cols], kx)
        else:
            sc = _mm_nt(qh * xq_blk[:, cols], kh * xk_blk[:, cols])
        sc = jnp.where(masks[levels] > 0.0, sc, 0.0)
        for lvl in range(levels):
            xl = x_lvl[lvl][:, cols]
            sc = sc + jnp.where(masks[lvl] > 0.0, _mm_nt(qh * xl, kh * xl), 0.0)
        st = st_ref[d, h]
        outs.append(_mm(sc, vh) + _mm_nt(qh * x_cum[:, cols], st))
        x_end = x_cum[0:1, cols] if reverse else x_cum[CHUNK - 1:CHUNK, cols]
        st_ref[d, h] = st * x_end + _mm_tn(vh, kx)
    return jnp.concatenate(outs, axis=-1)


def _gla_step(sides, tabs, st_ref):
    n_chunks = TM // CHUNK

    def min_block_sum(block):
        mins = [jnp.min(jnp.sum(side[3].reshape(TM // block, block, side[3].shape[-1]), axis=1))
                for side in sides]
        return functools.reduce(jnp.minimum, mins)

    conds = []
    taken = None
    for levels in SPLIT_LEVELS[:-1]:
        ok = min_block_sum(CHUNK >> levels) > MILD_DECAY
        conds.append(ok if taken is None else jnp.logical_and(ok, jnp.logical_not(taken)))
        taken = ok if taken is None else jnp.logical_or(taken, ok)
    conds.append(jnp.logical_not(taken))

    for levels, cond, (sel_ref, mask_ref) in zip(SPLIT_LEVELS, conds, tabs):
        @pl.when(cond)
        def _(levels=levels, sel_ref=sel_ref, mask_ref=mask_ref):
            for c in range(n_chunks):
                for i, (q, k, v, g, o_ref, reverse) in enumerate(sides):
                    cc = n_chunks - 1 - c if reverse else c
                    rows = slice(cc * CHUNK, (cc + 1) * CHUNK)
                    o_ref[rows, :] = _gla_chunk(q[rows], k[rows], v[rows], g[rows],
                                                sel_ref[int(reverse)], mask_ref[int(reverse)],
                                                st_ref, i, reverse, levels)


def _pairs(refs):
    return tuple(zip(refs[0::2], refs[1::2]))


def _gla_kernel(zf, zb, wlr, blr, *rest):
    tabs, (of_ref, ob_ref, st_ref) = _pairs(rest[:-3]), rest[-3:]
    w = BRANCH_W

    @pl.when(pl.program_id(0) == 0)
    def _():
        st_ref[...] = jnp.zeros_like(st_ref)

    sides = []
    for b in range(zf.shape[0]):
        for d, (z_ref, o_ref) in enumerate(((zf, of_ref), (zb, ob_ref))):
            pre = _mm(z_ref[b, :, 4 * w:4 * w + HEAD_W], wlr[d]) + blr[d:d + 1, :]
            g = _log_sigmoid(pre) * (1.0 / GLA_GATE_NORM)
            q = z_ref[b, :, 0:w] * (GLA_DK ** -0.5)
            sides.append((q, z_ref[b, :, w:2 * w], z_ref[b, :, 2 * w:3 * w], g, o_ref.at[b], d == 1))
    _gla_step(sides, tabs, st_ref)


def _hg_kernel(zf, zb, lbp, *rest):
    tabs, (of_ref, ob_ref, st_ref) = _pairs(rest[:-3]), rest[-3:]
    w = BRANCH_W

    @pl.when(pl.program_id(0) == 0)
    def _():
        st_ref[...] = jnp.zeros_like(st_ref)

    sides = []
    for b in range(zf.shape[0]):
        for d, (z_ref, o_ref) in enumerate(((zf, of_ref), (zb, ob_ref))):
            f = z_ref[b, :, (2 + d) * w:(3 + d) * w]
            lb = lbp[d:d + 1, :]
            log_lb = lbp[2 + d:3 + d, :]
            log_1m = lbp[4 + d:5 + d, :]
            c = log_1m + _log_sigmoid(f)
            g = jnp.maximum(log_lb, c) + _softplus_neg_abs(log_lb - c)
            k = (1.0 - lb) * _sigmoid(-f)
            q = _silu(z_ref[b, :, 0:w]) * (HEAD_W ** -0.5)
            sides.append((q, k, z_ref[b, :, w:2 * w], g, o_ref.at[b], d == 1))
    _gla_step(sides, tabs, st_ref)


def _mixer_specs(widths_cols, tile_fn):
    return [pl.BlockSpec((None, TM, w), lambda b, s, c=c: (b, tile_fn(s), c)) for w, c in widths_cols]


def _all_rows_spec(bsz, width, tile_fn):
    return pl.BlockSpec((bsz, TM, width), lambda s: (0, tile_fn(s), 0))


def _gla_call(z_gla, wlr, blr, tables, nc):
    bsz, n, _ = z_gla.shape
    nt = n // TM
    w = BRANCH_W
    fwd = lambda s: s
    bwd = lambda s: _bwd_tile(s, nc, nt)
    return pl.pallas_call(
        _gla_kernel,
        grid=(nt,),
        in_specs=[_all_rows_spec(bsz, W_GLA, fwd), _all_rows_spec(bsz, W_GLA, bwd),
                  _const_spec(wlr.shape), _const_spec(blr.shape)] + [_const_spec(t.shape) for t in tables],
        out_specs=[_all_rows_spec(bsz, w, fwd), _all_rows_spec(bsz, w, bwd)],
        out_shape=[jax.ShapeDtypeStruct((bsz, n, w), F32)] * 2,
        scratch_shapes=[pltpu.VMEM((2 * bsz, N_HEADS, HEAD_W, HEAD_W), F32)],
        compiler_params=_params(("arbitrary",)),
        name="gla",
    )(z_gla, z_gla, wlr, blr, *tables)


def _hg_call(z_hg, lbp, tables, nc):
    bsz, n, _ = z_hg.shape
    nt = n // TM
    w = BRANCH_W
    fwd = lambda s: s
    bwd = lambda s: _bwd_tile(s, nc, nt)
    return pl.pallas_call(
        _hg_kernel,
        grid=(nt,),
        in_specs=[_all_rows_spec(bsz, 4 * w, fwd), _all_rows_spec(bsz, 4 * w, bwd),
                  _const_spec(lbp.shape)] + [_const_spec(t.shape) for t in tables],
        out_specs=[_all_rows_spec(bsz, w, fwd), _all_rows_spec(bsz, w, bwd)],
        out_shape=[jax.ShapeDtypeStruct((bsz, n, w), F32)] * 2,
        scratch_shapes=[pltpu.VMEM((2 * bsz, N_HEADS, HEAD_W, HEAD_W), F32)],
        compiler_params=_params(("arbitrary",)),
        name="hgrn2",
    )(z_hg, z_hg, lbp, *tables)


def _cummax_rows(a, reverse):
    n = a.shape[0]
    row = lax.broadcasted_iota(jnp.int32, a.shape, 0)
    k = 1
    while k < n:
        if reverse:
            a_s, ok = pltpu.roll(a, n - k, 0), row < n - k
        else:
            a_s, ok = pltpu.roll(a, k, 0), row >= k
        a = jnp.maximum(a, jnp.where(ok, a_s, -jnp.inf))
        k *= 2
    return a


def _ml_chunk(q, k, v, gi, log_f, tri_sel, tri_mask, ct_ref, m_ref, slot, d, reverse):
    b = _mm_sel(tri_sel, log_f)
    a = gi - b
    m_prev = m_ref[slot:slot + 1, :]
    m_rel = jnp.maximum(_cummax_rows(a, reverse), m_prev)
    w_inter = jnp.exp(m_prev - m_rel)
    exp_neg_m = jnp.exp(-(b + m_rel))
    last = 0 if reverse else CHUNK - 1
    b_end = b[last:last + 1, :]
    m_new = b_end + m_rel[last:last + 1, :]
    w_end = jnp.exp(b_end - b + gi - m_new)
    decay = jnp.exp(b_end + m_prev - m_new)
    m_ref[slot:slot + 1, :] = m_new
    a_t = a.T
    ones = jnp.ones((CHUNK, HEAD_W), BF16)
    outs = []
    for h in range(N_HEADS):
        cols = slice(h * HEAD_W, (h + 1) * HEAD_W)
        qh, kh, vh = q[:, cols], k[:, cols], v[:, cols]
        c = d * N_HEADS + h
        log_w = jnp.where(tri_mask > 0.0, a_t[c:c + 1, :] - m_rel[:, c:c + 1], -jnp.inf)
        s = _mm_nt(qh, kh) * jnp.exp(log_w)
        v_ext = jnp.concatenate([vh.astype(BF16), ones], axis=-1)
        st = slot * N_HEADS + h
        ct = ct_ref[st]
        tot = _mm(s, v_ext) + w_inter[:, c:c + 1] * _mm_nt(qh, ct)
        num, den = tot[:, :HEAD_W], tot[:, HEAD_W:]
        outs.append(num / jnp.maximum(jnp.abs(den), exp_neg_m[:, c:c + 1]))
        ct_ref[st] = decay[:, c:c + 1] * ct + _mm_tn(v_ext, kh * w_end[:, c:c + 1])
    return jnp.concatenate(outs, axis=-1)


def _ml_kernel(zf, zf_p, zf_n, zb, zb_p, zb_n, cw, cb, gbias,
               tri_sel, tri_mask, of_ref, ob_ref, pad_ref, ct_ref, m_ref, *, nc, nt):
    s = pl.program_id(0)
    w = BRANCH_W
    n_chunks = TM // CHUNK

    @pl.when(s == 0)
    def _():
        ct_ref[...] = jnp.zeros_like(ct_ref)
        m_ref[...] = jnp.zeros_like(m_ref)

    dirs = ((zf, zf_p, zf_n, of_ref, s), (zb, zb_p, zb_n, ob_ref, _bwd_tile(s, nc, nt)))
    sides = []
    for b in range(zf.shape[0]):
        for d, (z_ref, p_ref, nx_ref, o_ref, tile) in enumerate(dirs):
            slot = 2 * b + d
            qk = _silu(_conv_tile(z_ref.at[b, :, 0:2 * w], p_ref.at[b], nx_ref.at[b],
                                  pad_ref.at[slot], cw, cb, tile, nc, nt))
            gi = z_ref[b, :, 4 * w:4 * w + HEAD_W] + gbias[0:1, :]
            log_f = _log_sigmoid(z_ref[b, :, 4 * w + HEAD_W:4 * w + 2 * HEAD_W] + gbias[1:2, :])
            sides.append((qk[:, :w] * (HEAD_W ** -0.5), qk[:, w:], z_ref[b, :, 2 * w:3 * w],
                          gi, log_f, o_ref.at[b], slot, d))
    for c in range(n_chunks):
        for q, k, v, gi, log_f, o_ref, slot, d in sides:
            cc = n_chunks - 1 - c if d == 1 else c
            rows = slice(cc * CHUNK, (cc + 1) * CHUNK)
            o_ref[rows, :] = _ml_chunk(q[rows], k[rows], v[rows], gi[rows], log_f[rows],
                                       tri_sel[d], tri_mask[d], ct_ref, m_ref, slot, d, d == 1)


def _ml_call(z_ml, cw, cb, gbias, tri_sel, tri_mask, nc):
    bsz, n, _ = z_ml.shape
    nt = n // TM
    w = BRANCH_W
    fwd = lambda s: s
    bwd = lambda s: _bwd_tile(s, nc, nt)
    rows8 = TM // SUBLANES
    last8 = n // SUBLANES - 1

    def side(fn):
        return [_all_rows_spec(bsz, W_ML, fn),
                pl.BlockSpec((bsz, SUBLANES, 2 * w), lambda s: (0, jnp.maximum(fn(s) * rows8 - 1, 0), 0)),
                pl.BlockSpec((bsz, SUBLANES, 2 * w), lambda s: (0, jnp.minimum((fn(s) + 1) * rows8, last8), 0))]

    return pl.pallas_call(
        functools.partial(_ml_kernel, nc=nc, nt=nt),
        grid=(nt,),
        in_specs=side(fwd) + side(bwd) + [
            _const_spec((CONV_W, 2 * w)), _const_spec((1, 2 * w)), _const_spec((2, HEAD_W)),
            _const_spec(tri_sel.shape), _const_spec(tri_mask.shape)],
        out_specs=[_all_rows_spec(bsz, w, fwd), _all_rows_spec(bsz, w, bwd)],
        out_shape=[jax.ShapeDtypeStruct((bsz, n, w), F32)] * 2,
        scratch_shapes=[pltpu.VMEM((2 * bsz, TM + 2 * SUBLANES, 2 * w), F32),
                        pltpu.VMEM((2 * bsz * N_HEADS, 2 * HEAD_W, HEAD_W), F32),
                        pltpu.VMEM((2 * bsz, HEAD_W), F32)],
        compiler_params=_params(("arbitrary",)),
        name="mlstm",
    )(*([z_ml] * 6), cw, cb, gbias, tri_sel, tri_mask)


def _gelu_tanh(x):
    return 0.5 * x * (1.0 + jnp.tanh(0.7978845608028654 * (x + 0.044715 * (x * x * x))))


def _route(logits):
    col = lax.broadcasted_iota(jnp.int32, logits.shape, 1)
    colf = col.astype(F32)
    is_g = jnp.logical_and(col >= N_EXPERTS, col < N_EXPERTS + N_GROUPS)
    is_e = col < N_EXPERTS
    neg = -jnp.inf
    big = 1e9
    gl = jnp.where(is_g, logits, neg)
    g_max = jnp.max(gl, axis=-1, keepdims=True)
    grp = jnp.min(jnp.where(gl == g_max, colf, big), axis=-1, keepdims=True) - N_EXPERTS
    p_grp = 1.0 / jnp.sum(jnp.exp(gl - g_max), axis=-1, keepdims=True)
    col_grp = lax.shift_right_logical(col, EXPERTS_PER_GROUP.bit_length() - 1)
    in_grp = jnp.logical_and(is_e, col_grp.astype(F32) == grp)
    e1 = jnp.where(in_grp, logits, neg)
    top1 = jnp.max(e1, axis=-1, keepdims=True)
    idx1 = jnp.min(jnp.where(e1 == top1, colf, big), axis=-1, keepdims=True)
    e2 = jnp.where(colf == idx1, neg, e1)
    top2 = jnp.max(e2, axis=-1, keepdims=True)
    idx2 = jnp.min(jnp.where(e2 == top2, colf, big), axis=-1, keepdims=True)
    t = jnp.exp(top2 - top1)
    w1 = p_grp / (1.0 + t)
    w2 = p_grp * (t / (1.0 + t))
    out = jnp.where(col == ROUTE_IDX, idx1, jnp.where(col == ROUTE_IDX + 1, idx2, 0.0))
    return jnp.where(col == ROUTE_W, w1, jnp.where(col == ROUTE_W + 1, w2, out))


def _route_onehot(route):
    colf = lax.broadcasted_iota(jnp.int32, route.shape, 1).astype(F32)
    oh1 = colf == route[:, ROUTE_IDX:ROUTE_IDX + 1]
    oh2 = colf == route[:, ROUTE_IDX + 1:ROUTE_IDX + 2]
    return jnp.where(jnp.logical_or(oh1, oh2), 1.0, 0.0), oh1, oh2


def _stage_c_kernel(u_ref, h_ref, mod_ref, rg_y, rg_f, rg_b, gla_g, gla_f, gla_b, hg_g, hg_f, hg_b,
                    ml_o, ml_f, ml_b, gains, w_merge, b_merge, w_branch, w_out, ffn_g, w_route, b_route,
                    h_out, u2_out, wt_out, cnt_out):
    ys = (
        _gelu_tanh(rg_y[...]) * (rg_f[...] + rg_b[...]),
        _head_rmsnorm(gla_f[...] + gla_b[...], gains[0:1, :]) * _silu(gla_g[...]),
        _head_rmsnorm(hg_f[...] + hg_b[...], gains[1:2, :]) * _silu(hg_g[...]),
        _sigmoid(ml_o[...]) * _head_rmsnorm(ml_f[...] + ml_b[...], gains[2:3, :]),
    )
    u = u_ref[...]
    d = u.shape[-1]
    merged = None
    for kk, y in enumerate(ys):
        gate = _sigmoid(jnp.dot(u, w_merge[:, kk * d:(kk + 1) * d], preferred_element_type=F32)
                              + b_merge[:, kk * d:(kk + 1) * d])
        term = gate * _mm(y, w_branch[kk])
        merged = term if merged is None else merged + term
    mix = _mm(merged, w_out[...])
    h_new = h_ref[...] + mod_ref[2:3, :] * mix
    h_out[...] = h_new
    u2 = _rmsnorm_rows(h_new, ffn_g[...]) * (1.0 + mod_ref[4:5, :]) + mod_ref[3:4, :]
    _store_tiled(u2_out, u2)
    logits = _mm_split(u2, w_route[...]) + b_route[...]
    route = _route(logits)
    wt_out[...] = route
    cnt = jnp.sum(_route_onehot(route)[0], axis=0, keepdims=True)
    cnt_out[...] = jnp.broadcast_to(cnt, cnt_out.shape)


def _stage_c_call(u, h, mod, z_rg, rg_f, rg_b, z_gla, gla_f, gla_b, z_hg, hg_f, hg_b, z_ml, ml_f, ml_b,
                  gains, w_merge, b_merge, w_branch, w_out, ffn_g, w_route, b_route, nc, t0):
    bsz, n, d = h.shape
    nt = n // TM
    w = BRANCH_W

    n_out = n - t0 * TM

    def tok(width, col=0):
        return pl.BlockSpec((None, TM, width), lambda b, t: (b, t + t0, col))

    def out_tok(width):
        return pl.BlockSpec((None, TM, width), lambda b, t: (b, t, 0))

    mod_spec = pl.BlockSpec((None, 6, d), lambda b, t: (jnp.where(t + t0 < nc, bsz, b), 0, 0))
    return pl.pallas_call(
        _stage_c_kernel,
        grid=(bsz, nt - t0),
        in_specs=[tok(d), tok(d), mod_spec,
                  tok(w, 1), tok(w), tok(w),
                  tok(w, 3), tok(w), tok(w),
                  tok(w, 4), tok(w), tok(w),
                  tok(w, 3), tok(w), tok(w),
                  _const_spec(gains.shape), _const_spec(w_merge.shape), _const_spec(b_merge.shape),
                  _const_spec(w_branch.shape), _const_spec(w_out.shape), _const_spec(ffn_g.shape),
                  _const_spec(w_route.shape), _const_spec(b_route.shape)],
        out_specs=[out_tok(d),
                   pl.BlockSpec((None, TM * SUBLANES, HEAD_W), lambda b, t: (b, t, 0)),
                   out_tok(HEAD_W),
                   pl.BlockSpec((None, None, SUBLANES, HEAD_W), lambda b, t: (b, t, 0, 0))],
        out_shape=[jax.ShapeDtypeStruct((bsz, n_out, d), F32),
                   jax.ShapeDtypeStruct((bsz, n_out * SUBLANES, HEAD_W), F32),
                   jax.ShapeDtypeStruct((bsz, n_out, HEAD_W), F32),
                   jax.ShapeDtypeStruct((bsz, nt - t0, SUBLANES, HEAD_W), F32)],
        compiler_params=_params(("parallel", "parallel")),
        name="stage_c",
    )(u, h, mod, z_rg, rg_f, rg_b, z_gla, gla_f, gla_b, z_hg, hg_f, hg_b, z_ml, ml_f, ml_b,
      gains, w_merge, b_merge, w_branch, w_out, ffn_g, w_route, b_route)


def _slots_to_smem(slots_vmem, slots_smem, sem):
    cp = pltpu.make_async_copy(slots_vmem, slots_smem, sem)
    cp.start()
    cp.wait()


def _row_copy(src, src_row, dst, dst_row, sem):
    def tile(ref, row):
        return ref.at[pl.ds(pl.multiple_of(row * SUBLANES, SUBLANES), SUBLANES), :]
    return pltpu.make_async_copy(tile(src, src_row), tile(dst, dst_row), sem)


def _to_tiles(x):
    return [x[:, k * HEAD_W:(k + 1) * HEAD_W] for k in range(SUBLANES)]


def _store_tiled(ref, x):
    rows = x.shape[0]
    for k, piece in enumerate(_to_tiles(x)):
        ref[pl.ds(k, rows, stride=SUBLANES), :] = piece


def _load_tiled(ref, rows):
    return jnp.concatenate([ref[pl.ds(k, rows, stride=SUBLANES), :] for k in range(SUBLANES)], axis=-1)


def _dispatch_kernel(seg_ref, route_ref, base_ref, tri_ref, x_ref, xs_hbm, slots_ref,
                     slots_smem, zero_buf, sem_s, sem_r, sem_z):
    @pl.when(pl.program_id(0) == 0)
    def _():
        zero_buf[...] = jnp.zeros_like(zero_buf)

        def fill(e):
            start_row = pl.multiple_of((seg_ref[e] - MOE_BLK) * SUBLANES, SUBLANES)
            return pltpu.make_async_copy(
                zero_buf, xs_hbm.at[pl.ds(start_row, MOE_BLK * SUBLANES), :], sem_z)

        for e in range(N_EXPERTS):
            @pl.when(seg_ref[N_EXPERTS + e] > 0)
            def _(e=e):
                fill(e).start()
        for e in range(N_EXPERTS):
            @pl.when(seg_ref[N_EXPERTS + e] > 0)
            def _(e=e):
                fill(e).wait()

        blk_rows = MOE_BLK * SUBLANES
        used = seg_ref[N_EXPERTS - 1] // MOE_BLK

        def tail(b):
            return pltpu.make_async_copy(
                zero_buf, xs_hbm.at[pl.ds(pl.multiple_of(b * blk_rows, blk_rows), blk_rows), :], sem_z)

        def tail_start(b, carry):
            tail(b).start()
            return carry

        def tail_wait(b, carry):
            tail(b).wait()
            return carry

        n_blk = xs_hbm.shape[0] // blk_rows
        lax.fori_loop(used, n_blk, tail_start, 0)
        lax.fori_loop(used, n_blk, tail_wait, 0)

    route = route_ref[...]
    onehot, oh1, oh2 = _route_onehot(route)
    rank = jnp.dot(tri_ref[...], onehot.astype(BF16), preferred_element_type=F32)
    pos = base_ref[...] + rank
    slot_a = jnp.sum(jnp.where(oh1, pos, 0.0), axis=-1, keepdims=True)
    slot_b = jnp.sum(jnp.where(oh2, pos, 0.0), axis=-1, keepdims=True)
    col = lax.broadcasted_iota(jnp.int32, route.shape, 1)
    both = jnp.where(col == 0, slot_a, jnp.where(col == 1, slot_b, 0.0))
    slots_ref[...] = both.T[0:SUBLANES, :].astype(jnp.int32)
    _slots_to_smem(slots_ref, slots_smem, sem_s)

    def start(t, carry):
        for j in range(2):
            _row_copy(x_ref, t, xs_hbm, slots_smem[j, t], sem_r).start(priority=j)
        return carry

    def wait(t, carry):
        for j in range(2):
            _row_copy(x_ref, 0, xs_hbm, 0, sem_r).wait()
        return carry

    lax.fori_loop(0, TM, start, 0, unroll=8)
    lax.fori_loop(0, TM, wait, 0, unroll=8)


def _experts_kernel(blk_expert, n_used, x_ref, w1, w3, w2, o_ref):
    del blk_expert

    @pl.when(pl.program_id(0) < n_used[0])
    def _():
        x = _load_tiled(x_ref, MOE_BLK)
        h1 = _mm(x, w1[...])
        h3 = _mm(x, w3[...])
        _store_tiled(o_ref, _mm(_silu(h1) * h3, w2[...]))

    @pl.when(pl.program_id(0) >= n_used[0])
    def _():
        o_ref[...] = jnp.zeros_like(o_ref)


def _combine_kernel(slots_smem, route_ref, h_ref, mod_ref, fin_g, ys_hbm, o_ref, buf, sem_r, *, final):
    def start(t, carry):
        for j in range(2):
            _row_copy(ys_hbm, slots_smem[j, t], buf.at[j], t, sem_r).start(priority=j)
        return carry

    def wait(t, carry):
        for j in range(2):
            _row_copy(ys_hbm, 0, buf.at[j], 0, sem_r).wait()
        return carry

    lax.fori_loop(0, TM, start, 0, unroll=8)
    lax.fori_loop(0, TM, wait, 0, unroll=8)
    route = route_ref[...]
    y = (route[:, ROUTE_W:ROUTE_W + 1] * _load_tiled(buf.at[0], TM)
         + route[:, ROUTE_W + 1:ROUTE_W + 2] * _load_tiled(buf.at[1], TM))
    out = h_ref[...] + mod_ref[5:6, :] * y
    if final:
        out = _rmsnorm_rows(out, fin_g[...])
    o_ref[...] = out


def _moe_call(u2, route, cnt, h, mod, fin_g, w1, w3, w2, layer, nc, t0, final):
    bsz, n_out, d = h.shape
    de = w1.shape[-1]
    rows = bsz * n_out
    tiles = rows // TM
    tiles_per_row = n_out // TM
    n_blocks = (2 * rows + N_EXPERTS * (MOE_BLK - 1)) // MOE_BLK + 1

    counts = cnt[:, :, 0, :N_EXPERTS].reshape(tiles, N_EXPERTS)
    total = jnp.sum(counts, axis=0)
    padded = jnp.ceil(total / MOE_BLK) * MOE_BLK
    seg_end = jnp.cumsum(padded)
    base = (seg_end - padded)[None, :] + jnp.cumsum(counts, axis=0) - counts
    base = _pad_cols(base, HEAD_W).reshape(tiles, 1, HEAD_W)
    blk_start = jnp.arange(n_blocks, dtype=F32) * MOE_BLK
    blk_expert = jnp.minimum(jnp.sum(blk_start[:, None] >= seg_end[None, :], axis=1), N_EXPERTS - 1)
    n_used = (seg_end[-1] / MOE_BLK).astype(jnp.int32).reshape(1)
    tri = jnp.asarray(np.tril(np.ones((TM, TM), np.float32), -1), BF16)

    assert d == SUBLANES * HEAD_W
    slot_rows = n_blocks * MOE_BLK * SUBLANES
    seg = jnp.concatenate([seg_end, padded]).astype(jnp.int32)
    xs, slots = pl.pallas_call(
        _dispatch_kernel,
        grid_spec=pltpu.PrefetchScalarGridSpec(
            num_scalar_prefetch=1,
            grid=(tiles,),
            in_specs=[pl.BlockSpec((TM, HEAD_W), lambda t, sg: (t, 0)),
                      pl.BlockSpec((None, 1, HEAD_W), lambda t, sg: (t, 0, 0)),
                      pl.BlockSpec((TM, TM), lambda t, sg: (0, 0)),
                      pl.BlockSpec((TM * SUBLANES, HEAD_W), lambda t, sg: (t, 0))],
            out_specs=[pl.BlockSpec(memory_space=pl.ANY),
                       pl.BlockSpec((SUBLANES, TM), lambda t, sg: (t, 0))],
            scratch_shapes=[pltpu.SMEM((SUBLANES, TM), jnp.int32),
                            pltpu.VMEM((MOE_BLK * SUBLANES, HEAD_W), F32),
                            pltpu.SemaphoreType.DMA, pltpu.SemaphoreType.DMA,
                            pltpu.SemaphoreType.DMA]),
        out_shape=[jax.ShapeDtypeStruct((slot_rows, HEAD_W), F32),
                   jax.ShapeDtypeStruct((tiles * SUBLANES, TM), jnp.int32)],
        compiler_params=_params(("arbitrary",)),
        name="moe_dispatch",
    )(seg, route.reshape(rows, HEAD_W), base, tri, u2.reshape(rows * SUBLANES, HEAD_W))

    blk_spec = pl.BlockSpec((MOE_BLK * SUBLANES, HEAD_W), lambda b, be, nu: (b, 0))
    used_blk_spec = pl.BlockSpec((MOE_BLK * SUBLANES, HEAD_W),
                                 lambda b, be, nu: (jnp.minimum(b, nu[0] - 1), 0))
    ys = pl.pallas_call(
        _experts_kernel,
        grid_spec=pltpu.PrefetchScalarGridSpec(
            num_scalar_prefetch=2,
            grid=(n_blocks,),
            in_specs=[used_blk_spec,
                      pl.BlockSpec((None, None, d, de), lambda b, be, nu: (layer, be[b], 0, 0)),
                      pl.BlockSpec((None, None, d, de), lambda b, be, nu: (layer, be[b], 0, 0)),
                      pl.BlockSpec((None, None, de, d), lambda b, be, nu: (layer, be[b], 0, 0))],
            out_specs=blk_spec),
        out_shape=jax.ShapeDtypeStruct((slot_rows, HEAD_W), F32),
        compiler_params=_params(("arbitrary",)),
        name="moe_experts",
    )(blk_expert.astype(jnp.int32), n_used, xs, w1, w3, w2)

    def mod_index(t):
        b, pos = t // tiles_per_row, t % tiles_per_row
        return (jnp.where(pos + t0 < nc, bsz, b), 0, 0)

    out = pl.pallas_call(
        functools.partial(_combine_kernel, final=final),
        grid=(tiles,),
        in_specs=[pl.BlockSpec((None, SUBLANES, TM), lambda t: (t, 0, 0), memory_space=pltpu.SMEM),
                  pl.BlockSpec((TM, HEAD_W), lambda t: (t, 0)),
                  pl.BlockSpec((TM, d), lambda t: (t, 0)),
                  pl.BlockSpec((None, 6, d), mod_index),
                  _const_spec((1, d)),
                  pl.BlockSpec(memory_space=pl.ANY)],
        out_specs=pl.BlockSpec((TM, d), lambda t: (t, 0)),
        out_shape=jax.ShapeDtypeStruct((rows, d), F32),
        scratch_shapes=[pltpu.VMEM((2, TM * SUBLANES, HEAD_W), F32),
                        pltpu.SemaphoreType.DMA],
        compiler_params=_params(("arbitrary",)),
        name="moe_combine",
    )(slots.reshape(tiles, SUBLANES, TM), route.reshape(rows, HEAD_W), h.reshape(rows, d), mod, fin_g, ys)
    return out.reshape(bsz, n_out, d)


def _pad_heads(w, dk):
    lead = w.shape[:-1]
    w = w.reshape(lead + (N_HEADS, dk))
    w = jnp.pad(w, [(0, 0)] * len(lead) + [(0, 0), (0, HEAD_W - dk)])
    return w.reshape(lead + (N_HEADS * HEAD_W,))


def _pad_cols(w, width):
    return jnp.pad(w, [(0, 0)] * (w.ndim - 1) + [(0, width - w.shape[-1])])


def _split_w_in(w_in):
    bw = BRANCH_W
    sizes = [bw, bw, N_HEADS * GLA_DK, N_HEADS * GLA_DK, bw, bw, 2 * GLA_RANK,
             bw, bw, 2 * bw, bw, bw, bw, bw, bw, 16]
    offs = np.cumsum([0] + sizes)
    p = [w_in[:, offs[i]:offs[i + 1]] for i in range(len(sizes))]
    w_rg = jnp.concatenate([p[0], p[1]], axis=1)
    w_gla = jnp.concatenate([_pad_heads(p[2], GLA_DK), _pad_heads(p[3], GLA_DK), p[4], p[5],
                             _pad_cols(p[6], HEAD_W)], axis=1)
    w_hg = jnp.concatenate([p[7], p[8], p[9], p[10]], axis=1)
    gates = p[15].reshape(-1, 2, 2, N_HEADS)
    w_gi = _pad_cols(gates[:, :, 0].reshape(-1, 2 * N_HEADS), HEAD_W)
    w_gf = _pad_cols(gates[:, :, 1].reshape(-1, 2 * N_HEADS), HEAD_W)
    w_ml = jnp.concatenate([p[11], p[12], p[13], p[14], w_gi, w_gf], axis=1)
    return [w.astype(BF16) for w in (w_rg, w_gla, w_hg, w_ml)]


def _block_diag(w):
    k, n = w.shape[-3], w.shape[-1]
    eye = jnp.eye(k, dtype=w.dtype)
    full = jnp.einsum('...kij,kl->...kilj', w, eye)
    return full.reshape(w.shape[:-3] + (k * n, k * n))


def kernel(x, c, ctx, c_ctx, ada_w, ada_b, norm_mix_g, norm_ffn_g, w_in, rg_conv_w, rg_conv_b, rg_gate_w, rg_gate_b, rg_lambda, gla_w_lr, gla_b_lr, gla_norm_g, hgrn_lb_logits, hgrn_norm_g, ml_conv_w, ml_conv_b, ml_gate_b, ml_norm_g, w_branch, w_merge, b_merge, w_out, moe_w_group, moe_b_group, moe_w_expert, moe_b_expert, moe_w1, moe_w3, moe_w2, final_norm_g):
    bsz, seq, d = x.shape
    n_ctx = ctx.shape[1]
    depth = ada_w.shape[0]
    assert n_ctx % TM == 0 and seq % TM == 0 and d == 2 * BRANCH_W
    nc = n_ctx // TM

    h = jnp.concatenate([ctx, x], axis=1)
    cvec = jnp.zeros((SUBLANES, d), F32).at[:bsz].set(c).at[bsz].set(c_ctx)
    mod_all = _mod_call(cvec, ada_w, ada_b).reshape(depth, SUBLANES, 6, d)[:, :bsz + 1]

    lb_cum = jnp.cumsum(jax.nn.softmax(hgrn_lb_logits.astype(F32), axis=0), axis=0)
    hgrn_lb = lb_cum - lb_cum[:1]
    tables = [t for levels in SPLIT_LEVELS for t in _decay_tables(levels)]
    sel0, masks0 = _decay_tables(0)
    tri_sel = sel0[:, :CHUNK, :]
    tri_mask = masks0[:, 0]

    out = None
    for l in range(depth):
        last = l == depth - 1
        mod = mod_all[l]
        w_rg, w_gla, w_hg, w_ml = _split_w_in(w_in[l])
        z_rg, z_gla, z_hg, z_ml, u = _stage_a_call(h, mod, norm_mix_g[l], w_rg, w_gla, w_hg, w_ml, nc)

        rg_f, rg_b = _rg_call(z_rg, rg_conv_w[l], rg_conv_b[l].reshape(1, -1),
                              _block_diag(rg_gate_w[l]).astype(BF16),
                              rg_gate_b[l].reshape(4, -1), rg_lambda[l], nc)

        wlr = jnp.zeros((2, HEAD_W, N_HEADS * HEAD_W), F32)
        wlr_p = _pad_heads(gla_w_lr[l], GLA_DK)
        wlr = wlr.at[0, :GLA_RANK].set(wlr_p[0]).at[1, GLA_RANK:2 * GLA_RANK].set(wlr_p[1])
        gla_f, gla_b = _gla_call(z_gla, wlr.astype(BF16), _pad_heads(gla_b_lr[l], GLA_DK), tables, nc)

        lb = hgrn_lb[l]
        lbp = jnp.concatenate([lb, jnp.log(lb), jnp.log1p(-lb), jnp.zeros((2, lb.shape[-1]), F32)], axis=0)
        hg_f, hg_b = _hg_call(z_hg, lbp, tables, nc)

        gbias = _pad_cols(ml_gate_b[l].transpose(1, 0, 2).reshape(2, -1), HEAD_W)
        ml_f, ml_b = _ml_call(z_ml, ml_conv_w[l], ml_conv_b[l].reshape(1, -1), gbias, tri_sel, tri_mask, nc)

        gains = jnp.zeros((SUBLANES, HEAD_W), F32).at[0].set(gla_norm_g[l]).at[1].set(hgrn_norm_g[l]).at[2].set(ml_norm_g[l])
        w_route = _split_cols(_pad_cols(jnp.concatenate([moe_w_expert[l], moe_w_group[l]], axis=1), HEAD_W))
        b_route = _pad_cols(jnp.concatenate([moe_b_expert[l], moe_b_group[l]]).reshape(1, -1), HEAD_W)
        t0 = nc if last else 0
        h_mid, u2, route, cnt = _stage_c_call(
            u, h, mod, z_rg, rg_f, rg_b, z_gla, gla_f, gla_b, z_hg, hg_f, hg_b, z_ml, ml_f, ml_b,
            gains, w_merge[l].astype(BF16), b_merge[l].reshape(1, -1), w_branch[l].astype(BF16),
            w_out[l].astype(BF16), norm_ffn_g[l].reshape(1, -1), w_route, b_route, nc, t0)

        res = _moe_call(u2, route, cnt, h_mid, mod, final_norm_g.reshape(1, -1), moe_w1,
                        moe_w3, moe_w2, l, nc, t0, last)
        if last:
            out = res
        else:
            h = res
    return out
```

```python
import functools

import numpy as np
import jax
import jax.numpy as jnp
from jax import lax
from jax.experimental import pallas as pl
from jax.experimental.pallas import tpu as pltpu

F32 = jnp.float32
BF16 = jnp.bfloat16

EPS = 1e-6
TM = 256
CHUNK = 64
SUBLANES = 8
N_HEADS = 4
HEAD_W = 128
BRANCH_W = 512
CONV_W = 4
CONV_LEFT = 2
RG_C = 8.0
GLA_DK = 64
GLA_RANK = 16
GLA_GATE_NORM = 16.0
N_GROUPS = 4
EXPERTS_PER_GROUP = 4
N_EXPERTS = 16
MOE_BLK = 512
ROUTE_IDX = 16
ROUTE_W = 18
N_LEVELS = 6
MILD_DECAY = -64.0
VMEM_LIMIT = 56 * 1024 * 1024

W_RG = 2 * BRANCH_W
W_GLA = 4 * BRANCH_W + HEAD_W
W_HG = 5 * BRANCH_W
W_ML = 4 * BRANCH_W + 2 * HEAD_W
SPLIT_LEVELS = (1, 2, N_LEVELS)


def _mm(a, b):
    return jnp.dot(a.astype(BF16), b.astype(BF16), preferred_element_type=F32)


def _mm_nt(a, b):
    return lax.dot_general(a.astype(BF16), b.astype(BF16), (((1,), (1,)), ((), ())),
                           preferred_element_type=F32)


def _mm_tn(a, b):
    return lax.dot_general(a.astype(BF16), b.astype(BF16), (((0,), (0,)), ((), ())),
                           preferred_element_type=F32)


def _mm_sel(sel, x):
    x1 = x.astype(BF16)
    x2 = (x - x1.astype(F32)).astype(BF16)
    dot = functools.partial(jnp.dot, preferred_element_type=F32)
    return dot(sel, x1) + dot(sel, x2)


def _split_cols(w):
    hi = w.astype(BF16)
    return jnp.concatenate([hi, (w - hi.astype(F32)).astype(BF16)], axis=1)


def _mm_split(x, w_split):
    n = w_split.shape[1] // 2
    x1 = x.astype(BF16)
    x2 = (x - x1.astype(F32)).astype(BF16)
    p = jnp.dot(x1, w_split, preferred_element_type=F32)
    return p[:, :n] + p[:, n:] + jnp.dot(x2, w_split[:, :n], preferred_element_type=F32)


def _softplus_neg_abs(x):
    return jnp.log(1.0 + jnp.exp(-jnp.abs(x)))


def _log_sigmoid(x):
    return jnp.minimum(x, 0.0) - _softplus_neg_abs(x)


def _sigmoid(x):
    return 0.5 * jnp.tanh(0.5 * x) + 0.5


def _silu(x):
    return x * _sigmoid(x)


def _rmsnorm_rows(x, g):
    return x * lax.rsqrt(jnp.mean(x * x, axis=-1, keepdims=True) + EPS) * g


def _head_rmsnorm(o, g):
    parts = [_rmsnorm_rows(o[:, h * HEAD_W:(h + 1) * HEAD_W], g) for h in range(N_HEADS)]
    return jnp.concatenate(parts, axis=-1)


def _bwd_tile(s, nc, nt):
    return jnp.where(s < nc, nc - 1 - s, nt - 1 - (s - nc))


def _const_spec(shape):
    nd = len(shape)
    return pl.BlockSpec(shape, lambda *_: (0,) * nd)


def _params(sem):
    return pltpu.CompilerParams(dimension_semantics=sem, vmem_limit_bytes=VMEM_LIMIT)


def _mod_kernel(c_ref, w_ref, b_ref, o_ref):
    cv = _silu(c_ref[...])
    o_ref[...] = jnp.dot(cv, w_ref[...], precision=lax.Precision.HIGHEST,
                         preferred_element_type=F32) + b_ref[...]


def _mod_call(cvec, ada_w, ada_b):
    depth, d, six_d = ada_w.shape
    tn = 1024
    return pl.pallas_call(
        _mod_kernel,
        grid=(depth, six_d // tn),
        in_specs=[pl.BlockSpec((SUBLANES, d), lambda l, j: (0, 0)),
                  pl.BlockSpec((None, d, tn), lambda l, j: (l, 0, j)),
                  pl.BlockSpec((None, 1, tn), lambda l, j: (l, 0, j))],
        out_specs=pl.BlockSpec((None, SUBLANES, tn), lambda l, j: (l, 0, j)),
        out_shape=jax.ShapeDtypeStruct((depth, SUBLANES, six_d), F32),
        compiler_params=_params(("parallel", "parallel")),
        name="adaln_mod",
    )(cvec, ada_w, ada_b.reshape(depth, 1, six_d))


def _stage_a_kernel(h_ref, mod_ref, g_ref, w_rg, w_gla, w_hg, w_ml,
                    z_rg, z_gla, z_hg, z_ml, u_ref):
    x = h_ref[...]
    u = _rmsnorm_rows(x, g_ref[...]) * (1.0 + mod_ref[1:2, :]) + mod_ref[0:1, :]
    ub = u.astype(BF16)
    u_ref[...] = ub
    z_rg[...] = jnp.dot(ub, w_rg[...], preferred_element_type=F32)
    z_gla[...] = jnp.dot(ub, w_gla[...], preferred_element_type=F32)
    z_hg[...] = jnp.dot(ub, w_hg[...], preferred_element_type=F32)
    z_ml[...] = jnp.dot(ub, w_ml[...], preferred_element_type=F32)


def _stage_a_call(h, mod, norm_g, w_rg, w_gla, w_hg, w_ml, nc):
    bsz, n, d = h.shape
    tm = TM

    def tok(w):
        return pl.BlockSpec((None, tm, w), lambda b, t: (b, t, 0))

    def weight(w):
        return pl.BlockSpec((d, w), lambda b, t: (0, 0), pipeline_mode=pl.Buffered(1))

    return pl.pallas_call(
        _stage_a_kernel,
        grid=(bsz, n // tm),
        in_specs=[tok(d),
                  pl.BlockSpec((None, 6, d), lambda b, t: (jnp.where(t < nc, bsz, b), 0, 0)),
                  _const_spec((1, d)),
                  weight(W_RG), weight(W_GLA), weight(W_HG), weight(W_ML)],
        out_specs=[tok(W_RG), tok(W_GLA), tok(W_HG), tok(W_ML), tok(d)],
        out_shape=[jax.ShapeDtypeStruct((bsz, n, W_RG), F32),
                   jax.ShapeDtypeStruct((bsz, n, W_GLA), F32),
                   jax.ShapeDtypeStruct((bsz, n, W_HG), F32),
                   jax.ShapeDtypeStruct((bsz, n, W_ML), F32),
                   jax.ShapeDtypeStruct((bsz, n, d), BF16)],
        compiler_params=_params(("parallel", "parallel")),
        name="stage_a",
    )(h, mod, norm_g.reshape(1, d), w_rg, w_gla, w_hg, w_ml)


def _conv_tile(x_ref, prev_ref, next_ref, pad_ref, cw_ref, cb_ref, tile, nc, nt):
    prev_ok = jnp.logical_and(tile != 0, tile != nc)
    next_ok = jnp.logical_and(tile != nc - 1, tile != nt - 1)
    pad_ref[0:SUBLANES, :] = jnp.where(prev_ok, prev_ref[...], 0.0)
    pad_ref[SUBLANES:SUBLANES + TM, :] = x_ref[...]
    pad_ref[SUBLANES + TM:2 * SUBLANES + TM, :] = jnp.where(next_ok, next_ref[...], 0.0)
    acc = cb_ref[...]
    for j in range(CONV_W):
        off = SUBLANES - CONV_LEFT + j
        acc = acc + cw_ref[j:j + 1, :] * pad_ref[off:off + TM, :]
    return acc


def _halo_specs(width, col_block, tile_fn, n):
    rows = TM // SUBLANES
    last = n // SUBLANES - 1
    cur = pl.BlockSpec((None, TM, width), lambda b, s: (b, tile_fn(s), col_block))
    prev = pl.BlockSpec((None, SUBLANES, width),
                        lambda b, s: (b, jnp.maximum(tile_fn(s) * rows - 1, 0), col_block))
    nxt = pl.BlockSpec((None, SUBLANES, width),
                       lambda b, s: (b, jnp.minimum((tile_fn(s) + 1) * rows, last), col_block))
    return [cur, prev, nxt]


def _scan_rows8(a, b, reverse):
    n = a.shape[0]
    pos = jnp.bitwise_and(lax.broadcasted_iota(jnp.int32, a.shape, 0), SUBLANES - 1)
    k = 1
    while k < SUBLANES:
        if reverse:
            a_s, b_s, ok = pltpu.roll(a, n - k, 0), pltpu.roll(b, n - k, 0), pos < SUBLANES - k
        else:
            a_s, b_s, ok = pltpu.roll(a, k, 0), pltpu.roll(b, k, 0), pos >= k
        b = b + a * jnp.where(ok, b_s, 0.0)
        a = a * jnp.where(ok, a_s, 1.0)
        k *= 2
    return a, b


def _scan_tile(a, b, carry, o_ref, reverse):
    a8, b8 = _scan_rows8(a, b, reverse)
    groups = a.shape[0] // SUBLANES
    order = range(groups - 1, -1, -1) if reverse else range(groups)
    for r in order:
        rows = slice(r * SUBLANES, (r + 1) * SUBLANES)
        h = b8[rows] + a8[rows] * carry
        o_ref[rows, :] = h
        carry = h[0:1, :] if reverse else h[SUBLANES - 1:SUBLANES, :]
    return carry


def _rg_kernel(xf, xf_p, xf_n, xb, xb_p, xb_n, cw, cb, gw, gb, lam,
               hf_ref, hb_ref, pad_ref, carry_ref, *, nc, nt):
    s = pl.program_id(1)

    @pl.when(s == 0)
    def _():
        carry_ref[...] = jnp.zeros_like(carry_ref)

    dirs = ((xf, xf_p, xf_n, hf_ref, s, False),
            (xb, xb_p, xb_n, hb_ref, _bwd_tile(s, nc, nt), True))
    for d, (x_ref, p_ref, n_ref, o_ref, tile, reverse) in enumerate(dirs):
        x = _conv_tile(x_ref, p_ref, n_ref, pad_ref, cw, cb, tile, nc, nt)
        r = _sigmoid(_mm(x, gw[d, 0]) + gb[2 * d:2 * d + 1, :])
        i = _sigmoid(_mm(x, gw[d, 1]) + gb[2 * d + 1:2 * d + 2, :])
        lam_d = lam[d:d + 1, :]
        softplus = jnp.maximum(-lam_d, 0.0) + jnp.log1p(jnp.exp(-jnp.abs(lam_d)))
        log_a = -RG_C * r * softplus
        a = jnp.exp(log_a)
        t = jnp.tanh(log_a)
        bt = jnp.sqrt(-2.0 * t / (1.0 - t)) * (i * x)
        carry_ref[d:d + 1, :] = _scan_tile(a, bt, carry_ref[d:d + 1, :], o_ref, reverse)


def _rg_call(z_rg, cw, cb, gw, gb, lam, nc):
    bsz, n, _ = z_rg.shape
    nt = n // TM
    w = BRANCH_W
    fwd = lambda s: s
    bwd = lambda s: _bwd_tile(s, nc, nt)
    out = lambda fn: pl.BlockSpec((None, TM, w), lambda b, s: (b, fn(s), 0))
    return pl.pallas_call(
        functools.partial(_rg_kernel, nc=nc, nt=nt),
        grid=(bsz, nt),
        in_specs=_halo_specs(w, 0, fwd, n) + _halo_specs(w, 0, bwd, n) + [
            _const_spec((CONV_W, w)), _const_spec((1, w)),
            _const_spec((2, 2, w, w)), _const_spec((4, w)), _const_spec((2, w))],
        out_specs=[out(fwd), out(bwd)],
        out_shape=[jax.ShapeDtypeStruct((bsz, n, w), F32)] * 2,
        scratch_shapes=[pltpu.VMEM((TM + 2 * SUBLANES, w), F32), pltpu.VMEM((2, w), F32)],
        compiler_params=_params(("parallel", "arbitrary")),
        name="rglru",
    )(z_rg, z_rg, z_rg, z_rg, z_rg, z_rg, cw, cb, gw, gb, lam)


def _decay_tables(levels):
    n = CHUNK
    blk = n >> levels
    sel = np.zeros((levels + 3, n, n), np.float32)
    masks = np.zeros((levels + 1, n, n), np.float32)
    for lvl in range(levels):
        half = n >> (lvl + 1)
        for r in range(n):
            start = (r // (2 * half)) * 2 * half
            ref = start + half - 1
            if r - start >= half:
                sel[lvl, r, ref + 1:r + 1] = 1.0
                masks[lvl, r, start:start + half] = 1.0
            else:
                sel[lvl, r, r + 1:ref + 1] = 1.0
    sel[levels] = np.tril(np.ones((n, n), np.float32))
    sel[levels + 1] = 1.0 - sel[levels]
    for r in range(n):
        start = (r // blk) * blk
        sel[levels + 2, r, r + 1:start + blk] = 1.0
        masks[levels, r, start:r + 1] = 1.0
    if levels == 0 or levels == N_LEVELS:
        sel = sel[:levels + 2]
    sel_b = sel[:, ::-1, ::-1].reshape(-1, n)
    masks_b = masks[:, ::-1, ::-1]
    return (jnp.asarray(np.stack([sel.reshape(-1, n), sel_b]), BF16),
            jnp.asarray(np.stack([masks, masks_b]), F32))


def _ref_rows(levels, reverse):
    def runs(block, offset):
        return [(s + offset, block) for s in range(0, CHUNK, block)]

    out = [runs(CHUNK >> lvl, (CHUNK >> (lvl + 1)) - 1) for lvl in range(levels)]
    out.append(runs(CHUNK, CHUNK - 1))
    out.append(runs(CHUNK >> levels, (CHUNK >> levels) - 1))
    if reverse:
        out = [[(CHUNK - 1 - r, n) for r, n in reversed(segs)] for segs in out]
    return out


def _rows_bcast(b, segs):
    return jnp.concatenate([jnp.broadcast_to(b[r:r + 1, :], (n, b.shape[-1])) for r, n in segs], axis=0)


def _gla_chunk(q, k, v, g, sel, masks, st_ref, d, reverse, levels):
    if levels == N_LEVELS:
        x = jnp.exp(_mm_sel(sel, g))
        x_lvl = [x[lvl * CHUNK:(lvl + 1) * CHUNK] for lvl in range(levels)]
        x_cum = x[levels * CHUNK:(levels + 1) * CHUNK]
        x_rest = x[(levels + 1) * CHUNK:(levels + 2) * CHUNK]
        xk_blk = xq_blk = None
    else:
        b = _mm_sel(sel[levels * CHUNK:(levels + 1) * CHUNK], g)
        refs = _ref_rows(levels, reverse)
        x_cum = jnp.exp(b)
        x_lvl = [jnp.exp(-jnp.abs(b - _rows_bcast(b, segs))) for segs in refs[:levels]]
        x_rest = jnp.exp(-jnp.abs(b - _rows_bcast(b, refs[levels])))
        if levels == 0:
            xk_blk, xq_blk = x_rest, jnp.exp(jnp.abs(b - _rows_bcast(b, refs[levels])))
        else:
            d_blk = jnp.abs(b - _rows_bcast(b, refs[levels + 1]))
            xk_blk, xq_blk = jnp.exp(-d_blk), jnp.exp(d_blk)
    outs = []
    for h in range(N_HEADS):
        cols = slice(h * HEAD_W, (h + 1) * HEAD_W)
        qh, kh, vh = q[:, cols], k[:, cols], v[:, cols]
        kx = kh * x_rest[:, cols]
        if xk_blk is None:
            sc = _mm_nt(qh, kh)
        elif levels == 0:
            sc = _mm_nt(qh * xq_blk[:, cols], kx)
        else:
            sc = _mm_nt(qh * xq_blk[:, cols], kh * xk_blk[:, cols])
        sc = jnp.where(masks[levels] > 0.0, sc, 0.0)
        for lvl in range(levels):
            xl = x_lvl[lvl][:, cols]
            sc = sc + jnp.where(masks[lvl] > 0.0, _mm_nt(qh * xl, kh * xl), 0.0)
        st = st_ref[d, h]
        outs.append(_mm(sc, vh) + _mm_nt(qh * x_cum[:, cols], st))
        x_end = x_cum[0:1, cols] if reverse else x_cum[CHUNK - 1:CHUNK, cols]
        st_ref[d, h] = st * x_end + _mm_tn(vh, kx)
    return jnp.concatenate(outs, axis=-1)


def _gla_step(sides, tabs, st_ref):
    n_chunks = TM // CHUNK

    def min_block_sum(block):
        mins = [jnp.min(jnp.sum(side[3].reshape(TM // block, block, side[3].shape[-1]), axis=1))
                for side in sides]
        return functools.reduce(jnp.minimum, mins)

    conds = []
    taken = None
    for levels in SPLIT_LEVELS[:-1]:
        ok = min_block_sum(CHUNK >> levels) > MILD_DECAY
        conds.append(ok if taken is None else jnp.logical_and(ok, jnp.logical_not(taken)))
        taken = ok if taken is None else jnp.logical_or(taken, ok)
    conds.append(jnp.logical_not(taken))

    for levels, cond, (sel_ref, mask_ref) in zip(SPLIT_LEVELS, conds, tabs):
        @pl.when(cond)
        def _(levels=levels, sel_ref=sel_ref, mask_ref=mask_ref):
            for c in range(n_chunks):
                for i, (q, k, v, g, o_ref, reverse) in enumerate(sides):
                    cc = n_chunks - 1 - c if reverse else c
                    rows = slice(cc * CHUNK, (cc + 1) * CHUNK)
                    o_ref[rows, :] = _gla_chunk(q[rows], k[rows], v[rows], g[rows],
                                                sel_ref[int(reverse)], mask_ref[int(reverse)],
                                                st_ref, i, reverse, levels).astype(o_ref.dtype)


def _pairs(refs):
    return tuple(zip(refs[0::2], refs[1::2]))


def _gla_kernel(zf, zb, wlr, blr, *rest):
    tabs, (of_ref, ob_ref, st_ref) = _pairs(rest[:-3]), rest[-3:]
    w = BRANCH_W

    @pl.when(pl.program_id(0) == 0)
    def _():
        st_ref[...] = jnp.zeros_like(st_ref)

    sides = []
    for b in range(zf.shape[0]):
        for d, (z_ref, o_ref) in enumerate(((zf, of_ref), (zb, ob_ref))):
            pre = _mm(z_ref[b, :, 4 * w:4 * w + HEAD_W], wlr[d]) + blr[d:d + 1, :]
            g = _log_sigmoid(pre) * (1.0 / GLA_GATE_NORM)
            q = z_ref[b, :, 0:w] * (GLA_DK ** -0.5)
            sides.append((q, z_ref[b, :, w:2 * w], z_ref[b, :, 2 * w:3 * w], g, o_ref.at[b], d == 1))
    _gla_step(sides, tabs, st_ref)


def _hg_kernel(zf, zb, lbp, *rest):
    tabs, (of_ref, ob_ref, st_ref) = _pairs(rest[:-3]), rest[-3:]
    w = BRANCH_W

    @pl.when(pl.program_id(0) == 0)
    def _():
        st_ref[...] = jnp.zeros_like(st_ref)

    sides = []
    for b in range(zf.shape[0]):
        for d, (z_ref, o_ref) in enumerate(((zf, of_ref), (zb, ob_ref))):
            f = z_ref[b, :, (2 + d) * w:(3 + d) * w]
            lb = lbp[d:d + 1, :]
            log_lb = lbp[2 + d:3 + d, :]
            log_1m = lbp[4 + d:5 + d, :]
            c = log_1m + _log_sigmoid(f)
            g = jnp.maximum(log_lb, c) + _softplus_neg_abs(log_lb - c)
            k = (1.0 - lb) * _sigmoid(-f)
            q = _silu(z_ref[b, :, 0:w]) * (HEAD_W ** -0.5)
            sides.append((q, k, z_ref[b, :, w:2 * w], g, o_ref.at[b], d == 1))
    _gla_step(sides, tabs, st_ref)


def _mixer_specs(widths_cols, tile_fn):
    return [pl.BlockSpec((None, TM, w), lambda b, s, c=c: (b, tile_fn(s), c)) for w, c in widths_cols]


def _all_rows_spec(bsz, width, tile_fn):
    return pl.BlockSpec((bsz, TM, width), lambda s: (0, tile_fn(s), 0))


def _gla_call(z_gla, wlr, blr, tables, nc):
    bsz, n, _ = z_gla.shape
    nt = n // TM
    w = BRANCH_W
    fwd = lambda s: s
    bwd = lambda s: _bwd_tile(s, nc, nt)
    return pl.pallas_call(
        _gla_kernel,
        grid=(nt,),
        in_specs=[_all_rows_spec(bsz, W_GLA, fwd), _all_rows_spec(bsz, W_GLA, bwd),
                  _const_spec(wlr.shape), _const_spec(blr.shape)] + [_const_spec(t.shape) for t in tables],
        out_specs=[_all_rows_spec(bsz, w, fwd), _all_rows_spec(bsz, w, bwd)],
        out_shape=[jax.ShapeDtypeStruct((bsz, n, w), BF16)] * 2,
        scratch_shapes=[pltpu.VMEM((2 * bsz, N_HEADS, HEAD_W, HEAD_W), F32)],
        compiler_params=_params(("arbitrary",)),
        name="gla",
    )(z_gla, z_gla, wlr, blr, *tables)


def _hg_call(z_hg, lbp, tables, nc):
    bsz, n, _ = z_hg.shape
    nt = n // TM
    w = BRANCH_W
    fwd = lambda s: s
    bwd = lambda s: _bwd_tile(s, nc, nt)
    return pl.pallas_call(
        _hg_kernel,
        grid=(nt,),
        in_specs=[_all_rows_spec(bsz, 4 * w, fwd), _all_rows_spec(bsz, 4 * w, bwd),
                  _const_spec(lbp.shape)] + [_const_spec(t.shape) for t in tables],
        out_specs=[_all_rows_spec(bsz, w, fwd), _all_rows_spec(bsz, w, bwd)],
        out_shape=[jax.ShapeDtypeStruct((bsz, n, w), BF16)] * 2,
        scratch_shapes=[pltpu.VMEM((2 * bsz, N_HEADS, HEAD_W, HEAD_W), F32)],
        compiler_params=_params(("arbitrary",)),
        name="hgrn2",
    )(z_hg, z_hg, lbp, *tables)


def _cummax_rows(a, reverse):
    n = a.shape[0]
    row = lax.broadcasted_iota(jnp.int32, a.shape, 0)
    k = 1
    while k < n:
        if reverse:
            a_s, ok = pltpu.roll(a, n - k, 0), row < n - k
        else:
            a_s, ok = pltpu.roll(a, k, 0), row >= k
        a = jnp.maximum(a, jnp.where(ok, a_s, -jnp.inf))
        k *= 2
    return a


def _ml_chunk(q, k, v, gi, log_f, tri_sel, tri_mask, ct_ref, m_ref, slot, d, reverse):
    b = _mm_sel(tri_sel, log_f)
    a = gi - b
    m_prev = m_ref[slot:slot + 1, :]
    m_rel = jnp.maximum(_cummax_rows(a, reverse), m_prev)
    w_inter = jnp.exp(m_prev - m_rel)
    exp_neg_m = jnp.exp(-(b + m_rel))
    last = 0 if reverse else CHUNK - 1
    b_end = b[last:last + 1, :]
    m_new = b_end + m_rel[last:last + 1, :]
    w_end = jnp.exp(b_end - b + gi - m_new)
    decay = jnp.exp(b_end + m_prev - m_new)
    m_ref[slot:slot + 1, :] = m_new
    a_t = a.T
    ones = jnp.ones((CHUNK, HEAD_W), BF16)
    outs = []
    for h in range(N_HEADS):
        cols = slice(h * HEAD_W, (h + 1) * HEAD_W)
        qh, kh, vh = q[:, cols], k[:, cols], v[:, cols]
        c = d * N_HEADS + h
        log_w = jnp.where(tri_mask > 0.0, a_t[c:c + 1, :] - m_rel[:, c:c + 1], -jnp.inf)
        s = _mm_nt(qh, kh) * jnp.exp(log_w)
        v_ext = jnp.concatenate([vh.astype(BF16), ones], axis=-1)
        st = slot * N_HEADS + h
        ct = ct_ref[st]
        tot = _mm(s, v_ext) + w_inter[:, c:c + 1] * _mm_nt(qh, ct)
        num, den = tot[:, :HEAD_W], tot[:, HEAD_W:]
        outs.append(num / jnp.maximum(jnp.abs(den), exp_neg_m[:, c:c + 1]))
        ct_ref[st] = decay[:, c:c + 1] * ct + _mm_tn(v_ext, kh * w_end[:, c:c + 1])
    return jnp.concatenate(outs, axis=-1)


def _ml_kernel(zf, zf_p, zf_n, zb, zb_p, zb_n, cw, cb, gbias,
               tri_sel, tri_mask, of_ref, ob_ref, pad_ref, ct_ref, m_ref, *, nc, nt):
    s = pl.program_id(0)
    w = BRANCH_W
    n_chunks = TM // CHUNK

    @pl.when(s == 0)
    def _():
        ct_ref[...] = jnp.zeros_like(ct_ref)
        m_ref[...] = jnp.zeros_like(m_ref)

    dirs = ((zf, zf_p, zf_n, of_ref, s), (zb, zb_p, zb_n, ob_ref, _bwd_tile(s, nc, nt)))
    sides = []
    for b in range(zf.shape[0]):
        for d, (z_ref, p_ref, nx_ref, o_ref, tile) in enumerate(dirs):
            slot = 2 * b + d
            qk = _silu(_conv_tile(z_ref.at[b, :, 0:2 * w], p_ref.at[b], nx_ref.at[b],
                                  pad_ref.at[slot], cw, cb, tile, nc, nt))
            gi = z_ref[b, :, 4 * w:4 * w + HEAD_W] + gbias[0:1, :]
            log_f = _log_sigmoid(z_ref[b, :, 4 * w + HEAD_W:4 * w + 2 * HEAD_W] + gbias[1:2, :])
            sides.append((qk[:, :w] * (HEAD_W ** -0.5), qk[:, w:], z_ref[b, :, 2 * w:3 * w],
                          gi, log_f, o_ref.at[b], slot, d))
    for c in range(n_chunks):
        for q, k, v, gi, log_f, o_ref, slot, d in sides:
            cc = n_chunks - 1 - c if d == 1 else c
            rows = slice(cc * CHUNK, (cc + 1) * CHUNK)
            o_ref[rows, :] = _ml_chunk(q[rows], k[rows], v[rows], gi[rows], log_f[rows], tri_sel[d],
                                       tri_mask[d], ct_ref, m_ref, slot, d, d == 1).astype(o_ref.dtype)


def _ml_call(z_ml, cw, cb, gbias, tri_sel, tri_mask, nc):
    bsz, n, _ = z_ml.shape
    nt = n // TM
    w = BRANCH_W
    fwd = lambda s: s
    bwd = lambda s: _bwd_tile(s, nc, nt)
    rows8 = TM // SUBLANES
    last8 = n // SUBLANES - 1

    def side(fn):
        return [_all_rows_spec(bsz, W_ML, fn),
                pl.BlockSpec((bsz, SUBLANES, 2 * w), lambda s: (0, jnp.maximum(fn(s) * rows8 - 1, 0), 0)),
                pl.BlockSpec((bsz, SUBLANES, 2 * w), lambda s: (0, jnp.minimum((fn(s) + 1) * rows8, last8), 0))]

    return pl.pallas_call(
        functools.partial(_ml_kernel, nc=nc, nt=nt),
        grid=(nt,),
        in_specs=side(fwd) + side(bwd) + [
            _const_spec((CONV_W, 2 * w)), _const_spec((1, 2 * w)), _const_spec((2, HEAD_W)),
            _const_spec(tri_sel.shape), _const_spec(tri_mask.shape)],
        out_specs=[_all_rows_spec(bsz, w, fwd), _all_rows_spec(bsz, w, bwd)],
        out_shape=[jax.ShapeDtypeStruct((bsz, n, w), BF16)] * 2,
        scratch_shapes=[pltpu.VMEM((2 * bsz, TM + 2 * SUBLANES, 2 * w), F32),
                        pltpu.VMEM((2 * bsz * N_HEADS, 2 * HEAD_W, HEAD_W), F32),
                        pltpu.VMEM((2 * bsz, HEAD_W), F32)],
        compiler_params=_params(("arbitrary",)),
        name="mlstm",
    )(*([z_ml] * 6), cw, cb, gbias, tri_sel, tri_mask)


def _gelu_tanh(x):
    return 0.5 * x * (1.0 + jnp.tanh(0.7978845608028654 * (x + 0.044715 * (x * x * x))))


def _route(logits):
    col = lax.broadcasted_iota(jnp.int32, logits.shape, 1)
    colf = col.astype(F32)
    is_g = jnp.logical_and(col >= N_EXPERTS, col < N_EXPERTS + N_GROUPS)
    is_e = col < N_EXPERTS
    neg = -jnp.inf
    big = 1e9
    gl = jnp.where(is_g, logits, neg)
    g_max = jnp.max(gl, axis=-1, keepdims=True)
    grp = jnp.min(jnp.where(gl == g_max, colf, big), axis=-1, keepdims=True) - N_EXPERTS
    p_grp = 1.0 / jnp.sum(jnp.exp(gl - g_max), axis=-1, keepdims=True)
    col_grp = lax.shift_right_logical(col, EXPERTS_PER_GROUP.bit_length() - 1)
    in_grp = jnp.logical_and(is_e, col_grp.astype(F32) == grp)
    e1 = jnp.where(in_grp, logits, neg)
    top1 = jnp.max(e1, axis=-1, keepdims=True)
    idx1 = jnp.min(jnp.where(e1 == top1, colf, big), axis=-1, keepdims=True)
    e2 = jnp.where(colf == idx1, neg, e1)
    top2 = jnp.max(e2, axis=-1, keepdims=True)
    idx2 = jnp.min(jnp.where(e2 == top2, colf, big), axis=-1, keepdims=True)
    t = jnp.exp(top2 - top1)
    w1 = p_grp / (1.0 + t)
    w2 = p_grp * (t / (1.0 + t))
    out = jnp.where(col == ROUTE_IDX, idx1, jnp.where(col == ROUTE_IDX + 1, idx2, 0.0))
    return jnp.where(col == ROUTE_W, w1, jnp.where(col == ROUTE_W + 1, w2, out))


def _route_onehot(route):
    colf = lax.broadcasted_iota(jnp.int32, route.shape, 1).astype(F32)
    oh1 = colf == route[:, ROUTE_IDX:ROUTE_IDX + 1]
    oh2 = colf == route[:, ROUTE_IDX + 1:ROUTE_IDX + 2]
    return jnp.where(jnp.logical_or(oh1, oh2), 1.0, 0.0), oh1, oh2


def _both(fwd_ref, bwd_ref):
    return fwd_ref[...].astype(F32) + bwd_ref[...].astype(F32)


def _stage_c_kernel(u_ref, h_ref, mod_ref, rg_y, rg_f, rg_b, gla_g, gla_f, gla_b, hg_g, hg_f, hg_b,
                    ml_o, ml_f, ml_b, gains, w_merge, b_merge, w_branch, w_out, ffn_g, w_route, b_route,
                    h_out, u2_out, wt_out, cnt_out):
    ys = (
        _gelu_tanh(rg_y[...]) * (rg_f[...] + rg_b[...]),
        _head_rmsnorm(_both(gla_f, gla_b), gains[0:1, :]) * _silu(gla_g[...]),
        _head_rmsnorm(_both(hg_f, hg_b), gains[1:2, :]) * _silu(hg_g[...]),
        _sigmoid(ml_o[...]) * _head_rmsnorm(_both(ml_f, ml_b), gains[2:3, :]),
    )
    u = u_ref[...]
    d = u.shape[-1]
    merged = None
    for kk, y in enumerate(ys):
        gate = _sigmoid(jnp.dot(u, w_merge[:, kk * d:(kk + 1) * d], preferred_element_type=F32)
                              + b_merge[:, kk * d:(kk + 1) * d])
        term = gate * _mm(y, w_branch[kk])
        merged = term if merged is None else merged + term
    mix = _mm(merged, w_out[...])
    h_new = h_ref[...] + mod_ref[2:3, :] * mix
    h_out[...] = h_new
    u2 = _rmsnorm_rows(h_new, ffn_g[...]) * (1.0 + mod_ref[4:5, :]) + mod_ref[3:4, :]
    _store_tiled(u2_out, u2)
    logits = _mm_split(u2, w_route[...]) + b_route[...]
    route = _route(logits)
    wt_out[...] = route
    cnt = jnp.sum(_route_onehot(route)[0], axis=0, keepdims=True)
    cnt_out[...] = jnp.broadcast_to(cnt, cnt_out.shape)


def _stage_c_call(u, h, mod, z_rg, rg_f, rg_b, z_gla, gla_f, gla_b, z_hg, hg_f, hg_b, z_ml, ml_f, ml_b,
                  gains, w_merge, b_merge, w_branch, w_out, ffn_g, w_route, b_route, nc, t0):
    bsz, n, d = h.shape
    nt = n // TM
    w = BRANCH_W

    n_out = n - t0 * TM

    def tok(width, col=0):
        return pl.BlockSpec((None, TM, width), lambda b, t: (b, t + t0, col))

    def out_tok(width):
        return pl.BlockSpec((None, TM, width), lambda b, t: (b, t, 0))

    mod_spec = pl.BlockSpec((None, 6, d), lambda b, t: (jnp.where(t + t0 < nc, bsz, b), 0, 0))
    return pl.pallas_call(
        _stage_c_kernel,
        grid=(bsz, nt - t0),
        in_specs=[tok(d), tok(d), mod_spec,
                  tok(w, 1), tok(w), tok(w),
                  tok(w, 3), tok(w), tok(w),
                  tok(w, 4), tok(w), tok(w),
                  tok(w, 3), tok(w), tok(w),
                  _const_spec(gains.shape), _const_spec(w_merge.shape), _const_spec(b_merge.shape),
                  _const_spec(w_branch.shape), _const_spec(w_out.shape), _const_spec(ffn_g.shape),
                  _const_spec(w_route.shape), _const_spec(b_route.shape)],
        out_specs=[out_tok(d),
                   pl.BlockSpec((None, TM * SUBLANES, HEAD_W), lambda b, t: (b, t, 0)),
                   out_tok(HEAD_W),
                   pl.BlockSpec((None, None, SUBLANES, HEAD_W), lambda b, t: (b, t, 0, 0))],
        out_shape=[jax.ShapeDtypeStruct((bsz, n_out, d), F32),
                   jax.ShapeDtypeStruct((bsz, n_out * SUBLANES, HEAD_W), F32),
                   jax.ShapeDtypeStruct((bsz, n_out, HEAD_W), F32),
                   jax.ShapeDtypeStruct((bsz, nt - t0, SUBLANES, HEAD_W), F32)],
        compiler_params=_params(("parallel", "parallel")),
        name="stage_c",
    )(u, h, mod, z_rg, rg_f, rg_b, z_gla, gla_f, gla_b, z_hg, hg_f, hg_b, z_ml, ml_f, ml_b,
      gains, w_merge, b_merge, w_branch, w_out, ffn_g, w_route, b_route)


def _slots_to_smem(slots_vmem, slots_smem, sem):
    cp = pltpu.make_async_copy(slots_vmem, slots_smem, sem)
    cp.start()
    cp.wait()


def _row_copy(src, src_row, dst, dst_row, sem):
    def tile(ref, row):
        return ref.at[pl.ds(pl.multiple_of(row * SUBLANES, SUBLANES), SUBLANES), :]
    return pltpu.make_async_copy(tile(src, src_row), tile(dst, dst_row), sem)


def _to_tiles(x):
    return [x[:, k * HEAD_W:(k + 1) * HEAD_W] for k in range(SUBLANES)]


def _store_tiled(ref, x):
    rows = x.shape[0]
    for k, piece in enumerate(_to_tiles(x)):
        ref[pl.ds(k, rows, stride=SUBLANES), :] = piece


def _load_tiled(ref, rows):
    return jnp.concatenate([ref[pl.ds(k, rows, stride=SUBLANES), :] for k in range(SUBLANES)], axis=-1)


def _dispatch_kernel(seg_ref, route_ref, base_ref, tri_ref, x_ref, xs_hbm, slots_ref,
                     slots_smem, zero_buf, sem_s, sem_r, sem_z):
    @pl.when(pl.program_id(0) == 0)
    def _():
        zero_buf[...] = jnp.zeros_like(zero_buf)

        def fill(e):
            start_row = pl.multiple_of((seg_ref[e] - MOE_BLK) * SUBLANES, SUBLANES)
            return pltpu.make_async_copy(
                zero_buf, xs_hbm.at[pl.ds(start_row, MOE_BLK * SUBLANES), :], sem_z)

        for e in range(N_EXPERTS):
            @pl.when(seg_ref[N_EXPERTS + e] > 0)
            def _(e=e):
                fill(e).start()
        for e in range(N_EXPERTS):
            @pl.when(seg_ref[N_EXPERTS + e] > 0)
            def _(e=e):
                fill(e).wait()

        blk_rows = MOE_BLK * SUBLANES
        used = seg_ref[N_EXPERTS - 1] // MOE_BLK

        def tail(b):
            return pltpu.make_async_copy(
                zero_buf, xs_hbm.at[pl.ds(pl.multiple_of(b * blk_rows, blk_rows), blk_rows), :], sem_z)

        def tail_start(b, carry):
            tail(b).start()
            return carry

        def tail_wait(b, carry):
            tail(b).wait()
            return carry

        n_blk = xs_hbm.shape[0] // blk_rows
        lax.fori_loop(used, n_blk, tail_start, 0)
        lax.fori_loop(used, n_blk, tail_wait, 0)

    route = route_ref[...]
    onehot, oh1, oh2 = _route_onehot(route)
    rank = jnp.dot(tri_ref[...], onehot.astype(BF16), preferred_element_type=F32)
    pos = base_ref[...] + rank
    slot_a = jnp.sum(jnp.where(oh1, pos, 0.0), axis=-1, keepdims=True)
    slot_b = jnp.sum(jnp.where(oh2, pos, 0.0), axis=-1, keepdims=True)
    col = lax.broadcasted_iota(jnp.int32, route.shape, 1)
    both = jnp.where(col == 0, slot_a, jnp.where(col == 1, slot_b, 0.0))
    slots_ref[...] = both.T[0:SUBLANES, :].astype(jnp.int32)
    _slots_to_smem(slots_ref, slots_smem, sem_s)

    def start(t, carry):
        for j in range(2):
            _row_copy(x_ref, t, xs_hbm, slots_smem[j, t], sem_r).start(priority=j)
        return carry

    def wait(t, carry):
        for j in range(2):
            _row_copy(x_ref, 0, xs_hbm, 0, sem_r).wait()
        return carry

    lax.fori_loop(0, TM, start, 0, unroll=8)
    lax.fori_loop(0, TM, wait, 0, unroll=8)


def _experts_kernel(blk_expert, n_used, x_ref, w1, w3, w2, o_ref):
    del blk_expert

    @pl.when(pl.program_id(0) < n_used[0])
    def _():
        x = _load_tiled(x_ref, MOE_BLK)
        h1 = _mm(x, w1[...])
        h3 = _mm(x, w3[...])
        _store_tiled(o_ref, _mm(_silu(h1) * h3, w2[...]))

    @pl.when(pl.program_id(0) >= n_used[0])
    def _():
        o_ref[...] = jnp.zeros_like(o_ref)


def _combine_kernel(slots_smem, route_ref, h_ref, mod_ref, fin_g, ys_hbm, o_ref, buf, sem_r, *, final):
    def start(t, carry):
        for j in range(2):
            _row_copy(ys_hbm, slots_smem[j, t], buf.at[j], t, sem_r).start(priority=j)
        return carry

    def wait(t, carry):
        for j in range(2):
            _row_copy(ys_hbm, 0, buf.at[j], 0, sem_r).wait()
        return carry

    lax.fori_loop(0, TM, start, 0, unroll=8)
    lax.fori_loop(0, TM, wait, 0, unroll=8)
    route = route_ref[...]
    y = (route[:, ROUTE_W:ROUTE_W + 1] * _load_tiled(buf.at[0], TM)
         + route[:, ROUTE_W + 1:ROUTE_W + 2] * _load_tiled(buf.at[1], TM))
    out = h_ref[...] + mod_ref[5:6, :] * y
    if final:
        out = _rmsnorm_rows(out, fin_g[...])
    o_ref[...] = out


def _moe_call(u2, route, cnt, h, mod, fin_g, w1, w3, w2, layer, nc, t0, final):
    bsz, n_out, d = h.shape
    de = w1.shape[-1]
    rows = bsz * n_out
    tiles = rows // TM
    tiles_per_row = n_out // TM
    n_blocks = (2 * rows + N_EXPERTS * (MOE_BLK - 1)) // MOE_BLK + 1

    counts = cnt[:, :, 0, :N_EXPERTS].reshape(tiles, N_EXPERTS)
    total = jnp.sum(counts, axis=0)
    padded = jnp.ceil(total / MOE_BLK) * MOE_BLK
    seg_end = jnp.cumsum(padded)
    base = (seg_end - padded)[None, :] + jnp.cumsum(counts, axis=0) - counts
    base = _pad_cols(base, HEAD_W).reshape(tiles, 1, HEAD_W)
    blk_start = jnp.arange(n_blocks, dtype=F32) * MOE_BLK
    blk_expert = jnp.minimum(jnp.sum(blk_start[:, None] >= seg_end[None, :], axis=1), N_EXPERTS - 1)
    n_used = (seg_end[-1] / MOE_BLK).astype(jnp.int32).reshape(1)
    tri = jnp.asarray(np.tril(np.ones((TM, TM), np.float32), -1), BF16)

    assert d == SUBLANES * HEAD_W
    slot_rows = n_blocks * MOE_BLK * SUBLANES
    seg = jnp.concatenate([seg_end, padded]).astype(jnp.int32)
    xs, slots = pl.pallas_call(
        _dispatch_kernel,
        grid_spec=pltpu.PrefetchScalarGridSpec(
            num_scalar_prefetch=1,
            grid=(tiles,),
            in_specs=[pl.BlockSpec((TM, HEAD_W), lambda t, sg: (t, 0)),
                      pl.BlockSpec((None, 1, HEAD_W), lambda t, sg: (t, 0, 0)),
                      pl.BlockSpec((TM, TM), lambda t, sg: (0, 0)),
                      pl.BlockSpec((TM * SUBLANES, HEAD_W), lambda t, sg: (t, 0))],
            out_specs=[pl.BlockSpec(memory_space=pl.ANY),
                       pl.BlockSpec((SUBLANES, TM), lambda t, sg: (t, 0))],
            scratch_shapes=[pltpu.SMEM((SUBLANES, TM), jnp.int32),
                            pltpu.VMEM((MOE_BLK * SUBLANES, HEAD_W), F32),
                            pltpu.SemaphoreType.DMA, pltpu.SemaphoreType.DMA,
                            pltpu.SemaphoreType.DMA]),
        out_shape=[jax.ShapeDtypeStruct((slot_rows, HEAD_W), F32),
                   jax.ShapeDtypeStruct((tiles * SUBLANES, TM), jnp.int32)],
        compiler_params=_params(("arbitrary",)),
        name="moe_dispatch",
    )(seg, route.reshape(rows, HEAD_W), base, tri, u2.reshape(rows * SUBLANES, HEAD_W))

    blk_spec = pl.BlockSpec((MOE_BLK * SUBLANES, HEAD_W), lambda b, be, nu: (b, 0))
    used_blk_spec = pl.BlockSpec((MOE_BLK * SUBLANES, HEAD_W),
                                 lambda b, be, nu: (jnp.minimum(b, nu[0] - 1), 0))
    ys = pl.pallas_call(
        _experts_kernel,
        grid_spec=pltpu.PrefetchScalarGridSpec(
            num_scalar_prefetch=2,
            grid=(n_blocks,),
            in_specs=[used_blk_spec,
                      pl.BlockSpec((None, None, d, de), lambda b, be, nu: (layer, be[b], 0, 0)),
                      pl.BlockSpec((None, None, d, de), lambda b, be, nu: (layer, be[b], 0, 0)),
                      pl.BlockSpec((None, None, de, d), lambda b, be, nu: (layer, be[b], 0, 0))],
            out_specs=blk_spec),
        out_shape=jax.ShapeDtypeStruct((slot_rows, HEAD_W), F32),
        compiler_params=_params(("arbitrary",)),
        name="moe_experts",
    )(blk_expert.astype(jnp.int32), n_used, xs, w1, w3, w2)

    def mod_index(t):
        b, pos = t // tiles_per_row, t % tiles_per_row
        return (jnp.where(pos + t0 < nc, bsz, b), 0, 0)

    out = pl.pallas_call(
        functools.partial(_combine_kernel, final=final),
        grid=(tiles,),
        in_specs=[pl.BlockSpec((None, SUBLANES, TM), lambda t: (t, 0, 0), memory_space=pltpu.SMEM),
                  pl.BlockSpec((TM, HEAD_W), lambda t: (t, 0)),
                  pl.BlockSpec((TM, d), lambda t: (t, 0)),
                  pl.BlockSpec((None, 6, d), mod_index),
                  _const_spec((1, d)),
                  pl.BlockSpec(memory_space=pl.ANY)],
        out_specs=pl.BlockSpec((TM, d), lambda t: (t, 0)),
        out_shape=jax.ShapeDtypeStruct((rows, d), F32),
        scratch_shapes=[pltpu.VMEM((2, TM * SUBLANES, HEAD_W), F32),
                        pltpu.SemaphoreType.DMA],
        compiler_params=_params(("arbitrary",)),
        name="moe_combine",
    )(slots.reshape(tiles, SUBLANES, TM), route.reshape(rows, HEAD_W), h.reshape(rows, d), mod, fin_g, ys)
    return out.reshape(bsz, n_out, d)


def _pad_heads(w, dk):
    lead = w.shape[:-1]
    w = w.reshape(lead + (N_HEADS, dk))
    w = jnp.pad(w, [(0, 0)] * len(lead) + [(0, 0), (0, HEAD_W - dk)])
    return w.reshape(lead + (N_HEADS * HEAD_W,))


def _pad_cols(w, width):
    return jnp.pad(w, [(0, 0)] * (w.ndim - 1) + [(0, width - w.shape[-1])])


def _split_w_in(w_in):
    bw = BRANCH_W
    sizes = [bw, bw, N_HEADS * GLA_DK, N_HEADS * GLA_DK, bw, bw, 2 * GLA_RANK,
             bw, bw, 2 * bw, bw, bw, bw, bw, bw, 16]
    offs = np.cumsum([0] + sizes)
    p = [w_in[:, offs[i]:offs[i + 1]] for i in range(len(sizes))]
    w_rg = jnp.concatenate([p[0], p[1]], axis=1)
    w_gla = jnp.concatenate([_pad_heads(p[2], GLA_DK), _pad_heads(p[3], GLA_DK), p[4], p[5],
                             _pad_cols(p[6], HEAD_W)], axis=1)
    w_hg = jnp.concatenate([p[7], p[8], p[9], p[10]], axis=1)
    gates = p[15].reshape(-1, 2, 2, N_HEADS)
    w_gi = _pad_cols(gates[:, :, 0].reshape(-1, 2 * N_HEADS), HEAD_W)
    w_gf = _pad_cols(gates[:, :, 1].reshape(-1, 2 * N_HEADS), HEAD_W)
    w_ml = jnp.concatenate([p[11], p[12], p[13], p[14], w_gi, w_gf], axis=1)
    return [w.astype(BF16) for w in (w_rg, w_gla, w_hg, w_ml)]


def _block_diag(w):
    k, n = w.shape[-3], w.shape[-1]
    eye = jnp.eye(k, dtype=w.dtype)
    full = jnp.einsum('...kij,kl->...kilj', w, eye)
    return full.reshape(w.shape[:-3] + (k * n, k * n))


def kernel(x, c, ctx, c_ctx, ada_w, ada_b, norm_mix_g, norm_ffn_g, w_in, rg_conv_w, rg_conv_b, rg_gate_w, rg_gate_b, rg_lambda, gla_w_lr, gla_b_lr, gla_norm_g, hgrn_lb_logits, hgrn_norm_g, ml_conv_w, ml_conv_b, ml_gate_b, ml_norm_g, w_branch, w_merge, b_merge, w_out, moe_w_group, moe_b_group, moe_w_expert, moe_b_expert, moe_w1, moe_w3, moe_w2, final_norm_g):
    bsz, seq, d = x.shape
    n_ctx = ctx.shape[1]
    depth = ada_w.shape[0]
    assert n_ctx % TM == 0 and seq % TM == 0 and d == 2 * BRANCH_W
    nc = n_ctx // TM

    h = jnp.concatenate([ctx, x], axis=1)
    cvec = jnp.zeros((SUBLANES, d), F32).at[:bsz].set(c).at[bsz].set(c_ctx)
    mod_all = _mod_call(cvec, ada_w, ada_b).reshape(depth, SUBLANES, 6, d)[:, :bsz + 1]

    lb_cum = jnp.cumsum(jax.nn.softmax(hgrn_lb_logits.astype(F32), axis=0), axis=0)
    hgrn_lb = lb_cum - lb_cum[:1]
    tables = [t for levels in SPLIT_LEVELS for t in _decay_tables(levels)]
    sel0, masks0 = _decay_tables(0)
    tri_sel = sel0[:, :CHUNK, :]
    tri_mask = masks0[:, 0]

    out = None
    for l in range(depth):
        last = l == depth - 1
        mod = mod_all[l]
        w_rg, w_gla, w_hg, w_ml = _split_w_in(w_in[l])
        z_rg, z_gla, z_hg, z_ml, u = _stage_a_call(h, mod, norm_mix_g[l], w_rg, w_gla, w_hg, w_ml, nc)

        rg_f, rg_b = _rg_call(z_rg, rg_conv_w[l], rg_conv_b[l].reshape(1, -1),
                              _block_diag(rg_gate_w[l]).astype(BF16),
                              rg_gate_b[l].reshape(4, -1), rg_lambda[l], nc)

        wlr = jnp.zeros((2, HEAD_W, N_HEADS * HEAD_W), F32)
        wlr_p = _pad_heads(gla_w_lr[l], GLA_DK)
        wlr = wlr.at[0, :GLA_RANK].set(wlr_p[0]).at[1, GLA_RANK:2 * GLA_RANK].set(wlr_p[1])
        gla_f, gla_b = _gla_call(z_gla, wlr.astype(BF16), _pad_heads(gla_b_lr[l], GLA_DK), tables, nc)

        lb = hgrn_lb[l]
        lbp = jnp.concatenate([lb, jnp.log(lb), jnp.log1p(-lb), jnp.zeros((2, lb.shape[-1]), F32)], axis=0)
        hg_f, hg_b = _hg_call(z_hg, lbp, tables, nc)

        gbias = _pad_cols(ml_gate_b[l].transpose(1, 0, 2).reshape(2, -1), HEAD_W)
        ml_f, ml_b = _ml_call(z_ml, ml_conv_w[l], ml_conv_b[l].reshape(1, -1), gbias, tri_sel, tri_mask, nc)

        gains = jnp.zeros((SUBLANES, HEAD_W), F32).at[0].set(gla_norm_g[l]).at[1].set(hgrn_norm_g[l]).at[2].set(ml_norm_g[l])
        w_route = _split_cols(_pad_cols(jnp.concatenate([moe_w_expert[l], moe_w_group[l]], axis=1), HEAD_W))
        b_route = _pad_cols(jnp.concatenate([moe_b_expert[l], moe_b_group[l]]).reshape(1, -1), HEAD_W)
        t0 = nc if last else 0
        h_mid, u2, route, cnt = _stage_c_call(
            u, h, mod, z_rg, rg_f, rg_b, z_gla, gla_f, gla_b, z_hg, hg_f, hg_b, z_ml, ml_f, ml_b,
            gains, w_merge[l].astype(BF16), b_merge[l].reshape(1, -1), w_branch[l].astype(BF16),
            w_out[l].astype(BF16), norm_ffn_g[l].reshape(1, -1), w_route, b_route, nc, t0)

        res = _moe_call(u2, route, cnt, h_mid, mod, final_norm_g.reshape(1, -1), moe_w1,
                        moe_w3, moe_w2, l, nc, t0, last)
        if last:
            out = res
        else:
            h = res
    return out
```
